```python
import jax, jax.numpy as jnp
from jax import lax
import numpy as np

D_MODEL = 1024
BATCH = 8
SEQ = 4096
DEPTH = 2

N_A_LAYERS = DEPTH // 2
N_B_LAYERS = DEPTH - N_A_LAYERS
N_DENSE = (DEPTH + 1) // 2
N_MOE = DEPTH // 2
RMS_EPS = 1e-6
A_HEADS = 4
A_INNER = D_MODEL
A_HEAD_DIM = A_INNER // A_HEADS
A_CONV = 4
A_CHUNK = 64
B_HEADS = 16
B_HEAD_DIM = D_MODEL // B_HEADS
B_KV_GROUPS = 4
B_REP = B_HEADS // B_KV_GROUPS
CMP_LEN = 32
CMP_STRIDE = 16
CMP_HIDDEN = 256
SEL_BLOCK = 64
SEL_TOPN = 16
WINDOW = 512
NSA_Q_BLOCK = 32
FORCED_SCORE = 1e4
FFN_DENSE = 2816
N_EXPERTS = 8
TOP_K = 2
FFN_EXPERT = 3584
FFN_TOKEN_BLOCK = 128

kernel_name = "yoco_mlstm_nsa_moe_block"


def rmsnorm(x, g):
    xf = x.astype(jnp.float32)
    y = xf * lax.rsqrt(jnp.mean(xf * xf, axis=-1, keepdims=True) + RMS_EPS) * g.astype(jnp.float32)
    return y.astype(x.dtype)


def alibi_slopes(n):
    return jnp.asarray(np.power(2.0, -8.0 * np.arange(1, n + 1) / n).astype(np.float32))


def safe_softmax(s):
    m = jnp.max(s, axis=-1, keepdims=True)
    m = jnp.where(jnp.isfinite(m), m, 0.0)
    e = jnp.exp(s - m)
    d = jnp.sum(e, axis=-1, keepdims=True)
    return e / jnp.where(d > 0, d, 1.0)


def causal_dwconv(x, w):
    K = w.shape[0]
    T = x.shape[1]
    xp = jnp.pad(x, ((0, 0), (K - 1, 0), (0, 0)))
    y = xp[:, 0:T] * w[0]
    for j in range(1, K):
        y = y + xp[:, j:j + T] * w[j]
    return y


def _mlstm_chunk(carry, inp):
    c, n, m = carry
    q, k, v, log_i, log_f = inp
    L = q.shape[2]
    causal = jnp.tril(jnp.ones((L, L), dtype=bool))
    b = jnp.cumsum(log_f, axis=-1)
    d = jnp.where(causal, b[..., :, None] - b[..., None, :] + log_i[..., None, :], -jnp.inf)
    inter = b + m[..., None]
    m_t = jnp.maximum(inter, jnp.max(d, axis=-1))
    w_intra = jnp.exp(d - m_t[..., None])
    w_inter = jnp.exp(inter - m_t)
    s = jnp.einsum('bhtd,bhsd->bhts', q, k) * w_intra
    num = jnp.einsum('bhts,bhsd->bhtd', s, v) + w_inter[..., None] * jnp.einsum('bhvk,bhtk->bhtv', c, q)
    den = jnp.sum(s, axis=-1) + w_inter * jnp.einsum('bhk,bhtk->bht', n, q)
    h = num / jnp.maximum(jnp.abs(den), jnp.exp(-m_t))[..., None]
    b_last = b[..., -1]
    g = b_last[..., None] - b + log_i
    m_new = jnp.maximum(b_last + m, jnp.max(g, axis=-1))
    a_prev = jnp.exp(b_last + m - m_new)
    a_s = jnp.exp(g - m_new[..., None])
    c_new = a_prev[..., None, None] * c + jnp.einsum('bhsv,bhsk->bhvk', v * a_s[..., None], k)
    n_new = a_prev[..., None] * n + jnp.einsum('bhs,bhsk->bhk', a_s, k)
    return (c_new, n_new, m_new), h


def mlstm_mixer(h, w_in, b_gate, conv_w, g_out, w_out):
    B, T, _ = h.shape
    f32 = jnp.float32
    proj = h @ w_in
    qk = causal_dwconv(proj[..., :2 * A_INNER], conv_w)
    q = qk[..., :A_INNER]
    k = qk[..., A_INNER:]
    v = proj[..., 2 * A_INNER:3 * A_INNER]
    o = proj[..., 3 * A_INNER:4 * A_INNER]
    gl = (proj[..., 4 * A_INNER:] + b_gate).astype(f32)
    log_i = gl[..., :A_HEADS]
    log_f = jax.nn.log_sigmoid(gl[..., A_HEADS:])
    nc = T // A_CHUNK

    def to_chunks(a):
        return a.astype(f32).reshape(B, nc, A_CHUNK, A_HEADS, A_HEAD_DIM).transpose(1, 0, 3, 2, 4)

    def gate_chunks(a):
        return a.reshape(B, nc, A_CHUNK, A_HEADS).transpose(1, 0, 3, 2)

    qc = to_chunks(q)
    kc = to_chunks(k) * (A_HEAD_DIM ** -0.5)
    vc = to_chunks(v)
    init = (jnp.zeros((B, A_HEADS, A_HEAD_DIM, A_HEAD_DIM), f32),
            jnp.zeros((B, A_HEADS, A_HEAD_DIM), f32),
            jnp.zeros((B, A_HEADS), f32))
    _, hc = lax.scan(_mlstm_chunk, init, (qc, kc, vc, gate_chunks(log_i), gate_chunks(log_f)))
    hs = hc.transpose(1, 0, 3, 2, 4).reshape(B, T, A_HEADS, A_HEAD_DIM)
    hs = hs * lax.rsqrt(jnp.mean(hs * hs, axis=-1, keepdims=True) + RMS_EPS)
    hs = hs.reshape(B, T, A_INNER) * g_out.astype(f32) * jax.nn.sigmoid(o.astype(f32))
    return hs.astype(h.dtype) @ w_out


def nsa_shared_kv(hkv, w_kv, cmp_pos, cmp_w1, cmp_w2):
    B, T, _ = hkv.shape
    G, Dh = B_KV_GROUPS, B_HEAD_DIM
    kv = (hkv @ w_kv).reshape(B, T, 6, G, Dh)
    n_cmp = (T - CMP_LEN) // CMP_STRIDE + 1
    idx = np.arange(n_cmp)[:, None] * CMP_STRIDE + np.arange(CMP_LEN)[None, :]
    blk = kv[:, idx, :2]
    blk = blk + cmp_pos[None, None, :, :, None, :]
    blk = blk.transpose(0, 3, 4, 1, 2, 5).reshape(B, 2, G, n_cmp, CMP_LEN * Dh)
    hid = jax.nn.silu(jnp.einsum('bcgnf,cfh->bcgnh', blk, cmp_w1))
    kvc = jnp.einsum('bcgnh,chd->bcgnd', hid, cmp_w2)
    n_sel = T // SEL_BLOCK
    k_slc = kv[:, :, 2].transpose(0, 2, 1, 3).reshape(B, G, n_sel, SEL_BLOCK, Dh)
    v_slc = kv[:, :, 3].transpose(0, 2, 1, 3).reshape(B, G, n_sel, SEL_BLOCK, Dh)
    pad = ((0, 0), (0, 0), (WINDOW, 0), (0, 0))
    k_win = jnp.pad(kv[:, :, 4].transpose(0, 2, 1, 3), pad)
    v_win = jnp.pad(kv[:, :, 5].transpose(0, 2, 1, 3), pad)
    return (kvc[:, 0], kvc[:, 1], k_slc, v_slc, k_win, v_win)


def nsa_mixer(h, shared, w_q, b_gate, w_out):
    k_cmp, v_cmp, k_slc, v_slc, k_win, v_win = shared
    B, T, _ = h.shape
    G, R, Dh = B_KV_GROUPS, B_REP, B_HEAD_DIM
    HD = B_HEADS * Dh
    f32 = jnp.float32
    scale = Dh ** -0.5
    proj = h @ w_q
    q = proj[..., :HD].reshape(B, T, G, R, Dh).transpose(0, 2, 3, 1, 4)
    gates = jax.nn.sigmoid((proj[..., HD:] + b_gate).astype(f32))
    gates = gates.reshape(B, T, G, R, 3).transpose(0, 2, 3, 1, 4)
    slopes = alibi_slopes(B_HEADS).reshape(G, R)
    sl5 = slopes[None, :, :, None, None]
    sl6 = slopes[None, :, :, None, None, None]
    n_cmp = k_cmp.shape[2]
    n_sel = k_slc.shape[2]
    n_top = min(SEL_TOPN, n_sel)
    cmp_end = jnp.arange(n_cmp) * CMP_STRIDE + (CMP_LEN - 1)
    ci = np.arange(n_cmp)[:, None] * CMP_STRIDE
    sj = np.arange(n_sel)[None, :] * SEL_BLOCK
    overlap = jnp.asarray(((ci < sj + SEL_BLOCK) & (ci + CMP_LEN > sj)).astype(np.float32))
    blk_j = jnp.arange(n_sel)
    bi = jnp.arange(B)[:, None, None, None]
    gi = jnp.arange(G)[None, :, None, None]
    QB = NSA_Q_BLOCK

    def block_fn(iq):
        t0 = iq * QB
        tpos = t0 + jnp.arange(QB)
        qb = lax.dynamic_slice_in_dim(q, t0, QB, axis=3)
        gb = lax.dynamic_slice_in_dim(gates, t0, QB, axis=3)
        dist_c = (tpos[:, None] - cmp_end[None, :]).astype(f32)
        s_c = jnp.einsum('bgrqd,bgcd->bgrqc', qb, k_cmp).astype(f32) * scale - sl5 * dist_c
        p_cmp = safe_softmax(jnp.where(dist_c >= 0, s_c, -jnp.inf))
        o_cmp = jnp.einsum('bgrqc,bgcd->bgrqd', p_cmp, v_cmp)
        imp = jnp.einsum('bgrqc,cj->bgqj', p_cmp, overlap)
        cur = tpos // SEL_BLOCK
        forced = (blk_j[None, :] == 0) | (blk_j[None, :] == cur[:, None]) | (blk_j[None, :] == cur[:, None] - 1)
        valid = blk_j[None, :] * SEL_BLOCK <= tpos[:, None]
        imp = jnp.where(forced[None, None], FORCED_SCORE, imp)
        imp = jnp.where(valid[None, None], imp, -jnp.inf)
        _, idx = lax.top_k(imp, n_top)
        ks = k_slc[bi, gi, idx]
        vs = v_slc[bi, gi, idx]
        kpos = idx[..., None] * SEL_BLOCK + jnp.arange(SEL_BLOCK)
        dist_s = (tpos[:, None, None] - kpos).astype(f32)
        s_s = jnp.einsum('bgrqd,bgqnkd->bgrqnk', qb, ks).astype(f32) * scale - sl6 * dist_s[:, :, None]
        s_s = jnp.where((dist_s >= 0)[:, :, None], s_s, -jnp.inf).reshape(B, G, R, QB, n_top * SEL_BLOCK)
        p_s = safe_softmax(s_s)
        o_slc = jnp.einsum('bgrqk,bgqkd->bgrqd', p_s, vs.reshape(B, G, QB, n_top * SEL_BLOCK, Dh))
        kw = lax.dynamic_slice_in_dim(k_win, t0, WINDOW + QB, axis=2)
        vw = lax.dynamic_slice_in_dim(v_win, t0, WINDOW + QB, axis=2)
        spos = t0 - WINDOW + jnp.arange(WINDOW + QB)
        dist_w = tpos[:, None] - spos[None, :]
        mask_w = (dist_w >= 0) & (dist_w < WINDOW) & (spos[None, :] >= 0)
        s_w = jnp.einsum('bgrqd,bgkd->bgrqk', qb, kw).astype(f32) * scale - sl5 * dist_w.astype(f32)
        p_w = safe_softmax(jnp.where(mask_w, s_w, -jnp.inf))
        o_win = jnp.einsum('bgrqk,bgkd->bgrqd', p_w, vw)
        return gb[..., 0:1] * o_cmp + gb[..., 1:2] * o_slc + gb[..., 2:3] * o_win

    outs = lax.map(block_fn, jnp.arange(T // QB))
    o = outs.transpose(1, 0, 4, 2, 3, 5).reshape(B, T, HD)
    return o.astype(h.dtype) @ w_out


def token_block_map(fn, x):
    B, T, D = x.shape
    nb = T // FFN_TOKEN_BLOCK
    xb = x.reshape(B, nb, FFN_TOKEN_BLOCK, D).transpose(1, 0, 2, 3).reshape(nb, B * FFN_TOKEN_BLOCK, D)
    yb = lax.map(fn, xb)
    return yb.reshape(nb, B, FFN_TOKEN_BLOCK, D).transpose(1, 0, 2, 3).reshape(B, T, D)


def swiglu_dense(h, w_gu, w_down):
    def fn(t):
        gu = t @ w_gu
        return (jax.nn.silu(gu[..., :FFN_DENSE]) * gu[..., FFN_DENSE:]) @ w_down
    return token_block_map(fn, h)


def moe_swiglu(h, w_router, w_gu, w_down):
    def fn(t):
        logits = (t @ w_router).astype(jnp.float32)
        top_v, top_i = lax.top_k(logits, TOP_K)
        top_w = jax.nn.softmax(top_v, axis=-1)
        gate = jnp.sum(jax.nn.one_hot(top_i, N_EXPERTS, dtype=jnp.float32) * top_w[..., None], axis=1)
        gu = jnp.einsum('nd,edf->nef', t, w_gu)
        act = jax.nn.silu(gu[..., :FFN_EXPERT]) * gu[..., FFN_EXPERT:] * gate[..., None].astype(t.dtype)
        return jnp.einsum('nef,efd->nd', act, w_down)
    return token_block_map(fn, h)


def setup_inputs(seed: int = 0) -> dict:
    key = jax.random.key(seed)
    ks = jax.random.split(key, 24)
    f32 = jnp.float32

    def nrm(k, shape, s):
        return jax.random.normal(k, shape, f32) * s

    D = D_MODEL
    HD = B_HEADS * B_HEAD_DIM
    GD = B_KV_GROUPS * B_HEAD_DIM
    i_bias = nrm(ks[4], (N_A_LAYERS, A_HEADS), 0.1)
    f_bias = jnp.linspace(3.0, 6.0, A_HEADS, dtype=f32)[None, :] + nrm(ks[5], (N_A_LAYERS, A_HEADS), 0.1)
    return {
        "x": nrm(ks[0], (BATCH, SEQ, D), 1.0),
        "norm_mix": 1.0 + nrm(ks[1], (DEPTH, D), 0.02),
        "norm_ffn": 1.0 + nrm(ks[2], (DEPTH, D), 0.02),
        "a_w_in": nrm(ks[3], (N_A_LAYERS, D, 4 * A_INNER + 2 * A_HEADS), D ** -0.5),
        "a_b_gate": jnp.concatenate([i_bias, f_bias], axis=-1),
        "a_conv": nrm(ks[6], (N_A_LAYERS, A_CONV, 2 * A_INNER), A_CONV ** -0.5),
        "a_norm_h": 1.0 + nrm(ks[7], (N_A_LAYERS, A_INNER), 0.02),
        "a_w_out": nrm(ks[8], (N_A_LAYERS, A_INNER, D), A_INNER ** -0.5),
        "norm_kv": 1.0 + nrm(ks[9], (D,), 0.02),
        "b_w_kv": nrm(ks[10], (D, 6 * GD), D ** -0.5),
        "b_cmp_pos": nrm(ks[11], (CMP_LEN, 2, B_HEAD_DIM), 0.02),
        "b_cmp_w1": nrm(ks[12], (2, CMP_LEN * B_HEAD_DIM, CMP_HIDDEN), (CMP_LEN * B_HEAD_DIM) ** -0.5),
        "b_cmp_w2": nrm(ks[13], (2, CMP_HIDDEN, B_HEAD_DIM), CMP_HIDDEN ** -0.5),
        "b_w_q": nrm(ks[14], (N_B_LAYERS, D, HD + 3 * B_HEADS), D ** -0.5),
        "b_b_gate": nrm(ks[15], (N_B_LAYERS, 3 * B_HEADS), 0.02),
        "b_w_out": nrm(ks[16], (N_B_LAYERS, HD, D), HD ** -0.5),
        "f_w_gu": nrm(ks[17], (N_DENSE, D, 2 * FFN_DENSE), D ** -0.5),
        "f_w_down": nrm(ks[18], (N_DENSE, FFN_DENSE, D), FFN_DENSE ** -0.5),
        "m_router": nrm(ks[19], (N_MOE, D, N_EXPERTS), D ** -0.5),
        "m_w_gu": nrm(ks[20], (N_MOE, N_EXPERTS, D, 2 * FFN_EXPERT), D ** -0.5),
        "m_w_down": nrm(ks[21], (N_MOE, N_EXPERTS, FFN_EXPERT, D), FFN_EXPERT ** -0.5),
        "norm_final": 1.0 + nrm(ks[22], (D,), 0.02),
    }


def reference(x, norm_mix, norm_ffn, a_w_in, a_b_gate, a_conv, a_norm_h, a_w_out, norm_kv, b_w_kv,
              b_cmp_pos, b_cmp_w1, b_cmp_w2, b_w_q, b_b_gate, b_w_out, f_w_gu, f_w_down,
              m_router, m_w_gu, m_w_down, norm_final):
    shared = None
    for i in range(DEPTH):
        if i < N_A_LAYERS:
            x = x + mlstm_mixer(rmsnorm(x, norm_mix[i]), a_w_in[i], a_b_gate[i], a_conv[i],
                                a_norm_h[i], a_w_out[i])
        else:
            if i == N_A_LAYERS:
                shared = nsa_shared_kv(rmsnorm(x, norm_kv), b_w_kv, b_cmp_pos, b_cmp_w1, b_cmp_w2)
            j = i - N_A_LAYERS
            x = x + nsa_mixer(rmsnorm(x, norm_mix[i]), shared, b_w_q[j], b_b_gate[j], b_w_out[j])
        hf = rmsnorm(x, norm_ffn[i])
        if i % 2 == 0:
            x = x + swiglu_dense(hf, f_w_gu[i // 2], f_w_down[i // 2])
        else:
            x = x + moe_swiglu(hf, m_router[i // 2], m_w_gu[i // 2], m_w_down[i // 2])
    return rmsnorm(x, norm_final)
```

```python
import functools

import numpy as np
import jax
import jax.numpy as jnp
from jax import lax
from jax.experimental import pallas as pl
from jax.experimental.pallas import tpu as pltpu

F32 = jnp.float32
BF16 = jnp.bfloat16

RMS_EPS = 1e-6
A_HEADS = 4
A_CONV = 4
B_HEADS = 16
B_KV_GROUPS = 4
B_REP = B_HEADS // B_KV_GROUPS
B_HEAD_DIM = 64
CMP_LEN = 32
CMP_STRIDE = 16
SEL_BLOCK = 64
SEL_TOPN = 16
WINDOW = 512
FORCED_SCORE = 1e4
N_EXPERTS = 8

LANES = 128
V7X_VMEM_BYTES = 64 * 1024 * 1024
VMEM_LIMIT = V7X_VMEM_BYTES - 8 * 1024 * 1024
NEG = -1e30


def _cparams(*sem):
    return pltpu.CompilerParams(dimension_semantics=sem, vmem_limit_bytes=VMEM_LIMIT)


def _dot(a, b):
    return jnp.dot(a, b, preferred_element_type=F32)


def _dot_nt(a, b):
    return lax.dot_general(a, b, (((1,), (1,)), ((), ())), preferred_element_type=F32)


def _dot_tn(a, b):
    return lax.dot_general(a, b, (((0,), (0,)), ((), ())), preferred_element_type=F32)


def _split3(a):
    a1 = a.astype(BF16)
    r1 = a - a1.astype(F32)
    a2 = r1.astype(BF16)
    a3 = (r1 - a2.astype(F32)).astype(BF16)
    return a1, a2, a3


def _dot_f32(a, b):
    a1, a2, a3 = _split3(a)
    b1, b2, b3 = _split3(b)
    return (_dot(a1, b1) + _dot(a1, b2) + _dot(a2, b1)
            + _dot(a2, b2) + _dot(a1, b3) + _dot(a3, b1))


def _dot_f32_nt(a, b):
    a1, a2, a3 = _split3(a)
    b1, b2, b3 = _split3(b)
    return (_dot_nt(a1, b1) + _dot_nt(a1, b2) + _dot_nt(a2, b1)
            + _dot_nt(a2, b2) + _dot_nt(a1, b3) + _dot_nt(a3, b1))


def _rms(xf, g):
    return xf * lax.rsqrt(jnp.mean(xf * xf, axis=-1, keepdims=True) + RMS_EPS) * g


def _silu(x):
    return x * jax.nn.sigmoid(x)


def _log_sigmoid(x):
    return jnp.minimum(x, 0.0) - jnp.log(1.0 + jnp.exp(-jnp.abs(x)))


def _norm_matmul_kernel(x_ref, g_ref, w_ref, o_ref, xn_ref, *, exact):
    @pl.when(pl.program_id(1) == 0)
    def _():
        xn_ref[...] = _rms(x_ref[...], g_ref[...]).astype(xn_ref.dtype)

    if exact:
        o_ref[...] = _dot_f32(xn_ref[...], w_ref[...]).astype(o_ref.dtype)
    else:
        o_ref[...] = _dot(xn_ref[...], w_ref[...]).astype(o_ref.dtype)


def norm_matmul(x, g, w, *, out_dtype, exact=False, tm=1024, tn=1024):
    n, d = x.shape
    dout = w.shape[1]
    tm = min(tm, n)
    tn = min(tn, dout)
    return pl.pallas_call(
        functools.partial(_norm_matmul_kernel, exact=exact),
        grid=(n // tm, dout // tn),
        in_specs=[pl.BlockSpec((tm, d), lambda i, j: (i, 0)),
                  pl.BlockSpec((1, d), lambda i, j: (0, 0)),
                  pl.BlockSpec((d, tn), lambda i, j: (0, j))],
        out_specs=pl.BlockSpec((tm, tn), lambda i, j: (i, j)),
        out_shape=jax.ShapeDtypeStruct((n, dout), out_dtype),
        scratch_shapes=[pltpu.VMEM((tm, d), F32 if exact else BF16)],
        compiler_params=_cparams("parallel", "arbitrary"),
        name="norm_matmul_f32" if exact else "norm_matmul",
    )(x, g.reshape(1, d), w)


def _matmul_res_kernel(a_ref, w_ref, r_ref, o_ref):
    o_ref[...] = r_ref[...] + _dot(a_ref[...], w_ref[...])


def matmul_residual(a, w, res, *, tm=1024):
    n, k = a.shape
    dout = w.shape[1]
    tm = min(tm, n)
    return pl.pallas_call(
        _matmul_res_kernel,
        grid=(n // tm,),
        in_specs=[pl.BlockSpec((tm, k), lambda i: (i, 0)),
                  pl.BlockSpec((k, dout), lambda i: (0, 0)),
                  pl.BlockSpec((tm, dout), lambda i: (i, 0))],
        out_specs=pl.BlockSpec((tm, dout), lambda i: (i, 0)),
        out_shape=jax.ShapeDtypeStruct((n, dout), F32),
        compiler_params=_cparams("parallel"),
        name="matmul_residual",
    )(a, w, res)


MLSTM_CHUNK = 256


def _mlstm_kernel(qk_ref, v_ref, o_ref, gcol_ref, grow_ref, bcol_ref, brow_ref, convw_ref, gout_ref,
                  out_ref, ct_ref, n_ref, m_ref, prev_ref):
    L = qk_ref.shape[0]
    H = A_HEADS
    inner = v_ref.shape[1]
    dh = inner // H

    @pl.when(pl.program_id(1) == 0)
    def _():
        ct_ref[...] = jnp.zeros_like(ct_ref)
        n_ref[...] = jnp.zeros_like(n_ref)
        m_ref[...] = jnp.zeros_like(m_ref)
        prev_ref[...] = jnp.zeros_like(prev_ref)

    row = lax.broadcasted_iota(jnp.int32, (L, L), 0)
    col = lax.broadcasted_iota(jnp.int32, (L, L), 1)
    causal = col <= row
    tril = jnp.where(causal, 1.0, 0.0).astype(BF16)
    triu = jnp.where(row <= col, 1.0, 0.0).astype(BF16)

    gc = gcol_ref[...] + brow_ref[...]
    gr = grow_ref[...] + bcol_ref[...]
    lfc1, lfc2, lfc3 = _split3(_log_sigmoid(gc))
    lfr1, lfr2, lfr3 = _split3(_log_sigmoid(gr))
    b_c = _dot(tril, lfc1) + _dot(tril, lfc2) + _dot(tril, lfc3)
    b_r = _dot(lfr1, triu) + _dot(lfr2, triu) + _dot(lfr3, triu)

    rowi = lax.broadcasted_iota(jnp.int32, (L, dh), 0)

    def conv(cur, prev, w):
        y = cur * w[A_CONV - 1:A_CONV, :]
        for s in range(1, A_CONV):
            sh = jnp.where(rowi < s, pltpu.roll(prev, s, 0), pltpu.roll(cur, s, 0))
            y = y + sh * w[A_CONV - 1 - s:A_CONV - s, :]
        return y

    for h in range(H):
        hs = slice(h * dh, (h + 1) * dh)
        ks = slice(inner + h * dh, inner + (h + 1) * dh)
        q = conv(qk_ref[:, hs].astype(F32), prev_ref[:, hs].astype(F32), convw_ref[:, hs])
        k = conv(qk_ref[:, ks].astype(F32), prev_ref[:, ks].astype(F32), convw_ref[:, ks]) * (dh ** -0.5)
        v = v_ref[:, hs]
        qb = q.astype(BF16)
        kb = k.astype(BF16)

        li_c = gc[:, h:h + 1]
        bc = b_c[:, H + h:H + h + 1]
        li_r = gr[h:h + 1, :]
        br = b_r[H + h:H + h + 1, :]
        m_prev = m_ref[h:h + 1, 0:1]

        d = jnp.where(causal, bc - br + li_r, -jnp.inf)
        inter = bc + m_prev
        m_t = jnp.maximum(inter, jnp.max(d, axis=-1, keepdims=True))
        w_inter = jnp.exp(inter - m_t)
        s = _dot_nt(qb, kb) * jnp.exp(d - m_t)
        ct = ct_ref[h]
        num = _dot(s.astype(BF16), v) + w_inter * _dot(qb, ct.astype(BF16))
        den = jnp.sum(s, axis=-1, keepdims=True) + w_inter * jnp.sum(q * n_ref[h], axis=-1, keepdims=True)
        hh = num / jnp.maximum(jnp.abs(den), jnp.exp(-m_t))
        hh = hh * lax.rsqrt(jnp.mean(hh * hh, axis=-1, keepdims=True) + RMS_EPS)
        out_ref[:, hs] = (hh * gout_ref[:, hs] * jax.nn.sigmoid(o_ref[:, hs].astype(F32))).astype(out_ref.dtype)

        b_last = bc[L - 1:L, :]
        g = b_last - bc + li_c
        m_new = jnp.maximum(b_last + m_prev, jnp.max(g, axis=0, keepdims=True))
        a_prev = jnp.exp(b_last + m_prev - m_new)
        a_s = jnp.exp(g - m_new)
        ct_ref[h] = a_prev * ct + _dot_tn(kb, (v.astype(F32) * a_s).astype(BF16))
        n_ref[h] = a_prev * n_ref[h] + jnp.sum(k * a_s, axis=0, keepdims=True)
        m_ref[h:h + 1, :] = jnp.broadcast_to(m_new, (1, LANES))

    prev_ref[...] = qk_ref[...]


def mlstm_core(proj, gcol, b_gate, conv_w, g_out, batch, seq):
    n = proj.shape[0]
    inner = proj.shape[1] // 4
    H = A_HEADS
    dh = inner // H
    L = min(MLSTM_CHUNK, seq)
    nc = seq // L
    grow = gcol[:, :2 * H].T
    brow = jnp.pad(b_gate, (0, LANES - 2 * H)).reshape(1, LANES)
    bcol = b_gate.reshape(2 * H, 1)
    return pl.pallas_call(
        _mlstm_kernel,
        grid=(batch, nc),
        in_specs=[pl.BlockSpec((L, 2 * inner), lambda b, c: (b * nc + c, 0)),
                  pl.BlockSpec((L, inner), lambda b, c: (b * nc + c, 2)),
                  pl.BlockSpec((L, inner), lambda b, c: (b * nc + c, 3)),
                  pl.BlockSpec((L, LANES), lambda b, c: (b * nc + c, 0)),
                  pl.BlockSpec((2 * H, L), lambda b, c: (0, b * nc + c)),
                  pl.BlockSpec((2 * H, 1), lambda b, c: (0, 0)),
                  pl.BlockSpec((1, LANES), lambda b, c: (0, 0)),
                  pl.BlockSpec((A_CONV, 2 * inner), lambda b, c: (0, 0)),
                  pl.BlockSpec((1, inner), lambda b, c: (0, 0))],
        out_specs=pl.BlockSpec((L, inner), lambda b, c: (b * nc + c, 0)),
        out_shape=jax.ShapeDtypeStruct((n, inner), BF16),
        scratch_shapes=[pltpu.VMEM((H, dh, dh), F32),
                        pltpu.VMEM((H, 1, dh), F32),
                        pltpu.VMEM((8, LANES), F32),
                        pltpu.VMEM((L, 2 * inner), BF16)],
        compiler_params=_cparams("parallel", "arbitrary"),
        name="mlstm_core",
    )(proj, proj, proj, gcol, grow, bcol, brow, conv_w, g_out.reshape(1, inner))


def _ffn_kernel(x_ref, g_ref, wg_ref, wu_ref, wd_ref, o_ref, xn_ref, acc_ref):
    j = pl.program_id(1)

    @pl.when(j == 0)
    def _():
        xf = x_ref[...]
        xn_ref[...] = _rms(xf, g_ref[...]).astype(BF16)
        acc_ref[...] = xf

    xn = xn_ref[...]
    a = (_silu(_dot(xn, wg_ref[...])) * _dot(xn, wu_ref[...])).astype(BF16)
    acc_ref[...] += _dot(a, wd_ref[...])

    @pl.when(j == pl.num_programs(1) - 1)
    def _():
        o_ref[...] = acc_ref[...]


def ffn_dense(x, g, w_gu, w_down, *, tm=512, tf=1408):
    n, d = x.shape
    f = w_down.shape[0]
    tm = min(tm, n)
    nf = f // tf
    return pl.pallas_call(
        _ffn_kernel,
        grid=(n // tm, nf),
        in_specs=[pl.BlockSpec((tm, d), lambda i, j: (i, 0)),
                  pl.BlockSpec((1, d), lambda i, j: (0, 0)),
                  pl.BlockSpec((d, tf), lambda i, j: (0, j)),
                  pl.BlockSpec((d, tf), lambda i, j: (0, j + nf)),
                  pl.BlockSpec((tf, d), lambda i, j: (j, 0))],
        out_specs=pl.BlockSpec((tm, d), lambda i, j: (i, 0)),
        out_shape=jax.ShapeDtypeStruct((n, d), F32),
        scratch_shapes=[pltpu.VMEM((tm, d), BF16), pltpu.VMEM((tm, d), F32)],
        compiler_params=_cparams("parallel", "arbitrary"),
        name="ffn_dense",
    )(x, g.reshape(1, d), w_gu, w_gu, w_down)


def _compress_kernel(r_ref, pos_ref, w1_ref, w2_ref, o_ref):
    r = r_ref[0, 0]
    w1 = w1_ref[0]
    half = r.shape[1]
    nc = r.shape[0]
    lo = _dot(r, w1[:half])
    hi = _dot(r, w1[half:])
    hid = lo + pltpu.roll(hi, nc - 1, 0) + _dot(pos_ref[0], w1)
    o_ref[0, 0] = _dot(_silu(hid).astype(BF16), w2_ref[0]).astype(o_ref.dtype)


def nsa_compress(r, pos, w1, w2):
    b, c2, nc, half = r.shape
    g = c2 // 2
    hidden = w1.shape[2]
    dh = w2.shape[2]
    return pl.pallas_call(
        _compress_kernel,
        grid=(b, c2),
        in_specs=[pl.BlockSpec((1, 1, nc, half), lambda i, j: (i, j, 0, 0)),
                  pl.BlockSpec((1, 1, 2 * half), lambda i, j: (j // g, 0, 0)),
                  pl.BlockSpec((1, 2 * half, hidden), lambda i, j: (j // g, 0, 0)),
                  pl.BlockSpec((1, hidden, dh), lambda i, j: (j // g, 0, 0))],
        out_specs=pl.BlockSpec((1, 1, nc, dh), lambda i, j: (i, j, 0, 0)),
        out_shape=jax.ShapeDtypeStruct((b, c2, nc, dh), F32),
        compiler_params=_cparams("parallel", "parallel"),
        name="nsa_compress",
    )(r, pos, w1, w2)


NSA_TQ = 128
NSA_TK_SLC = 512
NSA_TK_WIN = 256


def _nsa_kernel(q_ref, kc_ref, vc_ref, ks_ref, vs_ref, kw_ref, vw_ref, gl_ref, bg_ref, ov_ref,
                out_ref, m_ref, l_ref, acc_ref, *, tq, tks, tkw, n_top):
    R = B_REP
    M = R * tq
    dh = B_HEAD_DIM
    t0 = pl.program_id(2) * tq
    q = q_ref[0, 0].reshape(M, q_ref.shape[-1])

    kc = kc_ref[0, 0]
    ncmp = kc.shape[0]
    s = _dot_nt(q, kc)
    tpos_c = t0 + (lax.broadcasted_iota(jnp.int32, (M, ncmp), 0) & (tq - 1))
    cend = lax.broadcasted_iota(jnp.int32, (M, ncmp), 1) * CMP_STRIDE + (CMP_LEN - 1)
    ok_c = tpos_c >= cend
    s = jnp.where(ok_c, s, NEG)
    p = jnp.where(ok_c, jnp.exp(s - jnp.max(s, axis=-1, keepdims=True)), 0.0)
    dsum = jnp.sum(p, axis=-1, keepdims=True)
    p = p / jnp.where(dsum > 0, dsum, 1.0)
    o_cmp = _dot(p.astype(BF16), vc_ref[0, 0])

    psum = p[0:tq]
    for r in range(1, R):
        psum = psum + p[r * tq:(r + 1) * tq]
    p1, p2, p3 = _split3(psum)
    ov = ov_ref[...]
    imp = _dot(p1, ov) + _dot(p2, ov) + _dot(p3, ov)
    nsel = ks_ref.shape[2] // SEL_BLOCK
    nselp = max(nsel, 8)
    imp_t = imp.T[:nselp]
    jj = lax.broadcasted_iota(jnp.int32, (nselp, tq), 0)
    tt = t0 + lax.broadcasted_iota(jnp.int32, (nselp, tq), 1)
    cur = tt // SEL_BLOCK
    forced = (jj == 0) | (jj == cur) | (jj == cur - 1)
    iv = jnp.where(forced, FORCED_SCORE, imp_t)
    iv = jnp.where(jj * SEL_BLOCK <= tt, iv, -jnp.inf)
    rank = jnp.zeros((nselp, tq), F32)
    for j2 in range(nsel):
        rv = iv[j2:j2 + 1, :]
        rank = rank + jnp.where(jj > j2, jnp.where(rv >= iv, 1.0, 0.0), jnp.where(rv > iv, 1.0, 0.0))
    sel_t = jnp.where(rank < n_top, 1.0, 0.0).astype(BF16)

    def attend(k_ref, v_ref, tk, j_lo, j_hi, mask_fn):
        m_ref[...] = jnp.full_like(m_ref, NEG)
        l_ref[...] = jnp.zeros_like(l_ref)
        acc_ref[...] = jnp.zeros_like(acc_ref)

        def body(j, carry):
            k0 = pl.multiple_of(j * tk, tk)
            kk = k_ref[0, 0, pl.ds(k0, tk), :]
            vv = v_ref[0, 0, pl.ds(k0, tk), :]
            spos = k0 + lax.broadcasted_iota(jnp.int32, (tq, tk), 1)
            tpos = t0 + lax.broadcasted_iota(jnp.int32, (tq, tk), 0)
            ok = mask_fn(k0, tpos, spos)
            sc = _dot_nt(q, kk).reshape(R, tq, tk)
            sc = jnp.where(ok[None], sc, NEG).reshape(M, tk)
            m_old = m_ref[...]
            m_new = jnp.maximum(m_old, jnp.max(sc, axis=-1, keepdims=True))
            alpha = jnp.exp(m_old - m_new)
            pp = jnp.exp(sc - m_new)
            l_ref[...] = alpha * l_ref[...] + jnp.sum(pp, axis=-1, keepdims=True)
            acc_ref[...] = alpha * acc_ref[...] + _dot(pp.astype(BF16), vv)
            m_ref[...] = m_new
            return carry

        lax.fori_loop(j_lo, j_hi, body, 0)
        return acc_ref[...] / l_ref[...]

    def slc_mask(k0, tpos, spos):
        blk = (k0 + lax.broadcasted_iota(jnp.int32, (nselp, tks), 1)) // SEL_BLOCK
        expand = jnp.where(blk == lax.broadcasted_iota(jnp.int32, (nselp, tks), 0), 1.0, 0.0).astype(BF16)
        selx = _dot_tn(sel_t, expand)
        return (selx > 0.5) & (spos <= tpos)

    o_slc = attend(ks_ref, vs_ref, tks, 0, (t0 + tq + tks - 1) // tks, slc_mask)

    def win_mask(k0, tpos, spos):
        dist = tpos - spos
        return (dist >= 0) & (dist < WINDOW)

    j_lo = jnp.maximum(t0 - (WINDOW - 1), 0) // tkw
    o_win = attend(kw_ref, vw_ref, tkw, j_lo, (t0 + tq + tkw - 1) // tkw, win_mask)

    gates = jax.nn.sigmoid(gl_ref[0, 0, pl.ds(pl.multiple_of(t0, tq), tq), :] + bg_ref[0])
    for r in range(R):
        rs = slice(r * tq, (r + 1) * tq)
        o = (gates[:, 3 * r:3 * r + 1] * o_cmp[rs] + gates[:, 3 * r + 1:3 * r + 2] * o_slc[rs]
             + gates[:, 3 * r + 2:3 * r + 3] * o_win[rs])
        out_ref[0, 0, r] = o.astype(out_ref.dtype)


def nsa_attention(qf, kc, vc, ks, vs, kw, vw, gl, bg, ov):
    b, g, r, t, dq = qf.shape
    dh = B_HEAD_DIM
    tq = min(NSA_TQ, t)
    tks = min(NSA_TK_SLC, t)
    tkw = min(NSA_TK_WIN, t)
    ncmp = kc.shape[2]
    n_top = min(SEL_TOPN, t // SEL_BLOCK)
    kv_spec = lambda d: pl.BlockSpec((1, 1, t, d), lambda i, j, k: (i, j, 0, 0))
    return pl.pallas_call(
        functools.partial(_nsa_kernel, tq=tq, tks=tks, tkw=tkw, n_top=n_top),
        grid=(b, g, t // tq),
        in_specs=[pl.BlockSpec((1, 1, r, tq, dq), lambda i, j, k: (i, j, 0, k, 0)),
                  pl.BlockSpec((1, 1, ncmp, dq), lambda i, j, k: (i, j, 0, 0)),
                  pl.BlockSpec((1, 1, ncmp, dh), lambda i, j, k: (i, j, 0, 0)),
                  kv_spec(dq), kv_spec(dh), kv_spec(dq), kv_spec(dh),
                  pl.BlockSpec((1, 1, t, 3 * r), lambda i, j, k: (i, j, 0, 0)),
                  pl.BlockSpec((1, 1, 3 * r), lambda i, j, k: (j, 0, 0)),
                  pl.BlockSpec(ov.shape, lambda i, j, k: (0, 0))],
        out_specs=pl.BlockSpec((1, 1, r, tq, dh), lambda i, j, k: (i, j, 0, k, 0)),
        out_shape=jax.ShapeDtypeStruct((b, g, r, t, dh), BF16),
        scratch_shapes=[pltpu.VMEM((r * tq, 1), F32), pltpu.VMEM((r * tq, 1), F32),
                        pltpu.VMEM((r * tq, dh), F32)],
        compiler_params=_cparams("parallel", "parallel", "arbitrary"),
        name="nsa_attention",
    )(qf, kc, vc, ks, vs, kw, vw, gl, bg, ov)


def _alibi_slopes(n):
    return np.power(2.0, -8.0 * np.arange(1, n + 1) / n).astype(np.float32)


def _np_split3(a):
    a = np.asarray(a, np.float32)
    out = []
    r = a
    for _ in range(3):
        p = r.astype(BF16).astype(np.float32)
        out.append(p)
        r = (r - p).astype(np.float32)
    return out


def _pos_features(pos, width):
    hi = (pos // 64).astype(np.float32)
    lo = (pos % 64).astype(np.float32)
    f = np.zeros((pos.shape[0], width), np.float32)
    f[:, 0:3] = hi[:, None]
    f[:, 3:6] = lo[:, None]
    return f


def _slope_features(width):
    s1, s2, s3 = _np_split3(_alibi_slopes(B_HEADS))
    f = np.zeros((B_HEADS, width), np.float32)
    for i, s in enumerate((s1, s2, s3)):
        f[:, i] = 64.0 * s
        f[:, 3 + i] = s
    return f


def _router_kernel(x_ref, g_ref, w_ref, o_ref):
    logits = _dot_f32(_rms(x_ref[...], g_ref[...]), w_ref[...])
    lane = lax.broadcasted_iota(jnp.int32, logits.shape, 1)
    logits = jnp.where(lane < N_EXPERTS, logits, -jnp.inf)
    m1 = jnp.max(logits, axis=-1, keepdims=True)
    i1 = jnp.min(jnp.where(logits == m1, lane, LANES), axis=-1, keepdims=True)
    rest = jnp.where(lane == i1, -jnp.inf, logits)
    m2 = jnp.max(rest, axis=-1, keepdims=True)
    i2 = jnp.min(jnp.where(rest == m2, lane, LANES), axis=-1, keepdims=True)
    e2 = jnp.exp(m2 - m1)
    w1 = 1.0 / (1.0 + e2)
    o_ref[...] = jnp.where(lane == i1, w1, 0.0) + jnp.where(lane == i2, e2 * w1, 0.0)


def moe_router(x, g, w_router, *, tm=1024):
    n, d = x.shape
    tm = min(tm, n)
    w = jnp.pad(w_router, ((0, 0), (0, LANES - w_router.shape[1])))
    return pl.pallas_call(
        _router_kernel,
        grid=(n // tm,),
        in_specs=[pl.BlockSpec((tm, d), lambda i: (i, 0)),
                  pl.BlockSpec((1, d), lambda i: (0, 0)),
                  pl.BlockSpec((d, LANES), lambda i: (0, 0))],
        out_specs=pl.BlockSpec((tm, LANES), lambda i: (i, 0)),
        out_shape=jax.ShapeDtypeStruct((n, LANES), F32),
        compiler_params=_cparams("parallel"),
        name="moe_router",
    )(x, g.reshape(1, d), w)


def _moe_kernel(x_ref, g_ref, gate_ref, wg_ref, wu_ref, wd_ref, gf_ref, o_ref, xn_ref, acc_ref):
    e = pl.program_id(1)
    j = pl.program_id(2)

    @pl.when((e == 0) & (j == 0))
    def _():
        xf = x_ref[...]
        xn_ref[...] = _rms(xf, g_ref[...]).astype(BF16)
        acc_ref[...] = xf

    gate = gate_ref[...]
    lane = lax.broadcasted_iota(jnp.int32, gate.shape, 1)
    gcol = jnp.sum(jnp.where(lane == e, gate, 0.0), axis=-1, keepdims=True)
    xn = xn_ref[...]
    a = (_silu(_dot(xn, wg_ref[0])) * _dot(xn, wu_ref[0]) * gcol).astype(BF16)
    acc_ref[...] += _dot(a, wd_ref[0])

    @pl.when((e == pl.num_programs(1) - 1) & (j == pl.num_programs(2) - 1))
    def _():
        o_ref[...] = _rms(acc_ref[...], gf_ref[...])


def moe_dense(x, g, gate, w_gu, w_down, g_final, *, tm=512, tf=1792):
    n, d = x.shape
    ne, f, _ = w_down.shape
    tm = min(tm, n)
    nf = f // tf
    return pl.pallas_call(
        _moe_kernel,
        grid=(n // tm, ne, nf),
        in_specs=[pl.BlockSpec((tm, d), lambda i, e, j: (i, 0)),
                  pl.BlockSpec((1, d), lambda i, e, j: (0, 0)),
                  pl.BlockSpec((tm, LANES), lambda i, e, j: (i, 0)),
                  pl.BlockSpec((1, d, tf), lambda i, e, j: (e, 0, j)),
                  pl.BlockSpec((1, d, tf), lambda i, e, j: (e, 0, j + nf)),
                  pl.BlockSpec((1, tf, d), lambda i, e, j: (e, j, 0)),
                  pl.BlockSpec((1, d), lambda i, e, j: (0, 0))],
        out_specs=pl.BlockSpec((tm, d), lambda i, e, j: (i, 0)),
        out_shape=jax.ShapeDtypeStruct((n, d), F32),
        scratch_shapes=[pltpu.VMEM((tm, d), BF16), pltpu.VMEM((tm, d), F32)],
        compiler_params=_cparams("parallel", "arbitrary", "arbitrary"),
        name="moe_experts",
    )(x, g.reshape(1, d), gate, w_gu, w_gu, w_down, g_final.reshape(1, d))


def kernel(x, norm_mix, norm_ffn, a_w_in, a_b_gate, a_conv, a_norm_h, a_w_out, norm_kv, b_w_kv,
           b_cmp_pos, b_cmp_w1, b_cmp_w2, b_w_q, b_b_gate, b_w_out, f_w_gu, f_w_down,
           m_router, m_w_gu, m_w_down, norm_final):
    B, T, D = x.shape
    N = B * T
    G, R, dh = B_KV_GROUPS, B_REP, B_HEAD_DIM
    xs = x.reshape(N, D)

    inner4 = a_w_in.shape[2] - 2 * A_HEADS
    w_in = a_w_in[0]
    proj = norm_matmul(xs, norm_mix[0], w_in[:, :inner4].astype(BF16), out_dtype=BF16)
    w_gate = jnp.pad(w_in[:, inner4:], ((0, 0), (0, LANES - 2 * A_HEADS)))
    gcol = norm_matmul(xs, norm_mix[0], w_gate, out_dtype=F32, exact=True, tn=LANES)
    hs = mlstm_core(proj, gcol, a_b_gate[0], a_conv[0], a_norm_h[0], B, T)
    xs = matmul_residual(hs, a_w_out[0].astype(BF16), xs)
    xs = ffn_dense(xs, norm_ffn[0], f_w_gu[0].astype(BF16), f_w_down[0].astype(BF16))

    hd = B_HEADS * dh
    kvp = norm_matmul(xs, norm_kv, b_w_kv.astype(BF16), out_dtype=BF16, tn=b_w_kv.shape[1] // 2)
    w_q = b_w_q[0]
    qp = norm_matmul(xs, norm_mix[1], (w_q[:, :hd] * (dh ** -0.5)).astype(BF16), out_dtype=BF16)
    w_qg = jnp.pad(w_q[:, hd:], ((0, 0), (0, LANES - 3 * B_HEADS)))
    gl = norm_matmul(xs, norm_mix[1], w_qg, out_dtype=F32, exact=True, tn=LANES)

    kvt = kvp.reshape(B, T, 6 * G, dh).transpose(0, 2, 1, 3)
    ncmp = T // CMP_STRIDE
    pos = b_cmp_pos.transpose(1, 0, 2).reshape(2, 1, CMP_LEN * dh).astype(BF16)
    kvc = nsa_compress(kvt[:, :2 * G].reshape(B, 2 * G, ncmp, CMP_STRIDE * dh), pos,
                       b_cmp_w1.astype(BF16), b_cmp_w2.astype(BF16))

    feat_w = LANES - dh
    key_feat = jnp.asarray(_pos_features(np.arange(T), feat_w), BF16)
    cmp_feat = jnp.asarray(_pos_features(np.arange(ncmp) * CMP_STRIDE + CMP_LEN - 1, feat_w), BF16)
    q_feat = jnp.asarray(_slope_features(feat_w), BF16).reshape(G, R, 1, feat_w)

    def with_feat(a, f):
        return jnp.concatenate([a, jnp.broadcast_to(f, a.shape[:-1] + (feat_w,))], axis=-1)

    qf = with_feat(qp.reshape(B, T, G, R, dh).transpose(0, 2, 3, 1, 4), q_feat)
    kc = with_feat(kvc[:, :G].astype(BF16), cmp_feat)
    vc = kvc[:, G:].astype(BF16)
    ks = with_feat(kvt[:, 2 * G:3 * G], key_feat)
    vs = kvt[:, 3 * G:4 * G]
    kw = with_feat(kvt[:, 4 * G:5 * G], key_feat)
    vw = kvt[:, 5 * G:6 * G]
    glt = gl[:, :3 * B_HEADS].reshape(B, T, G, 3 * R).transpose(0, 2, 1, 3)
    bg = b_b_gate[0].reshape(G, 1, 3 * R)

    nsel = T // SEL_BLOCK
    ci = np.arange(ncmp)[:, None] * CMP_STRIDE
    sj = np.arange(LANES)[None, :] * SEL_BLOCK
    ov = ((ci < sj + SEL_BLOCK) & (ci + CMP_LEN > sj) & (np.arange(LANES)[None, :] < nsel)
          & (np.arange(ncmp)[:, None] < ncmp - 1))
    ov = jnp.asarray(ov.astype(np.float32), BF16)

    oa = nsa_attention(qf, kc, vc, ks, vs, kw, vw, glt, bg, ov)
    oa = oa.transpose(0, 3, 1, 2, 4).reshape(N, hd)
    xs = matmul_residual(oa, b_w_out[0].astype(BF16), xs)

    gate = moe_router(xs, norm_ffn[1], m_router[0])
    out = moe_dense(xs, norm_ffn[1], gate, m_w_gu[0].astype(BF16), m_w_down[0].astype(BF16), norm_final)
    return out.reshape(B, T, D)
```

```python
import functools

import numpy as np
import jax
import jax.numpy as jnp
from jax import lax
from jax.experimental import pallas as pl
from jax.experimental.pallas import tpu as pltpu

F32 = jnp.float32
BF16 = jnp.bfloat16

RMS_EPS = 1e-6
A_HEADS = 4
A_CONV = 4
B_HEADS = 16
B_KV_GROUPS = 4
B_REP = B_HEADS // B_KV_GROUPS
B_HEAD_DIM = 64
CMP_LEN = 32
CMP_STRIDE = 16
SEL_BLOCK = 64
SEL_TOPN = 16
WINDOW = 512
FORCED_SCORE = 1e4
N_EXPERTS = 8

LANES = 128
V7X_VMEM_BYTES = 64 * 1024 * 1024
VMEM_LIMIT = V7X_VMEM_BYTES - 8 * 1024 * 1024
NEG = -1e30


def _cparams(*sem):
    return pltpu.CompilerParams(dimension_semantics=sem, vmem_limit_bytes=VMEM_LIMIT)


def _dot(a, b):
    return jnp.dot(a, b, preferred_element_type=F32)


def _dot_nt(a, b):
    return lax.dot_general(a, b, (((1,), (1,)), ((), ())), preferred_element_type=F32)


def _dot_tn(a, b):
    return lax.dot_general(a, b, (((0,), (0,)), ((), ())), preferred_element_type=F32)


def _split3(a):
    a1 = a.astype(BF16)
    r1 = a - a1.astype(F32)
    a2 = r1.astype(BF16)
    a3 = (r1 - a2.astype(F32)).astype(BF16)
    return a1, a2, a3


def _dot_f32(a, b):
    a1, a2, a3 = _split3(a)
    b1, b2, b3 = _split3(b)
    return (_dot(a1, b1) + _dot(a1, b2) + _dot(a2, b1)
            + _dot(a2, b2) + _dot(a1, b3) + _dot(a3, b1))


def _dot_f32_nt(a, b):
    a1, a2, a3 = _split3(a)
    b1, b2, b3 = _split3(b)
    return (_dot_nt(a1, b1) + _dot_nt(a1, b2) + _dot_nt(a2, b1)
            + _dot_nt(a2, b2) + _dot_nt(a1, b3) + _dot_nt(a3, b1))


def _rms(xf, g):
    return xf * lax.rsqrt(jnp.mean(xf * xf, axis=-1, keepdims=True) + RMS_EPS) * g


def _silu(x):
    return x * jax.nn.sigmoid(x)


def _log_sigmoid(x):
    return jnp.minimum(x, 0.0) - jnp.log(1.0 + jnp.exp(-jnp.abs(x)))


def _norm_matmul_kernel(x_ref, g_ref, w_ref, o_ref, xn_ref, *, exact):
    @pl.when(pl.program_id(1) == 0)
    def _():
        xn_ref[...] = _rms(x_ref[...], g_ref[...]).astype(xn_ref.dtype)

    if exact:
        o_ref[...] = _dot_f32(xn_ref[...], w_ref[...]).astype(o_ref.dtype)
    else:
        o_ref[...] = _dot(xn_ref[...], w_ref[...]).astype(o_ref.dtype)


def norm_matmul(x, g, w, *, out_dtype, exact=False, tm=1024, tn=1024):
    n, d = x.shape
    dout = w.shape[1]
    tm = min(tm, n)
    tn = min(tn, dout)
    return pl.pallas_call(
        functools.partial(_norm_matmul_kernel, exact=exact),
        grid=(n // tm, dout // tn),
        in_specs=[pl.BlockSpec((tm, d), lambda i, j: (i, 0)),
                  pl.BlockSpec((1, d), lambda i, j: (0, 0)),
                  pl.BlockSpec((d, tn), lambda i, j: (0, j))],
        out_specs=pl.BlockSpec((tm, tn), lambda i, j: (i, j)),
        out_shape=jax.ShapeDtypeStruct((n, dout), out_dtype),
        scratch_shapes=[pltpu.VMEM((tm, d), F32 if exact else BF16)],
        compiler_params=_cparams("parallel", "arbitrary"),
        name="norm_matmul_f32" if exact else "norm_matmul",
    )(x, g.reshape(1, d), w)


def _matmul_res_kernel(a_ref, w_ref, r_ref, o_ref):
    o_ref[...] = r_ref[...] + _dot(a_ref[...], w_ref[...])


def matmul_residual(a, w, res, *, tm=1024):
    n, k = a.shape
    dout = w.shape[1]
    tm = min(tm, n)
    return pl.pallas_call(
        _matmul_res_kernel,
        grid=(n // tm,),
        in_specs=[pl.BlockSpec((tm, k), lambda i: (i, 0)),
                  pl.BlockSpec((k, dout), lambda i: (0, 0)),
                  pl.BlockSpec((tm, dout), lambda i: (i, 0))],
        out_specs=pl.BlockSpec((tm, dout), lambda i: (i, 0)),
        out_shape=jax.ShapeDtypeStruct((n, dout), F32),
        compiler_params=_cparams("parallel"),
        name="matmul_residual",
    )(a, w, res)


MLSTM_CHUNK = 256


def _mlstm_kernel(qk_ref, v_ref, o_ref, gcol_ref, grow_ref, bcol_ref, brow_ref, convw_ref, gout_ref,
                  out_ref, ct_ref, n_ref, m_ref, prev_ref):
    L = qk_ref.shape[0]
    H = A_HEADS
    inner = v_ref.shape[1]
    dh = inner // H

    @pl.when(pl.program_id(1) == 0)
    def _():
        ct_ref[...] = jnp.zeros_like(ct_ref)
        n_ref[...] = jnp.zeros_like(n_ref)
        m_ref[...] = jnp.zeros_like(m_ref)
        prev_ref[...] = jnp.zeros_like(prev_ref)

    row = lax.broadcasted_iota(jnp.int32, (L, L), 0)
    col = lax.broadcasted_iota(jnp.int32, (L, L), 1)
    causal = col <= row
    tril = jnp.where(causal, 1.0, 0.0).astype(BF16)
    triu = jnp.where(row <= col, 1.0, 0.0).astype(BF16)

    gc = gcol_ref[...] + brow_ref[...]
    gr = grow_ref[...] + bcol_ref[...]
    lfc1, lfc2, lfc3 = _split3(_log_sigmoid(gc))
    lfr1, lfr2, lfr3 = _split3(_log_sigmoid(gr))
    b_c = _dot(tril, lfc1) + _dot(tril, lfc2) + _dot(tril, lfc3)
    b_r = _dot(lfr1, triu) + _dot(lfr2, triu) + _dot(lfr3, triu)

    rowi = lax.broadcasted_iota(jnp.int32, (L, dh), 0)

    def conv(cur, prev, w):
        y = cur * w[A_CONV - 1:A_CONV, :]
        for s in range(1, A_CONV):
            sh = jnp.where(rowi < s, pltpu.roll(prev, s, 0), pltpu.roll(cur, s, 0))
            y = y + sh * w[A_CONV - 1 - s:A_CONV - s, :]
        return y

    for h in range(H):
        hs = slice(h * dh, (h + 1) * dh)
        ks = slice(inner + h * dh, inner + (h + 1) * dh)
        q = conv(qk_ref[:, hs].astype(F32), prev_ref[:, hs].astype(F32), convw_ref[:, hs])
        k = conv(qk_ref[:, ks].astype(F32), prev_ref[:, ks].astype(F32), convw_ref[:, ks]) * (dh ** -0.5)
        v = v_ref[:, hs]
        qb = q.astype(BF16)
        kb = k.astype(BF16)

        li_c = gc[:, h:h + 1]
        bc = b_c[:, H + h:H + h + 1]
        li_r = gr[h:h + 1, :]
        br = b_r[H + h:H + h + 1, :]
        m_prev = m_ref[h:h + 1, 0:1]

        d = jnp.where(causal, bc - br + li_r, -jnp.inf)
        inter = bc + m_prev
        m_t = jnp.maximum(inter, jnp.max(d, axis=-1, keepdims=True))
        w_inter = jnp.exp(inter - m_t)
        s = _dot_nt(qb, kb) * jnp.exp(d - m_t)
        ct = ct_ref[h]
        num = _dot(s.astype(BF16), v) + w_inter * _dot(qb, ct.astype(BF16))
        den = jnp.sum(s, axis=-1, keepdims=True) + w_inter * jnp.sum(q * n_ref[h], axis=-1, keepdims=True)
        hh = num / jnp.maximum(jnp.abs(den), jnp.exp(-m_t))
        hh = hh * lax.rsqrt(jnp.mean(hh * hh, axis=-1, keepdims=True) + RMS_EPS)
        out_ref[:, hs] = (hh * gout_ref[:, hs] * jax.nn.sigmoid(o_ref[:, hs].astype(F32))).astype(out_ref.dtype)

        b_last = bc[L - 1:L, :]
        g = b_last - bc + li_c
        m_new = jnp.maximum(b_last + m_prev, jnp.max(g, axis=0, keepdims=True))
        a_prev = jnp.exp(b_last + m_prev - m_new)
        a_s = jnp.exp(g - m_new)
        ct_ref[h] = a_prev * ct + _dot_tn(kb, (v.astype(F32) * a_s).astype(BF16))
        n_ref[h] = a_prev * n_ref[h] + jnp.sum(k * a_s, axis=0, keepdims=True)
        m_ref[h:h + 1, :] = jnp.broadcast_to(m_new, (1, LANES))

    prev_ref[...] = qk_ref[...]


def mlstm_core(proj, gcol, b_gate, conv_w, g_out, batch, seq):
    n = proj.shape[0]
    inner = proj.shape[1] // 4
    H = A_HEADS
    dh = inner // H
    L = min(MLSTM_CHUNK, seq)
    nc = seq // L
    grow = gcol[:, :2 * H].T
    brow = jnp.pad(b_gate, (0, LANES - 2 * H)).reshape(1, LANES)
    bcol = b_gate.reshape(2 * H, 1)
    return pl.pallas_call(
        _mlstm_kernel,
        grid=(batch, nc),
        in_specs=[pl.BlockSpec((L, 2 * inner), lambda b, c: (b * nc + c, 0)),
                  pl.BlockSpec((L, inner), lambda b, c: (b * nc + c, 2)),
                  pl.BlockSpec((L, inner), lambda b, c: (b * nc + c, 3)),
                  pl.BlockSpec((L, LANES), lambda b, c: (b * nc + c, 0)),
                  pl.BlockSpec((2 * H, L), lambda b, c: (0, b * nc + c)),
                  pl.BlockSpec((2 * H, 1), lambda b, c: (0, 0)),
                  pl.BlockSpec((1, LANES), lambda b, c: (0, 0)),
                  pl.BlockSpec((A_CONV, 2 * inner), lambda b, c: (0, 0)),
                  pl.BlockSpec((1, inner), lambda b, c: (0, 0))],
        out_specs=pl.BlockSpec((L, inner), lambda b, c: (b * nc + c, 0)),
        out_shape=jax.ShapeDtypeStruct((n, inner), BF16),
        scratch_shapes=[pltpu.VMEM((H, dh, dh), F32),
                        pltpu.VMEM((H, 1, dh), F32),
                        pltpu.VMEM((8, LANES), F32),
                        pltpu.VMEM((L, 2 * inner), BF16)],
        compiler_params=_cparams("parallel", "arbitrary"),
        name="mlstm_core",
    )(proj, proj, proj, gcol, grow, bcol, brow, conv_w, g_out.reshape(1, inner))


def _ffn_kernel(x_ref, g_ref, wg_ref, wu_ref, wd_ref, o_ref, xn_ref, acc_ref):
    j = pl.program_id(1)

    @pl.when(j == 0)
    def _():
        xf = x_ref[...]
        xn_ref[...] = _rms(xf, g_ref[...]).astype(BF16)
        acc_ref[...] = xf

    xn = xn_ref[...]
    a = (_silu(_dot(xn, wg_ref[...])) * _dot(xn, wu_ref[...])).astype(BF16)
    acc_ref[...] += _dot(a, wd_ref[...])

    @pl.when(j == pl.num_programs(1) - 1)
    def _():
        o_ref[...] = acc_ref[...]


def ffn_dense(x, g, w_gu, w_down, *, tm=512, tf=1408):
    n, d = x.shape
    f = w_down.shape[0]
    tm = min(tm, n)
    nf = f // tf
    return pl.pallas_call(
        _ffn_kernel,
        grid=(n // tm, nf),
        in_specs=[pl.BlockSpec((tm, d), lambda i, j: (i, 0)),
                  pl.BlockSpec((1, d), lambda i, j: (0, 0)),
                  pl.BlockSpec((d, tf), lambda i, j: (0, j)),
                  pl.BlockSpec((d, tf), lambda i, j: (0, j + nf)),
                  pl.BlockSpec((tf, d), lambda i, j: (j, 0))],
        out_specs=pl.BlockSpec((tm, d), lambda i, j: (i, 0)),
        out_shape=jax.ShapeDtypeStruct((n, d), F32),
        scratch_shapes=[pltpu.VMEM((tm, d), BF16), pltpu.VMEM((tm, d), F32)],
        compiler_params=_cparams("parallel", "arbitrary"),
        name="ffn_dense",
    )(x, g.reshape(1, d), w_gu, w_gu, w_down)


def _compress_kernel(r_ref, pos_ref, w1_ref, w2_ref, o_ref):
    r = r_ref[0, 0]
    w1 = w1_ref[0]
    half = r.shape[1]
    nc = r.shape[0]
    lo = _dot(r, w1[:half])
    hi = _dot(r, w1[half:])
    hid = lo + pltpu.roll(hi, nc - 1, 0) + _dot(pos_ref[0], w1)
    o_ref[0, 0] = _dot(_silu(hid).astype(BF16), w2_ref[0]).astype(o_ref.dtype)


def nsa_compress(r, pos, w1, w2):
    b, c2, nc, half = r.shape
    g = c2 // 2
    hidden = w1.shape[2]
    dh = w2.shape[2]
    return pl.pallas_call(
        _compress_kernel,
        grid=(b, c2),
        in_specs=[pl.BlockSpec((1, 1, nc, half), lambda i, j: (i, j, 0, 0)),
                  pl.BlockSpec((1, 1, 2 * half), lambda i, j: (j // g, 0, 0)),
                  pl.BlockSpec((1, 2 * half, hidden), lambda i, j: (j // g, 0, 0)),
                  pl.BlockSpec((1, hidden, dh), lambda i, j: (j // g, 0, 0))],
        out_specs=pl.BlockSpec((1, 1, nc, dh), lambda i, j: (i, j, 0, 0)),
        out_shape=jax.ShapeDtypeStruct((b, c2, nc, dh), F32),
        compiler_params=_cparams("parallel", "parallel"),
        name="nsa_compress",
    )(r, pos, w1, w2)


NSA_TQ = 128
NSA_TK_SLC = 512
NSA_TK_WIN = 256
NSA_ROWS = 32


def _nsa_kernel(q_ref, kc_ref, vc_ref, ks_ref, vs_ref, kw_ref, vw_ref, gl_ref, bg_ref, ov_ref,
                out_ref, m_ref, alpha_ref, acc_ref, ss_ref, ps_ref, bs_ref, sw_ref, pw_ref, bw_ref,
                *, tq, tks, tkw, n_top):
    R = B_REP
    M = R * tq
    dh = B_HEAD_DIM
    t0 = pl.program_id(2) * tq
    q = q_ref[0, 0].reshape(M, q_ref.shape[-1])

    kc = kc_ref[0, 0]
    ncmp = kc.shape[0]
    s = _dot_nt(q, kc)
    tpos_c = t0 + (lax.broadcasted_iota(jnp.int32, (M, ncmp), 0) & (tq - 1))
    cend = lax.broadcasted_iota(jnp.int32, (M, ncmp), 1) * CMP_STRIDE + (CMP_LEN - 1)
    ok_c = tpos_c >= cend
    s = jnp.where(ok_c, s, NEG)
    p = jnp.where(ok_c, jnp.exp(s - jnp.max(s, axis=-1, keepdims=True)), 0.0)
    dsum = jnp.sum(p, axis=-1, keepdims=True)
    p = p / jnp.where(dsum > 0, dsum, 1.0)
    o_cmp = _dot(p.astype(BF16), vc_ref[0, 0])

    psum = p[0:tq]
    for r in range(1, R):
        psum = psum + p[r * tq:(r + 1) * tq]
    p1, p2, p3 = _split3(psum)
    ov = ov_ref[...]
    imp = _dot(p1, ov) + _dot(p2, ov) + _dot(p3, ov)
    nsel = ks_ref.shape[2] // SEL_BLOCK
    nselp = max(nsel, 8)
    imp_t = imp.T[:nselp]
    jj = lax.broadcasted_iota(jnp.int32, (nselp, tq), 0)
    tt = t0 + lax.broadcasted_iota(jnp.int32, (nselp, tq), 1)
    cur = tt // SEL_BLOCK
    forced = (jj == 0) | (jj == cur) | (jj == cur - 1)
    iv = jnp.where(forced, FORCED_SCORE, imp_t)
    iv = jnp.where(jj * SEL_BLOCK <= tt, iv, -jnp.inf)
    rank = jnp.zeros((nselp, tq), F32)
    for j2 in range(nsel):
        rv = iv[j2:j2 + 1, :]
        rank = rank + jnp.where(jj > j2, jnp.where(rv >= iv, 1.0, 0.0), jnp.where(rv > iv, 1.0, 0.0))
    sel_t = jnp.where(rank < n_top, 1.0, 0.0).astype(BF16)

    def attend(k_ref, v_ref, s_ref, p_ref, bias_ref, tk, j_lo, j_hi, bias_fn):
        m_ref[...] = jnp.full_like(m_ref, NEG)
        acc_ref[...] = jnp.zeros_like(acc_ref)

        def body(j, carry):
            k0 = pl.multiple_of(j * tk, tk)
            kk = k_ref[0, 0, pl.ds(k0, tk), :]
            vv = v_ref[0, 0, pl.ds(k0, tk), :]
            bias_ref[...] = bias_fn(k0)
            s_ref[...] = _dot_nt(q, kk)
            for r in range(R):
                for c0 in range(0, tq, NSA_ROWS):
                    rows = slice(r * tq + c0, r * tq + c0 + NSA_ROWS)
                    sc = s_ref[rows, :] + bias_ref[c0:c0 + NSA_ROWS, :]
                    m_old = m_ref[rows, :]
                    m_new = jnp.maximum(m_old, jnp.max(sc, axis=-1, keepdims=True))
                    alpha_ref[rows, :] = jnp.exp(m_old - m_new)
                    m_ref[rows, :] = m_new
                    p_ref[rows, :] = jnp.exp(sc - jnp.concatenate([m_new] * (tk // LANES), axis=1)).astype(BF16)
            acc_ref[...] = alpha_ref[...] * acc_ref[...] + _dot(p_ref[...], vv)
            return carry

        lax.fori_loop(j_lo, j_hi, body, 0)
        acc = acc_ref[...]
        return acc[:, :dh] / acc[:, dh:dh + 1]

    def slc_bias(k0):
        blk = (k0 + lax.broadcasted_iota(jnp.int32, (nselp, tks), 1)) // SEL_BLOCK
        expand = jnp.where(blk == lax.broadcasted_iota(jnp.int32, (nselp, tks), 0), 1.0, 0.0).astype(BF16)
        selx = _dot_tn(sel_t, expand)
        spos = k0 + lax.broadcasted_iota(jnp.int32, (tq, tks), 1)
        tpos = t0 + lax.broadcasted_iota(jnp.int32, (tq, tks), 0)
        return jnp.where((selx > 0.5) & (spos <= tpos), 0.0, NEG)

    o_slc = attend(ks_ref, vs_ref, ss_ref, ps_ref, bs_ref, tks, 0, (t0 + tq + tks - 1) // tks, slc_bias)

    def win_bias(k0):
        spos = k0 + lax.broadcasted_iota(jnp.int32, (tq, tkw), 1)
        tpos = t0 + lax.broadcasted_iota(jnp.int32, (tq, tkw), 0)
        dist = tpos - spos
        return jnp.where((dist >= 0) & (dist < WINDOW), 0.0, NEG)

    j_lo = jnp.maximum(t0 - (WINDOW - 1), 0) // tkw
    o_win = attend(kw_ref, vw_ref, sw_ref, pw_ref, bw_ref, tkw, j_lo, (t0 + tq + tkw - 1) // tkw, win_bias)

    gates = jax.nn.sigmoid(gl_ref[0, 0, pl.ds(pl.multiple_of(t0, tq), tq), :] + bg_ref[0])
    for r in range(R):
        rs = slice(r * tq, (r + 1) * tq)
        o = (gates[:, 3 * r:3 * r + 1] * o_cmp[rs] + gates[:, 3 * r + 1:3 * r + 2] * o_slc[rs]
             + gates[:, 3 * r + 2:3 * r + 3] * o_win[rs])
        out_ref[0, 0, r] = o.astype(out_ref.dtype)


def nsa_attention(qf, kc, vc, ks, vs, kw, vw, gl, bg, ov):
    b, g, r, t, dq = qf.shape
    dh = B_HEAD_DIM
    tq = min(NSA_TQ, t)
    tks = min(NSA_TK_SLC, t)
    tkw = min(NSA_TK_WIN, t)
    ncmp = kc.shape[2]
    n_top = min(SEL_TOPN, t // SEL_BLOCK)
    kv_spec = lambda d: pl.BlockSpec((1, 1, t, d), lambda i, j, k: (i, j, 0, 0))
    return pl.pallas_call(
        functools.partial(_nsa_kernel, tq=tq, tks=tks, tkw=tkw, n_top=n_top),
        grid=(b, g, t // tq),
        in_specs=[pl.BlockSpec((1, 1, r, tq, dq), lambda i, j, k: (i, j, 0, k, 0)),
                  pl.BlockSpec((1, 1, ncmp, dq), lambda i, j, k: (i, j, 0, 0)),
                  pl.BlockSpec((1, 1, ncmp, dh), lambda i, j, k: (i, j, 0, 0)),
                  kv_spec(dq), kv_spec(dq), kv_spec(dq), kv_spec(dq),
                  pl.BlockSpec((1, 1, t, 3 * r), lambda i, j, k: (i, j, 0, 0)),
                  pl.BlockSpec((1, 1, 3 * r), lambda i, j, k: (j, 0, 0)),
                  pl.BlockSpec(ov.shape, lambda i, j, k: (0, 0))],
        out_specs=pl.BlockSpec((1, 1, r, tq, dh), lambda i, j, k: (i, j, 0, k, 0)),
        out_shape=jax.ShapeDtypeStruct((b, g, r, t, dh), BF16),
        scratch_shapes=[pltpu.VMEM((r * tq, LANES), F32),
                        pltpu.VMEM((r * tq, LANES), F32),
                        pltpu.VMEM((r * tq, LANES), F32),
                        pltpu.VMEM((r * tq, tks), F32), pltpu.VMEM((r * tq, tks), BF16),
                        pltpu.VMEM((tq, tks), F32),
                        pltpu.VMEM((r * tq, tkw), F32), pltpu.VMEM((r * tq, tkw), BF16),
                        pltpu.VMEM((tq, tkw), F32)],
        compiler_params=_cparams("parallel", "parallel", "arbitrary"),
        name="nsa_attention",
    )(qf, kc, vc, ks, vs, kw, vw, gl, bg, ov)


def _alibi_slopes(n):
    return np.power(2.0, -8.0 * np.arange(1, n + 1) / n).astype(np.float32)


def _np_split3(a):
    a = np.asarray(a, np.float32)
    out = []
    r = a
    for _ in range(3):
        p = r.astype(BF16).astype(np.float32)
        out.append(p)
        r = (r - p).astype(np.float32)
    return out


def _pos_features(pos, width):
    hi = (pos // 64).astype(np.float32)
    lo = (pos % 64).astype(np.float32)
    f = np.zeros((pos.shape[0], width), np.float32)
    f[:, 0:3] = hi[:, None]
    f[:, 3:6] = lo[:, None]
    return f


def _slope_features(width):
    s1, s2, s3 = _np_split3(_alibi_slopes(B_HEADS))
    f = np.zeros((B_HEADS, width), np.float32)
    for i, s in enumerate((s1, s2, s3)):
        f[:, i] = 64.0 * s
        f[:, 3 + i] = s
    return f


META_I1, META_I2, META_W1, META_W2 = 0, 1, 2, 3


def _router_kernel(x_ref, g_ref, w_ref, xn_ref, meta_ref, pos_ref, cnt_ref, run_ref):
    @pl.when(pl.program_id(0) == 0)
    def _():
        run_ref[...] = jnp.zeros_like(run_ref)

    xn = _rms(x_ref[...], g_ref[...])
    xn_ref[...] = xn
    logits = _dot_f32(xn, w_ref[...])
    tm = logits.shape[0]
    lane = lax.broadcasted_iota(jnp.int32, logits.shape, 1)
    logits = jnp.where(lane < N_EXPERTS, logits, -jnp.inf)
    m1 = jnp.max(logits, axis=-1, keepdims=True)
    i1 = jnp.min(jnp.where(logits == m1, lane, LANES), axis=-1, keepdims=True)
    rest = jnp.where(lane == i1, -jnp.inf, logits)
    m2 = jnp.max(rest, axis=-1, keepdims=True)
    i2 = jnp.min(jnp.where(rest == m2, lane, LANES), axis=-1, keepdims=True)
    e2 = jnp.exp(m2 - m1)
    w1 = 1.0 / (1.0 + e2)
    meta = jnp.where(lane == META_I1, i1.astype(F32), 0.0)
    meta = jnp.where(lane == META_I2, i2.astype(F32), meta)
    meta = jnp.where(lane == META_W1, w1, meta)
    meta_ref[...] = jnp.where(lane == META_W2, e2 * w1, meta)

    sel = jnp.where((lane == i1) | (lane == i2), 1.0, 0.0)
    row = lax.broadcasted_iota(jnp.int32, (tm, tm), 0)
    col = lax.broadcasted_iota(jnp.int32, (tm, tm), 1)
    before = jnp.where(col < row, 1.0, 0.0).astype(BF16)
    run = run_ref[...]
    pos_ref[...] = run + _dot(before, sel.astype(BF16))
    run = run + jnp.sum(sel, axis=0, keepdims=True)
    run_ref[...] = run
    cnt_ref[...] = run


def moe_router(x, g, w_router, *, tm=512):
    n, d = x.shape
    tm = min(tm, n)
    w = jnp.pad(w_router, ((0, 0), (0, LANES - w_router.shape[1])))
    row_spec = pl.BlockSpec((tm, LANES), lambda i: (i, 0))
    return pl.pallas_call(
        _router_kernel,
        grid=(n // tm,),
        in_specs=[pl.BlockSpec((tm, d), lambda i: (i, 0)),
                  pl.BlockSpec((1, d), lambda i: (0, 0)),
                  pl.BlockSpec((d, LANES), lambda i: (0, 0))],
        out_specs=[pl.BlockSpec((tm, d), lambda i: (i, 0)), row_spec, row_spec,
                   pl.BlockSpec((1, LANES), lambda i: (0, 0))],
        out_shape=[jax.ShapeDtypeStruct((n, d), F32), jax.ShapeDtypeStruct((n, LANES), F32),
                   jax.ShapeDtypeStruct((n, LANES), F32), jax.ShapeDtypeStruct((1, LANES), F32)],
        scratch_shapes=[pltpu.VMEM((1, LANES), F32)],
        compiler_params=_cparams("arbitrary"),
        name="moe_router",
    )(x, g.reshape(1, d), w)


MOE_TILE = 512


def _gather_rows_kernel(idx_ref, src_ref, o_ref, sem):
    rows = o_ref.shape[0]

    def row_copy(r):
        return pltpu.make_async_copy(src_ref.at[pl.ds(idx_ref[0, 0, r], 1), :], o_ref.at[pl.ds(r, 1), :], sem)

    def start(r, c):
        row_copy(r).start()
        return c

    def wait(r, c):
        row_copy(r).wait()
        return c

    lax.fori_loop(0, rows, start, 0)
    lax.fori_loop(0, rows, wait, 0)


def gather_rows(src, idx, *, tg=MOE_TILE):
    p = idx.shape[0]
    d = src.shape[1]
    return pl.pallas_call(
        _gather_rows_kernel,
        grid=(p // tg,),
        in_specs=[pl.BlockSpec((1, 1, tg), lambda i: (i, 0, 0), memory_space=pltpu.SMEM),
                  pl.BlockSpec(memory_space=pl.ANY)],
        out_specs=pl.BlockSpec((tg, d), lambda i: (i, 0)),
        out_shape=jax.ShapeDtypeStruct((p, d), src.dtype),
        scratch_shapes=[pltpu.SemaphoreType.DMA(())],
        compiler_params=_cparams("parallel"),
        name="moe_gather",
    )(idx.reshape(p // tg, 1, tg), src)


def _moe_kernel(te_ref, ok_ref, x_ref, wg_ref, wu_ref, wd_ref, o_ref, xb_ref):
    i = pl.program_id(0)
    j = pl.program_id(1)

    @pl.when(j == 0)
    def _():
        xb_ref[...] = x_ref[...].astype(BF16)
        o_ref[...] = jnp.zeros_like(o_ref)

    @pl.when(ok_ref[i] > 0)
    def _():
        xb = xb_ref[...]
        a = (_silu(_dot(xb, wg_ref[0])) * _dot(xb, wu_ref[0])).astype(BF16)
        o_ref[...] += _dot(a, wd_ref[0])


def moe_experts(xs, tile_expert, tile_ok, w_gu, w_down, *, tf=1792):
    p, d = xs.shape
    ne, f, _ = w_down.shape
    tm = MOE_TILE
    nf = f // tf
    grid_spec = pltpu.PrefetchScalarGridSpec(
        num_scalar_prefetch=2,
        grid=(p // tm, nf),
        in_specs=[pl.BlockSpec((tm, d), lambda i, j, te, ok: (i, 0)),
                  pl.BlockSpec((1, d, tf), lambda i, j, te, ok: (te[i], 0, j)),
                  pl.BlockSpec((1, d, tf), lambda i, j, te, ok: (te[i], 0, j + nf)),
                  pl.BlockSpec((1, tf, d), lambda i, j, te, ok: (te[i], j, 0))],
        out_specs=pl.BlockSpec((tm, d), lambda i, j, te, ok: (i, 0)),
        scratch_shapes=[pltpu.VMEM((tm, d), BF16)])
    return pl.pallas_call(
        _moe_kernel,
        grid_spec=grid_spec,
        out_shape=jax.ShapeDtypeStruct((p, d), F32),
        compiler_params=_cparams("parallel", "arbitrary"),
        name="moe_experts",
    )(tile_expert, tile_ok, xs, w_gu, w_gu, w_down)


def _combine_kernel(d1_ref, d2_ref, x_ref, meta_ref, y_ref, gf_ref, o_ref, y1_ref, y2_ref, sem1, sem2):
    rows = o_ref.shape[0]

    def copy1(r):
        return pltpu.make_async_copy(y_ref.at[pl.ds(d1_ref[0, 0, r], 1), :], y1_ref.at[pl.ds(r, 1), :], sem1)

    def copy2(r):
        return pltpu.make_async_copy(y_ref.at[pl.ds(d2_ref[0, 0, r], 1), :], y2_ref.at[pl.ds(r, 1), :], sem2)

    def start(r, c):
        copy1(r).start()
        copy2(r).start()
        return c

    def wait(r, c):
        copy1(r).wait()
        copy2(r).wait()
        return c

    lax.fori_loop(0, rows, start, 0)
    lax.fori_loop(0, rows, wait, 0)
    meta = meta_ref[...]
    w1 = meta[:, META_W1:META_W1 + 1]
    w2 = meta[:, META_W2:META_W2 + 1]
    o_ref[...] = _rms(x_ref[...] + w1 * y1_ref[...] + w2 * y2_ref[...], gf_ref[...])


def moe_combine(x, meta, ys, d1, d2, g_final, *, tc=256):
    n, d = x.shape
    tc = min(tc, n)
    idx_spec = pl.BlockSpec((1, 1, tc), lambda i: (i, 0, 0), memory_space=pltpu.SMEM)
    return pl.pallas_call(
        _combine_kernel,
        grid=(n // tc,),
        in_specs=[idx_spec, idx_spec,
                  pl.BlockSpec((tc, d), lambda i: (i, 0)),
                  pl.BlockSpec((tc, LANES), lambda i: (i, 0)),
                  pl.BlockSpec(memory_space=pl.ANY),
                  pl.BlockSpec((1, d), lambda i: (0, 0))],
        out_specs=pl.BlockSpec((tc, d), lambda i: (i, 0)),
        out_shape=jax.ShapeDtypeStruct((n, d), F32),
        scratch_shapes=[pltpu.VMEM((tc, d), F32), pltpu.VMEM((tc, d), F32),
                        pltpu.SemaphoreType.DMA(()), pltpu.SemaphoreType.DMA(())],
        compiler_params=_cparams("parallel"),
        name="moe_combine",
    )(d1.reshape(n // tc, 1, tc), d2.reshape(n // tc, 1, tc), x, meta, ys, g_final.reshape(1, d))


def kernel(x, norm_mix, norm_ffn, a_w_in, a_b_gate, a_conv, a_norm_h, a_w_out, norm_kv, b_w_kv,
           b_cmp_pos, b_cmp_w1, b_cmp_w2, b_w_q, b_b_gate, b_w_out, f_w_gu, f_w_down,
           m_router, m_w_gu, m_w_down, norm_final):
    B, T, D = x.shape
    N = B * T
    G, R, dh = B_KV_GROUPS, B_REP, B_HEAD_DIM
    xs = x.reshape(N, D)

    inner4 = a_w_in.shape[2] - 2 * A_HEADS
    w_in = a_w_in[0]
    proj = norm_matmul(xs, norm_mix[0], w_in[:, :inner4].astype(BF16), out_dtype=BF16)
    w_gate = jnp.pad(w_in[:, inner4:], ((0, 0), (0, LANES - 2 * A_HEADS)))
    gcol = norm_matmul(xs, norm_mix[0], w_gate, out_dtype=F32, exact=True, tn=LANES)
    hs = mlstm_core(proj, gcol, a_b_gate[0], a_conv[0], a_norm_h[0], B, T)
    xs = matmul_residual(hs, a_w_out[0].astype(BF16), xs)
    xs = ffn_dense(xs, norm_ffn[0], f_w_gu[0].astype(BF16), f_w_down[0].astype(BF16))

    hd = B_HEADS * dh
    kvp = norm_matmul(xs, norm_kv, b_w_kv.astype(BF16), out_dtype=BF16, tn=b_w_kv.shape[1] // 2)
    w_q = b_w_q[0]
    qp = norm_matmul(xs, norm_mix[1], (w_q[:, :hd] * (dh ** -0.5)).astype(BF16), out_dtype=BF16)
    w_qg = jnp.pad(w_q[:, hd:], ((0, 0), (0, LANES - 3 * B_HEADS)))
    gl = norm_matmul(xs, norm_mix[1], w_qg, out_dtype=F32, exact=True, tn=LANES)

    kvt = kvp.reshape(B, T, 6 * G, dh).transpose(0, 2, 1, 3)
    ncmp = T // CMP_STRIDE
    pos = b_cmp_pos.transpose(1, 0, 2).reshape(2, 1, CMP_LEN * dh).astype(BF16)
    kvc = nsa_compress(kvt[:, :2 * G].reshape(B, 2 * G, ncmp, CMP_STRIDE * dh), pos,
                       b_cmp_w1.astype(BF16), b_cmp_w2.astype(BF16))

    feat_w = LANES - dh
    key_feat = jnp.asarray(_pos_features(np.arange(T), feat_w), BF16)
    cmp_feat = jnp.asarray(_pos_features(np.arange(ncmp) * CMP_STRIDE + CMP_LEN - 1, feat_w), BF16)
    q_feat = jnp.asarray(_slope_features(feat_w), BF16).reshape(G, R, 1, feat_w)

    def with_feat(a, f):
        return jnp.concatenate([a, jnp.broadcast_to(f, a.shape[:-1] + (feat_w,))], axis=-1)

    qf = with_feat(qp.reshape(B, T, G, R, dh).transpose(0, 2, 3, 1, 4), q_feat)
    kc = with_feat(kvc[:, :G].astype(BF16), cmp_feat)
    vc = kvc[:, G:].astype(BF16)
    ks = with_feat(kvt[:, 2 * G:3 * G], key_feat)
    ones_col = np.zeros((1, feat_w), np.float32)
    ones_col[0, 0] = 1.0
    ones_col = jnp.asarray(ones_col, BF16)
    vs = with_feat(kvt[:, 3 * G:4 * G], ones_col)
    kw = with_feat(kvt[:, 4 * G:5 * G], key_feat)
    vw = with_feat(kvt[:, 5 * G:6 * G], ones_col)
    glt = gl[:, :3 * B_HEADS].reshape(B, T, G, 3 * R).transpose(0, 2, 1, 3)
    bg = b_b_gate[0].reshape(G, 1, 3 * R)

    nsel = T // SEL_BLOCK
    ci = np.arange(ncmp)[:, None] * CMP_STRIDE
    sj = np.arange(LANES)[None, :] * SEL_BLOCK
    ov = ((ci < sj + SEL_BLOCK) & (ci + CMP_LEN > sj) & (np.arange(LANES)[None, :] < nsel)
          & (np.arange(ncmp)[:, None] < ncmp - 1))
    ov = jnp.asarray(ov.astype(np.float32), BF16)

    oa = nsa_attention(qf, kc, vc, ks, vs, kw, vw, glt, bg, ov)
    oa = oa.transpose(0, 3, 1, 2, 4).reshape(N, hd)
    xs = matmul_residual(oa, b_w_out[0].astype(BF16), xs)

    xn, meta, pos, cnt = moe_router(xs, norm_ffn[1], m_router[0])
    ne = N_EXPERTS
    p_rows = 2 * N + ne * MOE_TILE
    i1 = meta[:, META_I1].astype(jnp.int32)
    i2 = meta[:, META_I2].astype(jnp.int32)
    counts = cnt[0, :ne].astype(jnp.int32)
    padded = (counts + MOE_TILE - 1) // MOE_TILE * MOE_TILE
    seg_end = jnp.cumsum(padded)
    seg_start = seg_end - padded
    pos8 = pos[:, :ne].astype(jnp.int32)
    d1 = seg_start[i1] + jnp.take_along_axis(pos8, i1[:, None], axis=1)[:, 0]
    d2 = seg_start[i2] + jnp.take_along_axis(pos8, i2[:, None], axis=1)[:, 0]
    tok = jnp.arange(N, dtype=jnp.int32)
    row_tok = jnp.zeros((p_rows,), jnp.int32).at[d1].set(tok).at[d2].set(tok)
    tile_start = jnp.arange(p_rows // MOE_TILE, dtype=jnp.int32) * MOE_TILE
    tile_expert = jnp.minimum(jnp.sum(tile_start[:, None] >= seg_end[None, :], axis=1), ne - 1).astype(jnp.int32)
    tile_ok = (tile_start < seg_end[ne - 1]).astype(jnp.int32)

    xsort = gather_rows(xn, row_tok)
    ys = moe_experts(xsort, tile_expert, tile_ok, m_w_gu[0].astype(BF16), m_w_down[0].astype(BF16))
    out = moe_combine(xs, meta, ys, d1, d2, norm_final)
    return out.reshape(B, T, D)
```

```python
import functools

import numpy as np
import jax
import jax.numpy as jnp
from jax import lax
from jax.experimental import pallas as pl
from jax.experimental.pallas import tpu as pltpu

F32 = jnp.float32
BF16 = jnp.bfloat16

RMS_EPS = 1e-6
A_HEADS = 4
A_CONV = 4
B_HEADS = 16
B_KV_GROUPS = 4
B_REP = B_HEADS // B_KV_GROUPS
B_HEAD_DIM = 64
CMP_LEN = 32
CMP_STRIDE = 16
SEL_BLOCK = 64
SEL_TOPN = 16
WINDOW = 512
FORCED_SCORE = 1e4
N_EXPERTS = 8

LANES = 128
V7X_VMEM_BYTES = 64 * 1024 * 1024
VMEM_LIMIT = V7X_VMEM_BYTES - 8 * 1024 * 1024
NEG = -1e30


def _cparams(*sem):
    return pltpu.CompilerParams(dimension_semantics=sem, vmem_limit_bytes=VMEM_LIMIT)


def _dot(a, b):
    return jnp.dot(a, b, preferred_element_type=F32)


def _dot_nt(a, b):
    return lax.dot_general(a, b, (((1,), (1,)), ((), ())), preferred_element_type=F32)


def _dot_tn(a, b):
    return lax.dot_general(a, b, (((0,), (0,)), ((), ())), preferred_element_type=F32)


def _split3(a):
    a1 = a.astype(BF16)
    r1 = a - a1.astype(F32)
    a2 = r1.astype(BF16)
    a3 = (r1 - a2.astype(F32)).astype(BF16)
    return a1, a2, a3


def _dot_f32(a, b):
    a1, a2, a3 = _split3(a)
    b1, b2, b3 = _split3(b)
    return (_dot(a1, b1) + _dot(a1, b2) + _dot(a2, b1)
            + _dot(a2, b2) + _dot(a1, b3) + _dot(a3, b1))


def _dot_f32_nt(a, b):
    a1, a2, a3 = _split3(a)
    b1, b2, b3 = _split3(b)
    return (_dot_nt(a1, b1) + _dot_nt(a1, b2) + _dot_nt(a2, b1)
            + _dot_nt(a2, b2) + _dot_nt(a1, b3) + _dot_nt(a3, b1))


def _rms(xf, g):
    return xf * lax.rsqrt(jnp.mean(xf * xf, axis=-1, keepdims=True) + RMS_EPS) * g


def _silu(x):
    return x * jax.nn.sigmoid(x)


def _log_sigmoid(x):
    return jnp.minimum(x, 0.0) - jnp.log(1.0 + jnp.exp(-jnp.abs(x)))


def _norm_matmul_kernel(x_ref, g_ref, w_ref, o_ref, xn_ref, *, exact):
    @pl.when(pl.program_id(1) == 0)
    def _():
        xn_ref[...] = _rms(x_ref[...], g_ref[...]).astype(xn_ref.dtype)

    if exact:
        o_ref[...] = _dot_f32(xn_ref[...], w_ref[...]).astype(o_ref.dtype)
    else:
        o_ref[...] = _dot(xn_ref[...], w_ref[...]).astype(o_ref.dtype)


def norm_matmul(x, g, w, *, out_dtype, exact=False, tm=1024, tn=1024):
    n, d = x.shape
    dout = w.shape[1]
    tm = min(tm, n)
    tn = min(tn, dout)
    return pl.pallas_call(
        functools.partial(_norm_matmul_kernel, exact=exact),
        grid=(n // tm, dout // tn),
        in_specs=[pl.BlockSpec((tm, d), lambda i, j: (i, 0)),
                  pl.BlockSpec((1, d), lambda i, j: (0, 0)),
                  pl.BlockSpec((d, tn), lambda i, j: (0, j))],
        out_specs=pl.BlockSpec((tm, tn), lambda i, j: (i, j)),
        out_shape=jax.ShapeDtypeStruct((n, dout), out_dtype),
        scratch_shapes=[pltpu.VMEM((tm, d), F32 if exact else BF16)],
        compiler_params=_cparams("parallel", "arbitrary"),
        name="norm_matmul_f32" if exact else "norm_matmul",
    )(x, g.reshape(1, d), w)


def _matmul_res_kernel(a_ref, w_ref, r_ref, o_ref):
    o_ref[...] = r_ref[...] + _dot(a_ref[...], w_ref[...])


def matmul_residual(a, w, res, *, tm=1024):
    n, k = a.shape
    dout = w.shape[1]
    tm = min(tm, n)
    return pl.pallas_call(
        _matmul_res_kernel,
        grid=(n // tm,),
        in_specs=[pl.BlockSpec((tm, k), lambda i: (i, 0)),
                  pl.BlockSpec((k, dout), lambda i: (0, 0)),
                  pl.BlockSpec((tm, dout), lambda i: (i, 0))],
        out_specs=pl.BlockSpec((tm, dout), lambda i: (i, 0)),
        out_shape=jax.ShapeDtypeStruct((n, dout), F32),
        compiler_params=_cparams("parallel"),
        name="matmul_residual",
    )(a, w, res)


MLSTM_CHUNK = 256


def _mlstm_kernel(qk_ref, v_ref, o_ref, gcol_ref, grow_ref, bcol_ref, brow_ref, convw_ref, gout_ref,
                  out_ref, ct_ref, n_ref, m_ref, prev_ref):
    L = qk_ref.shape[0]
    H = A_HEADS
    inner = v_ref.shape[1]
    dh = inner // H

    @pl.when(pl.program_id(1) == 0)
    def _():
        ct_ref[...] = jnp.zeros_like(ct_ref)
        n_ref[...] = jnp.zeros_like(n_ref)
        m_ref[...] = jnp.zeros_like(m_ref)
        prev_ref[...] = jnp.zeros_like(prev_ref)

    row = lax.broadcasted_iota(jnp.int32, (L, L), 0)
    col = lax.broadcasted_iota(jnp.int32, (L, L), 1)
    causal = col <= row
    tril = jnp.where(causal, 1.0, 0.0).astype(BF16)
    triu = jnp.where(row <= col, 1.0, 0.0).astype(BF16)

    gc = gcol_ref[...] + brow_ref[...]
    gr = grow_ref[...] + bcol_ref[...]
    lfc1, lfc2, lfc3 = _split3(_log_sigmoid(gc))
    lfr1, lfr2, lfr3 = _split3(_log_sigmoid(gr))
    b_c = _dot(tril, lfc1) + _dot(tril, lfc2) + _dot(tril, lfc3)
    b_r = _dot(lfr1, triu) + _dot(lfr2, triu) + _dot(lfr3, triu)

    rowi = lax.broadcasted_iota(jnp.int32, (L, dh), 0)

    def conv(cur, prev, w):
        y = cur * w[A_CONV - 1:A_CONV, :]
        for s in range(1, A_CONV):
            sh = jnp.where(rowi < s, pltpu.roll(prev, s, 0), pltpu.roll(cur, s, 0))
            y = y + sh * w[A_CONV - 1 - s:A_CONV - s, :]
        return y

    for h in range(H):
        hs = slice(h * dh, (h + 1) * dh)
        ks = slice(inner + h * dh, inner + (h + 1) * dh)
        q = conv(qk_ref[:, hs].astype(F32), prev_ref[:, hs].astype(F32), convw_ref[:, hs])
        k = conv(qk_ref[:, ks].astype(F32), prev_ref[:, ks].astype(F32), convw_ref[:, ks]) * (dh ** -0.5)
        v = v_ref[:, hs]
        qb = q.astype(BF16)
        kb = k.astype(BF16)

        li_c = gc[:, h:h + 1]
        bc = b_c[:, H + h:H + h + 1]
        li_r = gr[h:h + 1, :]
        br = b_r[H + h:H + h + 1, :]
        m_prev = m_ref[h:h + 1, 0:1]

        d = jnp.where(causal, bc - br + li_r, -jnp.inf)
        inter = bc + m_prev
        m_t = jnp.maximum(inter, jnp.max(d, axis=-1, keepdims=True))
        w_inter = jnp.exp(inter - m_t)
        s = _dot_nt(qb, kb) * jnp.exp(d - m_t)
        ct = ct_ref[h]
        num = _dot(s.astype(BF16), v) + w_inter * _dot(qb, ct.astype(BF16))
        den = jnp.sum(s, axis=-1, keepdims=True) + w_inter * jnp.sum(q * n_ref[h], axis=-1, keepdims=True)
        hh = num / jnp.maximum(jnp.abs(den), jnp.exp(-m_t))
        hh = hh * lax.rsqrt(jnp.mean(hh * hh, axis=-1, keepdims=True) + RMS_EPS)
        out_ref[:, hs] = (hh * gout_ref[:, hs] * jax.nn.sigmoid(o_ref[:, hs].astype(F32))).astype(out_ref.dtype)

        b_last = bc[L - 1:L, :]
        g = b_last - bc + li_c
        m_new = jnp.maximum(b_last + m_prev, jnp.max(g, axis=0, keepdims=True))
        a_prev = jnp.exp(b_last + m_prev - m_new)
        a_s = jnp.exp(g - m_new)
        ct_ref[h] = a_prev * ct + _dot_tn(kb, (v.astype(F32) * a_s).astype(BF16))
        n_ref[h] = a_prev * n_ref[h] + jnp.sum(k * a_s, axis=0, keepdims=True)
        m_ref[h:h + 1, :] = jnp.broadcast_to(m_new, (1, LANES))

    prev_ref[...] = qk_ref[...]


def mlstm_core(proj, gcol, b_gate, conv_w, g_out, batch, seq):
    n = proj.shape[0]
    inner = proj.shape[1] // 4
    H = A_HEADS
    dh = inner // H
    L = min(MLSTM_CHUNK, seq)
    nc = seq // L
    grow = gcol[:, :2 * H].T
    brow = jnp.pad(b_gate, (0, LANES - 2 * H)).reshape(1, LANES)
    bcol = b_gate.reshape(2 * H, 1)
    return pl.pallas_call(
        _mlstm_kernel,
        grid=(batch, nc),
        in_specs=[pl.BlockSpec((L, 2 * inner), lambda b, c: (b * nc + c, 0)),
                  pl.BlockSpec((L, inner), lambda b, c: (b * nc + c, 2)),
                  pl.BlockSpec((L, inner), lambda b, c: (b * nc + c, 3)),
                  pl.BlockSpec((L, LANES), lambda b, c: (b * nc + c, 0)),
                  pl.BlockSpec((2 * H, L), lambda b, c: (0, b * nc + c)),
                  pl.BlockSpec((2 * H, 1), lambda b, c: (0, 0)),
                  pl.BlockSpec((1, LANES), lambda b, c: (0, 0)),
                  pl.BlockSpec((A_CONV, 2 * inner), lambda b, c: (0, 0)),
                  pl.BlockSpec((1, inner), lambda b, c: (0, 0))],
        out_specs=pl.BlockSpec((L, inner), lambda b, c: (b * nc + c, 0)),
        out_shape=jax.ShapeDtypeStruct((n, inner), BF16),
        scratch_shapes=[pltpu.VMEM((H, dh, dh), F32),
                        pltpu.VMEM((H, 1, dh), F32),
                        pltpu.VMEM((8, LANES), F32),
                        pltpu.VMEM((L, 2 * inner), BF16)],
        compiler_params=_cparams("parallel", "arbitrary"),
        name="mlstm_core",
    )(proj, proj, proj, gcol, grow, bcol, brow, conv_w, g_out.reshape(1, inner))


def _ffn_kernel(x_ref, g_ref, wg_ref, wu_ref, wd_ref, o_ref, xn_ref, acc_ref):
    j = pl.program_id(1)

    @pl.when(j == 0)
    def _():
        xf = x_ref[...]
        xn_ref[...] = _rms(xf, g_ref[...]).astype(BF16)
        acc_ref[...] = xf

    xn = xn_ref[...]
    a = (_silu(_dot(xn, wg_ref[...])) * _dot(xn, wu_ref[...])).astype(BF16)
    acc_ref[...] += _dot(a, wd_ref[...])

    @pl.when(j == pl.num_programs(1) - 1)
    def _():
        o_ref[...] = acc_ref[...]


def ffn_dense(x, g, w_gu, w_down, *, tm=512, tf=1408):
    n, d = x.shape
    f = w_down.shape[0]
    tm = min(tm, n)
    nf = f // tf
    return pl.pallas_call(
        _ffn_kernel,
        grid=(n // tm, nf),
        in_specs=[pl.BlockSpec((tm, d), lambda i, j: (i, 0)),
                  pl.BlockSpec((1, d), lambda i, j: (0, 0)),
                  pl.BlockSpec((d, tf), lambda i, j: (0, j)),
                  pl.BlockSpec((d, tf), lambda i, j: (0, j + nf)),
                  pl.BlockSpec((tf, d), lambda i, j: (j, 0))],
        out_specs=pl.BlockSpec((tm, d), lambda i, j: (i, 0)),
        out_shape=jax.ShapeDtypeStruct((n, d), F32),
        scratch_shapes=[pltpu.VMEM((tm, d), BF16), pltpu.VMEM((tm, d), F32)],
        compiler_params=_cparams("parallel", "arbitrary"),
        name="ffn_dense",
    )(x, g.reshape(1, d), w_gu, w_gu, w_down)


def _compress_kernel(r_ref, pos_ref, w1_ref, w2_ref, o_ref):
    r = r_ref[0, 0]
    w1 = w1_ref[0]
    half = r.shape[1]
    nc = r.shape[0]
    lo = _dot(r, w1[:half])
    hi = _dot(r, w1[half:])
    hid = lo + pltpu.roll(hi, nc - 1, 0) + _dot(pos_ref[0], w1)
    o_ref[0, 0] = _dot(_silu(hid).astype(BF16), w2_ref[0]).astype(o_ref.dtype)


def nsa_compress(r, pos, w1, w2):
    b, c2, nc, half = r.shape
    g = c2 // 2
    hidden = w1.shape[2]
    dh = w2.shape[2]
    return pl.pallas_call(
        _compress_kernel,
        grid=(b, c2),
        in_specs=[pl.BlockSpec((1, 1, nc, half), lambda i, j: (i, j, 0, 0)),
                  pl.BlockSpec((1, 1, 2 * half), lambda i, j: (j // g, 0, 0)),
                  pl.BlockSpec((1, 2 * half, hidden), lambda i, j: (j // g, 0, 0)),
                  pl.BlockSpec((1, hidden, dh), lambda i, j: (j // g, 0, 0))],
        out_specs=pl.BlockSpec((1, 1, nc, dh), lambda i, j: (i, j, 0, 0)),
        out_shape=jax.ShapeDtypeStruct((b, c2, nc, dh), F32),
        compiler_params=_cparams("parallel", "parallel"),
        name="nsa_compress",
    )(r, pos, w1, w2)


NSA_TQ = 128
NSA_TK_SLC = 512
NSA_TK_WIN = 256
NSA_ROWS = 32
SEL_MASK = 2.0 ** 14
LOG2E = float(np.log2(np.e))


def _nsa_kernel(q_ref, kc_ref, vc_ref, ks_ref, vs_ref, kw_ref, vw_ref, gl_ref, bg_ref, ov_ref,
                out_ref, qx_ref, m_ref, acc_ref, bias_ref, s0_ref, p0_ref, a0_ref, s1_ref, p1_ref, a1_ref,
                *, tq, tks, tkw, n_top):
    R = B_REP
    M = R * tq
    dh = B_HEAD_DIM
    t0 = pl.program_id(2) * tq
    q = q_ref[0, 0].reshape(M, LANES)
    slots = ((s0_ref, p0_ref, a0_ref), (s1_ref, p1_ref, a1_ref))

    kc = kc_ref[0, 0]
    ncmp = kc.shape[0]
    s = _dot_nt(q, kc)
    tpos_c = t0 + (lax.broadcasted_iota(jnp.int32, (M, ncmp), 0) & (tq - 1))
    cend = lax.broadcasted_iota(jnp.int32, (M, ncmp), 1) * CMP_STRIDE + (CMP_LEN - 1)
    ok_c = tpos_c >= cend
    s = jnp.where(ok_c, s, NEG)
    p = jnp.where(ok_c, jnp.exp2(s - jnp.max(s, axis=-1, keepdims=True)), 0.0)
    dsum = jnp.sum(p, axis=-1, keepdims=True)
    p = p / jnp.where(dsum > 0, dsum, 1.0)
    o_cmp = _dot(p.astype(BF16), vc_ref[0, 0])

    psum = p[0:tq]
    for r in range(1, R):
        psum = psum + p[r * tq:(r + 1) * tq]
    p1, p2, p3 = _split3(psum)
    ov = ov_ref[...]
    imp = _dot(p1, ov) + _dot(p2, ov) + _dot(p3, ov)
    nsel = ks_ref.shape[2] // SEL_BLOCK
    nselp = max(nsel, 8)
    imp_t = imp.T[:nselp]
    jj = lax.broadcasted_iota(jnp.int32, (nselp, tq), 0)
    tt = t0 + lax.broadcasted_iota(jnp.int32, (nselp, tq), 1)
    cur = tt // SEL_BLOCK
    forced = (jj == 0) | (jj == cur) | (jj == cur - 1)
    iv = jnp.where(forced, FORCED_SCORE, imp_t)
    iv = jnp.where(jj * SEL_BLOCK <= tt, iv, -jnp.inf)
    rank = jnp.zeros((nselp, tq), F32)
    for j2 in range(nsel):
        rv = iv[j2:j2 + 1, :]
        rank = rank + jnp.where(jj > j2, jnp.where(rv >= iv, 1.0, 0.0), jnp.where(rv > iv, 1.0, 0.0))
    unsel = jnp.where(rank < n_top, 0.0, -SEL_MASK)
    if nselp < LANES:
        unsel = jnp.concatenate([unsel, jnp.zeros((LANES - nselp, tq), F32)], axis=0)
    qmask = unsel.T.astype(BF16)
    qx_ref[:, :LANES] = q
    for r in range(R):
        qx_ref[r * tq:(r + 1) * tq, LANES:] = qmask

    def tile_step(qq, k_ref, v_ref, k0, tk, slot, bias):
        s_ref, p_ref, a_ref = slots[slot]
        kk = k_ref[0, 0, pl.ds(k0, tk), :]
        vv = v_ref[0, 0, pl.ds(k0, tk), :]
        if bias is not None:
            bias_ref[:, :tk] = bias
        s_ref[:, :tk] = _dot_nt(qq, kk)
        for r in range(R):
            for c0 in range(0, tq, NSA_ROWS):
                rows = slice(r * tq + c0, r * tq + c0 + NSA_ROWS)
                sc = s_ref[rows, :tk]
                if bias is not None:
                    sc = sc + bias_ref[c0:c0 + NSA_ROWS, :tk]
                m_old = m_ref[rows, :]
                m_new = jnp.maximum(m_old, jnp.max(sc, axis=-1, keepdims=True))
                a_ref[rows, :] = jnp.exp2(m_old - m_new)
                m_ref[rows, :] = m_new
                p_ref[rows, :tk] = jnp.exp2(sc - jnp.concatenate([m_new] * (tk // LANES), axis=1)).astype(BF16)
        acc_ref[...] = a_ref[...] * acc_ref[...] + _dot(p_ref[:, :tk], vv)

    def reset():
        m_ref[...] = jnp.full_like(m_ref, NEG)
        acc_ref[...] = jnp.zeros_like(acc_ref)

    def result():
        acc = acc_ref[...]
        return acc[:, :dh] / acc[:, dh:dh + 1]

    reset()
    qx = qx_ref[...]
    n_full = t0 // tks

    def slc_pair(i, carry):
        tile_step(qx, ks_ref, vs_ref, pl.multiple_of(2 * i * tks, tks), tks, 0, None)
        tile_step(qx, ks_ref, vs_ref, pl.multiple_of((2 * i + 1) * tks, tks), tks, 1, None)
        return carry

    lax.fori_loop(0, n_full // 2, slc_pair, 0)

    @pl.when(n_full % 2 == 1)
    def _():
        tile_step(qx, ks_ref, vs_ref, pl.multiple_of((n_full - 1) * tks, tks), tks, 0, None)

    kd = pl.multiple_of(n_full * tks, tks)
    spos = kd + lax.broadcasted_iota(jnp.int32, (tq, tks), 1)
    tpos = t0 + lax.broadcasted_iota(jnp.int32, (tq, tks), 0)
    tile_step(qx, ks_ref, vs_ref, kd, tks, 1, jnp.where(spos <= tpos, 0.0, NEG))
    o_slc = result()

    reset()
    n_win = WINDOW // tkw + 1
    last = (t0 + tq - 1) // tkw
    for i in range(n_win):
        kw0 = (last - (n_win - 1) + i) * tkw
        spos = kw0 + lax.broadcasted_iota(jnp.int32, (tq, tkw), 1)
        dist = t0 + lax.broadcasted_iota(jnp.int32, (tq, tkw), 0) - spos
        bias = jnp.where((spos >= 0) & (dist >= 0) & (dist < WINDOW), 0.0, NEG)
        tile_step(q, kw_ref, vw_ref, pl.multiple_of(jnp.maximum(kw0, 0), tkw), tkw, i % 2, bias)
    o_win = result()

    gates = jax.nn.sigmoid(gl_ref[0, 0, pl.ds(pl.multiple_of(t0, tq), tq), :] + bg_ref[0])
    for r in range(R):
        rs = slice(r * tq, (r + 1) * tq)
        o = (gates[:, 3 * r:3 * r + 1] * o_cmp[rs] + gates[:, 3 * r + 1:3 * r + 2] * o_slc[rs]
             + gates[:, 3 * r + 2:3 * r + 3] * o_win[rs])
        out_ref[0, 0, r] = o.astype(out_ref.dtype)


def nsa_attention(qf, kc, vc, ks, vs, kw, vw, gl, bg, ov):
    b, g, r, t, dq = qf.shape
    dh = B_HEAD_DIM
    tq = min(NSA_TQ, t)
    tks = min(NSA_TK_SLC, t)
    tkw = min(NSA_TK_WIN, t)
    tkm = max(tks, tkw)
    ncmp = kc.shape[2]
    n_top = min(SEL_TOPN, t // SEL_BLOCK)
    kv_spec = lambda d: pl.BlockSpec((1, 1, t, d), lambda i, j, k: (i, j, 0, 0))
    slot = [pltpu.VMEM((r * tq, tkm), F32), pltpu.VMEM((r * tq, tkm), BF16), pltpu.VMEM((r * tq, LANES), F32)]
    return pl.pallas_call(
        functools.partial(_nsa_kernel, tq=tq, tks=tks, tkw=tkw, n_top=n_top),
        grid=(b, g, t // tq),
        in_specs=[pl.BlockSpec((1, 1, r, tq, dq), lambda i, j, k: (i, j, 0, k, 0)),
                  pl.BlockSpec((1, 1, ncmp, dq), lambda i, j, k: (i, j, 0, 0)),
                  pl.BlockSpec((1, 1, ncmp, dh), lambda i, j, k: (i, j, 0, 0)),
                  kv_spec(2 * dq), kv_spec(dq), kv_spec(dq), kv_spec(dq),
                  pl.BlockSpec((1, 1, t, 3 * r), lambda i, j, k: (i, j, 0, 0)),
                  pl.BlockSpec((1, 1, 3 * r), lambda i, j, k: (j, 0, 0)),
                  pl.BlockSpec(ov.shape, lambda i, j, k: (0, 0))],
        out_specs=pl.BlockSpec((1, 1, r, tq, dh), lambda i, j, k: (i, j, 0, k, 0)),
        out_shape=jax.ShapeDtypeStruct((b, g, r, t, dh), BF16),
        scratch_shapes=[pltpu.VMEM((r * tq, 2 * LANES), BF16),
                        pltpu.VMEM((r * tq, LANES), F32),
                        pltpu.VMEM((r * tq, LANES), F32),
                        pltpu.VMEM((tq, tkm), F32)]
                       + slot + slot,
        compiler_params=_cparams("parallel", "parallel", "arbitrary"),
        name="nsa_attention",
    )(qf, kc, vc, ks, vs, kw, vw, gl, bg, ov)


def _alibi_slopes(n):
    return np.power(2.0, -8.0 * np.arange(1, n + 1) / n).astype(np.float32)


def _np_split3(a):
    a = np.asarray(a, np.float32)
    out = []
    r = a
    for _ in range(3):
        p = r.astype(BF16).astype(np.float32)
        out.append(p)
        r = (r - p).astype(np.float32)
    return out


def _pos_features(pos, width):
    hi = (pos // 64).astype(np.float32)
    lo = (pos % 64).astype(np.float32)
    f = np.zeros((pos.shape[0], width), np.float32)
    f[:, 0:3] = hi[:, None]
    f[:, 3:6] = lo[:, None]
    return f


def _slope_features(width):
    s1, s2, s3 = _np_split3((_alibi_slopes(B_HEADS).astype(np.float64) * LOG2E).astype(np.float32))
    f = np.zeros((B_HEADS, width), np.float32)
    for i, s in enumerate((s1, s2, s3)):
        f[:, i] = 64.0 * s
        f[:, 3 + i] = s
    return f


META_I1, META_I2, META_W1, META_W2 = 0, 1, 2, 3


def _router_kernel(x_ref, g_ref, w_ref, xn_ref, meta_ref, pos_ref, cnt_ref, run_ref):
    @pl.when(pl.program_id(0) == 0)
    def _():
        run_ref[...] = jnp.zeros_like(run_ref)

    xn = _rms(x_ref[...], g_ref[...])
    xn_ref[...] = xn
    logits = _dot_f32(xn, w_ref[...])
    tm = logits.shape[0]
    lane = lax.broadcasted_iota(jnp.int32, logits.shape, 1)
    logits = jnp.where(lane < N_EXPERTS, logits, -jnp.inf)
    m1 = jnp.max(logits, axis=-1, keepdims=True)
    i1 = jnp.min(jnp.where(logits == m1, lane, LANES), axis=-1, keepdims=True)
    rest = jnp.where(lane == i1, -jnp.inf, logits)
    m2 = jnp.max(rest, axis=-1, keepdims=True)
    i2 = jnp.min(jnp.where(rest == m2, lane, LANES), axis=-1, keepdims=True)
    e2 = jnp.exp(m2 - m1)
    w1 = 1.0 / (1.0 + e2)
    meta = jnp.where(lane == META_I1, i1.astype(F32), 0.0)
    meta = jnp.where(lane == META_I2, i2.astype(F32), meta)
    meta = jnp.where(lane == META_W1, w1, meta)
    meta_ref[...] = jnp.where(lane == META_W2, e2 * w1, meta)

    sel = jnp.where((lane == i1) | (lane == i2), 1.0, 0.0)
    row = lax.broadcasted_iota(jnp.int32, (tm, tm), 0)
    col = lax.broadcasted_iota(jnp.int32, (tm, tm), 1)
    before = jnp.where(col < row, 1.0, 0.0).astype(BF16)
    run = run_ref[...]
    pos_ref[...] = run + _dot(before, sel.astype(BF16))
    run = run + jnp.sum(sel, axis=0, keepdims=True)
    run_ref[...] = run
    cnt_ref[...] = run


def moe_router(x, g, w_router, *, tm=512):
    n, d = x.shape
    tm = min(tm, n)
    w = jnp.pad(w_router, ((0, 0), (0, LANES - w_router.shape[1])))
    row_spec = pl.BlockSpec((tm, LANES), lambda i: (i, 0))
    return pl.pallas_call(
        _router_kernel,
        grid=(n // tm,),
        in_specs=[pl.BlockSpec((tm, d), lambda i: (i, 0)),
                  pl.BlockSpec((1, d), lambda i: (0, 0)),
                  pl.BlockSpec((d, LANES), lambda i: (0, 0))],
        out_specs=[pl.BlockSpec((tm, d), lambda i: (i, 0)), row_spec, row_spec,
                   pl.BlockSpec((1, LANES), lambda i: (0, 0))],
        out_shape=[jax.ShapeDtypeStruct((n, d), F32), jax.ShapeDtypeStruct((n, LANES), F32),
                   jax.ShapeDtypeStruct((n, LANES), F32), jax.ShapeDtypeStruct((1, LANES), F32)],
        scratch_shapes=[pltpu.VMEM((1, LANES), F32)],
        compiler_params=_cparams("arbitrary"),
        name="moe_router",
    )(x, g.reshape(1, d), w)


MOE_TILE = 512


def _gather_rows_kernel(idx_ref, src_ref, o_ref, sem):
    rows = o_ref.shape[0]

    def row_copy(r):
        return pltpu.make_async_copy(src_ref.at[pl.ds(idx_ref[0, 0, r], 1), :], o_ref.at[pl.ds(r, 1), :], sem)

    def start(r, c):
        row_copy(r).start()
        return c

    def wait(r, c):
        row_copy(r).wait()
        return c

    lax.fori_loop(0, rows, start, 0)
    lax.fori_loop(0, rows, wait, 0)


def gather_rows(src, idx, *, tg=MOE_TILE):
    p = idx.shape[0]
    d = src.shape[1]
    return pl.pallas_call(
        _gather_rows_kernel,
        grid=(p // tg,),
        in_specs=[pl.BlockSpec((1, 1, tg), lambda i: (i, 0, 0), memory_space=pltpu.SMEM),
                  pl.BlockSpec(memory_space=pl.ANY)],
        out_specs=pl.BlockSpec((tg, d), lambda i: (i, 0)),
        out_shape=jax.ShapeDtypeStruct((p, d), src.dtype),
        scratch_shapes=[pltpu.SemaphoreType.DMA(())],
        compiler_params=_cparams("parallel"),
        name="moe_gather",
    )(idx.reshape(p // tg, 1, tg), src)


def _moe_kernel(te_ref, ok_ref, x_ref, wg_ref, wu_ref, wd_ref, o_ref, xb_ref):
    i = pl.program_id(0)
    j = pl.program_id(1)

    @pl.when(j == 0)
    def _():
        xb_ref[...] = x_ref[...].astype(BF16)
        o_ref[...] = jnp.zeros_like(o_ref)

    @pl.when(ok_ref[i] > 0)
    def _():
        xb = xb_ref[...]
        a = (_silu(_dot(xb, wg_ref[0])) * _dot(xb, wu_ref[0])).astype(BF16)
        o_ref[...] += _dot(a, wd_ref[0])


def moe_experts(xs, tile_expert, tile_ok, w_gu, w_down, *, tf=1792):
    p, d = xs.shape
    ne, f, _ = w_down.shape
    tm = MOE_TILE
    nf = f // tf
    grid_spec = pltpu.PrefetchScalarGridSpec(
        num_scalar_prefetch=2,
        grid=(p // tm, nf),
        in_specs=[pl.BlockSpec((tm, d), lambda i, j, te, ok: (i, 0)),
                  pl.BlockSpec((1, d, tf), lambda i, j, te, ok: (te[i], 0, j)),
                  pl.BlockSpec((1, d, tf), lambda i, j, te, ok: (te[i], 0, j + nf)),
                  pl.BlockSpec((1, tf, d), lambda i, j, te, ok: (te[i], j, 0))],
        out_specs=pl.BlockSpec((tm, d), lambda i, j, te, ok: (i, 0)),
        scratch_shapes=[pltpu.VMEM((tm, d), BF16)])
    return pl.pallas_call(
        _moe_kernel,
        grid_spec=grid_spec,
        out_shape=jax.ShapeDtypeStruct((p, d), F32),
        compiler_params=_cparams("parallel", "arbitrary"),
        name="moe_experts",
    )(tile_expert, tile_ok, xs, w_gu, w_gu, w_down)


def _combine_kernel(d1_ref, d2_ref, x_ref, meta_ref, y_ref, gf_ref, o_ref, y1_ref, y2_ref, sem1, sem2):
    rows = o_ref.shape[0]

    def copy1(r):
        return pltpu.make_async_copy(y_ref.at[pl.ds(d1_ref[0, 0, r], 1), :], y1_ref.at[pl.ds(r, 1), :], sem1)

    def copy2(r):
        return pltpu.make_async_copy(y_ref.at[pl.ds(d2_ref[0, 0, r], 1), :], y2_ref.at[pl.ds(r, 1), :], sem2)

    def start(r, c):
        copy1(r).start()
        copy2(r).start()
        return c

    def wait(r, c):
        copy1(r).wait()
        copy2(r).wait()
        return c

    lax.fori_loop(0, rows, start, 0)
    lax.fori_loop(0, rows, wait, 0)
    meta = meta_ref[...]
    w1 = meta[:, META_W1:META_W1 + 1]
    w2 = meta[:, META_W2:META_W2 + 1]
    o_ref[...] = _rms(x_ref[...] + w1 * y1_ref[...] + w2 * y2_ref[...], gf_ref[...])


def moe_combine(x, meta, ys, d1, d2, g_final, *, tc=256):
    n, d = x.shape
    tc = min(tc, n)
    idx_spec = pl.BlockSpec((1, 1, tc), lambda i: (i, 0, 0), memory_space=pltpu.SMEM)
    return pl.pallas_call(
        _combine_kernel,
        grid=(n // tc,),
        in_specs=[idx_spec, idx_spec,
                  pl.BlockSpec((tc, d), lambda i: (i, 0)),
                  pl.BlockSpec((tc, LANES), lambda i: (i, 0)),
                  pl.BlockSpec(memory_space=pl.ANY),
                  pl.BlockSpec((1, d), lambda i: (0, 0))],
        out_specs=pl.BlockSpec((tc, d), lambda i: (i, 0)),
        out_shape=jax.ShapeDtypeStruct((n, d), F32),
        scratch_shapes=[pltpu.VMEM((tc, d), F32), pltpu.VMEM((tc, d), F32),
                        pltpu.SemaphoreType.DMA(()), pltpu.SemaphoreType.DMA(())],
        compiler_params=_cparams("parallel"),
        name="moe_combine",
    )(d1.reshape(n // tc, 1, tc), d2.reshape(n // tc, 1, tc), x, meta, ys, g_final.reshape(1, d))


def kernel(x, norm_mix, norm_ffn, a_w_in, a_b_gate, a_conv, a_norm_h, a_w_out, norm_kv, b_w_kv,
           b_cmp_pos, b_cmp_w1, b_cmp_w2, b_w_q, b_b_gate, b_w_out, f_w_gu, f_w_down,
           m_router, m_w_gu, m_w_down, norm_final):
    B, T, D = x.shape
    N = B * T
    G, R, dh = B_KV_GROUPS, B_REP, B_HEAD_DIM
    xs = x.reshape(N, D)

    inner4 = a_w_in.shape[2] - 2 * A_HEADS
    w_in = a_w_in[0]
    proj = norm_matmul(xs, norm_mix[0], w_in[:, :inner4].astype(BF16), out_dtype=BF16)
    w_gate = jnp.pad(w_in[:, inner4:], ((0, 0), (0, LANES - 2 * A_HEADS)))
    gcol = norm_matmul(xs, norm_mix[0], w_gate, out_dtype=F32, exact=True, tn=LANES)
    hs = mlstm_core(proj, gcol, a_b_gate[0], a_conv[0], a_norm_h[0], B, T)
    xs = matmul_residual(hs, a_w_out[0].astype(BF16), xs)
    xs = ffn_dense(xs, norm_ffn[0], f_w_gu[0].astype(BF16), f_w_down[0].astype(BF16))

    hd = B_HEADS * dh
    kvp = norm_matmul(xs, norm_kv, b_w_kv.astype(BF16), out_dtype=BF16, tn=b_w_kv.shape[1] // 2)
    w_q = b_w_q[0]
    qp = norm_matmul(xs, norm_mix[1], (w_q[:, :hd] * (dh ** -0.5 * LOG2E)).astype(BF16), out_dtype=BF16)
    w_qg = jnp.pad(w_q[:, hd:], ((0, 0), (0, LANES - 3 * B_HEADS)))
    gl = norm_matmul(xs, norm_mix[1], w_qg, out_dtype=F32, exact=True, tn=LANES)

    kvt = kvp.reshape(B, T, 6 * G, dh).transpose(0, 2, 1, 3)
    ncmp = T // CMP_STRIDE
    pos = b_cmp_pos.transpose(1, 0, 2).reshape(2, 1, CMP_LEN * dh).astype(BF16)
    kvc = nsa_compress(kvt[:, :2 * G].reshape(B, 2 * G, ncmp, CMP_STRIDE * dh), pos,
                       b_cmp_w1.astype(BF16), b_cmp_w2.astype(BF16))

    feat_w = LANES - dh
    key_feat = jnp.asarray(_pos_features(np.arange(T), feat_w), BF16)
    cmp_feat = jnp.asarray(_pos_features(np.arange(ncmp) * CMP_STRIDE + CMP_LEN - 1, feat_w), BF16)
    q_feat = jnp.asarray(_slope_features(feat_w), BF16).reshape(G, R, 1, feat_w)

    def with_feat(a, f):
        return jnp.concatenate([a, jnp.broadcast_to(f, a.shape[:-1] + (feat_w,))], axis=-1)

    qf = with_feat(qp.reshape(B, T, G, R, dh).transpose(0, 2, 3, 1, 4), q_feat)
    kc = with_feat(kvc[:, :G].astype(BF16), cmp_feat)
    vc = kvc[:, G:].astype(BF16)
    blk_onehot = (np.arange(T)[:, None] // SEL_BLOCK == np.arange(LANES)[None, :]).astype(np.float32)
    blk_onehot = jnp.asarray(blk_onehot, BF16)
    ks = with_feat(kvt[:, 2 * G:3 * G], key_feat)
    ks = jnp.concatenate([ks, jnp.broadcast_to(blk_onehot, ks.shape[:-1] + (LANES,))], axis=-1)
    ones_col = np.zeros((1, feat_w), np.float32)
    ones_col[0, 0] = 1.0
    ones_col = jnp.asarray(ones_col, BF16)
    vs = with_feat(kvt[:, 3 * G:4 * G], ones_col)
    kw = with_feat(kvt[:, 4 * G:5 * G], key_feat)
    vw = with_feat(kvt[:, 5 * G:6 * G], ones_col)
    glt = gl[:, :3 * B_HEADS].reshape(B, T, G, 3 * R).transpose(0, 2, 1, 3)
    bg = b_b_gate[0].reshape(G, 1, 3 * R)

    nsel = T // SEL_BLOCK
    ci = np.arange(ncmp)[:, None] * CMP_STRIDE
    sj = np.arange(LANES)[None, :] * SEL_BLOCK
    ov = ((ci < sj + SEL_BLOCK) & (ci + CMP_LEN > sj) & (np.arange(LANES)[None, :] < nsel)
          & (np.arange(ncmp)[:, None] < ncmp - 1))
    ov = jnp.asarray(ov.astype(np.float32), BF16)

    oa = nsa_attention(qf, kc, vc, ks, vs, kw, vw, glt, bg, ov)
    oa = oa.transpose(0, 3, 1, 2, 4).reshape(N, hd)
    xs = matmul_residual(oa, b_w_out[0].astype(BF16), xs)

    xn, meta, pos, cnt = moe_router(xs, norm_ffn[1], m_router[0])
    ne = N_EXPERTS
    p_rows = 2 * N + ne * MOE_TILE
    i1 = meta[:, META_I1].astype(jnp.int32)
    i2 = meta[:, META_I2].astype(jnp.int32)
    counts = cnt[0, :ne].astype(jnp.int32)
    padded = (counts + MOE_TILE - 1) // MOE_TILE * MOE_TILE
    seg_end = jnp.cumsum(padded)
    seg_start = seg_end - padded
    pos8 = pos[:, :ne].astype(jnp.int32)
    d1 = seg_start[i1] + jnp.take_along_axis(pos8, i1[:, None], axis=1)[:, 0]
    d2 = seg_start[i2] + jnp.take_along_axis(pos8, i2[:, None], axis=1)[:, 0]
    tok = jnp.arange(N, dtype=jnp.int32)
    row_tok = jnp.zeros((p_rows,), jnp.int32).at[d1].set(tok).at[d2].set(tok)
    tile_start = jnp.arange(p_rows // MOE_TILE, dtype=jnp.int32) * MOE_TILE
    tile_expert = jnp.minimum(jnp.sum(tile_start[:, None] >= seg_end[None, :], axis=1), ne - 1).astype(jnp.int32)
    tile_ok = (tile_start < seg_end[ne - 1]).astype(jnp.int32)

    xsort = gather_rows(xn, row_tok)
    ys = moe_experts(xsort, tile_expert, tile_ok, m_w_gu[0].astype(BF16), m_w_down[0].astype(BF16))
    out = moe_combine(xs, meta, ys, d1, d2, norm_final)
    return out.reshape(B, T, D)
```

```python
import functools

import numpy as np
import jax
import jax.numpy as jnp
from jax import lax
from jax.experimental import pallas as pl
from jax.experimental.pallas import tpu as pltpu

F32 = jnp.float32
BF16 = jnp.bfloat16

RMS_EPS = 1e-6
A_HEADS = 4
A_CONV = 4
B_HEADS = 16
B_KV_GROUPS = 4
B_REP = B_HEADS // B_KV_GROUPS
B_HEAD_DIM = 64
CMP_LEN = 32
CMP_STRIDE = 16
SEL_BLOCK = 64
SEL_TOPN = 16
WINDOW = 512
FORCED_SCORE = 1e4
N_EXPERTS = 8

LANES = 128
V7X_VMEM_BYTES = 64 * 1024 * 1024
VMEM_LIMIT = V7X_VMEM_BYTES - 8 * 1024 * 1024
NEG = -1e30


def _cparams(*sem):
    return pltpu.CompilerParams(dimension_semantics=sem, vmem_limit_bytes=VMEM_LIMIT)


def _dot(a, b):
    return jnp.dot(a, b, preferred_element_type=F32)


def _dot_nt(a, b):
    return lax.dot_general(a, b, (((1,), (1,)), ((), ())), preferred_element_type=F32)


def _dot_tn(a, b):
    return lax.dot_general(a, b, (((0,), (0,)), ((), ())), preferred_element_type=F32)


def _split3(a):
    a1 = a.astype(BF16)
    r1 = a - a1.astype(F32)
    a2 = r1.astype(BF16)
    a3 = (r1 - a2.astype(F32)).astype(BF16)
    return a1, a2, a3


def _dot_f32(a, b):
    a1, a2, a3 = _split3(a)
    b1, b2, b3 = _split3(b)
    return (_dot(a1, b1) + _dot(a1, b2) + _dot(a2, b1)
            + _dot(a2, b2) + _dot(a1, b3) + _dot(a3, b1))


def _dot_f32_nt(a, b):
    a1, a2, a3 = _split3(a)
    b1, b2, b3 = _split3(b)
    return (_dot_nt(a1, b1) + _dot_nt(a1, b2) + _dot_nt(a2, b1)
            + _dot_nt(a2, b2) + _dot_nt(a1, b3) + _dot_nt(a3, b1))


def _rms(xf, g):
    return xf * lax.rsqrt(jnp.mean(xf * xf, axis=-1, keepdims=True) + RMS_EPS) * g


def _silu(x):
    return x * jax.nn.sigmoid(x)


def _log_sigmoid(x):
    return jnp.minimum(x, 0.0) - jnp.log(1.0 + jnp.exp(-jnp.abs(x)))


def _norm_matmul_kernel(x_ref, g_ref, w_ref, o_ref, xn_ref, *, exact):
    @pl.when(pl.program_id(1) == 0)
    def _():
        xn_ref[...] = _rms(x_ref[...], g_ref[...]).astype(xn_ref.dtype)

    if exact:
        o_ref[...] = _dot_f32(xn_ref[...], w_ref[...]).astype(o_ref.dtype)
    else:
        o_ref[...] = _dot(xn_ref[...], w_ref[...]).astype(o_ref.dtype)


def norm_matmul(x, g, w, *, out_dtype, exact=False, tm=1024, tn=1024):
    n, d = x.shape
    dout = w.shape[1]
    tm = min(tm, n)
    tn = min(tn, dout)
    return pl.pallas_call(
        functools.partial(_norm_matmul_kernel, exact=exact),
        grid=(n // tm, dout // tn),
        in_specs=[pl.BlockSpec((tm, d), lambda i, j: (i, 0)),
                  pl.BlockSpec((1, d), lambda i, j: (0, 0)),
                  pl.BlockSpec((d, tn), lambda i, j: (0, j))],
        out_specs=pl.BlockSpec((tm, tn), lambda i, j: (i, j)),
        out_shape=jax.ShapeDtypeStruct((n, dout), out_dtype),
        scratch_shapes=[pltpu.VMEM((tm, d), F32 if exact else BF16)],
        compiler_params=_cparams("parallel", "arbitrary"),
        name="norm_matmul_f32" if exact else "norm_matmul",
    )(x, g.reshape(1, d), w)


def _matmul_res_kernel(a_ref, w_ref, r_ref, o_ref):
    o_ref[...] = r_ref[...] + _dot(a_ref[...], w_ref[...])


def matmul_residual(a, w, res, *, tm=1024):
    n, k = a.shape
    dout = w.shape[1]
    tm = min(tm, n)
    return pl.pallas_call(
        _matmul_res_kernel,
        grid=(n // tm,),
        in_specs=[pl.BlockSpec((tm, k), lambda i: (i, 0)),
                  pl.BlockSpec((k, dout), lambda i: (0, 0)),
                  pl.BlockSpec((tm, dout), lambda i: (i, 0))],
        out_specs=pl.BlockSpec((tm, dout), lambda i: (i, 0)),
        out_shape=jax.ShapeDtypeStruct((n, dout), F32),
        compiler_params=_cparams("parallel"),
        name="matmul_residual",
    )(a, w, res)


MLSTM_CHUNK = 256


def _mlstm_kernel(qk_ref, v_ref, o_ref, gcol_ref, grow_ref, bcol_ref, brow_ref, convw_ref, gout_ref,
                  out_ref, ct_ref, n_ref, m_ref, prev_ref):
    L = qk_ref.shape[0]
    H = A_HEADS
    inner = v_ref.shape[1]
    dh = inner // H

    @pl.when(pl.program_id(1) == 0)
    def _():
        ct_ref[...] = jnp.zeros_like(ct_ref)
        n_ref[...] = jnp.zeros_like(n_ref)
        m_ref[...] = jnp.zeros_like(m_ref)
        prev_ref[...] = jnp.zeros_like(prev_ref)

    row = lax.broadcasted_iota(jnp.int32, (L, L), 0)
    col = lax.broadcasted_iota(jnp.int32, (L, L), 1)
    causal = col <= row
    tril = jnp.where(causal, 1.0, 0.0).astype(BF16)
    triu = jnp.where(row <= col, 1.0, 0.0).astype(BF16)

    gc = gcol_ref[...] + brow_ref[...]
    gr = grow_ref[...] + bcol_ref[...]
    lfc1, lfc2, lfc3 = _split3(_log_sigmoid(gc))
    lfr1, lfr2, lfr3 = _split3(_log_sigmoid(gr))
    b_c = _dot(tril, lfc1) + _dot(tril, lfc2) + _dot(tril, lfc3)
    b_r = _dot(lfr1, triu) + _dot(lfr2, triu) + _dot(lfr3, triu)

    rowi = lax.broadcasted_iota(jnp.int32, (L, dh), 0)

    def conv(cur, prev, w):
        y = cur * w[A_CONV - 1:A_CONV, :]
        for s in range(1, A_CONV):
            sh = jnp.where(rowi < s, pltpu.roll(prev, s, 0), pltpu.roll(cur, s, 0))
            y = y + sh * w[A_CONV - 1 - s:A_CONV - s, :]
        return y

    for h in range(H):
        hs = slice(h * dh, (h + 1) * dh)
        ks = slice(inner + h * dh, inner + (h + 1) * dh)
        q = conv(qk_ref[:, hs].astype(F32), prev_ref[:, hs].astype(F32), convw_ref[:, hs])
        k = conv(qk_ref[:, ks].astype(F32), prev_ref[:, ks].astype(F32), convw_ref[:, ks]) * (dh ** -0.5)
        v = v_ref[:, hs]
        qb = q.astype(BF16)
        kb = k.astype(BF16)

        li_c = gc[:, h:h + 1]
        bc = b_c[:, H + h:H + h + 1]
        li_r = gr[h:h + 1, :]
        br = b_r[H + h:H + h + 1, :]
        m_prev = m_ref[h:h + 1, 0:1]

        d = jnp.where(causal, bc - br + li_r, -jnp.inf)
        inter = bc + m_prev
        m_t = jnp.maximum(inter, jnp.max(d, axis=-1, keepdims=True))
        w_inter = jnp.exp(inter - m_t)
        s = _dot_nt(qb, kb) * jnp.exp(d - m_t)
        ct = ct_ref[h]
        num = _dot(s.astype(BF16), v) + w_inter * _dot(qb, ct.astype(BF16))
        den = jnp.sum(s, axis=-1, keepdims=True) + w_inter * jnp.sum(q * n_ref[h], axis=-1, keepdims=True)
        hh = num / jnp.maximum(jnp.abs(den), jnp.exp(-m_t))
        hh = hh * lax.rsqrt(jnp.mean(hh * hh, axis=-1, keepdims=True) + RMS_EPS)
        out_ref[:, hs] = (hh * gout_ref[:, hs] * jax.nn.sigmoid(o_ref[:, hs].astype(F32))).astype(out_ref.dtype)

        b_last = bc[L - 1:L, :]
        g = b_last - bc + li_c
        m_new = jnp.maximum(b_last + m_prev, jnp.max(g, axis=0, keepdims=True))
        a_prev = jnp.exp(b_last + m_prev - m_new)
        a_s = jnp.exp(g - m_new)
        ct_ref[h] = a_prev * ct + _dot_tn(kb, (v.astype(F32) * a_s).astype(BF16))
        n_ref[h] = a_prev * n_ref[h] + jnp.sum(k * a_s, axis=0, keepdims=True)
        m_ref[h:h + 1, :] = jnp.broadcast_to(m_new, (1, LANES))

    prev_ref[...] = qk_ref[...]


def mlstm_core(proj, gcol, b_gate, conv_w, g_out, batch, seq):
    n = proj.shape[0]
    inner = proj.shape[1] // 4
    H = A_HEADS
    dh = inner // H
    L = min(MLSTM_CHUNK, seq)
    nc = seq // L
    grow = gcol[:, :2 * H].T
    brow = jnp.pad(b_gate, (0, LANES - 2 * H)).reshape(1, LANES)
    bcol = b_gate.reshape(2 * H, 1)
    return pl.pallas_call(
        _mlstm_kernel,
        grid=(batch, nc),
        in_specs=[pl.BlockSpec((L, 2 * inner), lambda b, c: (b * nc + c, 0)),
                  pl.BlockSpec((L, inner), lambda b, c: (b * nc + c, 2)),
                  pl.BlockSpec((L, inner), lambda b, c: (b * nc + c, 3)),
                  pl.BlockSpec((L, LANES), lambda b, c: (b * nc + c, 0)),
                  pl.BlockSpec((2 * H, L), lambda b, c: (0, b * nc + c)),
                  pl.BlockSpec((2 * H, 1), lambda b, c: (0, 0)),
                  pl.BlockSpec((1, LANES), lambda b, c: (0, 0)),
                  pl.BlockSpec((A_CONV, 2 * inner), lambda b, c: (0, 0)),
                  pl.BlockSpec((1, inner), lambda b, c: (0, 0))],
        out_specs=pl.BlockSpec((L, inner), lambda b, c: (b * nc + c, 0)),
        out_shape=jax.ShapeDtypeStruct((n, inner), BF16),
        scratch_shapes=[pltpu.VMEM((H, dh, dh), F32),
                        pltpu.VMEM((H, 1, dh), F32),
                        pltpu.VMEM((8, LANES), F32),
                        pltpu.VMEM((L, 2 * inner), BF16)],
        compiler_params=_cparams("parallel", "arbitrary"),
        name="mlstm_core",
    )(proj, proj, proj, gcol, grow, bcol, brow, conv_w, g_out.reshape(1, inner))


def _ffn_kernel(x_ref, g_ref, wg_ref, wu_ref, wd_ref, o_ref, xn_ref, acc_ref):
    j = pl.program_id(1)

    @pl.when(j == 0)
    def _():
        xf = x_ref[...]
        xn_ref[...] = _rms(xf, g_ref[...]).astype(BF16)
        acc_ref[...] = xf

    xn = xn_ref[...]
    a = (_silu(_dot(xn, wg_ref[...])) * _dot(xn, wu_ref[...])).astype(BF16)
    acc_ref[...] += _dot(a, wd_ref[...])

    @pl.when(j == pl.num_programs(1) - 1)
    def _():
        o_ref[...] = acc_ref[...]


def ffn_dense(x, g, w_gu, w_down, *, tm=512, tf=1408):
    n, d = x.shape
    f = w_down.shape[0]
    tm = min(tm, n)
    nf = f // tf
    return pl.pallas_call(
        _ffn_kernel,
        grid=(n // tm, nf),
        in_specs=[pl.BlockSpec((tm, d), lambda i, j: (i, 0)),
                  pl.BlockSpec((1, d), lambda i, j: (0, 0)),
                  pl.BlockSpec((d, tf), lambda i, j: (0, j)),
                  pl.BlockSpec((d, tf), lambda i, j: (0, j + nf)),
                  pl.BlockSpec((tf, d), lambda i, j: (j, 0))],
        out_specs=pl.BlockSpec((tm, d), lambda i, j: (i, 0)),
        out_shape=jax.ShapeDtypeStruct((n, d), F32),
        scratch_shapes=[pltpu.VMEM((tm, d), BF16), pltpu.VMEM((tm, d), F32)],
        compiler_params=_cparams("parallel", "arbitrary"),
        name="ffn_dense",
    )(x, g.reshape(1, d), w_gu, w_gu, w_down)


def _compress_kernel(r_ref, pos_ref, w1_ref, w2_ref, o_ref):
    r = r_ref[0, 0]
    w1 = w1_ref[0]
    half = r.shape[1]
    nc = r.shape[0]
    lo = _dot(r, w1[:half])
    hi = _dot(r, w1[half:])
    hid = lo + pltpu.roll(hi, nc - 1, 0) + _dot(pos_ref[0], w1)
    o_ref[0, 0] = _dot(_silu(hid).astype(BF16), w2_ref[0]).astype(o_ref.dtype)


def nsa_compress(r, pos, w1, w2):
    b, c2, nc, half = r.shape
    g = c2 // 2
    hidden = w1.shape[2]
    dh = w2.shape[2]
    return pl.pallas_call(
        _compress_kernel,
        grid=(b, c2),
        in_specs=[pl.BlockSpec((1, 1, nc, half), lambda i, j: (i, j, 0, 0)),
                  pl.BlockSpec((1, 1, 2 * half), lambda i, j: (j // g, 0, 0)),
                  pl.BlockSpec((1, 2 * half, hidden), lambda i, j: (j // g, 0, 0)),
                  pl.BlockSpec((1, hidden, dh), lambda i, j: (j // g, 0, 0))],
        out_specs=pl.BlockSpec((1, 1, nc, dh), lambda i, j: (i, j, 0, 0)),
        out_shape=jax.ShapeDtypeStruct((b, c2, nc, dh), F32),
        compiler_params=_cparams("parallel", "parallel"),
        name="nsa_compress",
    )(r, pos, w1, w2)


NSA_TQ = 128
NSA_TK_SLC = 512
NSA_TK_WIN = 256
NSA_ROWS = 32
SEL_MASK = 2.0 ** 14
LOG2E = float(np.log2(np.e))


def _nsa_kernel(q_ref, qf_ref, kc_ref, vc_ref, ks_ref, vs_ref, kw_ref, vw_ref, kf_ref, oh_ref, one_ref,
                gl_ref, bg_ref, ov_ref,
                out_ref, qx_ref, ksx_ref, vsx_ref, kwx_ref, vwx_ref, m_ref, acc_ref, bias_ref,
                s0_ref, p0_ref, a0_ref, s1_ref, p1_ref, a1_ref, *, tq, tks, tkw, n_top):
    R = B_REP
    M = R * tq
    dh = B_HEAD_DIM
    t0 = pl.program_id(2) * tq
    slots = ((s0_ref, p0_ref, a0_ref), (s1_ref, p1_ref, a1_ref))

    @pl.when(pl.program_id(2) == 0)
    def _():
        kf = kf_ref[...]
        ksx_ref[:, :LANES] = ks_ref[0] + kf
        ksx_ref[:, LANES:] = oh_ref[...]
        kwx_ref[...] = kw_ref[0] + kf
        vsx_ref[...] = vs_ref[0] + one_ref[...]
        vwx_ref[...] = vw_ref[0] + one_ref[...]

    qb = q_ref[0]
    for r in range(R):
        qx_ref[r * tq:(r + 1) * tq, :LANES] = qb[:, r * LANES:(r + 1) * LANES] + qf_ref[0, r]
    q = qx_ref[:, :LANES]

    kc = kc_ref[0, 0]
    ncmp = kc.shape[0]
    s = _dot_nt(q, kc)
    tpos_c = t0 + (lax.broadcasted_iota(jnp.int32, (M, ncmp), 0) & (tq - 1))
    cend = lax.broadcasted_iota(jnp.int32, (M, ncmp), 1) * CMP_STRIDE + (CMP_LEN - 1)
    ok_c = tpos_c >= cend
    s = jnp.where(ok_c, s, NEG)
    p = jnp.where(ok_c, jnp.exp2(s - jnp.max(s, axis=-1, keepdims=True)), 0.0)
    dsum = jnp.sum(p, axis=-1, keepdims=True)
    p = p / jnp.where(dsum > 0, dsum, 1.0)
    o_cmp = _dot(p.astype(BF16), vc_ref[0, 0])

    psum = p[0:tq]
    for r in range(1, R):
        psum = psum + p[r * tq:(r + 1) * tq]
    p1, p2, p3 = _split3(psum)
    ov = ov_ref[...]
    imp = _dot(p1, ov) + _dot(p2, ov) + _dot(p3, ov)
    nsel = ks_ref.shape[1] // SEL_BLOCK
    nselp = max(nsel, 8)
    imp_t = imp.T[:nselp]
    jj = lax.broadcasted_iota(jnp.int32, (nselp, tq), 0)
    tt = t0 + lax.broadcasted_iota(jnp.int32, (nselp, tq), 1)
    cur = tt // SEL_BLOCK
    forced = (jj == 0) | (jj == cur) | (jj == cur - 1)
    iv = jnp.where(forced, FORCED_SCORE, imp_t)
    iv = jnp.where(jj * SEL_BLOCK <= tt, iv, -jnp.inf)
    rank = jnp.zeros((nselp, tq), F32)
    for j2 in range(nsel):
        rv = iv[j2:j2 + 1, :]
        rank = rank + jnp.where(jj > j2, jnp.where(rv >= iv, 1.0, 0.0), jnp.where(rv > iv, 1.0, 0.0))
    unsel = jnp.where(rank < n_top, 0.0, -SEL_MASK)
    if nselp < LANES:
        unsel = jnp.concatenate([unsel, jnp.zeros((LANES - nselp, tq), F32)], axis=0)
    qmask = unsel.T.astype(BF16)
    for r in range(R):
        qx_ref[r * tq:(r + 1) * tq, LANES:] = qmask

    def tile_step(qq, k_ref, v_ref, k0, tk, slot, bias):
        s_ref, p_ref, a_ref = slots[slot]
        kk = k_ref[pl.ds(k0, tk), :]
        vv = v_ref[pl.ds(k0, tk), :]
        if bias is not None:
            bias_ref[:, :tk] = bias
        s_ref[:, :tk] = _dot_nt(qq, kk)
        for r in range(R):
            for c0 in range(0, tq, NSA_ROWS):
                rows = slice(r * tq + c0, r * tq + c0 + NSA_ROWS)
                sc = s_ref[rows, :tk]
                if bias is not None:
                    sc = sc + bias_ref[c0:c0 + NSA_ROWS, :tk]
                m_old = m_ref[rows, :]
                m_new = jnp.maximum(m_old, jnp.max(sc, axis=-1, keepdims=True))
                a_ref[rows, :] = jnp.exp2(m_old - m_new)
                m_ref[rows, :] = m_new
                p_ref[rows, :tk] = jnp.exp2(sc - jnp.concatenate([m_new] * (tk // LANES), axis=1)).astype(BF16)
        acc_ref[...] = a_ref[...] * acc_ref[...] + _dot(p_ref[:, :tk], vv)

    def reset():
        m_ref[...] = jnp.full_like(m_ref, NEG)
        acc_ref[...] = jnp.zeros_like(acc_ref)

    def result():
        acc = acc_ref[...]
        return acc[:, :dh] / acc[:, dh:dh + 1]

    reset()
    qx = qx_ref[...]
    n_full = t0 // tks

    def slc_pair(i, carry):
        tile_step(qx, ksx_ref, vsx_ref, pl.multiple_of(2 * i * tks, tks), tks, 0, None)
        tile_step(qx, ksx_ref, vsx_ref, pl.multiple_of((2 * i + 1) * tks, tks), tks, 1, None)
        return carry

    lax.fori_loop(0, n_full // 2, slc_pair, 0)

    @pl.when(n_full % 2 == 1)
    def _():
        tile_step(qx, ksx_ref, vsx_ref, pl.multiple_of((n_full - 1) * tks, tks), tks, 0, None)

    kd = pl.multiple_of(n_full * tks, tks)
    spos = kd + lax.broadcasted_iota(jnp.int32, (tq, tks), 1)
    tpos = t0 + lax.broadcasted_iota(jnp.int32, (tq, tks), 0)
    tile_step(qx, ksx_ref, vsx_ref, kd, tks, 1, jnp.where(spos <= tpos, 0.0, NEG))
    o_slc = result()

    reset()
    n_win = WINDOW // tkw + 1
    last = (t0 + tq - 1) // tkw
    for i in range(n_win):
        kw0 = (last - (n_win - 1) + i) * tkw
        spos = kw0 + lax.broadcasted_iota(jnp.int32, (tq, tkw), 1)
        dist = t0 + lax.broadcasted_iota(jnp.int32, (tq, tkw), 0) - spos
        bias = jnp.where((spos >= 0) & (dist >= 0) & (dist < WINDOW), 0.0, NEG)
        tile_step(q, kwx_ref, vwx_ref, pl.multiple_of(jnp.maximum(kw0, 0), tkw), tkw, i % 2, bias)
    o_win = result()

    gates = jax.nn.sigmoid(gl_ref[0, 0, pl.ds(pl.multiple_of(t0, tq), tq), :] + bg_ref[0])
    outs = []
    for r in range(R):
        rs = slice(r * tq, (r + 1) * tq)
        outs.append(gates[:, 3 * r:3 * r + 1] * o_cmp[rs] + gates[:, 3 * r + 1:3 * r + 2] * o_slc[rs]
                    + gates[:, 3 * r + 2:3 * r + 3] * o_win[rs])
    out_ref[0] = jnp.concatenate(outs, axis=1).astype(out_ref.dtype)


def nsa_attention(qp, q_feat, kc, vc, kvx, kv_off, key_feat, blk_onehot, ones_col, gl, bg, ov):
    b, t, _ = qp.shape
    g, r = q_feat.shape[:2]
    dh = B_HEAD_DIM
    tq = min(NSA_TQ, t)
    tks = min(NSA_TK_SLC, t)
    tkw = min(NSA_TK_WIN, t)
    tkm = max(tks, tkw)
    ncmp = kc.shape[2]
    n_top = min(SEL_TOPN, t // SEL_BLOCK)
    kv_spec = lambda c: pl.BlockSpec((1, t, LANES), lambda i, j, k: (i, 0, kv_off + c * g + j))
    const_spec = lambda a: pl.BlockSpec(a.shape, lambda i, j, k: (0,) * a.ndim)
    slot = [pltpu.VMEM((r * tq, tkm), F32), pltpu.VMEM((r * tq, tkm), BF16), pltpu.VMEM((r * tq, LANES), F32)]
    return pl.pallas_call(
        functools.partial(_nsa_kernel, tq=tq, tks=tks, tkw=tkw, n_top=n_top),
        grid=(b, g, t // tq),
        in_specs=[pl.BlockSpec((1, tq, r * LANES), lambda i, j, k: (i, k, j)),
                  pl.BlockSpec((1, r, 1, LANES), lambda i, j, k: (j, 0, 0, 0)),
                  pl.BlockSpec((1, 1, ncmp, LANES), lambda i, j, k: (i, j, 0, 0)),
                  pl.BlockSpec((1, 1, ncmp, dh), lambda i, j, k: (i, j, 0, 0)),
                  kv_spec(0), kv_spec(1), kv_spec(2), kv_spec(3),
                  const_spec(key_feat), const_spec(blk_onehot), const_spec(ones_col),
                  pl.BlockSpec((1, 1, t, 3 * r), lambda i, j, k: (i, j, 0, 0)),
                  pl.BlockSpec((1, 1, 3 * r), lambda i, j, k: (j, 0, 0)),
                  const_spec(ov)],
        out_specs=pl.BlockSpec((1, tq, r * dh), lambda i, j, k: (i, k, j)),
        out_shape=jax.ShapeDtypeStruct((b, t, g * r * dh), BF16),
        scratch_shapes=[pltpu.VMEM((r * tq, 2 * LANES), BF16),
                        pltpu.VMEM((t, 2 * LANES), BF16),
                        pltpu.VMEM((t, LANES), BF16),
                        pltpu.VMEM((t, LANES), BF16),
                        pltpu.VMEM((t, LANES), BF16),
                        pltpu.VMEM((r * tq, LANES), F32),
                        pltpu.VMEM((r * tq, LANES), F32),
                        pltpu.VMEM((tq, tkm), F32)]
                       + slot + slot,
        compiler_params=_cparams("arbitrary", "arbitrary", "arbitrary"),
        name="nsa_attention",
    )(qp, q_feat, kc, vc, kvx, kvx, kvx, kvx, key_feat, blk_onehot, ones_col, gl, bg, ov)


def _alibi_slopes(n):
    return np.power(2.0, -8.0 * np.arange(1, n + 1) / n).astype(np.float32)


def _np_split3(a):
    a = np.asarray(a, np.float32)
    out = []
    r = a
    for _ in range(3):
        p = r.astype(BF16).astype(np.float32)
        out.append(p)
        r = (r - p).astype(np.float32)
    return out


def _pos_features(pos, width):
    hi = (pos // 64).astype(np.float32)
    lo = (pos % 64).astype(np.float32)
    f = np.zeros((pos.shape[0], width), np.float32)
    f[:, 0:3] = hi[:, None]
    f[:, 3:6] = lo[:, None]
    return f


def _slope_features(width):
    s1, s2, s3 = _np_split3((_alibi_slopes(B_HEADS).astype(np.float64) * LOG2E).astype(np.float32))
    f = np.zeros((B_HEADS, width), np.float32)
    for i, s in enumerate((s1, s2, s3)):
        f[:, i] = 64.0 * s
        f[:, 3 + i] = s
    return f


META_I1, META_I2, META_W1, META_W2 = 0, 1, 2, 3


def _router_kernel(x_ref, g_ref, w_ref, xn_ref, meta_ref, pos_ref, cnt_ref, run_ref):
    @pl.when(pl.program_id(0) == 0)
    def _():
        run_ref[...] = jnp.zeros_like(run_ref)

    xn = _rms(x_ref[...], g_ref[...])
    xn_ref[...] = xn
    logits = _dot_f32(xn, w_ref[...])
    tm = logits.shape[0]
    lane = lax.broadcasted_iota(jnp.int32, logits.shape, 1)
    logits = jnp.where(lane < N_EXPERTS, logits, -jnp.inf)
    m1 = jnp.max(logits, axis=-1, keepdims=True)
    i1 = jnp.min(jnp.where(logits == m1, lane, LANES), axis=-1, keepdims=True)
    rest = jnp.where(lane == i1, -jnp.inf, logits)
    m2 = jnp.max(rest, axis=-1, keepdims=True)
    i2 = jnp.min(jnp.where(rest == m2, lane, LANES), axis=-1, keepdims=True)
    e2 = jnp.exp(m2 - m1)
    w1 = 1.0 / (1.0 + e2)
    meta = jnp.where(lane == META_I1, i1.astype(F32), 0.0)
    meta = jnp.where(lane == META_I2, i2.astype(F32), meta)
    meta = jnp.where(lane == META_W1, w1, meta)
    meta_ref[...] = jnp.where(lane == META_W2, e2 * w1, meta)

    sel = jnp.where((lane == i1) | (lane == i2), 1.0, 0.0)
    row = lax.broadcasted_iota(jnp.int32, (tm, tm), 0)
    col = lax.broadcasted_iota(jnp.int32, (tm, tm), 1)
    before = jnp.where(col < row, 1.0, 0.0).astype(BF16)
    run = run_ref[...]
    pos_ref[...] = run + _dot(before, sel.astype(BF16))
    run = run + jnp.sum(sel, axis=0, keepdims=True)
    run_ref[...] = run
    cnt_ref[...] = run


def moe_router(x, g, w_router, *, tm=512):
    n, d = x.shape
    tm = min(tm, n)
    w = jnp.pad(w_router, ((0, 0), (0, LANES - w_router.shape[1])))
    row_spec = pl.BlockSpec((tm, LANES), lambda i: (i, 0))
    return pl.pallas_call(
        _router_kernel,
        grid=(n // tm,),
        in_specs=[pl.BlockSpec((tm, d), lambda i: (i, 0)),
                  pl.BlockSpec((1, d), lambda i: (0, 0)),
                  pl.BlockSpec((d, LANES), lambda i: (0, 0))],
        out_specs=[pl.BlockSpec((tm, d), lambda i: (i, 0)), row_spec, row_spec,
                   pl.BlockSpec((1, LANES), lambda i: (0, 0))],
        out_shape=[jax.ShapeDtypeStruct((n, d), F32), jax.ShapeDtypeStruct((n, LANES), F32),
                   jax.ShapeDtypeStruct((n, LANES), F32), jax.ShapeDtypeStruct((1, LANES), F32)],
        scratch_shapes=[pltpu.VMEM((1, LANES), F32)],
        compiler_params=_cparams("arbitrary"),
        name="moe_router",
    )(x, g.reshape(1, d), w)


MOE_TILE = 512


def _gather_rows_kernel(idx_ref, src_ref, o_ref, sem):
    rows = o_ref.shape[0]

    def row_copy(r):
        return pltpu.make_async_copy(src_ref.at[pl.ds(idx_ref[0, 0, r], 1), :], o_ref.at[pl.ds(r, 1), :], sem)

    def start(r, c):
        row_copy(r).start()
        return c

    def wait(r, c):
        row_copy(r).wait()
        return c

    lax.fori_loop(0, rows, start, 0)
    lax.fori_loop(0, rows, wait, 0)


def gather_rows(src, idx, *, tg=MOE_TILE):
    p = idx.shape[0]
    d = src.shape[1]
    return pl.pallas_call(
        _gather_rows_kernel,
        grid=(p // tg,),
        in_specs=[pl.BlockSpec((1, 1, tg), lambda i: (i, 0, 0), memory_space=pltpu.SMEM),
                  pl.BlockSpec(memory_space=pl.ANY)],
        out_specs=pl.BlockSpec((tg, d), lambda i: (i, 0)),
        out_shape=jax.ShapeDtypeStruct((p, d), src.dtype),
        scratch_shapes=[pltpu.SemaphoreType.DMA(())],
        compiler_params=_cparams("parallel"),
        name="moe_gather",
    )(idx.reshape(p // tg, 1, tg), src)


def _moe_kernel(te_ref, ok_ref, x_ref, wg_ref, wu_ref, wd_ref, o_ref, xb_ref):
    i = pl.program_id(0)
    j = pl.program_id(1)

    @pl.when(j == 0)
    def _():
        xb_ref[...] = x_ref[...].astype(BF16)
        o_ref[...] = jnp.zeros_like(o_ref)

    @pl.when(ok_ref[i] > 0)
    def _():
        xb = xb_ref[...]
        a = (_silu(_dot(xb, wg_ref[0])) * _dot(xb, wu_ref[0])).astype(BF16)
        o_ref[...] += _dot(a, wd_ref[0])


def moe_experts(xs, tile_expert, tile_ok, w_gu, w_down, *, tf=1792):
    p, d = xs.shape
    ne, f, _ = w_down.shape
    tm = MOE_TILE
    nf = f // tf
    grid_spec = pltpu.PrefetchScalarGridSpec(
        num_scalar_prefetch=2,
        grid=(p // tm, nf),
        in_specs=[pl.BlockSpec((tm, d), lambda i, j, te, ok: (i, 0)),
                  pl.BlockSpec((1, d, tf), lambda i, j, te, ok: (te[i], 0, j)),
                  pl.BlockSpec((1, d, tf), lambda i, j, te, ok: (te[i], 0, j + nf)),
                  pl.BlockSpec((1, tf, d), lambda i, j, te, ok: (te[i], j, 0))],
        out_specs=pl.BlockSpec((tm, d), lambda i, j, te, ok: (i, 0)),
        scratch_shapes=[pltpu.VMEM((tm, d), BF16)])
    return pl.pallas_call(
        _moe_kernel,
        grid_spec=grid_spec,
        out_shape=jax.ShapeDtypeStruct((p, d), F32),
        compiler_params=_cparams("parallel", "arbitrary"),
        name="moe_experts",
    )(tile_expert, tile_ok, xs, w_gu, w_gu, w_down)


def _combine_kernel(d1_ref, d2_ref, x_ref, meta_ref, y_ref, gf_ref, o_ref, y1_ref, y2_ref, sem1, sem2):
    rows = o_ref.shape[0]

    def copy1(r):
        return pltpu.make_async_copy(y_ref.at[pl.ds(d1_ref[0, 0, r], 1), :], y1_ref.at[pl.ds(r, 1), :], sem1)

    def copy2(r):
        return pltpu.make_async_copy(y_ref.at[pl.ds(d2_ref[0, 0, r], 1), :], y2_ref.at[pl.ds(r, 1), :], sem2)

    def start(r, c):
        copy1(r).start()
        copy2(r).start()
        return c

    def wait(r, c):
        copy1(r).wait()
        copy2(r).wait()
        return c

    lax.fori_loop(0, rows, start, 0)
    lax.fori_loop(0, rows, wait, 0)
    meta = meta_ref[...]
    w1 = meta[:, META_W1:META_W1 + 1]
    w2 = meta[:, META_W2:META_W2 + 1]
    o_ref[...] = _rms(x_ref[...] + w1 * y1_ref[...] + w2 * y2_ref[...], gf_ref[...])


def moe_combine(x, meta, ys, d1, d2, g_final, *, tc=256):
    n, d = x.shape
    tc = min(tc, n)
    idx_spec = pl.BlockSpec((1, 1, tc), lambda i: (i, 0, 0), memory_space=pltpu.SMEM)
    return pl.pallas_call(
        _combine_kernel,
        grid=(n // tc,),
        in_specs=[idx_spec, idx_spec,
                  pl.BlockSpec((tc, d), lambda i: (i, 0)),
                  pl.BlockSpec((tc, LANES), lambda i: (i, 0)),
                  pl.BlockSpec(memory_space=pl.ANY),
                  pl.BlockSpec((1, d), lambda i: (0, 0))],
        out_specs=pl.BlockSpec((tc, d), lambda i: (i, 0)),
        out_shape=jax.ShapeDtypeStruct((n, d), F32),
        scratch_shapes=[pltpu.VMEM((tc, d), F32), pltpu.VMEM((tc, d), F32),
                        pltpu.SemaphoreType.DMA(()), pltpu.SemaphoreType.DMA(())],
        compiler_params=_cparams("parallel"),
        name="moe_combine",
    )(d1.reshape(n // tc, 1, tc), d2.reshape(n // tc, 1, tc), x, meta, ys, g_final.reshape(1, d))


def kernel(x, norm_mix, norm_ffn, a_w_in, a_b_gate, a_conv, a_norm_h, a_w_out, norm_kv, b_w_kv,
           b_cmp_pos, b_cmp_w1, b_cmp_w2, b_w_q, b_b_gate, b_w_out, f_w_gu, f_w_down,
           m_router, m_w_gu, m_w_down, norm_final):
    B, T, D = x.shape
    N = B * T
    G, R, dh = B_KV_GROUPS, B_REP, B_HEAD_DIM
    xs = x.reshape(N, D)

    inner4 = a_w_in.shape[2] - 2 * A_HEADS
    w_in = a_w_in[0]
    proj = norm_matmul(xs, norm_mix[0], w_in[:, :inner4].astype(BF16), out_dtype=BF16)
    w_gate = jnp.pad(w_in[:, inner4:], ((0, 0), (0, LANES - 2 * A_HEADS)))
    gcol = norm_matmul(xs, norm_mix[0], w_gate, out_dtype=F32, exact=True, tn=LANES)
    hs = mlstm_core(proj, gcol, a_b_gate[0], a_conv[0], a_norm_h[0], B, T)
    xs = matmul_residual(hs, a_w_out[0].astype(BF16), xs)
    xs = ffn_dense(xs, norm_ffn[0], f_w_gu[0].astype(BF16), f_w_down[0].astype(BF16))

    hd = B_HEADS * dh
    feat_w = LANES - dh
    ncmp = T // CMP_STRIDE

    def slabs(w):
        d_in, cols = w.shape
        return jnp.pad(w.reshape(d_in, cols // dh, dh), ((0, 0), (0, 0), (0, feat_w))).reshape(d_in, -1)

    cmp_cols = 2 * G * dh
    w_kvx = jnp.concatenate([b_w_kv[:, :cmp_cols], slabs(b_w_kv[:, cmp_cols:])], axis=1).astype(BF16)
    kvx = norm_matmul(xs, norm_kv, w_kvx, out_dtype=BF16, tn=w_kvx.shape[1] // 2)
    w_q = b_w_q[0]
    qp = norm_matmul(xs, norm_mix[1], slabs(w_q[:, :hd] * (dh ** -0.5 * LOG2E)).astype(BF16), out_dtype=BF16)
    w_qg = jnp.pad(w_q[:, hd:], ((0, 0), (0, LANES - 3 * B_HEADS)))
    gl = norm_matmul(xs, norm_mix[1], w_qg, out_dtype=F32, exact=True, tn=LANES)

    kvt = kvx[:, :cmp_cols].reshape(B, T, 2 * G, dh).transpose(0, 2, 1, 3)
    pos = b_cmp_pos.transpose(1, 0, 2).reshape(2, 1, CMP_LEN * dh).astype(BF16)
    kvc = nsa_compress(kvt.reshape(B, 2 * G, ncmp, CMP_STRIDE * dh), pos,
                       b_cmp_w1.astype(BF16), b_cmp_w2.astype(BF16))

    def hi_lanes(f):
        return jnp.asarray(np.concatenate([np.zeros((f.shape[0], dh), np.float32), f], axis=1), BF16)

    key_feat = hi_lanes(_pos_features(np.arange(T), feat_w))
    cmp_feat = jnp.asarray(_pos_features(np.arange(ncmp) * CMP_STRIDE + CMP_LEN - 1, feat_w), BF16)
    q_feat = hi_lanes(_slope_features(feat_w)).reshape(G, R, 1, LANES)
    ones_col = np.zeros((1, feat_w), np.float32)
    ones_col[0, 0] = 1.0
    ones_col = hi_lanes(ones_col)
    blk_onehot = (np.arange(T)[:, None] // SEL_BLOCK == np.arange(LANES)[None, :]).astype(np.float32)
    blk_onehot = jnp.asarray(blk_onehot, BF16)
    kc = jnp.concatenate([kvc[:, :G].astype(BF16), jnp.broadcast_to(cmp_feat, (B, G, ncmp, feat_w))], axis=-1)
    vc = kvc[:, G:].astype(BF16)
    glt = gl[:, :3 * B_HEADS].reshape(B, T, G, 3 * R).transpose(0, 2, 1, 3)
    bg = b_b_gate[0].reshape(G, 1, 3 * R)

    nsel = T // SEL_BLOCK
    ci = np.arange(ncmp)[:, None] * CMP_STRIDE
    sj = np.arange(LANES)[None, :] * SEL_BLOCK
    ov = ((ci < sj + SEL_BLOCK) & (ci + CMP_LEN > sj) & (np.arange(LANES)[None, :] < nsel)
          & (np.arange(ncmp)[:, None] < ncmp - 1))
    ov = jnp.asarray(ov.astype(np.float32), BF16)

    oa = nsa_attention(qp.reshape(B, T, B_HEADS * LANES), q_feat, kc, vc, kvx.reshape(B, T, -1),
                       cmp_cols // LANES, key_feat, blk_onehot, ones_col, glt, bg, ov)
    xs = matmul_residual(oa.reshape(N, hd), b_w_out[0].astype(BF16), xs)

    xn, meta, pos, cnt = moe_router(xs, norm_ffn[1], m_router[0])
    ne = N_EXPERTS
    p_rows = 2 * N + ne * MOE_TILE
    i1 = meta[:, META_I1].astype(jnp.int32)
    i2 = meta[:, META_I2].astype(jnp.int32)
    counts = cnt[0, :ne].astype(jnp.int32)
    padded = (counts + MOE_TILE - 1) // MOE_TILE * MOE_TILE
    seg_end = jnp.cumsum(padded)
    seg_start = seg_end - padded
    pos8 = pos[:, :ne].astype(jnp.int32)
    d1 = seg_start[i1] + jnp.take_along_axis(pos8, i1[:, None], axis=1)[:, 0]
    d2 = seg_start[i2] + jnp.take_along_axis(pos8, i2[:, None], axis=1)[:, 0]
    tok = jnp.arange(N, dtype=jnp.int32)
    row_tok = jnp.zeros((p_rows,), jnp.int32).at[d1].set(tok).at[d2].set(tok)
    tile_start = jnp.arange(p_rows // MOE_TILE, dtype=jnp.int32) * MOE_TILE
    tile_expert = jnp.minimum(jnp.sum(tile_start[:, None] >= seg_end[None, :], axis=1), ne - 1).astype(jnp.int32)
    tile_ok = (tile_start < seg_end[ne - 1]).astype(jnp.int32)

    xsort = gather_rows(xn, row_tok)
    ys = moe_experts(xsort, tile_expert, tile_ok, m_w_gu[0].astype(BF16), m_w_down[0].astype(BF16))
    out = moe_combine(xs, meta, ys, d1, d2, norm_final)
    return out.reshape(B, T, D)
```

```python
import functools

import numpy as np
import jax
import jax.numpy as jnp
from jax import lax
from jax.experimental import pallas as pl
from jax.experimental.pallas import tpu as pltpu

F32 = jnp.float32
BF16 = jnp.bfloat16

RMS_EPS = 1e-6
A_HEADS = 4
A_CONV = 4
B_HEADS = 16
B_KV_GROUPS = 4
B_REP = B_HEADS // B_KV_GROUPS
B_HEAD_DIM = 64
CMP_LEN = 32
CMP_STRIDE = 16
SEL_BLOCK = 64
SEL_TOPN = 16
WINDOW = 512
FORCED_SCORE = 1e4
N_EXPERTS = 8

LANES = 128
V7X_VMEM_BYTES = 64 * 1024 * 1024
VMEM_LIMIT = V7X_VMEM_BYTES - 8 * 1024 * 1024
NEG = -1e30


def _cparams(*sem):
    return pltpu.CompilerParams(dimension_semantics=sem, vmem_limit_bytes=VMEM_LIMIT)


def _dot(a, b):
    return jnp.dot(a, b, preferred_element_type=F32)


def _dot_nt(a, b):
    return lax.dot_general(a, b, (((1,), (1,)), ((), ())), preferred_element_type=F32)


def _dot_tn(a, b):
    return lax.dot_general(a, b, (((0,), (0,)), ((), ())), preferred_element_type=F32)


def _split3(a):
    a1 = a.astype(BF16)
    r1 = a - a1.astype(F32)
    a2 = r1.astype(BF16)
    a3 = (r1 - a2.astype(F32)).astype(BF16)
    return a1, a2, a3


def _dot_f32(a, b):
    a1, a2, a3 = _split3(a)
    b1, b2, b3 = _split3(b)
    return (_dot(a1, b1) + _dot(a1, b2) + _dot(a2, b1)
            + _dot(a2, b2) + _dot(a1, b3) + _dot(a3, b1))


def _dot_f32_nt(a, b):
    a1, a2, a3 = _split3(a)
    b1, b2, b3 = _split3(b)
    return (_dot_nt(a1, b1) + _dot_nt(a1, b2) + _dot_nt(a2, b1)
            + _dot_nt(a2, b2) + _dot_nt(a1, b3) + _dot_nt(a3, b1))


def _rms(xf, g):
    return xf * lax.rsqrt(jnp.mean(xf * xf, axis=-1, keepdims=True) + RMS_EPS) * g


def _silu(x):
    return x * jax.nn.sigmoid(x)


def _log_sigmoid(x):
    return jnp.minimum(x, 0.0) - jnp.log(1.0 + jnp.exp(-jnp.abs(x)))


def _norm_matmul_kernel(x_ref, g_ref, w_ref, o_ref, xn_ref, *, exact):
    @pl.when(pl.program_id(1) == 0)
    def _():
        xn_ref[...] = _rms(x_ref[...], g_ref[...]).astype(xn_ref.dtype)

    if exact:
        o_ref[...] = _dot_f32(xn_ref[...], w_ref[...]).astype(o_ref.dtype)
    else:
        o_ref[...] = _dot(xn_ref[...], w_ref[...]).astype(o_ref.dtype)


def norm_matmul(x, g, w, *, out_dtype, exact=False, tm=1024, tn=1024):
    n, d = x.shape
    dout = w.shape[1]
    tm = min(tm, n)
    tn = min(tn, dout)
    return pl.pallas_call(
        functools.partial(_norm_matmul_kernel, exact=exact),
        grid=(n // tm, dout // tn),
        in_specs=[pl.BlockSpec((tm, d), lambda i, j: (i, 0)),
                  pl.BlockSpec((1, d), lambda i, j: (0, 0)),
                  pl.BlockSpec((d, tn), lambda i, j: (0, j))],
        out_specs=pl.BlockSpec((tm, tn), lambda i, j: (i, j)),
        out_shape=jax.ShapeDtypeStruct((n, dout), out_dtype),
        scratch_shapes=[pltpu.VMEM((tm, d), F32 if exact else BF16)],
        compiler_params=_cparams("parallel", "arbitrary"),
        name="norm_matmul_f32" if exact else "norm_matmul",
    )(x, g.reshape(1, d), w)


def _matmul_res_kernel(a_ref, w_ref, r_ref, o_ref):
    o_ref[...] = r_ref[...] + _dot(a_ref[...], w_ref[...])


def matmul_residual(a, w, res, *, tm=1024):
    n, k = a.shape
    dout = w.shape[1]
    tm = min(tm, n)
    return pl.pallas_call(
        _matmul_res_kernel,
        grid=(n // tm,),
        in_specs=[pl.BlockSpec((tm, k), lambda i: (i, 0)),
                  pl.BlockSpec((k, dout), lambda i: (0, 0)),
                  pl.BlockSpec((tm, dout), lambda i: (i, 0))],
        out_specs=pl.BlockSpec((tm, dout), lambda i: (i, 0)),
        out_shape=jax.ShapeDtypeStruct((n, dout), F32),
        compiler_params=_cparams("parallel"),
        name="matmul_residual",
    )(a, w, res)


MLSTM_CHUNK = 256


def _mlstm_kernel(qk_ref, v_ref, o_ref, gcol_ref, grow_ref, bcol_ref, brow_ref, convw_ref, gout_ref,
                  out_ref, ct_ref, n_ref, m_ref, prev_ref):
    L = qk_ref.shape[0]
    H = A_HEADS
    inner = v_ref.shape[1]
    dh = inner // H

    @pl.when(pl.program_id(1) == 0)
    def _():
        ct_ref[...] = jnp.zeros_like(ct_ref)
        n_ref[...] = jnp.zeros_like(n_ref)
        m_ref[...] = jnp.zeros_like(m_ref)
        prev_ref[...] = jnp.zeros_like(prev_ref)

    row = lax.broadcasted_iota(jnp.int32, (L, L), 0)
    col = lax.broadcasted_iota(jnp.int32, (L, L), 1)
    causal = col <= row
    tril = jnp.where(causal, 1.0, 0.0).astype(BF16)
    triu = jnp.where(row <= col, 1.0, 0.0).astype(BF16)

    gc = gcol_ref[...] + brow_ref[...]
    gr = grow_ref[...] + bcol_ref[...]
    lfc1, lfc2, lfc3 = _split3(_log_sigmoid(gc))
    lfr1, lfr2, lfr3 = _split3(_log_sigmoid(gr))
    b_c = _dot(tril, lfc1) + _dot(tril, lfc2) + _dot(tril, lfc3)
    b_r = _dot(lfr1, triu) + _dot(lfr2, triu) + _dot(lfr3, triu)

    rowi = lax.broadcasted_iota(jnp.int32, (L, dh), 0)

    def conv(cur, prev, w):
        y = cur * w[A_CONV - 1:A_CONV, :]
        for s in range(1, A_CONV):
            sh = jnp.where(rowi < s, pltpu.roll(prev, s, 0), pltpu.roll(cur, s, 0))
            y = y + sh * w[A_CONV - 1 - s:A_CONV - s, :]
        return y

    for h in range(H):
        hs = slice(h * dh, (h + 1) * dh)
        ks = slice(inner + h * dh, inner + (h + 1) * dh)
        q = conv(qk_ref[:, hs].astype(F32), prev_ref[:, hs].astype(F32), convw_ref[:, hs])
        k = conv(qk_ref[:, ks].astype(F32), prev_ref[:, ks].astype(F32), convw_ref[:, ks]) * (dh ** -0.5)
        v = v_ref[:, hs]
        qb = q.astype(BF16)
        kb = k.astype(BF16)

        li_c = gc[:, h:h + 1]
        bc = b_c[:, H + h:H + h + 1]
        li_r = gr[h:h + 1, :]
        br = b_r[H + h:H + h + 1, :]
        m_prev = m_ref[h:h + 1, 0:1]

        d = jnp.where(causal, bc - br + li_r, -jnp.inf)
        inter = bc + m_prev
        m_t = jnp.maximum(inter, jnp.max(d, axis=-1, keepdims=True))
        w_inter = jnp.exp(inter - m_t)
        s = _dot_nt(qb, kb) * jnp.exp(d - m_t)
        ct = ct_ref[h]
        num = _dot(s.astype(BF16), v) + w_inter * _dot(qb, ct.astype(BF16))
        den = jnp.sum(s, axis=-1, keepdims=True) + w_inter * jnp.sum(q * n_ref[h], axis=-1, keepdims=True)
        hh = num / jnp.maximum(jnp.abs(den), jnp.exp(-m_t))
        hh = hh * lax.rsqrt(jnp.mean(hh * hh, axis=-1, keepdims=True) + RMS_EPS)
        out_ref[:, hs] = (hh * gout_ref[:, hs] * jax.nn.sigmoid(o_ref[:, hs].astype(F32))).astype(out_ref.dtype)

        b_last = bc[L - 1:L, :]
        g = b_last - bc + li_c
        m_new = jnp.maximum(b_last + m_prev, jnp.max(g, axis=0, keepdims=True))
        a_prev = jnp.exp(b_last + m_prev - m_new)
        a_s = jnp.exp(g - m_new)
        ct_ref[h] = a_prev * ct + _dot_tn(kb, (v.astype(F32) * a_s).astype(BF16))
        n_ref[h] = a_prev * n_ref[h] + jnp.sum(k * a_s, axis=0, keepdims=True)
        m_ref[h:h + 1, :] = jnp.broadcast_to(m_new, (1, LANES))

    prev_ref[...] = qk_ref[...]


def mlstm_core(proj, gcol, b_gate, conv_w, g_out, batch, seq):
    n = proj.shape[0]
    inner = proj.shape[1] // 4
    H = A_HEADS
    dh = inner // H
    L = min(MLSTM_CHUNK, seq)
    nc = seq // L
    grow = gcol[:, :2 * H].T
    brow = jnp.pad(b_gate, (0, LANES - 2 * H)).reshape(1, LANES)
    bcol = b_gate.reshape(2 * H, 1)
    return pl.pallas_call(
        _mlstm_kernel,
        grid=(batch, nc),
        in_specs=[pl.BlockSpec((L, 2 * inner), lambda b, c: (b * nc + c, 0)),
                  pl.BlockSpec((L, inner), lambda b, c: (b * nc + c, 2)),
                  pl.BlockSpec((L, inner), lambda b, c: (b * nc + c, 3)),
                  pl.BlockSpec((L, LANES), lambda b, c: (b * nc + c, 0)),
                  pl.BlockSpec((2 * H, L), lambda b, c: (0, b * nc + c)),
                  pl.BlockSpec((2 * H, 1), lambda b, c: (0, 0)),
                  pl.BlockSpec((1, LANES), lambda b, c: (0, 0)),
                  pl.BlockSpec((A_CONV, 2 * inner), lambda b, c: (0, 0)),
                  pl.BlockSpec((1, inner), lambda b, c: (0, 0))],
        out_specs=pl.BlockSpec((L, inner), lambda b, c: (b * nc + c, 0)),
        out_shape=jax.ShapeDtypeStruct((n, inner), BF16),
        scratch_shapes=[pltpu.VMEM((H, dh, dh), F32),
                        pltpu.VMEM((H, 1, dh), F32),
                        pltpu.VMEM((8, LANES), F32),
                        pltpu.VMEM((L, 2 * inner), BF16)],
        compiler_params=_cparams("parallel", "arbitrary"),
        name="mlstm_core",
    )(proj, proj, proj, gcol, grow, bcol, brow, conv_w, g_out.reshape(1, inner))


def _ffn_kernel(x_ref, g_ref, wg_ref, wu_ref, wd_ref, o_ref, xn_ref, acc_ref):
    j = pl.program_id(1)

    @pl.when(j == 0)
    def _():
        xf = x_ref[...]
        xn_ref[...] = _rms(xf, g_ref[...]).astype(BF16)
        acc_ref[...] = xf

    xn = xn_ref[...]
    a = (_silu(_dot(xn, wg_ref[...])) * _dot(xn, wu_ref[...])).astype(BF16)
    acc_ref[...] += _dot(a, wd_ref[...])

    @pl.when(j == pl.num_programs(1) - 1)
    def _():
        o_ref[...] = acc_ref[...]


def ffn_dense(x, g, w_gu, w_down, *, tm=512, tf=1408):
    n, d = x.shape
    f = w_down.shape[0]
    tm = min(tm, n)
    nf = f // tf
    return pl.pallas_call(
        _ffn_kernel,
        grid=(n // tm, nf),
        in_specs=[pl.BlockSpec((tm, d), lambda i, j: (i, 0)),
                  pl.BlockSpec((1, d), lambda i, j: (0, 0)),
                  pl.BlockSpec((d, tf), lambda i, j: (0, j)),
                  pl.BlockSpec((d, tf), lambda i, j: (0, j + nf)),
                  pl.BlockSpec((tf, d), lambda i, j: (j, 0))],
        out_specs=pl.BlockSpec((tm, d), lambda i, j: (i, 0)),
        out_shape=jax.ShapeDtypeStruct((n, d), F32),
        scratch_shapes=[pltpu.VMEM((tm, d), BF16), pltpu.VMEM((tm, d), F32)],
        compiler_params=_cparams("parallel", "arbitrary"),
        name="ffn_dense",
    )(x, g.reshape(1, d), w_gu, w_gu, w_down)


def _compress_kernel(r_ref, pos_ref, w1_ref, w2_ref, o_ref):
    r = r_ref[0, 0]
    w1 = w1_ref[0]
    half = r.shape[1]
    nc = r.shape[0]
    lo = _dot(r, w1[:half])
    hi = _dot(r, w1[half:])
    hid = lo + pltpu.roll(hi, nc - 1, 0) + _dot(pos_ref[0], w1)
    o_ref[0, 0] = _dot(_silu(hid).astype(BF16), w2_ref[0]).astype(o_ref.dtype)


def nsa_compress(r, pos, w1, w2):
    b, c2, nc, half = r.shape
    g = c2 // 2
    hidden = w1.shape[2]
    dh = w2.shape[2]
    return pl.pallas_call(
        _compress_kernel,
        grid=(b, c2),
        in_specs=[pl.BlockSpec((1, 1, nc, half), lambda i, j: (i, j, 0, 0)),
                  pl.BlockSpec((1, 1, 2 * half), lambda i, j: (j // g, 0, 0)),
                  pl.BlockSpec((1, 2 * half, hidden), lambda i, j: (j // g, 0, 0)),
                  pl.BlockSpec((1, hidden, dh), lambda i, j: (j // g, 0, 0))],
        out_specs=pl.BlockSpec((1, 1, nc, dh), lambda i, j: (i, j, 0, 0)),
        out_shape=jax.ShapeDtypeStruct((b, c2, nc, dh), F32),
        compiler_params=_cparams("parallel", "parallel"),
        name="nsa_compress",
    )(r, pos, w1, w2)


NSA_TQ = 128
NSA_TK_SLC = 512
NSA_TK_WIN = 256
NSA_ROWS = 32
SEL_MASK = 2.0 ** 14
LOG2E = float(np.log2(np.e))


def _nsa_kernel(q_ref, qf_ref, kc_ref, vc_ref, ks_ref, vs_ref, kw_ref, vw_ref, kf_ref, oh_ref, one_ref,
                gl_ref, bg_ref, ov_ref,
                out_ref, qx_ref, ksx_ref, vsx_ref, kwx_ref, vwx_ref, m_ref, acc_ref, bias_ref,
                s0_ref, p0_ref, a0_ref, s1_ref, p1_ref, a1_ref, *, tq, tks, tkw, n_top):
    R = B_REP
    M = R * tq
    dh = B_HEAD_DIM
    t0 = pl.program_id(2) * tq
    slots = ((s0_ref, p0_ref, a0_ref), (s1_ref, p1_ref, a1_ref))

    @pl.when(pl.program_id(2) == 0)
    def _():
        kf = kf_ref[...]
        ksx_ref[:, :LANES] = ks_ref[0] + kf
        ksx_ref[:, LANES:] = oh_ref[...]
        kwx_ref[...] = kw_ref[0] + kf
        vsx_ref[...] = vs_ref[0] + one_ref[...]
        vwx_ref[...] = vw_ref[0] + one_ref[...]

    qb = q_ref[0]
    for r in range(R):
        qx_ref[r * tq:(r + 1) * tq, :LANES] = qb[:, r * LANES:(r + 1) * LANES] + qf_ref[0, r]
    q = qx_ref[:, :LANES]

    kc = kc_ref[0, 0]
    ncmp = kc.shape[0]
    s = _dot_nt(q, kc)
    tpos_c = t0 + (lax.broadcasted_iota(jnp.int32, (M, ncmp), 0) & (tq - 1))
    cend = lax.broadcasted_iota(jnp.int32, (M, ncmp), 1) * CMP_STRIDE + (CMP_LEN - 1)
    ok_c = tpos_c >= cend
    s = jnp.where(ok_c, s, NEG)
    p = jnp.where(ok_c, jnp.exp2(s - jnp.max(s, axis=-1, keepdims=True)), 0.0)
    dsum = jnp.sum(p, axis=-1, keepdims=True)
    p = p / jnp.where(dsum > 0, dsum, 1.0)
    o_cmp = _dot(p.astype(BF16), vc_ref[0, 0])

    psum = p[0:tq]
    for r in range(1, R):
        psum = psum + p[r * tq:(r + 1) * tq]
    p1, p2, p3 = _split3(psum)
    ov = ov_ref[...]
    imp = _dot(p1, ov) + _dot(p2, ov) + _dot(p3, ov)
    nsel = ks_ref.shape[1] // SEL_BLOCK
    nselp = max(nsel, 8)
    imp_t = imp.T[:nselp]
    jj = lax.broadcasted_iota(jnp.int32, (nselp, tq), 0)
    tt = t0 + lax.broadcasted_iota(jnp.int32, (nselp, tq), 1)
    cur = tt // SEL_BLOCK
    forced = (jj == 0) | (jj == cur) | (jj == cur - 1)
    iv = jnp.where(forced, FORCED_SCORE, imp_t)
    iv = jnp.where(jj * SEL_BLOCK <= tt, iv, -jnp.inf)
    rank = jnp.zeros((nselp, tq), F32)
    for j2 in range(nsel):
        rv = iv[j2:j2 + 1, :]
        rank = rank + jnp.where(jj > j2, jnp.where(rv >= iv, 1.0, 0.0), jnp.where(rv > iv, 1.0, 0.0))
    unsel = jnp.where(rank < n_top, 0.0, -SEL_MASK)
    if nselp < LANES:
        unsel = jnp.concatenate([unsel, jnp.zeros((LANES - nselp, tq), F32)], axis=0)
    qmask = unsel.T.astype(BF16)
    for r in range(R):
        qx_ref[r * tq:(r + 1) * tq, LANES:] = qmask

    def tile_step(qq, k_ref, v_ref, k0, tk, slot, bias):
        s_ref, p_ref, a_ref = slots[slot]
        kk = k_ref[pl.ds(k0, tk), :]
        vv = v_ref[pl.ds(k0, tk), :]
        if bias is not None:
            bias_ref[:, :tk] = bias
        s_ref[:, :tk] = _dot_nt(qq, kk)
        for r in range(R):
            for c0 in range(0, tq, NSA_ROWS):
                rows = slice(r * tq + c0, r * tq + c0 + NSA_ROWS)
                sc = s_ref[rows, :tk]
                if bias is not None:
                    sc = sc + bias_ref[c0:c0 + NSA_ROWS, :tk]
                m_old = m_ref[rows, :]
                m_new = jnp.maximum(m_old, jnp.max(sc, axis=-1, keepdims=True))
                a_ref[rows, :] = jnp.exp2(m_old - m_new)
                m_ref[rows, :] = m_new
                p_ref[rows, :tk] = jnp.exp2(sc - jnp.concatenate([m_new] * (tk // LANES), axis=1)).astype(BF16)
        acc_ref[...] = a_ref[...] * acc_ref[...] + _dot(p_ref[:, :tk], vv)

    def reset():
        m_ref[...] = jnp.full_like(m_ref, NEG)
        acc_ref[...] = jnp.zeros_like(acc_ref)

    def result():
        acc = acc_ref[...]
        return acc[:, :dh] / acc[:, dh:dh + 1]

    reset()
    qx = qx_ref[...]
    n_full = t0 // tks

    def slc_pair(i, carry):
        tile_step(qx, ksx_ref, vsx_ref, pl.multiple_of(2 * i * tks, tks), tks, 0, None)
        tile_step(qx, ksx_ref, vsx_ref, pl.multiple_of((2 * i + 1) * tks, tks), tks, 1, None)
        return carry

    lax.fori_loop(0, n_full // 2, slc_pair, 0)

    @pl.when(n_full % 2 == 1)
    def _():
        tile_step(qx, ksx_ref, vsx_ref, pl.multiple_of((n_full - 1) * tks, tks), tks, 0, None)

    kd = pl.multiple_of(n_full * tks, tks)
    spos = kd + lax.broadcasted_iota(jnp.int32, (tq, tks), 1)
    tpos = t0 + lax.broadcasted_iota(jnp.int32, (tq, tks), 0)
    tile_step(qx, ksx_ref, vsx_ref, kd, tks, 1, jnp.where(spos <= tpos, 0.0, NEG))
    o_slc = result()

    reset()
    n_win = WINDOW // tkw + 1
    last = (t0 + tq - 1) // tkw
    for i in range(n_win):
        kw0 = (last - (n_win - 1) + i) * tkw
        spos = kw0 + lax.broadcasted_iota(jnp.int32, (tq, tkw), 1)
        dist = t0 + lax.broadcasted_iota(jnp.int32, (tq, tkw), 0) - spos
        bias = jnp.where((spos >= 0) & (dist >= 0) & (dist < WINDOW), 0.0, NEG)
        tile_step(q, kwx_ref, vwx_ref, pl.multiple_of(jnp.maximum(kw0, 0), tkw), tkw, i % 2, bias)
    o_win = result()

    gates = jax.nn.sigmoid(gl_ref[0, 0, pl.ds(pl.multiple_of(t0, tq), tq), :] + bg_ref[0])
    outs = []
    for r in range(R):
        rs = slice(r * tq, (r + 1) * tq)
        outs.append(gates[:, 3 * r:3 * r + 1] * o_cmp[rs] + gates[:, 3 * r + 1:3 * r + 2] * o_slc[rs]
                    + gates[:, 3 * r + 2:3 * r + 3] * o_win[rs])
    out_ref[0] = jnp.concatenate(outs, axis=1).astype(out_ref.dtype)


def nsa_attention(qp, q_feat, kc, vc, kvx, kv_off, key_feat, blk_onehot, ones_col, gl, bg, ov):
    b, t, _ = qp.shape
    g, r = q_feat.shape[:2]
    dh = B_HEAD_DIM
    tq = min(NSA_TQ, t)
    tks = min(NSA_TK_SLC, t)
    tkw = min(NSA_TK_WIN, t)
    tkm = max(tks, tkw)
    ncmp = kc.shape[2]
    n_top = min(SEL_TOPN, t // SEL_BLOCK)
    kv_spec = lambda c: pl.BlockSpec((1, t, LANES), lambda i, j, k: (i, 0, kv_off + c * g + j))
    const_spec = lambda a: pl.BlockSpec(a.shape, lambda i, j, k: (0,) * a.ndim)
    slot = [pltpu.VMEM((r * tq, tkm), F32), pltpu.VMEM((r * tq, tkm), BF16), pltpu.VMEM((r * tq, LANES), F32)]
    return pl.pallas_call(
        functools.partial(_nsa_kernel, tq=tq, tks=tks, tkw=tkw, n_top=n_top),
        grid=(b, g, t // tq),
        in_specs=[pl.BlockSpec((1, tq, r * LANES), lambda i, j, k: (i, k, j)),
                  pl.BlockSpec((1, r, 1, LANES), lambda i, j, k: (j, 0, 0, 0)),
                  pl.BlockSpec((1, 1, ncmp, LANES), lambda i, j, k: (i, j, 0, 0)),
                  pl.BlockSpec((1, 1, ncmp, dh), lambda i, j, k: (i, j, 0, 0)),
                  kv_spec(0), kv_spec(1), kv_spec(2), kv_spec(3),
                  const_spec(key_feat), const_spec(blk_onehot), const_spec(ones_col),
                  pl.BlockSpec((1, 1, t, 3 * r), lambda i, j, k: (i, j, 0, 0)),
                  pl.BlockSpec((1, 1, 3 * r), lambda i, j, k: (j, 0, 0)),
                  const_spec(ov)],
        out_specs=pl.BlockSpec((1, tq, r * dh), lambda i, j, k: (i, k, j)),
        out_shape=jax.ShapeDtypeStruct((b, t, g * r * dh), BF16),
        scratch_shapes=[pltpu.VMEM((r * tq, 2 * LANES), BF16),
                        pltpu.VMEM((t, 2 * LANES), BF16),
                        pltpu.VMEM((t, LANES), BF16),
                        pltpu.VMEM((t, LANES), BF16),
                        pltpu.VMEM((t, LANES), BF16),
                        pltpu.VMEM((r * tq, LANES), F32),
                        pltpu.VMEM((r * tq, LANES), F32),
                        pltpu.VMEM((tq, tkm), F32)]
                       + slot + slot,
        compiler_params=_cparams("arbitrary", "arbitrary", "arbitrary"),
        name="nsa_attention",
    )(qp, q_feat, kc, vc, kvx, kvx, kvx, kvx, key_feat, blk_onehot, ones_col, gl, bg, ov)


def _alibi_slopes(n):
    return np.power(2.0, -8.0 * np.arange(1, n + 1) / n).astype(np.float32)


def _np_split3(a):
    a = np.asarray(a, np.float32)
    out = []
    r = a
    for _ in range(3):
        p = r.astype(BF16).astype(np.float32)
        out.append(p)
        r = (r - p).astype(np.float32)
    return out


def _pos_features(pos, width):
    hi = (pos // 64).astype(np.float32)
    lo = (pos % 64).astype(np.float32)
    f = np.zeros((pos.shape[0], width), np.float32)
    f[:, 0:3] = hi[:, None]
    f[:, 3:6] = lo[:, None]
    return f


def _slope_features(width):
    s1, s2, s3 = _np_split3((_alibi_slopes(B_HEADS).astype(np.float64) * LOG2E).astype(np.float32))
    f = np.zeros((B_HEADS, width), np.float32)
    for i, s in enumerate((s1, s2, s3)):
        f[:, i] = 64.0 * s
        f[:, 3 + i] = s
    return f


META_I1, META_I2, META_W1, META_W2 = 0, 1, 2, 3


def _router_kernel(x_ref, g_ref, w_ref, xn_ref, meta_ref, pos_ref, cnt_ref, run_ref):
    @pl.when(pl.program_id(0) == 0)
    def _():
        run_ref[...] = jnp.zeros_like(run_ref)

    xn = _rms(x_ref[...], g_ref[...])
    xn_ref[...] = xn
    logits = _dot_f32(xn, w_ref[...])
    tm = logits.shape[0]
    lane = lax.broadcasted_iota(jnp.int32, logits.shape, 1)
    logits = jnp.where(lane < N_EXPERTS, logits, -jnp.inf)
    m1 = jnp.max(logits, axis=-1, keepdims=True)
    i1 = jnp.min(jnp.where(logits == m1, lane, LANES), axis=-1, keepdims=True)
    rest = jnp.where(lane == i1, -jnp.inf, logits)
    m2 = jnp.max(rest, axis=-1, keepdims=True)
    i2 = jnp.min(jnp.where(rest == m2, lane, LANES), axis=-1, keepdims=True)
    e2 = jnp.exp(m2 - m1)
    w1 = 1.0 / (1.0 + e2)
    meta = jnp.where(lane == META_I1, i1.astype(F32), 0.0)
    meta = jnp.where(lane == META_I2, i2.astype(F32), meta)
    meta = jnp.where(lane == META_W1, w1, meta)
    meta_ref[...] = jnp.where(lane == META_W2, e2 * w1, meta)

    sel = jnp.where((lane == i1) | (lane == i2), 1.0, 0.0)
    row = lax.broadcasted_iota(jnp.int32, (tm, tm), 0)
    col = lax.broadcasted_iota(jnp.int32, (tm, tm), 1)
    before = jnp.where(col < row, 1.0, 0.0).astype(BF16)
    run = run_ref[...]
    pos_ref[...] = run + _dot(before, sel.astype(BF16))
    run = run + jnp.sum(sel, axis=0, keepdims=True)
    run_ref[...] = run
    cnt_ref[...] = run


def moe_router(x, g, w_router, *, tm=512):
    n, d = x.shape
    tm = min(tm, n)
    w = jnp.pad(w_router, ((0, 0), (0, LANES - w_router.shape[1])))
    row_spec = pl.BlockSpec((tm, LANES), lambda i: (i, 0))
    return pl.pallas_call(
        _router_kernel,
        grid=(n // tm,),
        in_specs=[pl.BlockSpec((tm, d), lambda i: (i, 0)),
                  pl.BlockSpec((1, d), lambda i: (0, 0)),
                  pl.BlockSpec((d, LANES), lambda i: (0, 0))],
        out_specs=[pl.BlockSpec((tm, d), lambda i: (i, 0)), row_spec, row_spec,
                   pl.BlockSpec((1, LANES), lambda i: (0, 0))],
        out_shape=[jax.ShapeDtypeStruct((n, d), F32), jax.ShapeDtypeStruct((n, LANES), F32),
                   jax.ShapeDtypeStruct((n, LANES), F32), jax.ShapeDtypeStruct((1, LANES), F32)],
        scratch_shapes=[pltpu.VMEM((1, LANES), F32)],
        compiler_params=_cparams("arbitrary"),
        name="moe_router",
    )(x, g.reshape(1, d), w)


MOE_TILE = 512


def _moe_kernel(te_ref, ok_ref, idx0_ref, idxn_ref, x_hbm, wg_ref, wu_ref, wd_ref, o_ref, xbuf, xb_ref, sem):
    i = pl.program_id(0)
    j = pl.program_id(1)
    nt = pl.num_programs(0)
    nf = pl.num_programs(1)
    tm = xb_ref.shape[0]
    per_step = tm // nf
    slot = i % 2

    def row_copy(idx_ref, r, s):
        return pltpu.make_async_copy(x_hbm.at[pl.ds(idx_ref[0, 0, r], 1), :], xbuf.at[s, pl.ds(r, 1), :], sem.at[s])

    @pl.when((i == 0) & (j == 0))
    def _():
        def start0(r, c):
            row_copy(idx0_ref, r, 0).start()
            return c
        lax.fori_loop(0, tm, start0, 0)

    @pl.when(j == 0)
    def _():
        for r in range(tm):
            row_copy(idx0_ref, r, slot).wait()
        xb_ref[...] = xbuf[slot].astype(BF16)
        o_ref[...] = jnp.zeros_like(o_ref)

    def prefetch():
        base = j * per_step
        for r in range(per_step):
            row_copy(idxn_ref, base + r, 1 - slot).start()

    def compute():
        xb = xb_ref[...]
        a = (_silu(_dot(xb, wg_ref[0])) * _dot(xb, wu_ref[0])).astype(BF16)
        o_ref[...] += _dot(a, wd_ref[0])

    has_next = i + 1 < nt
    ok = ok_ref[i] > 0

    @pl.when(has_next & ok)
    def _():
        prefetch()
        compute()

    @pl.when(has_next & jnp.logical_not(ok))
    def _():
        prefetch()

    @pl.when(jnp.logical_not(has_next) & ok)
    def _():
        compute()


def moe_experts(x, row_tok, tile_expert, tile_ok, w_gu, w_down, *, tf=1792):
    p = row_tok.shape[0]
    d = x.shape[1]
    ne, f, _ = w_down.shape
    tm = MOE_TILE
    nf = f // tf
    nt = p // tm
    idx = row_tok.reshape(nt, 1, tm)
    grid_spec = pltpu.PrefetchScalarGridSpec(
        num_scalar_prefetch=2,
        grid=(nt, nf),
        in_specs=[pl.BlockSpec((1, 1, tm), lambda i, j, te, ok: (0, 0, 0), memory_space=pltpu.SMEM),
                  pl.BlockSpec((1, 1, tm), lambda i, j, te, ok: (jnp.minimum(i + 1, nt - 1), 0, 0),
                               memory_space=pltpu.SMEM),
                  pl.BlockSpec(memory_space=pl.ANY),
                  pl.BlockSpec((1, d, tf), lambda i, j, te, ok: (te[i], 0, j)),
                  pl.BlockSpec((1, d, tf), lambda i, j, te, ok: (te[i], 0, j + nf)),
                  pl.BlockSpec((1, tf, d), lambda i, j, te, ok: (te[i], j, 0))],
        out_specs=pl.BlockSpec((tm, d), lambda i, j, te, ok: (i, 0)),
        scratch_shapes=[pltpu.VMEM((2, tm, d), F32), pltpu.VMEM((tm, d), BF16),
                        pltpu.SemaphoreType.DMA((2,))])
    return pl.pallas_call(
        _moe_kernel,
        grid_spec=grid_spec,
        out_shape=jax.ShapeDtypeStruct((p, d), F32),
        compiler_params=_cparams("arbitrary", "arbitrary"),
        name="moe_experts",
    )(tile_expert, tile_ok, idx, idx, x, w_gu, w_gu, w_down)


def _combine_kernel(d1_ref, d2_ref, x_ref, meta_ref, y_ref, gf_ref, o_ref, y1_ref, y2_ref, sem1, sem2):
    rows = o_ref.shape[0]

    def copy1(r):
        return pltpu.make_async_copy(y_ref.at[pl.ds(d1_ref[0, 0, r], 1), :], y1_ref.at[pl.ds(r, 1), :], sem1)

    def copy2(r):
        return pltpu.make_async_copy(y_ref.at[pl.ds(d2_ref[0, 0, r], 1), :], y2_ref.at[pl.ds(r, 1), :], sem2)

    def start(r, c):
        copy1(r).start()
        copy2(r).start()
        return c

    def wait(r, c):
        copy1(r).wait()
        copy2(r).wait()
        return c

    lax.fori_loop(0, rows, start, 0)
    lax.fori_loop(0, rows, wait, 0)
    meta = meta_ref[...]
    w1 = meta[:, META_W1:META_W1 + 1]
    w2 = meta[:, META_W2:META_W2 + 1]
    o_ref[...] = _rms(x_ref[...] + w1 * y1_ref[...] + w2 * y2_ref[...], gf_ref[...])


def moe_combine(x, meta, ys, d1, d2, g_final, *, tc=256):
    n, d = x.shape
    tc = min(tc, n)
    idx_spec = pl.BlockSpec((1, 1, tc), lambda i: (i, 0, 0), memory_space=pltpu.SMEM)
    return pl.pallas_call(
        _combine_kernel,
        grid=(n // tc,),
        in_specs=[idx_spec, idx_spec,
                  pl.BlockSpec((tc, d), lambda i: (i, 0)),
                  pl.BlockSpec((tc, LANES), lambda i: (i, 0)),
                  pl.BlockSpec(memory_space=pl.ANY),
                  pl.BlockSpec((1, d), lambda i: (0, 0))],
        out_specs=pl.BlockSpec((tc, d), lambda i: (i, 0)),
        out_shape=jax.ShapeDtypeStruct((n, d), F32),
        scratch_shapes=[pltpu.VMEM((tc, d), F32), pltpu.VMEM((tc, d), F32),
                        pltpu.SemaphoreType.DMA(()), pltpu.SemaphoreType.DMA(())],
        compiler_params=_cparams("parallel"),
        name="moe_combine",
    )(d1.reshape(n // tc, 1, tc), d2.reshape(n // tc, 1, tc), x, meta, ys, g_final.reshape(1, d))


def kernel(x, norm_mix, norm_ffn, a_w_in, a_b_gate, a_conv, a_norm_h, a_w_out, norm_kv, b_w_kv,
           b_cmp_pos, b_cmp_w1, b_cmp_w2, b_w_q, b_b_gate, b_w_out, f_w_gu, f_w_down,
           m_router, m_w_gu, m_w_down, norm_final):
    B, T, D = x.shape
    N = B * T
    G, R, dh = B_KV_GROUPS, B_REP, B_HEAD_DIM
    xs = x.reshape(N, D)

    inner4 = a_w_in.shape[2] - 2 * A_HEADS
    w_in = a_w_in[0]
    proj = norm_matmul(xs, norm_mix[0], w_in[:, :inner4].astype(BF16), out_dtype=BF16)
    w_gate = jnp.pad(w_in[:, inner4:], ((0, 0), (0, LANES - 2 * A_HEADS)))
    gcol = norm_matmul(xs, norm_mix[0], w_gate, out_dtype=F32, exact=True, tn=LANES)
    hs = mlstm_core(proj, gcol, a_b_gate[0], a_conv[0], a_norm_h[0], B, T)
    xs = matmul_residual(hs, a_w_out[0].astype(BF16), xs)
    xs = ffn_dense(xs, norm_ffn[0], f_w_gu[0].astype(BF16), f_w_down[0].astype(BF16))

    hd = B_HEADS * dh
    feat_w = LANES - dh
    ncmp = T // CMP_STRIDE

    def slabs(w):
        d_in, cols = w.shape
        return jnp.pad(w.reshape(d_in, cols // dh, dh), ((0, 0), (0, 0), (0, feat_w))).reshape(d_in, -1)

    cmp_cols = 2 * G * dh
    w_kvx = jnp.concatenate([b_w_kv[:, :cmp_cols], slabs(b_w_kv[:, cmp_cols:])], axis=1).astype(BF16)
    kvx = norm_matmul(xs, norm_kv, w_kvx, out_dtype=BF16, tn=w_kvx.shape[1] // 2)
    w_q = b_w_q[0]
    qp = norm_matmul(xs, norm_mix[1], slabs(w_q[:, :hd] * (dh ** -0.5 * LOG2E)).astype(BF16), out_dtype=BF16)
    w_qg = jnp.pad(w_q[:, hd:], ((0, 0), (0, LANES - 3 * B_HEADS)))
    gl = norm_matmul(xs, norm_mix[1], w_qg, out_dtype=F32, exact=True, tn=LANES)

    kvt = kvx[:, :cmp_cols].reshape(B, T, 2 * G, dh).transpose(0, 2, 1, 3)
    pos = b_cmp_pos.transpose(1, 0, 2).reshape(2, 1, CMP_LEN * dh).astype(BF16)
    kvc = nsa_compress(kvt.reshape(B, 2 * G, ncmp, CMP_STRIDE * dh), pos,
                       b_cmp_w1.astype(BF16), b_cmp_w2.astype(BF16))

    def hi_lanes(f):
        return jnp.asarray(np.concatenate([np.zeros((f.shape[0], dh), np.float32), f], axis=1), BF16)

    key_feat = hi_lanes(_pos_features(np.arange(T), feat_w))
    cmp_feat = jnp.asarray(_pos_features(np.arange(ncmp) * CMP_STRIDE + CMP_LEN - 1, feat_w), BF16)
    q_feat = hi_lanes(_slope_features(feat_w)).reshape(G, R, 1, LANES)
    ones_col = np.zeros((1, feat_w), np.float32)
    ones_col[0, 0] = 1.0
    ones_col = hi_lanes(ones_col)
    blk_onehot = (np.arange(T)[:, None] // SEL_BLOCK == np.arange(LANES)[None, :]).astype(np.float32)
    blk_onehot = jnp.asarray(blk_onehot, BF16)
    kc = jnp.concatenate([kvc[:, :G].astype(BF16), jnp.broadcast_to(cmp_feat, (B, G, ncmp, feat_w))], axis=-1)
    vc = kvc[:, G:].astype(BF16)
    glt = gl[:, :3 * B_HEADS].reshape(B, T, G, 3 * R).transpose(0, 2, 1, 3)
    bg = b_b_gate[0].reshape(G, 1, 3 * R)

    nsel = T // SEL_BLOCK
    ci = np.arange(ncmp)[:, None] * CMP_STRIDE
    sj = np.arange(LANES)[None, :] * SEL_BLOCK
    ov = ((ci < sj + SEL_BLOCK) & (ci + CMP_LEN > sj) & (np.arange(LANES)[None, :] < nsel)
          & (np.arange(ncmp)[:, None] < ncmp - 1))
    ov = jnp.asarray(ov.astype(np.float32), BF16)

    oa = nsa_attention(qp.reshape(B, T, B_HEADS * LANES), q_feat, kc, vc, kvx.reshape(B, T, -1),
                       cmp_cols // LANES, key_feat, blk_onehot, ones_col, glt, bg, ov)
    xs = matmul_residual(oa.reshape(N, hd), b_w_out[0].astype(BF16), xs)

    xn, meta, pos, cnt = moe_router(xs, norm_ffn[1], m_router[0])
    ne = N_EXPERTS
    p_rows = 2 * N + ne * MOE_TILE
    i1 = meta[:, META_I1].astype(jnp.int32)
    i2 = meta[:, META_I2].astype(jnp.int32)
    counts = cnt[0, :ne].astype(jnp.int32)
    padded = (counts + MOE_TILE - 1) // MOE_TILE * MOE_TILE
    seg_end = jnp.cumsum(padded)
    seg_start = seg_end - padded
    pos8 = pos[:, :ne].astype(jnp.int32)
    d1 = seg_start[i1] + jnp.take_along_axis(pos8, i1[:, None], axis=1)[:, 0]
    d2 = seg_start[i2] + jnp.take_along_axis(pos8, i2[:, None], axis=1)[:, 0]
    tok = jnp.arange(N, dtype=jnp.int32)
    row_tok = jnp.zeros((p_rows,), jnp.int32).at[d1].set(tok).at[d2].set(tok)
    tile_start = jnp.arange(p_rows // MOE_TILE, dtype=jnp.int32) * MOE_TILE
    tile_expert = jnp.minimum(jnp.sum(tile_start[:, None] >= seg_end[None, :], axis=1), ne - 1).astype(jnp.int32)
    tile_ok = (tile_start < seg_end[ne - 1]).astype(jnp.int32)

    ys = moe_experts(xn, row_tok, tile_expert, tile_ok, m_w_gu[0].astype(BF16), m_w_down[0].astype(BF16))
    out = moe_combine(xs, meta, ys, d1, d2, norm_final)
    return out.reshape(B, T, D)
```

```python
import functools

import numpy as np
import jax
import jax.numpy as jnp
from jax import lax
from jax.experimental import pallas as pl
from jax.experimental.pallas import tpu as pltpu

F32 = jnp.float32
BF16 = jnp.bfloat16

RMS_EPS = 1e-6
A_HEADS = 4
A_CONV = 4
B_HEADS = 16
B_KV_GROUPS = 4
B_REP = B_HEADS // B_KV_GROUPS
B_HEAD_DIM = 64
CMP_LEN = 32
CMP_STRIDE = 16
SEL_BLOCK = 64
SEL_TOPN = 16
WINDOW = 512
FORCED_SCORE = 1e4
N_EXPERTS = 8

LANES = 128
V7X_VMEM_BYTES = 64 * 1024 * 1024
VMEM_LIMIT = V7X_VMEM_BYTES - 8 * 1024 * 1024
NEG = -1e30


def _cparams(*sem):
    return pltpu.CompilerParams(dimension_semantics=sem, vmem_limit_bytes=VMEM_LIMIT)


def _dot(a, b):
    return jnp.dot(a, b, preferred_element_type=F32)


def _dot_nt(a, b):
    return lax.dot_general(a, b, (((1,), (1,)), ((), ())), preferred_element_type=F32)


def _dot_tn(a, b):
    return lax.dot_general(a, b, (((0,), (0,)), ((), ())), preferred_element_type=F32)


def _split3(a):
    a1 = a.astype(BF16)
    r1 = a - a1.astype(F32)
    a2 = r1.astype(BF16)
    a3 = (r1 - a2.astype(F32)).astype(BF16)
    return a1, a2, a3


def _dot_f32(a, b):
    a1, a2, a3 = _split3(a)
    b1, b2, b3 = _split3(b)
    return (_dot(a1, b1) + _dot(a1, b2) + _dot(a2, b1)
            + _dot(a2, b2) + _dot(a1, b3) + _dot(a3, b1))


def _dot_f32_nt(a, b):
    a1, a2, a3 = _split3(a)
    b1, b2, b3 = _split3(b)
    return (_dot_nt(a1, b1) + _dot_nt(a1, b2) + _dot_nt(a2, b1)
            + _dot_nt(a2, b2) + _dot_nt(a1, b3) + _dot_nt(a3, b1))


def _rms(xf, g):
    return xf * lax.rsqrt(jnp.mean(xf * xf, axis=-1, keepdims=True) + RMS_EPS) * g


def _silu(x):
    return x * jax.nn.sigmoid(x)


def _log_sigmoid(x):
    return jnp.minimum(x, 0.0) - jnp.log(1.0 + jnp.exp(-jnp.abs(x)))


def _norm_matmul_kernel(x_ref, g_ref, w_ref, o_ref, xn_ref, *, exact):
    @pl.when(pl.program_id(1) == 0)
    def _():
        xn_ref[...] = _rms(x_ref[...], g_ref[...]).astype(xn_ref.dtype)

    if exact:
        o_ref[...] = _dot_f32(xn_ref[...], w_ref[...]).astype(o_ref.dtype)
    else:
        o_ref[...] = _dot(xn_ref[...], w_ref[...]).astype(o_ref.dtype)


def norm_matmul(x, g, w, *, out_dtype, exact=False, tm=1024, tn=1024):
    n, d = x.shape
    dout = w.shape[1]
    tm = min(tm, n)
    tn = min(tn, dout)
    return pl.pallas_call(
        functools.partial(_norm_matmul_kernel, exact=exact),
        grid=(n // tm, dout // tn),
        in_specs=[pl.BlockSpec((tm, d), lambda i, j: (i, 0)),
                  pl.BlockSpec((1, d), lambda i, j: (0, 0)),
                  pl.BlockSpec((d, tn), lambda i, j: (0, j))],
        out_specs=pl.BlockSpec((tm, tn), lambda i, j: (i, j)),
        out_shape=jax.ShapeDtypeStruct((n, dout), out_dtype),
        scratch_shapes=[pltpu.VMEM((tm, d), F32 if exact else BF16)],
        compiler_params=_cparams("parallel", "arbitrary"),
        name="norm_matmul_f32" if exact else "norm_matmul",
    )(x, g.reshape(1, d), w)


def _matmul_res_kernel(a_ref, w_ref, r_ref, o_ref):
    o_ref[...] = r_ref[...] + _dot(a_ref[...], w_ref[...])


def matmul_residual(a, w, res, *, tm=1024):
    n, k = a.shape
    dout = w.shape[1]
    tm = min(tm, n)
    return pl.pallas_call(
        _matmul_res_kernel,
        grid=(n // tm,),
        in_specs=[pl.BlockSpec((tm, k), lambda i: (i, 0)),
                  pl.BlockSpec((k, dout), lambda i: (0, 0)),
                  pl.BlockSpec((tm, dout), lambda i: (i, 0))],
        out_specs=pl.BlockSpec((tm, dout), lambda i: (i, 0)),
        out_shape=jax.ShapeDtypeStruct((n, dout), F32),
        compiler_params=_cparams("parallel"),
        name="matmul_residual",
    )(a, w, res)


MLSTM_CHUNK = 256


def _mlstm_kernel(qk_ref, v_ref, o_ref, gcol_ref, grow_ref, bcol_ref, brow_ref, convw_ref, gout_ref,
                  out_ref, ct_ref, n_ref, m_ref, prev_ref):
    L = qk_ref.shape[0]
    H = A_HEADS
    inner = v_ref.shape[1]
    dh = inner // H

    @pl.when(pl.program_id(1) == 0)
    def _():
        ct_ref[...] = jnp.zeros_like(ct_ref)
        n_ref[...] = jnp.zeros_like(n_ref)
        m_ref[...] = jnp.zeros_like(m_ref)
        prev_ref[...] = jnp.zeros_like(prev_ref)

    row = lax.broadcasted_iota(jnp.int32, (L, L), 0)
    col = lax.broadcasted_iota(jnp.int32, (L, L), 1)
    causal = col <= row
    tril = jnp.where(causal, 1.0, 0.0).astype(BF16)
    triu = jnp.where(row <= col, 1.0, 0.0).astype(BF16)

    gc = gcol_ref[...] + brow_ref[...]
    gr = grow_ref[...] + bcol_ref[...]
    lfc1, lfc2, lfc3 = _split3(_log_sigmoid(gc))
    lfr1, lfr2, lfr3 = _split3(_log_sigmoid(gr))
    b_c = _dot(tril, lfc1) + _dot(tril, lfc2) + _dot(tril, lfc3)
    b_r = _dot(lfr1, triu) + _dot(lfr2, triu) + _dot(lfr3, triu)

    rowi = lax.broadcasted_iota(jnp.int32, (L, dh), 0)

    def conv(cur, prev, w):
        y = cur * w[A_CONV - 1:A_CONV, :]
        for s in range(1, A_CONV):
            sh = jnp.where(rowi < s, pltpu.roll(prev, s, 0), pltpu.roll(cur, s, 0))
            y = y + sh * w[A_CONV - 1 - s:A_CONV - s, :]
        return y

    for h in range(H):
        hs = slice(h * dh, (h + 1) * dh)
        ks = slice(inner + h * dh, inner + (h + 1) * dh)
        q = conv(qk_ref[:, hs].astype(F32), prev_ref[:, hs].astype(F32), convw_ref[:, hs])
        k = conv(qk_ref[:, ks].astype(F32), prev_ref[:, ks].astype(F32), convw_ref[:, ks]) * (dh ** -0.5)
        v = v_ref[:, hs]
        qb = q.astype(BF16)
        kb = k.astype(BF16)

        li_c = gc[:, h:h + 1]
        bc = b_c[:, H + h:H + h + 1]
        li_r = gr[h:h + 1, :]
        br = b_r[H + h:H + h + 1, :]
        m_prev = m_ref[h:h + 1, 0:1]

        d = jnp.where(causal, bc - br + li_r, -jnp.inf)
        inter = bc + m_prev
        m_t = jnp.maximum(inter, jnp.max(d, axis=-1, keepdims=True))
        w_inter = jnp.exp(inter - m_t)
        s = _dot_nt(qb, kb) * jnp.exp(d - m_t)
        ct = ct_ref[h]
        num = _dot(s.astype(BF16), v) + w_inter * _dot(qb, ct.astype(BF16))
        den = jnp.sum(s, axis=-1, keepdims=True) + w_inter * jnp.sum(q * n_ref[h], axis=-1, keepdims=True)
        hh = num / jnp.maximum(jnp.abs(den), jnp.exp(-m_t))
        hh = hh * lax.rsqrt(jnp.mean(hh * hh, axis=-1, keepdims=True) + RMS_EPS)
        out_ref[:, hs] = (hh * gout_ref[:, hs] * jax.nn.sigmoid(o_ref[:, hs].astype(F32))).astype(out_ref.dtype)

        b_last = bc[L - 1:L, :]
        g = b_last - bc + li_c
        m_new = jnp.maximum(b_last + m_prev, jnp.max(g, axis=0, keepdims=True))
        a_prev = jnp.exp(b_last + m_prev - m_new)
        a_s = jnp.exp(g - m_new)
        ct_ref[h] = a_prev * ct + _dot_tn(kb, (v.astype(F32) * a_s).astype(BF16))
        n_ref[h] = a_prev * n_ref[h] + jnp.sum(k * a_s, axis=0, keepdims=True)
        m_ref[h:h + 1, :] = jnp.broadcast_to(m_new, (1, LANES))

    prev_ref[...] = qk_ref[...]


def mlstm_core(proj, gcol, b_gate, conv_w, g_out, batch, seq):
    n = proj.shape[0]
    inner = proj.shape[1] // 4
    H = A_HEADS
    dh = inner // H
    L = min(MLSTM_CHUNK, seq)
    nc = seq // L
    grow = gcol[:, :2 * H].T
    brow = jnp.pad(b_gate, (0, LANES - 2 * H)).reshape(1, LANES)
    bcol = b_gate.reshape(2 * H, 1)
    return pl.pallas_call(
        _mlstm_kernel,
        grid=(batch, nc),
        in_specs=[pl.BlockSpec((L, 2 * inner), lambda b, c: (b * nc + c, 0)),
                  pl.BlockSpec((L, inner), lambda b, c: (b * nc + c, 2)),
                  pl.BlockSpec((L, inner), lambda b, c: (b * nc + c, 3)),
                  pl.BlockSpec((L, LANES), lambda b, c: (b * nc + c, 0)),
                  pl.BlockSpec((2 * H, L), lambda b, c: (0, b * nc + c)),
                  pl.BlockSpec((2 * H, 1), lambda b, c: (0, 0)),
                  pl.BlockSpec((1, LANES), lambda b, c: (0, 0)),
                  pl.BlockSpec((A_CONV, 2 * inner), lambda b, c: (0, 0)),
                  pl.BlockSpec((1, inner), lambda b, c: (0, 0))],
        out_specs=pl.BlockSpec((L, inner), lambda b, c: (b * nc + c, 0)),
        out_shape=jax.ShapeDtypeStruct((n, inner), BF16),
        scratch_shapes=[pltpu.VMEM((H, dh, dh), F32),
                        pltpu.VMEM((H, 1, dh), F32),
                        pltpu.VMEM((8, LANES), F32),
                        pltpu.VMEM((L, 2 * inner), BF16)],
        compiler_params=_cparams("parallel", "arbitrary"),
        name="mlstm_core",
    )(proj, proj, proj, gcol, grow, bcol, brow, conv_w, g_out.reshape(1, inner))


def _ffn_kernel(x_ref, g_ref, wg_ref, wu_ref, wd_ref, o_ref, xn_ref, acc_ref):
    j = pl.program_id(1)

    @pl.when(j == 0)
    def _():
        xf = x_ref[...]
        xn_ref[...] = _rms(xf, g_ref[...]).astype(BF16)
        acc_ref[...] = xf

    xn = xn_ref[...]
    a = (_silu(_dot(xn, wg_ref[...])) * _dot(xn, wu_ref[...])).astype(BF16)
    acc_ref[...] += _dot(a, wd_ref[...])

    @pl.when(j == pl.num_programs(1) - 1)
    def _():
        o_ref[...] = acc_ref[...]


def ffn_dense(x, g, w_gu, w_down, *, tm=512, tf=1408):
    n, d = x.shape
    f = w_down.shape[0]
    tm = min(tm, n)
    nf = f // tf
    return pl.pallas_call(
        _ffn_kernel,
        grid=(n // tm, nf),
        in_specs=[pl.BlockSpec((tm, d), lambda i, j: (i, 0)),
                  pl.BlockSpec((1, d), lambda i, j: (0, 0)),
                  pl.BlockSpec((d, tf), lambda i, j: (0, j)),
                  pl.BlockSpec((d, tf), lambda i, j: (0, j + nf)),
                  pl.BlockSpec((tf, d), lambda i, j: (j, 0))],
        out_specs=pl.BlockSpec((tm, d), lambda i, j: (i, 0)),
        out_shape=jax.ShapeDtypeStruct((n, d), F32),
        scratch_shapes=[pltpu.VMEM((tm, d), BF16), pltpu.VMEM((tm, d), F32)],
        compiler_params=_cparams("parallel", "arbitrary"),
        name="ffn_dense",
    )(x, g.reshape(1, d), w_gu, w_gu, w_down)


def _compress_kernel(r_ref, pos_ref, w1_ref, w2_ref, o_ref):
    r = r_ref[0, 0]
    w1 = w1_ref[0]
    half = r.shape[1]
    nc = r.shape[0]
    lo = _dot(r, w1[:half])
    hi = _dot(r, w1[half:])
    hid = lo + pltpu.roll(hi, nc - 1, 0) + _dot(pos_ref[0], w1)
    o_ref[0, 0] = _dot(_silu(hid).astype(BF16), w2_ref[0]).astype(o_ref.dtype)


def nsa_compress(r, pos, w1, w2):
    b, c2, nc, half = r.shape
    g = c2 // 2
    hidden = w1.shape[2]
    dh = w2.shape[2]
    return pl.pallas_call(
        _compress_kernel,
        grid=(b, c2),
        in_specs=[pl.BlockSpec((1, 1, nc, half), lambda i, j: (i, j, 0, 0)),
                  pl.BlockSpec((1, 1, 2 * half), lambda i, j: (j // g, 0, 0)),
                  pl.BlockSpec((1, 2 * half, hidden), lambda i, j: (j // g, 0, 0)),
                  pl.BlockSpec((1, hidden, dh), lambda i, j: (j // g, 0, 0))],
        out_specs=pl.BlockSpec((1, 1, nc, dh), lambda i, j: (i, j, 0, 0)),
        out_shape=jax.ShapeDtypeStruct((b, c2, nc, dh), F32),
        compiler_params=_cparams("parallel", "parallel"),
        name="nsa_compress",
    )(r, pos, w1, w2)


NSA_TQ = 256
NSA_TK_SLC = 512
NSA_TK_WIN = 256
NSA_ROWS = 32
SEL_MASK = 2.0 ** 14
LOG2E = float(np.log2(np.e))


def _nsa_kernel(q_ref, qf_ref, kc_ref, vc_ref, ks_ref, vs_ref, kw_ref, vw_ref, kf_ref, oh_ref, one_ref,
                gl_ref, bg_ref, ov_ref,
                out_ref, qx_ref, ksx_ref, vsx_ref, kwx_ref, vwx_ref, m_ref, acc_ref, bias_ref,
                s0_ref, p0_ref, a0_ref, s1_ref, p1_ref, a1_ref, *, tq, tks, tkw, n_top):
    R = B_REP
    M = R * tq
    dh = B_HEAD_DIM
    t0 = pl.program_id(2) * tq
    slots = ((s0_ref, p0_ref, a0_ref), (s1_ref, p1_ref, a1_ref))

    @pl.when(pl.program_id(2) == 0)
    def _():
        kf = kf_ref[...]
        ksx_ref[:, :LANES] = ks_ref[0] + kf
        ksx_ref[:, LANES:] = oh_ref[...]
        kwx_ref[...] = kw_ref[0] + kf
        vsx_ref[...] = vs_ref[0] + one_ref[...]
        vwx_ref[...] = vw_ref[0] + one_ref[...]

    qb = q_ref[0]
    for r in range(R):
        qx_ref[r * tq:(r + 1) * tq, :LANES] = qb[:, r * LANES:(r + 1) * LANES] + qf_ref[0, r]
    q = qx_ref[:, :LANES]

    kc = kc_ref[0, 0]
    ncmp = kc.shape[0]
    s = _dot_nt(q, kc)
    tpos_c = t0 + (lax.broadcasted_iota(jnp.int32, (M, ncmp), 0) & (tq - 1))
    cend = lax.broadcasted_iota(jnp.int32, (M, ncmp), 1) * CMP_STRIDE + (CMP_LEN - 1)
    ok_c = tpos_c >= cend
    s = jnp.where(ok_c, s, NEG)
    p = jnp.where(ok_c, jnp.exp2(s - jnp.max(s, axis=-1, keepdims=True)), 0.0)
    dsum = jnp.sum(p, axis=-1, keepdims=True)
    p = p / jnp.where(dsum > 0, dsum, 1.0)
    o_cmp = _dot(p.astype(BF16), vc_ref[0, 0])

    psum = p[0:tq]
    for r in range(1, R):
        psum = psum + p[r * tq:(r + 1) * tq]
    p1, p2, p3 = _split3(psum)
    ov = ov_ref[...]
    imp = _dot(p1, ov) + _dot(p2, ov) + _dot(p3, ov)
    nsel = ks_ref.shape[1] // SEL_BLOCK
    nselp = max(nsel, 8)
    imp_t = imp.T[:nselp]
    jj = lax.broadcasted_iota(jnp.int32, (nselp, tq), 0)
    tt = t0 + lax.broadcasted_iota(jnp.int32, (nselp, tq), 1)
    cur = tt // SEL_BLOCK
    forced = (jj == 0) | (jj == cur) | (jj == cur - 1)
    iv = jnp.where(forced, FORCED_SCORE, imp_t)
    iv = jnp.where(jj * SEL_BLOCK <= tt, iv, -jnp.inf)
    rank = jnp.zeros((nselp, tq), F32)
    for j2 in range(nsel):
        rv = iv[j2:j2 + 1, :]
        rank = rank + jnp.where(jj > j2, jnp.where(rv >= iv, 1.0, 0.0), jnp.where(rv > iv, 1.0, 0.0))
    unsel = jnp.where(rank < n_top, 0.0, -SEL_MASK)
    if nselp < LANES:
        unsel = jnp.concatenate([unsel, jnp.zeros((LANES - nselp, tq), F32)], axis=0)
    qmask = unsel.T.astype(BF16)
    for r in range(R):
        qx_ref[r * tq:(r + 1) * tq, LANES:] = qmask

    def tile_step(qq, k_ref, v_ref, k0, tk, slot, bias):
        s_ref, p_ref, a_ref = slots[slot]
        kk = k_ref[pl.ds(k0, tk), :]
        vv = v_ref[pl.ds(k0, tk), :]
        if bias is not None:
            bias_ref[:, :tk] = bias
        s_ref[:, :tk] = _dot_nt(qq, kk)
        for r in range(R):
            for c0 in range(0, tq, NSA_ROWS):
                rows = slice(r * tq + c0, r * tq + c0 + NSA_ROWS)
                sc = s_ref[rows, :tk]
                if bias is not None:
                    sc = sc + bias_ref[c0:c0 + NSA_ROWS, :tk]
                m_old = m_ref[rows, :]
                m_new = jnp.maximum(m_old, jnp.max(sc, axis=-1, keepdims=True))
                a_ref[rows, :] = jnp.exp2(m_old - m_new)
                m_ref[rows, :] = m_new
                p_ref[rows, :tk] = jnp.exp2(sc - jnp.concatenate([m_new] * (tk // LANES), axis=1)).astype(BF16)
        acc_ref[...] = a_ref[...] * acc_ref[...] + _dot(p_ref[:, :tk], vv)

    def reset():
        m_ref[...] = jnp.full_like(m_ref, NEG)
        acc_ref[...] = jnp.zeros_like(acc_ref)

    def result():
        acc = acc_ref[...]
        return acc[:, :dh] / acc[:, dh:dh + 1]

    reset()
    qx = qx_ref[...]
    n_full = t0 // tks

    def slc_pair(i, carry):
        tile_step(qx, ksx_ref, vsx_ref, pl.multiple_of(2 * i * tks, tks), tks, 0, None)
        tile_step(qx, ksx_ref, vsx_ref, pl.multiple_of((2 * i + 1) * tks, tks), tks, 1, None)
        return carry

    lax.fori_loop(0, n_full // 2, slc_pair, 0)

    @pl.when(n_full % 2 == 1)
    def _():
        tile_step(qx, ksx_ref, vsx_ref, pl.multiple_of((n_full - 1) * tks, tks), tks, 0, None)

    kd = pl.multiple_of(n_full * tks, tks)
    spos = kd + lax.broadcasted_iota(jnp.int32, (tq, tks), 1)
    tpos = t0 + lax.broadcasted_iota(jnp.int32, (tq, tks), 0)
    tile_step(qx, ksx_ref, vsx_ref, kd, tks, 1, jnp.where(spos <= tpos, 0.0, NEG))
    o_slc = result()

    reset()
    n_win = WINDOW // tkw + 1
    last = (t0 + tq - 1) // tkw
    for i in range(n_win):
        kw0 = (last - (n_win - 1) + i) * tkw
        spos = kw0 + lax.broadcasted_iota(jnp.int32, (tq, tkw), 1)
        dist = t0 + lax.broadcasted_iota(jnp.int32, (tq, tkw), 0) - spos
        bias = jnp.where((spos >= 0) & (dist >= 0) & (dist < WINDOW), 0.0, NEG)
        tile_step(q, kwx_ref, vwx_ref, pl.multiple_of(jnp.maximum(kw0, 0), tkw), tkw, i % 2, bias)
    o_win = result()

    gates = jax.nn.sigmoid(gl_ref[0, 0, pl.ds(pl.multiple_of(t0, tq), tq), :] + bg_ref[0])
    outs = []
    for r in range(R):
        rs = slice(r * tq, (r + 1) * tq)
        outs.append(gates[:, 3 * r:3 * r + 1] * o_cmp[rs] + gates[:, 3 * r + 1:3 * r + 2] * o_slc[rs]
                    + gates[:, 3 * r + 2:3 * r + 3] * o_win[rs])
    out_ref[0] = jnp.concatenate(outs, axis=1).astype(out_ref.dtype)


def nsa_attention(qp, q_feat, kc, vc, kvx, kv_off, key_feat, blk_onehot, ones_col, gl, bg, ov):
    b, t, _ = qp.shape
    g, r = q_feat.shape[:2]
    dh = B_HEAD_DIM
    tq = min(NSA_TQ, t)
    tks = min(NSA_TK_SLC, t)
    tkw = min(NSA_TK_WIN, t)
    tkm = max(tks, tkw)
    ncmp = kc.shape[2]
    n_top = min(SEL_TOPN, t // SEL_BLOCK)
    kv_spec = lambda c: pl.BlockSpec((1, t, LANES), lambda i, j, k: (i, 0, kv_off + c * g + j))
    const_spec = lambda a: pl.BlockSpec(a.shape, lambda i, j, k: (0,) * a.ndim)
    slot = [pltpu.VMEM((r * tq, tkm), F32), pltpu.VMEM((r * tq, tkm), BF16), pltpu.VMEM((r * tq, LANES), F32)]
    return pl.pallas_call(
        functools.partial(_nsa_kernel, tq=tq, tks=tks, tkw=tkw, n_top=n_top),
        grid=(b, g, t // tq),
        in_specs=[pl.BlockSpec((1, tq, r * LANES), lambda i, j, k: (i, k, j)),
                  pl.BlockSpec((1, r, 1, LANES), lambda i, j, k: (j, 0, 0, 0)),
                  pl.BlockSpec((1, 1, ncmp, LANES), lambda i, j, k: (i, j, 0, 0)),
                  pl.BlockSpec((1, 1, ncmp, dh), lambda i, j, k: (i, j, 0, 0)),
                  kv_spec(0), kv_spec(1), kv_spec(2), kv_spec(3),
                  const_spec(key_feat), const_spec(blk_onehot), const_spec(ones_col),
                  pl.BlockSpec((1, 1, t, 3 * r), lambda i, j, k: (i, j, 0, 0)),
                  pl.BlockSpec((1, 1, 3 * r), lambda i, j, k: (j, 0, 0)),
                  const_spec(ov)],
        out_specs=pl.BlockSpec((1, tq, r * dh), lambda i, j, k: (i, k, j)),
        out_shape=jax.ShapeDtypeStruct((b, t, g * r * dh), BF16),
        scratch_shapes=[pltpu.VMEM((r * tq, 2 * LANES), BF16),
                        pltpu.VMEM((t, 2 * LANES), BF16),
                        pltpu.VMEM((t, LANES), BF16),
                        pltpu.VMEM((t, LANES), BF16),
                        pltpu.VMEM((t, LANES), BF16),
                        pltpu.VMEM((r * tq, LANES), F32),
                        pltpu.VMEM((r * tq, LANES), F32),
                        pltpu.VMEM((tq, tkm), F32)]
                       + slot + slot,
        compiler_params=_cparams("arbitrary", "arbitrary", "arbitrary"),
        name="nsa_attention",
    )(qp, q_feat, kc, vc, kvx, kvx, kvx, kvx, key_feat, blk_onehot, ones_col, gl, bg, ov)


def _alibi_slopes(n):
    return np.power(2.0, -8.0 * np.arange(1, n + 1) / n).astype(np.float32)


def _np_split3(a):
    a = np.asarray(a, np.float32)
    out = []
    r = a
    for _ in range(3):
        p = r.astype(BF16).astype(np.float32)
        out.append(p)
        r = (r - p).astype(np.float32)
    return out


def _pos_features(pos, width):
    hi = (pos // 64).astype(np.float32)
    lo = (pos % 64).astype(np.float32)
    f = np.zeros((pos.shape[0], width), np.float32)
    f[:, 0:3] = hi[:, None]
    f[:, 3:6] = lo[:, None]
    return f


def _slope_features(width):
    s1, s2, s3 = _np_split3((_alibi_slopes(B_HEADS).astype(np.float64) * LOG2E).astype(np.float32))
    f = np.zeros((B_HEADS, width), np.float32)
    for i, s in enumerate((s1, s2, s3)):
        f[:, i] = 64.0 * s
        f[:, 3 + i] = s
    return f


META_I1, META_I2, META_W1, META_W2 = 0, 1, 2, 3


def _router_kernel(x_ref, g_ref, w_ref, xn_ref, meta_ref, pos_ref, cnt_ref, run_ref):
    @pl.when(pl.program_id(0) == 0)
    def _():
        run_ref[...] = jnp.zeros_like(run_ref)

    xn = _rms(x_ref[...], g_ref[...])
    xn_ref[...] = xn
    logits = _dot_f32(xn, w_ref[...])
    tm = logits.shape[0]
    lane = lax.broadcasted_iota(jnp.int32, logits.shape, 1)
    logits = jnp.where(lane < N_EXPERTS, logits, -jnp.inf)
    m1 = jnp.max(logits, axis=-1, keepdims=True)
    i1 = jnp.min(jnp.where(logits == m1, lane, LANES), axis=-1, keepdims=True)
    rest = jnp.where(lane == i1, -jnp.inf, logits)
    m2 = jnp.max(rest, axis=-1, keepdims=True)
    i2 = jnp.min(jnp.where(rest == m2, lane, LANES), axis=-1, keepdims=True)
    e2 = jnp.exp(m2 - m1)
    w1 = 1.0 / (1.0 + e2)
    meta = jnp.where(lane == META_I1, i1.astype(F32), 0.0)
    meta = jnp.where(lane == META_I2, i2.astype(F32), meta)
    meta = jnp.where(lane == META_W1, w1, meta)
    meta_ref[...] = jnp.where(lane == META_W2, e2 * w1, meta)

    sel = jnp.where((lane == i1) | (lane == i2), 1.0, 0.0)
    row = lax.broadcasted_iota(jnp.int32, (tm, tm), 0)
    col = lax.broadcasted_iota(jnp.int32, (tm, tm), 1)
    before = jnp.where(col < row, 1.0, 0.0).astype(BF16)
    run = run_ref[...]
    pos_ref[...] = run + _dot(before, sel.astype(BF16))
    run = run + jnp.sum(sel, axis=0, keepdims=True)
    run_ref[...] = run
    cnt_ref[...] = run


def moe_router(x, g, w_router, *, tm=512):
    n, d = x.shape
    tm = min(tm, n)
    w = jnp.pad(w_router, ((0, 0), (0, LANES - w_router.shape[1])))
    row_spec = pl.BlockSpec((tm, LANES), lambda i: (i, 0))
    return pl.pallas_call(
        _router_kernel,
        grid=(n // tm,),
        in_specs=[pl.BlockSpec((tm, d), lambda i: (i, 0)),
                  pl.BlockSpec((1, d), lambda i: (0, 0)),
                  pl.BlockSpec((d, LANES), lambda i: (0, 0))],
        out_specs=[pl.BlockSpec((tm, d), lambda i: (i, 0)), row_spec, row_spec,
                   pl.BlockSpec((1, LANES), lambda i: (0, 0))],
        out_shape=[jax.ShapeDtypeStruct((n, d), F32), jax.ShapeDtypeStruct((n, LANES), F32),
                   jax.ShapeDtypeStruct((n, LANES), F32), jax.ShapeDtypeStruct((1, LANES), F32)],
        scratch_shapes=[pltpu.VMEM((1, LANES), F32)],
        compiler_params=_cparams("arbitrary"),
        name="moe_router",
    )(x, g.reshape(1, d), w)


MOE_TILE = 512


def _moe_kernel(te_ref, ok_ref, idx0_ref, idxn_ref, x_hbm, wg_ref, wu_ref, wd_ref, o_ref, xbuf, xb_ref, sem):
    i = pl.program_id(0)
    j = pl.program_id(1)
    nt = pl.num_programs(0)
    nf = pl.num_programs(1)
    tm = xb_ref.shape[0]
    per_step = tm // nf
    slot = i % 2

    def row_copy(idx_ref, r, s):
        return pltpu.make_async_copy(x_hbm.at[pl.ds(idx_ref[0, 0, r], 1), :], xbuf.at[s, pl.ds(r, 1), :], sem.at[s])

    @pl.when((i == 0) & (j == 0))
    def _():
        def start0(r, c):
            row_copy(idx0_ref, r, 0).start()
            return c
        lax.fori_loop(0, tm, start0, 0)

    @pl.when(j == 0)
    def _():
        for r in range(tm):
            row_copy(idx0_ref, r, slot).wait()
        xb_ref[...] = xbuf[slot].astype(BF16)
        o_ref[...] = jnp.zeros_like(o_ref)

    def prefetch():
        base = j * per_step
        for r in range(per_step):
            row_copy(idxn_ref, base + r, 1 - slot).start()

    def compute():
        xb = xb_ref[...]
        a = (_silu(_dot(xb, wg_ref[0])) * _dot(xb, wu_ref[0])).astype(BF16)
        o_ref[...] += _dot(a, wd_ref[0])

    has_next = i + 1 < nt
    ok = ok_ref[i] > 0

    @pl.when(has_next & ok)
    def _():
        prefetch()
        compute()

    @pl.when(has_next & jnp.logical_not(ok))
    def _():
        prefetch()

    @pl.when(jnp.logical_not(has_next) & ok)
    def _():
        compute()


def moe_experts(x, row_tok, tile_expert, tile_ok, w_gu, w_down, *, tf=1792):
    p = row_tok.shape[0]
    d = x.shape[1]
    ne, f, _ = w_down.shape
    tm = MOE_TILE
    nf = f // tf
    nt = p // tm
    idx = row_tok.reshape(nt, 1, tm)
    grid_spec = pltpu.PrefetchScalarGridSpec(
        num_scalar_prefetch=2,
        grid=(nt, nf),
        in_specs=[pl.BlockSpec((1, 1, tm), lambda i, j, te, ok: (0, 0, 0), memory_space=pltpu.SMEM),
                  pl.BlockSpec((1, 1, tm), lambda i, j, te, ok: (jnp.minimum(i + 1, nt - 1), 0, 0),
                               memory_space=pltpu.SMEM),
                  pl.BlockSpec(memory_space=pl.ANY),
                  pl.BlockSpec((1, d, tf), lambda i, j, te, ok: (te[i], 0, j)),
                  pl.BlockSpec((1, d, tf), lambda i, j, te, ok: (te[i], 0, j + nf)),
                  pl.BlockSpec((1, tf, d), lambda i, j, te, ok: (te[i], j, 0))],
        out_specs=pl.BlockSpec((tm, d), lambda i, j, te, ok: (i, 0)),
        scratch_shapes=[pltpu.VMEM((2, tm, d), F32), pltpu.VMEM((tm, d), BF16),
                        pltpu.SemaphoreType.DMA((2,))])
    return pl.pallas_call(
        _moe_kernel,
        grid_spec=grid_spec,
        out_shape=jax.ShapeDtypeStruct((p, d), F32),
        compiler_params=_cparams("arbitrary", "arbitrary"),
        name="moe_experts",
    )(tile_expert, tile_ok, idx, idx, x, w_gu, w_gu, w_down)


def _combine_kernel(d1_ref, d2_ref, x_ref, meta_ref, y_ref, gf_ref, o_ref, y1_ref, y2_ref, sem1, sem2):
    rows = o_ref.shape[0]

    def copy1(r):
        return pltpu.make_async_copy(y_ref.at[pl.ds(d1_ref[0, 0, r], 1), :], y1_ref.at[pl.ds(r, 1), :], sem1)

    def copy2(r):
        return pltpu.make_async_copy(y_ref.at[pl.ds(d2_ref[0, 0, r], 1), :], y2_ref.at[pl.ds(r, 1), :], sem2)

    def start(r, c):
        copy1(r).start()
        copy2(r).start()
        return c

    def wait(r, c):
        copy1(r).wait()
        copy2(r).wait()
        return c

    lax.fori_loop(0, rows, start, 0)
    lax.fori_loop(0, rows, wait, 0)
    meta = meta_ref[...]
    w1 = meta[:, META_W1:META_W1 + 1]
    w2 = meta[:, META_W2:META_W2 + 1]
    o_ref[...] = _rms(x_ref[...] + w1 * y1_ref[...] + w2 * y2_ref[...], gf_ref[...])


def moe_combine(x, meta, ys, d1, d2, g_final, *, tc=256):
    n, d = x.shape
    tc = min(tc, n)
    idx_spec = pl.BlockSpec((1, 1, tc), lambda i: (i, 0, 0), memory_space=pltpu.SMEM)
    return pl.pallas_call(
        _combine_kernel,
        grid=(n // tc,),
        in_specs=[idx_spec, idx_spec,
                  pl.BlockSpec((tc, d), lambda i: (i, 0)),
                  pl.BlockSpec((tc, LANES), lambda i: (i, 0)),
                  pl.BlockSpec(memory_space=pl.ANY),
                  pl.BlockSpec((1, d), lambda i: (0, 0))],
        out_specs=pl.BlockSpec((tc, d), lambda i: (i, 0)),
        out_shape=jax.ShapeDtypeStruct((n, d), F32),
        scratch_shapes=[pltpu.VMEM((tc, d), F32), pltpu.VMEM((tc, d), F32),
                        pltpu.SemaphoreType.DMA(()), pltpu.SemaphoreType.DMA(())],
        compiler_params=_cparams("parallel"),
        name="moe_combine",
    )(d1.reshape(n // tc, 1, tc), d2.reshape(n // tc, 1, tc), x, meta, ys, g_final.reshape(1, d))


def kernel(x, norm_mix, norm_ffn, a_w_in, a_b_gate, a_conv, a_norm_h, a_w_out, norm_kv, b_w_kv,
           b_cmp_pos, b_cmp_w1, b_cmp_w2, b_w_q, b_b_gate, b_w_out, f_w_gu, f_w_down,
           m_router, m_w_gu, m_w_down, norm_final):
    B, T, D = x.shape
    N = B * T
    G, R, dh = B_KV_GROUPS, B_REP, B_HEAD_DIM
    xs = x.reshape(N, D)

    inner4 = a_w_in.shape[2] - 2 * A_HEADS
    w_in = a_w_in[0]
    proj = norm_matmul(xs, norm_mix[0], w_in[:, :inner4].astype(BF16), out_dtype=BF16)
    w_gate = jnp.pad(w_in[:, inner4:], ((0, 0), (0, LANES - 2 * A_HEADS)))
    gcol = norm_matmul(xs, norm_mix[0], w_gate, out_dtype=F32, exact=True, tn=LANES)
    hs = mlstm_core(proj, gcol, a_b_gate[0], a_conv[0], a_norm_h[0], B, T)
    xs = matmul_residual(hs, a_w_out[0].astype(BF16), xs)
    xs = ffn_dense(xs, norm_ffn[0], f_w_gu[0].astype(BF16), f_w_down[0].astype(BF16))

    hd = B_HEADS * dh
    feat_w = LANES - dh
    ncmp = T // CMP_STRIDE

    def slabs(w):
        d_in, cols = w.shape
        return jnp.pad(w.reshape(d_in, cols // dh, dh), ((0, 0), (0, 0), (0, feat_w))).reshape(d_in, -1)

    cmp_cols = 2 * G * dh
    w_kvx = jnp.concatenate([b_w_kv[:, :cmp_cols], slabs(b_w_kv[:, cmp_cols:])], axis=1).astype(BF16)
    kvx = norm_matmul(xs, norm_kv, w_kvx, out_dtype=BF16, tn=w_kvx.shape[1] // 2)
    w_q = b_w_q[0]
    qp = norm_matmul(xs, norm_mix[1], slabs(w_q[:, :hd] * (dh ** -0.5 * LOG2E)).astype(BF16), out_dtype=BF16)
    w_qg = jnp.pad(w_q[:, hd:], ((0, 0), (0, LANES - 3 * B_HEADS)))
    gl = norm_matmul(xs, norm_mix[1], w_qg, out_dtype=F32, exact=True, tn=LANES)

    kvt = kvx[:, :cmp_cols].reshape(B, T, 2 * G, dh).transpose(0, 2, 1, 3)
    pos = b_cmp_pos.transpose(1, 0, 2).reshape(2, 1, CMP_LEN * dh).astype(BF16)
    kvc = nsa_compress(kvt.reshape(B, 2 * G, ncmp, CMP_STRIDE * dh), pos,
                       b_cmp_w1.astype(BF16), b_cmp_w2.astype(BF16))

    def hi_lanes(f):
        return jnp.asarray(np.concatenate([np.zeros((f.shape[0], dh), np.float32), f], axis=1), BF16)

    key_feat = hi_lanes(_pos_features(np.arange(T), feat_w))
    cmp_feat = jnp.asarray(_pos_features(np.arange(ncmp) * CMP_STRIDE + CMP_LEN - 1, feat_w), BF16)
    q_feat = hi_lanes(_slope_features(feat_w)).reshape(G, R, 1, LANES)
    ones_col = np.zeros((1, feat_w), np.float32)
    ones_col[0, 0] = 1.0
    ones_col = hi_lanes(ones_col)
    blk_onehot = (np.arange(T)[:, None] // SEL_BLOCK == np.arange(LANES)[None, :]).astype(np.float32)
    blk_onehot = jnp.asarray(blk_onehot, BF16)
    kc = jnp.concatenate([kvc[:, :G].astype(BF16), jnp.broadcast_to(cmp_feat, (B, G, ncmp, feat_w))], axis=-1)
    vc = kvc[:, G:].astype(BF16)
    glt = gl[:, :3 * B_HEADS].reshape(B, T, G, 3 * R).transpose(0, 2, 1, 3)
    bg = b_b_gate[0].reshape(G, 1, 3 * R)

    nsel = T // SEL_BLOCK
    ci = np.arange(ncmp)[:, None] * CMP_STRIDE
    sj = np.arange(LANES)[None, :] * SEL_BLOCK
    ov = ((ci < sj + SEL_BLOCK) & (ci + CMP_LEN > sj) & (np.arange(LANES)[None, :] < nsel)
          & (np.arange(ncmp)[:, None] < ncmp - 1))
    ov = jnp.asarray(ov.astype(np.float32), BF16)

    oa = nsa_attention(qp.reshape(B, T, B_HEADS * LANES), q_feat, kc, vc, kvx.reshape(B, T, -1),
                       cmp_cols // LANES, key_feat, blk_onehot, ones_col, glt, bg, ov)
    xs = matmul_residual(oa.reshape(N, hd), b_w_out[0].astype(BF16), xs)

    xn, meta, pos, cnt = moe_router(xs, norm_ffn[1], m_router[0])
    ne = N_EXPERTS
    p_rows = 2 * N + ne * MOE_TILE
    i1 = meta[:, META_I1].astype(jnp.int32)
    i2 = meta[:, META_I2].astype(jnp.int32)
    counts = cnt[0, :ne].astype(jnp.int32)
    padded = (counts + MOE_TILE - 1) // MOE_TILE * MOE_TILE
    seg_end = jnp.cumsum(padded)
    seg_start = seg_end - padded
    pos8 = pos[:, :ne].astype(jnp.int32)
    d1 = seg_start[i1] + jnp.take_along_axis(pos8, i1[:, None], axis=1)[:, 0]
    d2 = seg_start[i2] + jnp.take_along_axis(pos8, i2[:, None], axis=1)[:, 0]
    tok = jnp.arange(N, dtype=jnp.int32)
    row_tok = jnp.zeros((p_rows,), jnp.int32).at[d1].set(tok).at[d2].set(tok)
    tile_start = jnp.arange(p_rows // MOE_TILE, dtype=jnp.int32) * MOE_TILE
    tile_expert = jnp.minimum(jnp.sum(tile_start[:, None] >= seg_end[None, :], axis=1), ne - 1).astype(jnp.int32)
    tile_ok = (tile_start < seg_end[ne - 1]).astype(jnp.int32)

    ys = moe_experts(xn, row_tok, tile_expert, tile_ok, m_w_gu[0].astype(BF16), m_w_down[0].astype(BF16))
    out = moe_combine(xs, meta, ys, d1, d2, norm_final)
    return out.reshape(B, T, D)
```

```python
import functools

import numpy as np
import jax
import jax.numpy as jnp
from jax import lax
from jax.experimental import pallas as pl
from jax.experimental.pallas import tpu as pltpu

F32 = jnp.float32
BF16 = jnp.bfloat16

RMS_EPS = 1e-6
A_HEADS = 4
A_CONV = 4
B_HEADS = 16
B_KV_GROUPS = 4
B_REP = B_HEADS // B_KV_GROUPS
B_HEAD_DIM = 64
CMP_LEN = 32
CMP_STRIDE = 16
SEL_BLOCK = 64
SEL_TOPN = 16
WINDOW = 512
FORCED_SCORE = 1e4
N_EXPERTS = 8

LANES = 128
V7X_VMEM_BYTES = 64 * 1024 * 1024
VMEM_LIMIT = V7X_VMEM_BYTES - 8 * 1024 * 1024
NEG = -1e30


def _cparams(*sem):
    return pltpu.CompilerParams(dimension_semantics=sem, vmem_limit_bytes=VMEM_LIMIT)


def _dot(a, b):
    return jnp.dot(a, b, preferred_element_type=F32)


def _dot_nt(a, b):
    return lax.dot_general(a, b, (((1,), (1,)), ((), ())), preferred_element_type=F32)


def _dot_tn(a, b):
    return lax.dot_general(a, b, (((0,), (0,)), ((), ())), preferred_element_type=F32)


def _split3(a):
    a1 = a.astype(BF16)
    r1 = a - a1.astype(F32)
    a2 = r1.astype(BF16)
    a3 = (r1 - a2.astype(F32)).astype(BF16)
    return a1, a2, a3


def _dot_f32(a, b):
    a1, a2, a3 = _split3(a)
    b1, b2, b3 = _split3(b)
    return (_dot(a1, b1) + _dot(a1, b2) + _dot(a2, b1)
            + _dot(a2, b2) + _dot(a1, b3) + _dot(a3, b1))


def _dot_f32_nt(a, b):
    a1, a2, a3 = _split3(a)
    b1, b2, b3 = _split3(b)
    return (_dot_nt(a1, b1) + _dot_nt(a1, b2) + _dot_nt(a2, b1)
            + _dot_nt(a2, b2) + _dot_nt(a1, b3) + _dot_nt(a3, b1))


def _rms(xf, g):
    return xf * lax.rsqrt(jnp.mean(xf * xf, axis=-1, keepdims=True) + RMS_EPS) * g


def _silu(x):
    return x * jax.nn.sigmoid(x)


def _log_sigmoid(x):
    return jnp.minimum(x, 0.0) - jnp.log(1.0 + jnp.exp(-jnp.abs(x)))


def _norm_matmul_kernel(*refs, side):
    if side:
        x_ref, g_ref, w_ref, ws_ref, o_ref, os_ref, xn_ref = refs
    else:
        x_ref, g_ref, w_ref, o_ref, xn_ref = refs

    @pl.when(pl.program_id(1) == 0)
    def _():
        xn = _rms(x_ref[...], g_ref[...])
        xn_ref[...] = xn.astype(xn_ref.dtype)
        if side:
            os_ref[...] = _dot_f32(xn, ws_ref[...])

    o_ref[...] = _dot(xn_ref[...], w_ref[...]).astype(o_ref.dtype)


def norm_matmul(x, g, w, *, w_side=None, tm=1024, tn=1024):
    n, d = x.shape
    dout = w.shape[1]
    tm = min(tm, n)
    tn = min(tn, dout)
    side = w_side is not None
    in_specs = [pl.BlockSpec((tm, d), lambda i, j: (i, 0)),
                pl.BlockSpec((1, d), lambda i, j: (0, 0)),
                pl.BlockSpec((d, tn), lambda i, j: (0, j))]
    out_specs = [pl.BlockSpec((tm, tn), lambda i, j: (i, j))]
    out_shape = [jax.ShapeDtypeStruct((n, dout), BF16)]
    args = [x, g.reshape(1, d), w]
    if side:
        in_specs.append(pl.BlockSpec((d, LANES), lambda i, j: (0, 0)))
        out_specs.append(pl.BlockSpec((tm, LANES), lambda i, j: (i, 0)))
        out_shape.append(jax.ShapeDtypeStruct((n, LANES), F32))
        args.append(w_side)
    out = pl.pallas_call(
        functools.partial(_norm_matmul_kernel, side=side),
        grid=(n // tm, dout // tn),
        in_specs=in_specs,
        out_specs=out_specs,
        out_shape=out_shape,
        scratch_shapes=[pltpu.VMEM((tm, d), BF16)],
        compiler_params=_cparams("parallel", "arbitrary"),
        name="norm_matmul_side" if side else "norm_matmul",
    )(*args)
    return out if side else out[0]


def _matmul_res_kernel(a_ref, w_ref, r_ref, o_ref):
    o_ref[...] = r_ref[...] + _dot(a_ref[...], w_ref[...])


def matmul_residual(a, w, res, *, tm=1024):
    n, k = a.shape
    dout = w.shape[1]
    tm = min(tm, n)
    return pl.pallas_call(
        _matmul_res_kernel,
        grid=(n // tm,),
        in_specs=[pl.BlockSpec((tm, k), lambda i: (i, 0)),
                  pl.BlockSpec((k, dout), lambda i: (0, 0)),
                  pl.BlockSpec((tm, dout), lambda i: (i, 0))],
        out_specs=pl.BlockSpec((tm, dout), lambda i: (i, 0)),
        out_shape=jax.ShapeDtypeStruct((n, dout), F32),
        compiler_params=_cparams("parallel"),
        name="matmul_residual",
    )(a, w, res)


MLSTM_CHUNK = 256


def _mlstm_kernel(qk_ref, v_ref, o_ref, gcol_ref, grow_ref, bcol_ref, brow_ref, convw_ref, gout_ref,
                  out_ref, ct_ref, n_ref, m_ref, prev_ref):
    L = qk_ref.shape[0]
    H = A_HEADS
    inner = v_ref.shape[1]
    dh = inner // H

    @pl.when(pl.program_id(1) == 0)
    def _():
        ct_ref[...] = jnp.zeros_like(ct_ref)
        n_ref[...] = jnp.zeros_like(n_ref)
        m_ref[...] = jnp.zeros_like(m_ref)
        prev_ref[...] = jnp.zeros_like(prev_ref)

    row = lax.broadcasted_iota(jnp.int32, (L, L), 0)
    col = lax.broadcasted_iota(jnp.int32, (L, L), 1)
    causal = col <= row
    tril = jnp.where(causal, 1.0, 0.0).astype(BF16)
    triu = jnp.where(row <= col, 1.0, 0.0).astype(BF16)

    gc = gcol_ref[...] + brow_ref[...]
    gr = grow_ref[...] + bcol_ref[...]
    lfc1, lfc2, lfc3 = _split3(_log_sigmoid(gc))
    lfr1, lfr2, lfr3 = _split3(_log_sigmoid(gr))
    b_c = _dot(tril, lfc1) + _dot(tril, lfc2) + _dot(tril, lfc3)
    b_r = _dot(lfr1, triu) + _dot(lfr2, triu) + _dot(lfr3, triu)

    rowi = lax.broadcasted_iota(jnp.int32, (L, dh), 0)

    def conv(cur, prev, w):
        y = cur * w[A_CONV - 1:A_CONV, :]
        for s in range(1, A_CONV):
            sh = jnp.where(rowi < s, pltpu.roll(prev, s, 0), pltpu.roll(cur, s, 0))
            y = y + sh * w[A_CONV - 1 - s:A_CONV - s, :]
        return y

    for h in range(H):
        hs = slice(h * dh, (h + 1) * dh)
        ks = slice(inner + h * dh, inner + (h + 1) * dh)
        q = conv(qk_ref[:, hs].astype(F32), prev_ref[:, hs].astype(F32), convw_ref[:, hs])
        k = conv(qk_ref[:, ks].astype(F32), prev_ref[:, ks].astype(F32), convw_ref[:, ks]) * (dh ** -0.5)
        v = v_ref[:, hs]
        qb = q.astype(BF16)
        kb = k.astype(BF16)

        li_c = gc[:, h:h + 1]
        bc = b_c[:, H + h:H + h + 1]
        li_r = gr[h:h + 1, :]
        br = b_r[H + h:H + h + 1, :]
        m_prev = m_ref[h:h + 1, 0:1]

        d = jnp.where(causal, bc - br + li_r, -jnp.inf)
        inter = bc + m_prev
        m_t = jnp.maximum(inter, jnp.max(d, axis=-1, keepdims=True))
        w_inter = jnp.exp(inter - m_t)
        s = _dot_nt(qb, kb) * jnp.exp(d - m_t)
        ct = ct_ref[h]
        num = _dot(s.astype(BF16), v) + w_inter * _dot(qb, ct.astype(BF16))
        den = jnp.sum(s, axis=-1, keepdims=True) + w_inter * jnp.sum(q * n_ref[h], axis=-1, keepdims=True)
        hh = num / jnp.maximum(jnp.abs(den), jnp.exp(-m_t))
        hh = hh * lax.rsqrt(jnp.mean(hh * hh, axis=-1, keepdims=True) + RMS_EPS)
        out_ref[:, hs] = (hh * gout_ref[:, hs] * jax.nn.sigmoid(o_ref[:, hs].astype(F32))).astype(out_ref.dtype)

        b_last = bc[L - 1:L, :]
        g = b_last - bc + li_c
        m_new = jnp.maximum(b_last + m_prev, jnp.max(g, axis=0, keepdims=True))
        a_prev = jnp.exp(b_last + m_prev - m_new)
        a_s = jnp.exp(g - m_new)
        ct_ref[h] = a_prev * ct + _dot_tn(kb, (v.astype(F32) * a_s).astype(BF16))
        n_ref[h] = a_prev * n_ref[h] + jnp.sum(k * a_s, axis=0, keepdims=True)
        m_ref[h:h + 1, :] = jnp.broadcast_to(m_new, (1, LANES))

    prev_ref[...] = qk_ref[...]


def mlstm_core(proj, gcol, b_gate, conv_w, g_out, batch, seq):
    n = proj.shape[0]
    inner = proj.shape[1] // 4
    H = A_HEADS
    dh = inner // H
    L = min(MLSTM_CHUNK, seq)
    nc = seq // L
    grow = gcol[:, :2 * H].T
    brow = jnp.pad(b_gate, (0, LANES - 2 * H)).reshape(1, LANES)
    bcol = b_gate.reshape(2 * H, 1)
    return pl.pallas_call(
        _mlstm_kernel,
        grid=(batch, nc),
        in_specs=[pl.BlockSpec((L, 2 * inner), lambda b, c: (b * nc + c, 0)),
                  pl.BlockSpec((L, inner), lambda b, c: (b * nc + c, 2)),
                  pl.BlockSpec((L, inner), lambda b, c: (b * nc + c, 3)),
                  pl.BlockSpec((L, LANES), lambda b, c: (b * nc + c, 0)),
                  pl.BlockSpec((2 * H, L), lambda b, c: (0, b * nc + c)),
                  pl.BlockSpec((2 * H, 1), lambda b, c: (0, 0)),
                  pl.BlockSpec((1, LANES), lambda b, c: (0, 0)),
                  pl.BlockSpec((A_CONV, 2 * inner), lambda b, c: (0, 0)),
                  pl.BlockSpec((1, inner), lambda b, c: (0, 0))],
        out_specs=pl.BlockSpec((L, inner), lambda b, c: (b * nc + c, 0)),
        out_shape=jax.ShapeDtypeStruct((n, inner), BF16),
        scratch_shapes=[pltpu.VMEM((H, dh, dh), F32),
                        pltpu.VMEM((H, 1, dh), F32),
                        pltpu.VMEM((8, LANES), F32),
                        pltpu.VMEM((L, 2 * inner), BF16)],
        compiler_params=_cparams("parallel", "arbitrary"),
        name="mlstm_core",
    )(proj, proj, proj, gcol, grow, bcol, brow, conv_w, g_out.reshape(1, inner))


def _ffn_kernel(x_ref, g_ref, wg_ref, wu_ref, wd_ref, o_ref, xn_ref, acc_ref):
    j = pl.program_id(1)

    @pl.when(j == 0)
    def _():
        xf = x_ref[...]
        xn_ref[...] = _rms(xf, g_ref[...]).astype(BF16)
        acc_ref[...] = xf

    xn = xn_ref[...]
    a = (_silu(_dot(xn, wg_ref[...])) * _dot(xn, wu_ref[...])).astype(BF16)
    acc_ref[...] += _dot(a, wd_ref[...])

    @pl.when(j == pl.num_programs(1) - 1)
    def _():
        o_ref[...] = acc_ref[...]


def ffn_dense(x, g, w_gu, w_down, *, tm=512, tf=1408):
    n, d = x.shape
    f = w_down.shape[0]
    tm = min(tm, n)
    nf = f // tf
    return pl.pallas_call(
        _ffn_kernel,
        grid=(n // tm, nf),
        in_specs=[pl.BlockSpec((tm, d), lambda i, j: (i, 0)),
                  pl.BlockSpec((1, d), lambda i, j: (0, 0)),
                  pl.BlockSpec((d, tf), lambda i, j: (0, j)),
                  pl.BlockSpec((d, tf), lambda i, j: (0, j + nf)),
                  pl.BlockSpec((tf, d), lambda i, j: (j, 0))],
        out_specs=pl.BlockSpec((tm, d), lambda i, j: (i, 0)),
        out_shape=jax.ShapeDtypeStruct((n, d), F32),
        scratch_shapes=[pltpu.VMEM((tm, d), BF16), pltpu.VMEM((tm, d), F32)],
        compiler_params=_cparams("parallel", "arbitrary"),
        name="ffn_dense",
    )(x, g.reshape(1, d), w_gu, w_gu, w_down)


def _compress_kernel(r_ref, pos_ref, w1_ref, w2_ref, o_ref):
    r = r_ref[0, 0]
    w1 = w1_ref[0]
    half = r.shape[1]
    nc = r.shape[0]
    lo = _dot(r, w1[:half])
    hi = _dot(r, w1[half:])
    hid = lo + pltpu.roll(hi, nc - 1, 0) + _dot(pos_ref[0], w1)
    o_ref[0, 0] = _dot(_silu(hid).astype(BF16), w2_ref[0]).astype(o_ref.dtype)


def nsa_compress(r, pos, w1, w2):
    b, c2, nc, half = r.shape
    g = c2 // 2
    hidden = w1.shape[2]
    dh = w2.shape[2]
    return pl.pallas_call(
        _compress_kernel,
        grid=(b, c2),
        in_specs=[pl.BlockSpec((1, 1, nc, half), lambda i, j: (i, j, 0, 0)),
                  pl.BlockSpec((1, 1, 2 * half), lambda i, j: (j // g, 0, 0)),
                  pl.BlockSpec((1, 2 * half, hidden), lambda i, j: (j // g, 0, 0)),
                  pl.BlockSpec((1, hidden, dh), lambda i, j: (j // g, 0, 0))],
        out_specs=pl.BlockSpec((1, 1, nc, dh), lambda i, j: (i, j, 0, 0)),
        out_shape=jax.ShapeDtypeStruct((b, c2, nc, dh), F32),
        compiler_params=_cparams("parallel", "parallel"),
        name="nsa_compress",
    )(r, pos, w1, w2)


NSA_TQ = 256
NSA_TK_SLC = 512
NSA_TK_WIN = 256
NSA_ROWS = 32
SEL_MASK = 2.0 ** 14
LOG2E = float(np.log2(np.e))


def _nsa_kernel(q_ref, qf_ref, kc_ref, vc_ref, ks_ref, vs_ref, kw_ref, vw_ref, kf_ref, oh_ref, one_ref,
                gl_ref, bg_ref, ov_ref,
                out_ref, qx_ref, ksx_ref, vsx_ref, kwx_ref, vwx_ref,
                ms_ref, accs_ref, biass_ref, s0_ref, p0_ref, a0_ref, s1_ref, p1_ref, a1_ref,
                mw_ref, accw_ref, biasw_ref, s2_ref, p2_ref, a2_ref, s3_ref, p3_ref, a3_ref, live_ref,
                *, tq, tks, tkw, n_top):
    R = B_REP
    M = R * tq
    dh = B_HEAD_DIM
    t0 = pl.program_id(2) * tq
    slc_state = (ms_ref, accs_ref, biass_ref, ((s0_ref, p0_ref, a0_ref), (s1_ref, p1_ref, a1_ref)))
    win_state = (mw_ref, accw_ref, biasw_ref, ((s2_ref, p2_ref, a2_ref), (s3_ref, p3_ref, a3_ref)))

    @pl.when(pl.program_id(2) == 0)
    def _():
        kf = kf_ref[...]
        ksx_ref[:, :LANES] = ks_ref[0] + kf
        ksx_ref[:, LANES:] = oh_ref[...]
        kwx_ref[...] = kw_ref[0] + kf
        vsx_ref[...] = vs_ref[0] + one_ref[...]
        vwx_ref[...] = vw_ref[0] + one_ref[...]

    qb = q_ref[0]
    for r in range(R):
        qx_ref[r * tq:(r + 1) * tq, :LANES] = qb[:, r * LANES:(r + 1) * LANES] + qf_ref[0, r]
    q = qx_ref[:, :LANES]

    kc = kc_ref[0, 0]
    ncmp = kc.shape[0]
    s = _dot_nt(q, kc)
    tpos_c = t0 + (lax.broadcasted_iota(jnp.int32, (M, ncmp), 0) & (tq - 1))
    cend = lax.broadcasted_iota(jnp.int32, (M, ncmp), 1) * CMP_STRIDE + (CMP_LEN - 1)
    ok_c = tpos_c >= cend
    s = jnp.where(ok_c, s, NEG)
    p = jnp.where(ok_c, jnp.exp2(s - jnp.max(s, axis=-1, keepdims=True)), 0.0)
    dsum = jnp.sum(p, axis=-1, keepdims=True)
    p = p / jnp.where(dsum > 0, dsum, 1.0)
    o_cmp = _dot(p.astype(BF16), vc_ref[0, 0])

    psum = p[0:tq]
    for r in range(1, R):
        psum = psum + p[r * tq:(r + 1) * tq]
    p1, p2, p3 = _split3(psum)
    ov = ov_ref[...]
    imp = _dot(p1, ov) + _dot(p2, ov) + _dot(p3, ov)
    nsel = ks_ref.shape[1] // SEL_BLOCK
    nselp = max(nsel, 8)
    imp_t = imp.T[:nselp]
    jj = lax.broadcasted_iota(jnp.int32, (nselp, tq), 0)
    tt = t0 + lax.broadcasted_iota(jnp.int32, (nselp, tq), 1)
    cur = tt // SEL_BLOCK
    forced = (jj == 0) | (jj == cur) | (jj == cur - 1)
    iv = jnp.where(forced, FORCED_SCORE, imp_t)
    iv = jnp.where(jj * SEL_BLOCK <= tt, iv, -jnp.inf)
    rank = jnp.zeros((nselp, tq), F32)
    for j2 in range(nsel):
        rv = iv[j2:j2 + 1, :]
        rank = rank + jnp.where(jj > j2, jnp.where(rv >= iv, 1.0, 0.0), jnp.where(rv > iv, 1.0, 0.0))
    unsel = jnp.where(rank < n_top, 0.0, -SEL_MASK)
    if nselp < LANES:
        unsel = jnp.concatenate([unsel, jnp.zeros((LANES - nselp, tq), F32)], axis=0)
    qmask = unsel.T.astype(BF16)
    for r in range(R):
        qx_ref[r * tq:(r + 1) * tq, LANES:] = qmask

    def tile_step(qq, k_ref, v_ref, k0, tk, state, slot, bias):
        m_ref, acc_ref, bias_ref, slots = state
        s_ref, p_ref, a_ref = slots[slot]
        kk = k_ref[pl.ds(k0, tk), :]
        vv = v_ref[pl.ds(k0, tk), :]
        if bias is not None:
            bias_ref[:, :tk] = bias
        s_ref[:, :tk] = _dot_nt(qq, kk)
        for r in range(R):
            for c0 in range(0, tq, NSA_ROWS):
                rows = slice(r * tq + c0, r * tq + c0 + NSA_ROWS)
                sc = s_ref[rows, :tk]
                if bias is not None:
                    sc = sc + bias_ref[c0:c0 + NSA_ROWS, :tk]
                m_old = m_ref[rows, :]
                m_new = jnp.maximum(m_old, jnp.max(sc, axis=-1, keepdims=True))
                a_ref[rows, :] = jnp.exp2(m_old - m_new)
                m_ref[rows, :] = m_new
                p_ref[rows, :tk] = jnp.exp2(sc - jnp.concatenate([m_new] * (tk // LANES), axis=1)).astype(BF16)
        acc_ref[...] = a_ref[...] * acc_ref[...] + _dot(p_ref[:, :tk], vv)

    def reset(state):
        state[0][...] = jnp.full_like(state[0], NEG)
        state[1][...] = jnp.zeros_like(state[1])

    def result(state):
        acc = state[1][...]
        return acc[:, :dh] / acc[:, dh:dh + 1]

    reset(slc_state)
    qx = qx_ref[...]
    n_full = t0 // tks

    blocks_per_tile = tks // SEL_BLOCK
    n_live = jnp.int32(0)
    for j in range(nsel // blocks_per_tile):
        sel_j = rank[j * blocks_per_tile:(j + 1) * blocks_per_tile, :] < n_top
        live = (jnp.max(jnp.where(sel_j, 1.0, 0.0)) > 0.5) & (j < n_full)
        live_ref[n_live] = j
        n_live = n_live + live.astype(jnp.int32)

    def slc_pair(i, carry):
        tile_step(qx, ksx_ref, vsx_ref, pl.multiple_of(live_ref[2 * i] * tks, tks), tks, slc_state, 0, None)
        tile_step(qx, ksx_ref, vsx_ref, pl.multiple_of(live_ref[2 * i + 1] * tks, tks), tks, slc_state, 1, None)
        return carry

    lax.fori_loop(0, n_live // 2, slc_pair, 0)

    @pl.when(n_live % 2 == 1)
    def _():
        tile_step(qx, ksx_ref, vsx_ref, pl.multiple_of(live_ref[n_live - 1] * tks, tks), tks, slc_state, 0, None)

    kd = pl.multiple_of(n_full * tks, tks)
    spos = kd + lax.broadcasted_iota(jnp.int32, (tq, tks), 1)
    tpos = t0 + lax.broadcasted_iota(jnp.int32, (tq, tks), 0)
    tile_step(qx, ksx_ref, vsx_ref, kd, tks, slc_state, 1, jnp.where(spos <= tpos, 0.0, NEG))
    reset(win_state)
    n_win = WINDOW // tkw + max(tq // tkw, 1)
    last = (t0 + tq - 1) // tkw
    for i in range(n_win):
        kw0 = (last - (n_win - 1) + i) * tkw
        spos = kw0 + lax.broadcasted_iota(jnp.int32, (tq, tkw), 1)
        dist = t0 + lax.broadcasted_iota(jnp.int32, (tq, tkw), 0) - spos
        bias = jnp.where((spos >= 0) & (dist >= 0) & (dist < WINDOW), 0.0, NEG)
        tile_step(q, kwx_ref, vwx_ref, pl.multiple_of(jnp.maximum(kw0, 0), tkw), tkw, win_state, i % 2, bias)
    o_slc = result(slc_state)
    o_win = result(win_state)

    gates = jax.nn.sigmoid(gl_ref[0, 0, pl.ds(pl.multiple_of(t0, tq), tq), :] + bg_ref[0])
    outs = []
    for r in range(R):
        rs = slice(r * tq, (r + 1) * tq)
        outs.append(gates[:, 3 * r:3 * r + 1] * o_cmp[rs] + gates[:, 3 * r + 1:3 * r + 2] * o_slc[rs]
                    + gates[:, 3 * r + 2:3 * r + 3] * o_win[rs])
    out_ref[0] = jnp.concatenate(outs, axis=1).astype(out_ref.dtype)


def nsa_attention(qp, q_feat, kc, vc, kvx, kv_off, key_feat, blk_onehot, ones_col, gl, bg, ov):
    b, t, _ = qp.shape
    g, r = q_feat.shape[:2]
    dh = B_HEAD_DIM
    tq = min(NSA_TQ, t)
    tks = min(NSA_TK_SLC, t)
    tkw = min(NSA_TK_WIN, t)
    ncmp = kc.shape[2]
    n_top = min(SEL_TOPN, t // SEL_BLOCK)
    kv_spec = lambda c: pl.BlockSpec((1, t, LANES), lambda i, j, k: (i, 0, kv_off + c * g + j))
    const_spec = lambda a: pl.BlockSpec(a.shape, lambda i, j, k: (0,) * a.ndim)

    def branch_scratch(tk):
        slot = [pltpu.VMEM((r * tq, tk), F32), pltpu.VMEM((r * tq, tk), BF16), pltpu.VMEM((r * tq, LANES), F32)]
        return [pltpu.VMEM((r * tq, LANES), F32), pltpu.VMEM((r * tq, LANES), F32),
                pltpu.VMEM((tq, tk), F32)] + slot + slot

    return pl.pallas_call(
        functools.partial(_nsa_kernel, tq=tq, tks=tks, tkw=tkw, n_top=n_top),
        grid=(b, g, t // tq),
        in_specs=[pl.BlockSpec((1, tq, r * LANES), lambda i, j, k: (i, k, j)),
                  pl.BlockSpec((1, r, 1, LANES), lambda i, j, k: (j, 0, 0, 0)),
                  pl.BlockSpec((1, 1, ncmp, LANES), lambda i, j, k: (i, j, 0, 0)),
                  pl.BlockSpec((1, 1, ncmp, dh), lambda i, j, k: (i, j, 0, 0)),
                  kv_spec(0), kv_spec(1), kv_spec(2), kv_spec(3),
                  const_spec(key_feat), const_spec(blk_onehot), const_spec(ones_col),
                  pl.BlockSpec((1, 1, t, 3 * r), lambda i, j, k: (i, j, 0, 0)),
                  pl.BlockSpec((1, 1, 3 * r), lambda i, j, k: (j, 0, 0)),
                  const_spec(ov)],
        out_specs=pl.BlockSpec((1, tq, r * dh), lambda i, j, k: (i, k, j)),
        out_shape=jax.ShapeDtypeStruct((b, t, g * r * dh), BF16),
        scratch_shapes=[pltpu.VMEM((r * tq, 2 * LANES), BF16),
                        pltpu.VMEM((t, 2 * LANES), BF16),
                        pltpu.VMEM((t, LANES), BF16),
                        pltpu.VMEM((t, LANES), BF16),
                        pltpu.VMEM((t, LANES), BF16),
                        ] + branch_scratch(tks) + branch_scratch(tkw)
                       + [pltpu.SMEM((t // tks + 1,), jnp.int32)],
        compiler_params=_cparams("arbitrary", "arbitrary", "arbitrary"),
        name="nsa_attention",
    )(qp, q_feat, kc, vc, kvx, kvx, kvx, kvx, key_feat, blk_onehot, ones_col, gl, bg, ov)


def _alibi_slopes(n):
    return np.power(2.0, -8.0 * np.arange(1, n + 1) / n).astype(np.float32)


def _np_split3(a):
    a = np.asarray(a, np.float32)
    out = []
    r = a
    for _ in range(3):
        p = r.astype(BF16).astype(np.float32)
        out.append(p)
        r = (r - p).astype(np.float32)
    return out


def _pos_features(pos, width):
    hi = (pos // 64).astype(np.float32)
    lo = (pos % 64).astype(np.float32)
    f = np.zeros((pos.shape[0], width), np.float32)
    f[:, 0:3] = hi[:, None]
    f[:, 3:6] = lo[:, None]
    return f


def _slope_features(width):
    s1, s2, s3 = _np_split3((_alibi_slopes(B_HEADS).astype(np.float64) * LOG2E).astype(np.float32))
    f = np.zeros((B_HEADS, width), np.float32)
    for i, s in enumerate((s1, s2, s3)):
        f[:, i] = 64.0 * s
        f[:, 3 + i] = s
    return f


META_I1, META_I2, META_W1, META_W2 = 0, 1, 2, 3


def _router_kernel(x_ref, g_ref, w_ref, xn_ref, meta_ref, pos_ref, cnt_ref, run_ref):
    @pl.when(pl.program_id(0) == 0)
    def _():
        run_ref[...] = jnp.zeros_like(run_ref)

    xn = _rms(x_ref[...], g_ref[...])
    xn_ref[...] = xn
    logits = _dot_f32(xn, w_ref[...])
    tm = logits.shape[0]
    lane = lax.broadcasted_iota(jnp.int32, logits.shape, 1)
    logits = jnp.where(lane < N_EXPERTS, logits, -jnp.inf)
    m1 = jnp.max(logits, axis=-1, keepdims=True)
    i1 = jnp.min(jnp.where(logits == m1, lane, LANES), axis=-1, keepdims=True)
    rest = jnp.where(lane == i1, -jnp.inf, logits)
    m2 = jnp.max(rest, axis=-1, keepdims=True)
    i2 = jnp.min(jnp.where(rest == m2, lane, LANES), axis=-1, keepdims=True)
    e2 = jnp.exp(m2 - m1)
    w1 = 1.0 / (1.0 + e2)
    meta = jnp.where(lane == META_I1, i1.astype(F32), 0.0)
    meta = jnp.where(lane == META_I2, i2.astype(F32), meta)
    meta = jnp.where(lane == META_W1, w1, meta)
    meta_ref[...] = jnp.where(lane == META_W2, e2 * w1, meta)

    sel = jnp.where((lane == i1) | (lane == i2), 1.0, 0.0)
    row = lax.broadcasted_iota(jnp.int32, (tm, tm), 0)
    col = lax.broadcasted_iota(jnp.int32, (tm, tm), 1)
    before = jnp.where(col < row, 1.0, 0.0).astype(BF16)
    run = run_ref[...]
    pos_ref[...] = run + _dot(before, sel.astype(BF16))
    run = run + jnp.sum(sel, axis=0, keepdims=True)
    run_ref[...] = run
    cnt_ref[...] = run


def moe_router(x, g, w_router, *, tm=512):
    n, d = x.shape
    tm = min(tm, n)
    w = jnp.pad(w_router, ((0, 0), (0, LANES - w_router.shape[1])))
    row_spec = pl.BlockSpec((tm, LANES), lambda i: (i, 0))
    return pl.pallas_call(
        _router_kernel,
        grid=(n // tm,),
        in_specs=[pl.BlockSpec((tm, d), lambda i: (i, 0)),
                  pl.BlockSpec((1, d), lambda i: (0, 0)),
                  pl.BlockSpec((d, LANES), lambda i: (0, 0))],
        out_specs=[pl.BlockSpec((tm, d), lambda i: (i, 0)), row_spec, row_spec,
                   pl.BlockSpec((1, LANES), lambda i: (0, 0))],
        out_shape=[jax.ShapeDtypeStruct((n, d), F32), jax.ShapeDtypeStruct((n, LANES), F32),
                   jax.ShapeDtypeStruct((n, LANES), F32), jax.ShapeDtypeStruct((1, LANES), F32)],
        scratch_shapes=[pltpu.VMEM((1, LANES), F32)],
        compiler_params=_cparams("arbitrary"),
        name="moe_router",
    )(x, g.reshape(1, d), w)


MOE_TILE = 512


def _moe_kernel(te_ref, ok_ref, idx0_ref, idxn_ref, x_hbm, wg_ref, wu_ref, wd_ref, o_ref, xbuf, xb_ref, sem):
    i = pl.program_id(0)
    j = pl.program_id(1)
    nt = pl.num_programs(0)
    nf = pl.num_programs(1)
    tm = xb_ref.shape[0]
    per_step = tm // nf
    slot = i % 2

    def row_copy(idx_ref, r, s):
        return pltpu.make_async_copy(x_hbm.at[pl.ds(idx_ref[0, 0, r], 1), :], xbuf.at[s, pl.ds(r, 1), :], sem.at[s])

    @pl.when((i == 0) & (j == 0))
    def _():
        def start0(r, c):
            row_copy(idx0_ref, r, 0).start()
            return c
        lax.fori_loop(0, tm, start0, 0)

    @pl.when(j == 0)
    def _():
        for r in range(tm):
            row_copy(idx0_ref, r, slot).wait()
        xb_ref[...] = xbuf[slot].astype(BF16)
        o_ref[...] = jnp.zeros_like(o_ref)

    def prefetch():
        base = j * per_step
        for r in range(per_step):
            row_copy(idxn_ref, base + r, 1 - slot).start()

    def compute():
        xb = xb_ref[...]
        a = (_silu(_dot(xb, wg_ref[0])) * _dot(xb, wu_ref[0])).astype(BF16)
        o_ref[...] += _dot(a, wd_ref[0])

    has_next = i + 1 < nt
    ok = ok_ref[i] > 0

    @pl.when(has_next & ok)
    def _():
        prefetch()
        compute()

    @pl.when(has_next & jnp.logical_not(ok))
    def _():
        prefetch()

    @pl.when(jnp.logical_not(has_next) & ok)
    def _():
        compute()


def moe_experts(x, row_tok, tile_expert, tile_ok, w_gu, w_down, *, tf=1792):
    p = row_tok.shape[0]
    d = x.shape[1]
    ne, f, _ = w_down.shape
    tm = MOE_TILE
    nf = f // tf
    nt = p // tm
    idx = row_tok.reshape(nt, 1, tm)
    grid_spec = pltpu.PrefetchScalarGridSpec(
        num_scalar_prefetch=2,
        grid=(nt, nf),
        in_specs=[pl.BlockSpec((1, 1, tm), lambda i, j, te, ok: (0, 0, 0), memory_space=pltpu.SMEM),
                  pl.BlockSpec((1, 1, tm), lambda i, j, te, ok: (jnp.minimum(i + 1, nt - 1), 0, 0),
                               memory_space=pltpu.SMEM),
                  pl.BlockSpec(memory_space=pl.ANY),
                  pl.BlockSpec((1, d, tf), lambda i, j, te, ok: (te[i], 0, j)),
                  pl.BlockSpec((1, d, tf), lambda i, j, te, ok: (te[i], 0, j + nf)),
                  pl.BlockSpec((1, tf, d), lambda i, j, te, ok: (te[i], j, 0))],
        out_specs=pl.BlockSpec((tm, d), lambda i, j, te, ok: (i, 0)),
        scratch_shapes=[pltpu.VMEM((2, tm, d), F32), pltpu.VMEM((tm, d), BF16),
                        pltpu.SemaphoreType.DMA((2,))])
    return pl.pallas_call(
        _moe_kernel,
        grid_spec=grid_spec,
        out_shape=jax.ShapeDtypeStruct((p, d), F32),
        compiler_params=_cparams("arbitrary", "arbitrary"),
        name="moe_experts",
    )(tile_expert, tile_ok, idx, idx, x, w_gu, w_gu, w_down)


def _combine_kernel(d1_ref, d2_ref, x_ref, meta_ref, y_ref, gf_ref, o_ref, y1_ref, y2_ref, sem1, sem2):
    rows = o_ref.shape[0]

    def copy1(r):
        return pltpu.make_async_copy(y_ref.at[pl.ds(d1_ref[0, 0, r], 1), :], y1_ref.at[pl.ds(r, 1), :], sem1)

    def copy2(r):
        return pltpu.make_async_copy(y_ref.at[pl.ds(d2_ref[0, 0, r], 1), :], y2_ref.at[pl.ds(r, 1), :], sem2)

    def start(r, c):
        copy1(r).start()
        copy2(r).start()
        return c

    def wait(r, c):
        copy1(r).wait()
        copy2(r).wait()
        return c

    lax.fori_loop(0, rows, start, 0)
    lax.fori_loop(0, rows, wait, 0)
    meta = meta_ref[...]
    w1 = meta[:, META_W1:META_W1 + 1]
    w2 = meta[:, META_W2:META_W2 + 1]
    o_ref[...] = _rms(x_ref[...] + w1 * y1_ref[...] + w2 * y2_ref[...], gf_ref[...])


def moe_combine(x, meta, ys, d1, d2, g_final, *, tc=256):
    n, d = x.shape
    tc = min(tc, n)
    idx_spec = pl.BlockSpec((1, 1, tc), lambda i: (i, 0, 0), memory_space=pltpu.SMEM)
    return pl.pallas_call(
        _combine_kernel,
        grid=(n // tc,),
        in_specs=[idx_spec, idx_spec,
                  pl.BlockSpec((tc, d), lambda i: (i, 0)),
                  pl.BlockSpec((tc, LANES), lambda i: (i, 0)),
                  pl.BlockSpec(memory_space=pl.ANY),
                  pl.BlockSpec((1, d), lambda i: (0, 0))],
        out_specs=pl.BlockSpec((tc, d), lambda i: (i, 0)),
        out_shape=jax.ShapeDtypeStruct((n, d), F32),
        scratch_shapes=[pltpu.VMEM((tc, d), F32), pltpu.VMEM((tc, d), F32),
                        pltpu.SemaphoreType.DMA(()), pltpu.SemaphoreType.DMA(())],
        compiler_params=_cparams("parallel"),
        name="moe_combine",
    )(d1.reshape(n // tc, 1, tc), d2.reshape(n // tc, 1, tc), x, meta, ys, g_final.reshape(1, d))


def kernel(x, norm_mix, norm_ffn, a_w_in, a_b_gate, a_conv, a_norm_h, a_w_out, norm_kv, b_w_kv,
           b_cmp_pos, b_cmp_w1, b_cmp_w2, b_w_q, b_b_gate, b_w_out, f_w_gu, f_w_down,
           m_router, m_w_gu, m_w_down, norm_final):
    B, T, D = x.shape
    N = B * T
    G, R, dh = B_KV_GROUPS, B_REP, B_HEAD_DIM
    xs = x.reshape(N, D)

    inner4 = a_w_in.shape[2] - 2 * A_HEADS
    w_in = a_w_in[0]
    w_gate = jnp.pad(w_in[:, inner4:], ((0, 0), (0, LANES - 2 * A_HEADS)))
    proj, gcol = norm_matmul(xs, norm_mix[0], w_in[:, :inner4].astype(BF16), w_side=w_gate)
    hs = mlstm_core(proj, gcol, a_b_gate[0], a_conv[0], a_norm_h[0], B, T)
    xs = matmul_residual(hs, a_w_out[0].astype(BF16), xs)
    xs = ffn_dense(xs, norm_ffn[0], f_w_gu[0].astype(BF16), f_w_down[0].astype(BF16))

    hd = B_HEADS * dh
    feat_w = LANES - dh
    ncmp = T // CMP_STRIDE

    def slabs(w):
        d_in, cols = w.shape
        return jnp.pad(w.reshape(d_in, cols // dh, dh), ((0, 0), (0, 0), (0, feat_w))).reshape(d_in, -1)

    cmp_cols = 2 * G * dh
    w_kvx = jnp.concatenate([b_w_kv[:, :cmp_cols], slabs(b_w_kv[:, cmp_cols:])], axis=1).astype(BF16)
    kvx = norm_matmul(xs, norm_kv, w_kvx, tn=w_kvx.shape[1] // 2)
    w_q = b_w_q[0]
    w_qg = jnp.pad(w_q[:, hd:], ((0, 0), (0, LANES - 3 * B_HEADS)))
    qp, gl = norm_matmul(xs, norm_mix[1], slabs(w_q[:, :hd] * (dh ** -0.5 * LOG2E)).astype(BF16), w_side=w_qg)

    kvt = kvx[:, :cmp_cols].reshape(B, T, 2 * G, dh).transpose(0, 2, 1, 3)
    pos = b_cmp_pos.transpose(1, 0, 2).reshape(2, 1, CMP_LEN * dh).astype(BF16)
    kvc = nsa_compress(kvt.reshape(B, 2 * G, ncmp, CMP_STRIDE * dh), pos,
                       b_cmp_w1.astype(BF16), b_cmp_w2.astype(BF16))

    def hi_lanes(f):
        return jnp.asarray(np.concatenate([np.zeros((f.shape[0], dh), np.float32), f], axis=1), BF16)

    key_feat = hi_lanes(_pos_features(np.arange(T), feat_w))
    cmp_feat = jnp.asarray(_pos_features(np.arange(ncmp) * CMP_STRIDE + CMP_LEN - 1, feat_w), BF16)
    q_feat = hi_lanes(_slope_features(feat_w)).reshape(G, R, 1, LANES)
    ones_col = np.zeros((1, feat_w), np.float32)
    ones_col[0, 0] = 1.0
    ones_col = hi_lanes(ones_col)
    blk_onehot = (np.arange(T)[:, None] // SEL_BLOCK == np.arange(LANES)[None, :]).astype(np.float32)
    blk_onehot = jnp.asarray(blk_onehot, BF16)
    kc = jnp.concatenate([kvc[:, :G].astype(BF16), jnp.broadcast_to(cmp_feat, (B, G, ncmp, feat_w))], axis=-1)
    vc = kvc[:, G:].astype(BF16)
    glt = gl[:, :3 * B_HEADS].reshape(B, T, G, 3 * R).transpose(0, 2, 1, 3)
    bg = b_b_gate[0].reshape(G, 1, 3 * R)

    nsel = T // SEL_BLOCK
    ci = np.arange(ncmp)[:, None] * CMP_STRIDE
    sj = np.arange(LANES)[None, :] * SEL_BLOCK
    ov = ((ci < sj + SEL_BLOCK) & (ci + CMP_LEN > sj) & (np.arange(LANES)[None, :] < nsel)
          & (np.arange(ncmp)[:, None] < ncmp - 1))
    ov = jnp.asarray(ov.astype(np.float32), BF16)

    oa = nsa_attention(qp.reshape(B, T, B_HEADS * LANES), q_feat, kc, vc, kvx.reshape(B, T, -1),
                       cmp_cols // LANES, key_feat, blk_onehot, ones_col, glt, bg, ov)
    xs = matmul_residual(oa.reshape(N, hd), b_w_out[0].astype(BF16), xs)

    xn, meta, pos, cnt = moe_router(xs, norm_ffn[1], m_router[0])
    ne = N_EXPERTS
    p_rows = 2 * N + ne * MOE_TILE
    i1 = meta[:, META_I1].astype(jnp.int32)
    i2 = meta[:, META_I2].astype(jnp.int32)
    counts = cnt[0, :ne].astype(jnp.int32)
    padded = (counts + MOE_TILE - 1) // MOE_TILE * MOE_TILE
    seg_end = jnp.cumsum(padded)
    seg_start = seg_end - padded
    pos8 = pos[:, :ne].astype(jnp.int32)
    d1 = seg_start[i1] + jnp.take_along_axis(pos8, i1[:, None], axis=1)[:, 0]
    d2 = seg_start[i2] + jnp.take_along_axis(pos8, i2[:, None], axis=1)[:, 0]
    tok = jnp.arange(N, dtype=jnp.int32)
    row_tok = jnp.zeros((p_rows,), jnp.int32).at[jnp.concatenate([d1, d2])].set(jnp.concatenate([tok, tok]))
    tile_start = jnp.arange(p_rows // MOE_TILE, dtype=jnp.int32) * MOE_TILE
    tile_expert = jnp.minimum(jnp.sum(tile_start[:, None] >= seg_end[None, :], axis=1), ne - 1).astype(jnp.int32)
    tile_ok = (tile_start < seg_end[ne - 1]).astype(jnp.int32)

    ys = moe_experts(xn, row_tok, tile_expert, tile_ok, m_w_gu[0].astype(BF16), m_w_down[0].astype(BF16))
    out = moe_combine(xs, meta, ys, d1, d2, norm_final)
    return out.reshape(B, T, D)
```

```python
import functools

import numpy as np
import jax
import jax.numpy as jnp
from jax import lax
from jax.experimental import pallas as pl
from jax.experimental.pallas import tpu as pltpu

F32 = jnp.float32
BF16 = jnp.bfloat16

RMS_EPS = 1e-6
A_HEADS = 4
A_CONV = 4
B_HEADS = 16
B_KV_GROUPS = 4
B_REP = B_HEADS // B_KV_GROUPS
B_HEAD_DIM = 64
CMP_LEN = 32
CMP_STRIDE = 16
SEL_BLOCK = 64
SEL_TOPN = 16
WINDOW = 512
FORCED_SCORE = 1e4
N_EXPERTS = 8

LANES = 128
V7X_VMEM_BYTES = 64 * 1024 * 1024
VMEM_LIMIT = V7X_VMEM_BYTES - 8 * 1024 * 1024
NEG = -1e30


def _cparams(*sem):
    return pltpu.CompilerParams(dimension_semantics=sem, vmem_limit_bytes=VMEM_LIMIT)


def _dot(a, b):
    return jnp.dot(a, b, preferred_element_type=F32)


def _dot_nt(a, b):
    return lax.dot_general(a, b, (((1,), (1,)), ((), ())), preferred_element_type=F32)


def _dot_tn(a, b):
    return lax.dot_general(a, b, (((0,), (0,)), ((), ())), preferred_element_type=F32)


def _split3(a):
    a1 = a.astype(BF16)
    r1 = a - a1.astype(F32)
    a2 = r1.astype(BF16)
    a3 = (r1 - a2.astype(F32)).astype(BF16)
    return a1, a2, a3


def _dot_f32(a, b):
    a1, a2, a3 = _split3(a)
    b1, b2, b3 = _split3(b)
    return (_dot(a1, b1) + _dot(a1, b2) + _dot(a2, b1)
            + _dot(a2, b2) + _dot(a1, b3) + _dot(a3, b1))


def _dot_f32_nt(a, b):
    a1, a2, a3 = _split3(a)
    b1, b2, b3 = _split3(b)
    return (_dot_nt(a1, b1) + _dot_nt(a1, b2) + _dot_nt(a2, b1)
            + _dot_nt(a2, b2) + _dot_nt(a1, b3) + _dot_nt(a3, b1))


def _rms(xf, g):
    return xf * lax.rsqrt(jnp.mean(xf * xf, axis=-1, keepdims=True) + RMS_EPS) * g


def _silu(x):
    return x * jax.nn.sigmoid(x)


def _log_sigmoid(x):
    return jnp.minimum(x, 0.0) - jnp.log(1.0 + jnp.exp(-jnp.abs(x)))


def _norm_matmul_kernel(*refs, side):
    if side:
        x_ref, g_ref, w_ref, ws_ref, o_ref, os_ref, xn_ref = refs
    else:
        x_ref, g_ref, w_ref, o_ref, xn_ref = refs

    @pl.when(pl.program_id(1) == 0)
    def _():
        xn = _rms(x_ref[...], g_ref[...])
        xn_ref[...] = xn.astype(xn_ref.dtype)
        if side:
            os_ref[...] = _dot_f32(xn, ws_ref[...])

    o_ref[...] = _dot(xn_ref[...], w_ref[...]).astype(o_ref.dtype)


def norm_matmul(x, g, w, *, w_side=None, tm=1024, tn=1024):
    n, d = x.shape
    dout = w.shape[1]
    tm = min(tm, n)
    tn = min(tn, dout)
    side = w_side is not None
    in_specs = [pl.BlockSpec((tm, d), lambda i, j: (i, 0)),
                pl.BlockSpec((1, d), lambda i, j: (0, 0)),
                pl.BlockSpec((d, tn), lambda i, j: (0, j))]
    out_specs = [pl.BlockSpec((tm, tn), lambda i, j: (i, j))]
    out_shape = [jax.ShapeDtypeStruct((n, dout), BF16)]
    args = [x, g.reshape(1, d), w]
    if side:
        in_specs.append(pl.BlockSpec((d, LANES), lambda i, j: (0, 0)))
        out_specs.append(pl.BlockSpec((tm, LANES), lambda i, j: (i, 0)))
        out_shape.append(jax.ShapeDtypeStruct((n, LANES), F32))
        args.append(w_side)
    out = pl.pallas_call(
        functools.partial(_norm_matmul_kernel, side=side),
        grid=(n // tm, dout // tn),
        in_specs=in_specs,
        out_specs=out_specs,
        out_shape=out_shape,
        scratch_shapes=[pltpu.VMEM((tm, d), BF16)],
        compiler_params=_cparams("parallel", "arbitrary"),
        name="norm_matmul_side" if side else "norm_matmul",
    )(*args)
    return out if side else out[0]


MLSTM_CHUNK = 256


def _mlstm_kernel(qk_ref, v_ref, o_ref, gcol_ref, grow_ref, bcol_ref, brow_ref, convw_ref, gout_ref,
                  out_ref, ct_ref, n_ref, m_ref, prev_ref):
    L = qk_ref.shape[0]
    H = A_HEADS
    inner = v_ref.shape[1]
    dh = inner // H

    @pl.when(pl.program_id(1) == 0)
    def _():
        ct_ref[...] = jnp.zeros_like(ct_ref)
        n_ref[...] = jnp.zeros_like(n_ref)
        m_ref[...] = jnp.zeros_like(m_ref)
        prev_ref[...] = jnp.zeros_like(prev_ref)

    row = lax.broadcasted_iota(jnp.int32, (L, L), 0)
    col = lax.broadcasted_iota(jnp.int32, (L, L), 1)
    causal = col <= row
    tril = jnp.where(causal, 1.0, 0.0).astype(BF16)
    triu = jnp.where(row <= col, 1.0, 0.0).astype(BF16)

    gc = gcol_ref[...] + brow_ref[...]
    gr = grow_ref[...] + bcol_ref[...]
    lfc1, lfc2, lfc3 = _split3(_log_sigmoid(gc))
    lfr1, lfr2, lfr3 = _split3(_log_sigmoid(gr))
    b_c = _dot(tril, lfc1) + _dot(tril, lfc2) + _dot(tril, lfc3)
    b_r = _dot(lfr1, triu) + _dot(lfr2, triu) + _dot(lfr3, triu)

    rowi = lax.broadcasted_iota(jnp.int32, (L, dh), 0)

    def conv(cur, prev, w):
        y = cur * w[A_CONV - 1:A_CONV, :]
        for s in range(1, A_CONV):
            sh = jnp.where(rowi < s, pltpu.roll(prev, s, 0), pltpu.roll(cur, s, 0))
            y = y + sh * w[A_CONV - 1 - s:A_CONV - s, :]
        return y

    for h in range(H):
        hs = slice(h * dh, (h + 1) * dh)
        ks = slice(inner + h * dh, inner + (h + 1) * dh)
        q = conv(qk_ref[:, hs].astype(F32), prev_ref[:, hs].astype(F32), convw_ref[:, hs])
        k = conv(qk_ref[:, ks].astype(F32), prev_ref[:, ks].astype(F32), convw_ref[:, ks]) * (dh ** -0.5)
        v = v_ref[:, hs]
        qb = q.astype(BF16)
        kb = k.astype(BF16)

        li_c = gc[:, h:h + 1]
        bc = b_c[:, H + h:H + h + 1]
        li_r = gr[h:h + 1, :]
        br = b_r[H + h:H + h + 1, :]
        m_prev = m_ref[h:h + 1, 0:1]

        d = jnp.where(causal, bc - br + li_r, -jnp.inf)
        inter = bc + m_prev
        m_t = jnp.maximum(inter, jnp.max(d, axis=-1, keepdims=True))
        w_inter = jnp.exp(inter - m_t)
        s = _dot_nt(qb, kb) * jnp.exp(d - m_t)
        ct = ct_ref[h]
        num = _dot(s.astype(BF16), v) + w_inter * _dot(qb, ct.astype(BF16))
        den = jnp.sum(s, axis=-1, keepdims=True) + w_inter * jnp.sum(q * n_ref[h], axis=-1, keepdims=True)
        hh = num / jnp.maximum(jnp.abs(den), jnp.exp(-m_t))
        hh = hh * lax.rsqrt(jnp.mean(hh * hh, axis=-1, keepdims=True) + RMS_EPS)
        out_ref[:, hs] = (hh * gout_ref[:, hs] * jax.nn.sigmoid(o_ref[:, hs].astype(F32))).astype(out_ref.dtype)

        b_last = bc[L - 1:L, :]
        g = b_last - bc + li_c
        m_new = jnp.maximum(b_last + m_prev, jnp.max(g, axis=0, keepdims=True))
        a_prev = jnp.exp(b_last + m_prev - m_new)
        a_s = jnp.exp(g - m_new)
        ct_ref[h] = a_prev * ct + _dot_tn(kb, (v.astype(F32) * a_s).astype(BF16))
        n_ref[h] = a_prev * n_ref[h] + jnp.sum(k * a_s, axis=0, keepdims=True)
        m_ref[h:h + 1, :] = jnp.broadcast_to(m_new, (1, LANES))

    prev_ref[...] = qk_ref[...]


def mlstm_core(proj, gcol, b_gate, conv_w, g_out, batch, seq):
    n = proj.shape[0]
    inner = proj.shape[1] // 4
    H = A_HEADS
    dh = inner // H
    L = min(MLSTM_CHUNK, seq)
    nc = seq // L
    grow = gcol[:, :2 * H].T
    brow = jnp.pad(b_gate, (0, LANES - 2 * H)).reshape(1, LANES)
    bcol = b_gate.reshape(2 * H, 1)
    return pl.pallas_call(
        _mlstm_kernel,
        grid=(batch, nc),
        in_specs=[pl.BlockSpec((L, 2 * inner), lambda b, c: (b * nc + c, 0)),
                  pl.BlockSpec((L, inner), lambda b, c: (b * nc + c, 2)),
                  pl.BlockSpec((L, inner), lambda b, c: (b * nc + c, 3)),
                  pl.BlockSpec((L, LANES), lambda b, c: (b * nc + c, 0)),
                  pl.BlockSpec((2 * H, L), lambda b, c: (0, b * nc + c)),
                  pl.BlockSpec((2 * H, 1), lambda b, c: (0, 0)),
                  pl.BlockSpec((1, LANES), lambda b, c: (0, 0)),
                  pl.BlockSpec((A_CONV, 2 * inner), lambda b, c: (0, 0)),
                  pl.BlockSpec((1, inner), lambda b, c: (0, 0))],
        out_specs=pl.BlockSpec((L, inner), lambda b, c: (b * nc + c, 0)),
        out_shape=jax.ShapeDtypeStruct((n, inner), BF16),
        scratch_shapes=[pltpu.VMEM((H, dh, dh), F32),
                        pltpu.VMEM((H, 1, dh), F32),
                        pltpu.VMEM((8, LANES), F32),
                        pltpu.VMEM((L, 2 * inner), BF16)],
        compiler_params=_cparams("parallel", "arbitrary"),
        name="mlstm_core",
    )(proj, proj, proj, gcol, grow, bcol, brow, conv_w, g_out.reshape(1, inner))


def _ffn_kernel(x_ref, h_ref, wo_ref, g_ref, wg_ref, wu_ref, wd_ref, o_ref, xn_ref, acc_ref):
    j = pl.program_id(1)

    @pl.when(j == 0)
    def _():
        xf = x_ref[...] + _dot(h_ref[...], wo_ref[...])
        xn_ref[...] = _rms(xf, g_ref[...]).astype(BF16)
        acc_ref[...] = xf

    xn = xn_ref[...]
    a = (_silu(_dot(xn, wg_ref[...])) * _dot(xn, wu_ref[...])).astype(BF16)
    acc_ref[...] += _dot(a, wd_ref[...])

    @pl.when(j == pl.num_programs(1) - 1)
    def _():
        o_ref[...] = acc_ref[...]


def ffn_dense(x, h, w_out, g, w_gu, w_down, *, tm=512, tf=1408):
    n, d = x.shape
    f = w_down.shape[0]
    dh_in = h.shape[1]
    tm = min(tm, n)
    nf = f // tf
    return pl.pallas_call(
        _ffn_kernel,
        grid=(n // tm, nf),
        in_specs=[pl.BlockSpec((tm, d), lambda i, j: (i, 0)),
                  pl.BlockSpec((tm, dh_in), lambda i, j: (i, 0)),
                  pl.BlockSpec((dh_in, d), lambda i, j: (0, 0)),
                  pl.BlockSpec((1, d), lambda i, j: (0, 0)),
                  pl.BlockSpec((d, tf), lambda i, j: (0, j)),
                  pl.BlockSpec((d, tf), lambda i, j: (0, j + nf)),
                  pl.BlockSpec((tf, d), lambda i, j: (j, 0))],
        out_specs=pl.BlockSpec((tm, d), lambda i, j: (i, 0)),
        out_shape=jax.ShapeDtypeStruct((n, d), F32),
        scratch_shapes=[pltpu.VMEM((tm, d), BF16), pltpu.VMEM((tm, d), F32)],
        compiler_params=_cparams("parallel", "arbitrary"),
        name="ffn_dense",
    )(x, h, w_out, g.reshape(1, d), w_gu, w_gu, w_down)


def _compress_kernel(r_ref, pos_ref, w1_ref, w2_ref, o_ref):
    r = r_ref[0, 0]
    w1 = w1_ref[0]
    half = r.shape[1]
    nc = r.shape[0]
    lo = _dot(r, w1[:half])
    hi = _dot(r, w1[half:])
    hid = lo + pltpu.roll(hi, nc - 1, 0) + _dot(pos_ref[0], w1)
    o_ref[0, 0] = _dot(_silu(hid).astype(BF16), w2_ref[0]).astype(o_ref.dtype)


def nsa_compress(r, pos, w1, w2):
    b, c2, nc, half = r.shape
    g = c2 // 2
    hidden = w1.shape[2]
    dh = w2.shape[2]
    return pl.pallas_call(
        _compress_kernel,
        grid=(b, c2),
        in_specs=[pl.BlockSpec((1, 1, nc, half), lambda i, j: (i, j, 0, 0)),
                  pl.BlockSpec((1, 1, 2 * half), lambda i, j: (j // g, 0, 0)),
                  pl.BlockSpec((1, 2 * half, hidden), lambda i, j: (j // g, 0, 0)),
                  pl.BlockSpec((1, hidden, dh), lambda i, j: (j // g, 0, 0))],
        out_specs=pl.BlockSpec((1, 1, nc, dh), lambda i, j: (i, j, 0, 0)),
        out_shape=jax.ShapeDtypeStruct((b, c2, nc, dh), F32),
        compiler_params=_cparams("parallel", "parallel"),
        name="nsa_compress",
    )(r, pos, w1, w2)


NSA_TQ = 256
NSA_TK_SLC = 512
NSA_TK_WIN = 256
NSA_ROWS = 32
SEL_MASK = 2.0 ** 14
LOG2E = float(np.log2(np.e))


def _nsa_kernel(q_ref, qf_ref, kc_ref, vc_ref, ks_ref, vs_ref, kw_ref, vw_ref, kf_ref, oh_ref, one_ref,
                gl_ref, bg_ref, ov_ref,
                out_ref, qx_ref, ksx_ref, vsx_ref, kwx_ref, vwx_ref,
                ms_ref, accs_ref, biass_ref, s0_ref, p0_ref, a0_ref, s1_ref, p1_ref, a1_ref,
                mw_ref, accw_ref, biasw_ref, s2_ref, p2_ref, a2_ref, s3_ref, p3_ref, a3_ref, live_ref,
                *, tq, tks, tkw, n_top):
    R = B_REP
    M = R * tq
    dh = B_HEAD_DIM
    t0 = pl.program_id(2) * tq
    slc_state = (ms_ref, accs_ref, biass_ref, ((s0_ref, p0_ref, a0_ref), (s1_ref, p1_ref, a1_ref)))
    win_state = (mw_ref, accw_ref, biasw_ref, ((s2_ref, p2_ref, a2_ref), (s3_ref, p3_ref, a3_ref)))

    @pl.when(pl.program_id(2) == 0)
    def _():
        kf = kf_ref[...]
        ksx_ref[:, :LANES] = ks_ref[0] + kf
        ksx_ref[:, LANES:] = oh_ref[...]
        kwx_ref[...] = kw_ref[0] + kf
        vsx_ref[...] = vs_ref[0] + one_ref[...]
        vwx_ref[...] = vw_ref[0] + one_ref[...]

    qb = q_ref[0]
    for r in range(R):
        qx_ref[r * tq:(r + 1) * tq, :LANES] = qb[:, r * LANES:(r + 1) * LANES] + qf_ref[0, r]
    q = qx_ref[:, :LANES]

    kc = kc_ref[0, 0]
    ncmp = kc.shape[0]
    s = _dot_nt(q, kc)
    tpos_c = t0 + (lax.broadcasted_iota(jnp.int32, (M, ncmp), 0) & (tq - 1))
    cend = lax.broadcasted_iota(jnp.int32, (M, ncmp), 1) * CMP_STRIDE + (CMP_LEN - 1)
    ok_c = tpos_c >= cend
    s = jnp.where(ok_c, s, NEG)
    p = jnp.where(ok_c, jnp.exp2(s - jnp.max(s, axis=-1, keepdims=True)), 0.0)
    dsum = jnp.sum(p, axis=-1, keepdims=True)
    p = p / jnp.where(dsum > 0, dsum, 1.0)
    o_cmp = _dot(p.astype(BF16), vc_ref[0, 0])

    psum = p[0:tq]
    for r in range(1, R):
        psum = psum + p[r * tq:(r + 1) * tq]
    p1, p2, p3 = _split3(psum)
    ov = ov_ref[...]
    imp = _dot(p1, ov) + _dot(p2, ov) + _dot(p3, ov)
    nsel = ks_ref.shape[1] // SEL_BLOCK
    nselp = max(nsel, 8)
    imp_t = imp.T[:nselp]
    jj = lax.broadcasted_iota(jnp.int32, (nselp, tq), 0)
    tt = t0 + lax.broadcasted_iota(jnp.int32, (nselp, tq), 1)
    cur = tt // SEL_BLOCK
    forced = (jj == 0) | (jj == cur) | (jj == cur - 1)
    iv = jnp.where(forced, FORCED_SCORE, imp_t)
    iv = jnp.where(jj * SEL_BLOCK <= tt, iv, -jnp.inf)
    rank = jnp.zeros((nselp, tq), F32)
    for j2 in range(nsel):
        rv = iv[j2:j2 + 1, :]
        rank = rank + jnp.where(jj > j2, jnp.where(rv >= iv, 1.0, 0.0), jnp.where(rv > iv, 1.0, 0.0))
    unsel = jnp.where(rank < n_top, 0.0, -SEL_MASK)
    if nselp < LANES:
        unsel = jnp.concatenate([unsel, jnp.zeros((LANES - nselp, tq), F32)], axis=0)
    qmask = unsel.T.astype(BF16)
    for r in range(R):
        qx_ref[r * tq:(r + 1) * tq, LANES:] = qmask

    def tile_step(qq, k_ref, v_ref, k0, tk, state, slot, bias):
        m_ref, acc_ref, bias_ref, slots = state
        s_ref, p_ref, a_ref = slots[slot]
        kk = k_ref[pl.ds(k0, tk), :]
        vv = v_ref[pl.ds(k0, tk), :]
        if bias is not None:
            bias_ref[:, :tk] = bias
        s_ref[:, :tk] = _dot_nt(qq, kk)
        for r in range(R):
            for c0 in range(0, tq, NSA_ROWS):
                rows = slice(r * tq + c0, r * tq + c0 + NSA_ROWS)
                sc = s_ref[rows, :tk]
                if bias is not None:
                    sc = sc + bias_ref[c0:c0 + NSA_ROWS, :tk]
                m_old = m_ref[rows, :]
                m_new = jnp.maximum(m_old, jnp.max(sc, axis=-1, keepdims=True))
                a_ref[rows, :] = jnp.exp2(m_old - m_new)
                m_ref[rows, :] = m_new
                p_ref[rows, :tk] = jnp.exp2(sc - jnp.concatenate([m_new] * (tk // LANES), axis=1)).astype(BF16)
        acc_ref[...] = a_ref[...] * acc_ref[...] + _dot(p_ref[:, :tk], vv)

    def reset(state):
        state[0][...] = jnp.full_like(state[0], NEG)
        state[1][...] = jnp.zeros_like(state[1])

    def result(state):
        acc = state[1][...]
        return acc[:, :dh] / acc[:, dh:dh + 1]

    reset(slc_state)
    qx = qx_ref[...]
    n_full = t0 // tks

    blocks_per_tile = tks // SEL_BLOCK
    n_live = jnp.int32(0)
    for j in range(nsel // blocks_per_tile):
        sel_j = rank[j * blocks_per_tile:(j + 1) * blocks_per_tile, :] < n_top
        live = (jnp.max(jnp.where(sel_j, 1.0, 0.0)) > 0.5) & (j < n_full)
        live_ref[n_live] = j
        n_live = n_live + live.astype(jnp.int32)

    def slc_pair(i, carry):
        tile_step(qx, ksx_ref, vsx_ref, pl.multiple_of(live_ref[2 * i] * tks, tks), tks, slc_state, 0, None)
        tile_step(qx, ksx_ref, vsx_ref, pl.multiple_of(live_ref[2 * i + 1] * tks, tks), tks, slc_state, 1, None)
        return carry

    lax.fori_loop(0, n_live // 2, slc_pair, 0)

    @pl.when(n_live % 2 == 1)
    def _():
        tile_step(qx, ksx_ref, vsx_ref, pl.multiple_of(live_ref[n_live - 1] * tks, tks), tks, slc_state, 0, None)

    kd = pl.multiple_of(n_full * tks, tks)
    spos = kd + lax.broadcasted_iota(jnp.int32, (tq, tks), 1)
    tpos = t0 + lax.broadcasted_iota(jnp.int32, (tq, tks), 0)
    tile_step(qx, ksx_ref, vsx_ref, kd, tks, slc_state, 1, jnp.where(spos <= tpos, 0.0, NEG))
    reset(win_state)
    n_win = WINDOW // tkw + max(tq // tkw, 1)
    last = (t0 + tq - 1) // tkw
    for i in range(n_win):
        kw0 = (last - (n_win - 1) + i) * tkw
        spos = kw0 + lax.broadcasted_iota(jnp.int32, (tq, tkw), 1)
        dist = t0 + lax.broadcasted_iota(jnp.int32, (tq, tkw), 0) - spos
        bias = jnp.where((spos >= 0) & (dist >= 0) & (dist < WINDOW), 0.0, NEG)
        tile_step(q, kwx_ref, vwx_ref, pl.multiple_of(jnp.maximum(kw0, 0), tkw), tkw, win_state, i % 2, bias)
    o_slc = result(slc_state)
    o_win = result(win_state)

    gates = jax.nn.sigmoid(gl_ref[0, 0, pl.ds(pl.multiple_of(t0, tq), tq), :] + bg_ref[0])
    outs = []
    for r in range(R):
        rs = slice(r * tq, (r + 1) * tq)
        outs.append(gates[:, 3 * r:3 * r + 1] * o_cmp[rs] + gates[:, 3 * r + 1:3 * r + 2] * o_slc[rs]
                    + gates[:, 3 * r + 2:3 * r + 3] * o_win[rs])
    out_ref[0] = jnp.concatenate(outs, axis=1).astype(out_ref.dtype)


def nsa_attention(qp, q_feat, kc, vc, kvx, kv_off, key_feat, blk_onehot, ones_col, gl, bg, ov):
    b, t, _ = qp.shape
    g, r = q_feat.shape[:2]
    dh = B_HEAD_DIM
    tq = min(NSA_TQ, t)
    tks = min(NSA_TK_SLC, t)
    tkw = min(NSA_TK_WIN, t)
    ncmp = kc.shape[2]
    n_top = min(SEL_TOPN, t // SEL_BLOCK)
    kv_spec = lambda c: pl.BlockSpec((1, t, LANES), lambda i, j, k: (i, 0, kv_off + c * g + j))
    const_spec = lambda a: pl.BlockSpec(a.shape, lambda i, j, k: (0,) * a.ndim)

    def branch_scratch(tk):
        slot = [pltpu.VMEM((r * tq, tk), F32), pltpu.VMEM((r * tq, tk), BF16), pltpu.VMEM((r * tq, LANES), F32)]
        return [pltpu.VMEM((r * tq, LANES), F32), pltpu.VMEM((r * tq, LANES), F32),
                pltpu.VMEM((tq, tk), F32)] + slot + slot

    return pl.pallas_call(
        functools.partial(_nsa_kernel, tq=tq, tks=tks, tkw=tkw, n_top=n_top),
        grid=(b, g, t // tq),
        in_specs=[pl.BlockSpec((1, tq, r * LANES), lambda i, j, k: (i, k, j)),
                  pl.BlockSpec((1, r, 1, LANES), lambda i, j, k: (j, 0, 0, 0)),
                  pl.BlockSpec((1, 1, ncmp, LANES), lambda i, j, k: (i, j, 0, 0)),
                  pl.BlockSpec((1, 1, ncmp, dh), lambda i, j, k: (i, j, 0, 0)),
                  kv_spec(0), kv_spec(1), kv_spec(2), kv_spec(3),
                  const_spec(key_feat), const_spec(blk_onehot), const_spec(ones_col),
                  pl.BlockSpec((1, 1, t, 3 * r), lambda i, j, k: (i, j, 0, 0)),
                  pl.BlockSpec((1, 1, 3 * r), lambda i, j, k: (j, 0, 0)),
                  const_spec(ov)],
        out_specs=pl.BlockSpec((1, tq, r * dh), lambda i, j, k: (i, k, j)),
        out_shape=jax.ShapeDtypeStruct((b, t, g * r * dh), BF16),
        scratch_shapes=[pltpu.VMEM((r * tq, 2 * LANES), BF16),
                        pltpu.VMEM((t, 2 * LANES), BF16),
                        pltpu.VMEM((t, LANES), BF16),
                        pltpu.VMEM((t, LANES), BF16),
                        pltpu.VMEM((t, LANES), BF16),
                        ] + branch_scratch(tks) + branch_scratch(tkw)
                       + [pltpu.SMEM((t // tks + 1,), jnp.int32)],
        compiler_params=_cparams("arbitrary", "arbitrary", "arbitrary"),
        name="nsa_attention",
    )(qp, q_feat, kc, vc, kvx, kvx, kvx, kvx, key_feat, blk_onehot, ones_col, gl, bg, ov)


def _alibi_slopes(n):
    return np.power(2.0, -8.0 * np.arange(1, n + 1) / n).astype(np.float32)


def _np_split3(a):
    a = np.asarray(a, np.float32)
    out = []
    r = a
    for _ in range(3):
        p = r.astype(BF16).astype(np.float32)
        out.append(p)
        r = (r - p).astype(np.float32)
    return out


def _pos_features(pos, width):
    hi = (pos // 64).astype(np.float32)
    lo = (pos % 64).astype(np.float32)
    f = np.zeros((pos.shape[0], width), np.float32)
    f[:, 0:3] = hi[:, None]
    f[:, 3:6] = lo[:, None]
    return f


def _slope_features(width):
    s1, s2, s3 = _np_split3((_alibi_slopes(B_HEADS).astype(np.float64) * LOG2E).astype(np.float32))
    f = np.zeros((B_HEADS, width), np.float32)
    for i, s in enumerate((s1, s2, s3)):
        f[:, i] = 64.0 * s
        f[:, 3 + i] = s
    return f


META_I1, META_I2, META_W1, META_W2 = 0, 1, 2, 3


def _router_kernel(x_ref, h_ref, wo_ref, g_ref, w_ref, xo_ref, xn_ref, meta_ref, pos_ref, cnt_ref, run_ref):
    @pl.when(pl.program_id(0) == 0)
    def _():
        run_ref[...] = jnp.zeros_like(run_ref)

    xf = x_ref[...] + _dot(h_ref[...], wo_ref[...])
    xo_ref[...] = xf
    xn = _rms(xf, g_ref[...])
    xn_ref[...] = xn
    logits = _dot_f32(xn, w_ref[...])
    tm = logits.shape[0]
    lane = lax.broadcasted_iota(jnp.int32, logits.shape, 1)
    logits = jnp.where(lane < N_EXPERTS, logits, -jnp.inf)
    m1 = jnp.max(logits, axis=-1, keepdims=True)
    i1 = jnp.min(jnp.where(logits == m1, lane, LANES), axis=-1, keepdims=True)
    rest = jnp.where(lane == i1, -jnp.inf, logits)
    m2 = jnp.max(rest, axis=-1, keepdims=True)
    i2 = jnp.min(jnp.where(rest == m2, lane, LANES), axis=-1, keepdims=True)
    e2 = jnp.exp(m2 - m1)
    w1 = 1.0 / (1.0 + e2)
    meta = jnp.where(lane == META_I1, i1.astype(F32), 0.0)
    meta = jnp.where(lane == META_I2, i2.astype(F32), meta)
    meta = jnp.where(lane == META_W1, w1, meta)
    meta_ref[...] = jnp.where(lane == META_W2, e2 * w1, meta)

    sel = jnp.where((lane == i1) | (lane == i2), 1.0, 0.0)
    row = lax.broadcasted_iota(jnp.int32, (tm, tm), 0)
    col = lax.broadcasted_iota(jnp.int32, (tm, tm), 1)
    before = jnp.where(col < row, 1.0, 0.0).astype(BF16)
    run = run_ref[...]
    pos_ref[...] = run + _dot(before, sel.astype(BF16))
    run = run + jnp.sum(sel, axis=0, keepdims=True)
    run_ref[...] = run
    cnt_ref[...] = run


def moe_router(x, h, w_out, g, w_router, *, tm=512):
    n, d = x.shape
    dh_in = h.shape[1]
    tm = min(tm, n)
    w = jnp.pad(w_router, ((0, 0), (0, LANES - w_router.shape[1])))
    row_spec = pl.BlockSpec((tm, LANES), lambda i: (i, 0))
    return pl.pallas_call(
        _router_kernel,
        grid=(n // tm,),
        in_specs=[pl.BlockSpec((tm, d), lambda i: (i, 0)),
                  pl.BlockSpec((tm, dh_in), lambda i: (i, 0)),
                  pl.BlockSpec((dh_in, d), lambda i: (0, 0)),
                  pl.BlockSpec((1, d), lambda i: (0, 0)),
                  pl.BlockSpec((d, LANES), lambda i: (0, 0))],
        out_specs=[pl.BlockSpec((tm, d), lambda i: (i, 0)), pl.BlockSpec((tm, d), lambda i: (i, 0)),
                   row_spec, row_spec,
                   pl.BlockSpec((1, LANES), lambda i: (0, 0))],
        out_shape=[jax.ShapeDtypeStruct((n, d), F32), jax.ShapeDtypeStruct((n, d), F32),
                   jax.ShapeDtypeStruct((n, LANES), F32),
                   jax.ShapeDtypeStruct((n, LANES), F32), jax.ShapeDtypeStruct((1, LANES), F32)],
        scratch_shapes=[pltpu.VMEM((1, LANES), F32)],
        compiler_params=_cparams("arbitrary"),
        name="moe_router",
    )(x, h, w_out, g.reshape(1, d), w)


MOE_TILE = 512


def _moe_kernel(te_ref, ok_ref, idx0_ref, idxn_ref, x_hbm, wg_ref, wu_ref, wd_ref, o_ref, xbuf, xb_ref, sem):
    i = pl.program_id(0)
    j = pl.program_id(1)
    nt = pl.num_programs(0)
    nf = pl.num_programs(1)
    tm = xb_ref.shape[0]
    per_step = tm // nf
    slot = i % 2

    def row_copy(idx_ref, r, s):
        return pltpu.make_async_copy(x_hbm.at[pl.ds(idx_ref[0, 0, r], 1), :], xbuf.at[s, pl.ds(r, 1), :], sem.at[s])

    @pl.when((i == 0) & (j == 0))
    def _():
        def start0(r, c):
            row_copy(idx0_ref, r, 0).start()
            return c
        lax.fori_loop(0, tm, start0, 0)

    @pl.when(j == 0)
    def _():
        for r in range(tm):
            row_copy(idx0_ref, r, slot).wait()
        xb_ref[...] = xbuf[slot].astype(BF16)
        o_ref[...] = jnp.zeros_like(o_ref)

    def prefetch():
        base = j * per_step
        for r in range(per_step):
            row_copy(idxn_ref, base + r, 1 - slot).start()

    def compute():
        xb = xb_ref[...]
        a = (_silu(_dot(xb, wg_ref[0])) * _dot(xb, wu_ref[0])).astype(BF16)
        o_ref[...] += _dot(a, wd_ref[0])

    has_next = i + 1 < nt
    ok = ok_ref[i] > 0

    @pl.when(has_next & ok)
    def _():
        prefetch()
        compute()

    @pl.when(has_next & jnp.logical_not(ok))
    def _():
        prefetch()

    @pl.when(jnp.logical_not(has_next) & ok)
    def _():
        compute()


def moe_experts(x, row_tok, tile_expert, tile_ok, w_gu, w_down, *, tf=1792):
    p = row_tok.shape[0]
    d = x.shape[1]
    ne, f, _ = w_down.shape
    tm = MOE_TILE
    nf = f // tf
    nt = p // tm
    idx = row_tok.reshape(nt, 1, tm)
    grid_spec = pltpu.PrefetchScalarGridSpec(
        num_scalar_prefetch=2,
        grid=(nt, nf),
        in_specs=[pl.BlockSpec((1, 1, tm), lambda i, j, te, ok: (0, 0, 0), memory_space=pltpu.SMEM),
                  pl.BlockSpec((1, 1, tm), lambda i, j, te, ok: (jnp.minimum(i + 1, nt - 1), 0, 0),
                               memory_space=pltpu.SMEM),
                  pl.BlockSpec(memory_space=pl.ANY),
                  pl.BlockSpec((1, d, tf), lambda i, j, te, ok: (te[i], 0, j)),
                  pl.BlockSpec((1, d, tf), lambda i, j, te, ok: (te[i], 0, j + nf)),
                  pl.BlockSpec((1, tf, d), lambda i, j, te, ok: (te[i], j, 0))],
        out_specs=pl.BlockSpec((tm, d), lambda i, j, te, ok: (i, 0)),
        scratch_shapes=[pltpu.VMEM((2, tm, d), F32), pltpu.VMEM((tm, d), BF16),
                        pltpu.SemaphoreType.DMA((2,))])
    return pl.pallas_call(
        _moe_kernel,
        grid_spec=grid_spec,
        out_shape=jax.ShapeDtypeStruct((p, d), F32),
        compiler_params=_cparams("arbitrary", "arbitrary"),
        name="moe_experts",
    )(tile_expert, tile_ok, idx, idx, x, w_gu, w_gu, w_down)


def _combine_kernel(d0_ref, dn_ref, x_ref, meta_ref, y_ref, gf_ref, o_ref, ybuf, sem):
    i = pl.program_id(0)
    nt = pl.num_programs(0)
    rows = o_ref.shape[0]
    slot = i % 2

    def row_copy(d_ref, k, r, s):
        return pltpu.make_async_copy(y_ref.at[pl.ds(d_ref[0, k, r], 1), :], ybuf.at[s, k, pl.ds(r, 1), :], sem.at[s])

    @pl.when(i == 0)
    def _():
        def start0(r, c):
            row_copy(d0_ref, 0, r, 0).start()
            row_copy(d0_ref, 1, r, 0).start()
            return c
        lax.fori_loop(0, rows, start0, 0)

    for r in range(rows):
        row_copy(d0_ref, 0, r, slot).wait()
        row_copy(d0_ref, 1, r, slot).wait()

    @pl.when(i + 1 < nt)
    def _():
        for r in range(rows):
            row_copy(dn_ref, 0, r, 1 - slot).start()
            row_copy(dn_ref, 1, r, 1 - slot).start()

    meta = meta_ref[...]
    w1 = meta[:, META_W1:META_W1 + 1]
    w2 = meta[:, META_W2:META_W2 + 1]
    o_ref[...] = _rms(x_ref[...] + w1 * ybuf[slot, 0] + w2 * ybuf[slot, 1], gf_ref[...])


def moe_combine(x, meta, ys, d1, d2, g_final, *, tc=256):
    n, d = x.shape
    tc = min(tc, n)
    nt = n // tc
    dd = jnp.stack([d1.reshape(nt, tc), d2.reshape(nt, tc)], axis=1)
    return pl.pallas_call(
        _combine_kernel,
        grid=(nt,),
        in_specs=[pl.BlockSpec((1, 2, tc), lambda i: (0, 0, 0), memory_space=pltpu.SMEM),
                  pl.BlockSpec((1, 2, tc), lambda i: (jnp.minimum(i + 1, nt - 1), 0, 0), memory_space=pltpu.SMEM),
                  pl.BlockSpec((tc, d), lambda i: (i, 0)),
                  pl.BlockSpec((tc, LANES), lambda i: (i, 0)),
                  pl.BlockSpec(memory_space=pl.ANY),
                  pl.BlockSpec((1, d), lambda i: (0, 0))],
        out_specs=pl.BlockSpec((tc, d), lambda i: (i, 0)),
        out_shape=jax.ShapeDtypeStruct((n, d), F32),
        scratch_shapes=[pltpu.VMEM((2, 2, tc, d), F32), pltpu.SemaphoreType.DMA((2,))],
        compiler_params=_cparams("arbitrary"),
        name="moe_combine",
    )(dd, dd, x, meta, ys, g_final.reshape(1, d))


def kernel(x, norm_mix, norm_ffn, a_w_in, a_b_gate, a_conv, a_norm_h, a_w_out, norm_kv, b_w_kv,
           b_cmp_pos, b_cmp_w1, b_cmp_w2, b_w_q, b_b_gate, b_w_out, f_w_gu, f_w_down,
           m_router, m_w_gu, m_w_down, norm_final):
    B, T, D = x.shape
    N = B * T
    G, R, dh = B_KV_GROUPS, B_REP, B_HEAD_DIM
    xs = x.reshape(N, D)

    inner4 = a_w_in.shape[2] - 2 * A_HEADS
    w_in = a_w_in[0]
    w_gate = jnp.pad(w_in[:, inner4:], ((0, 0), (0, LANES - 2 * A_HEADS)))
    proj, gcol = norm_matmul(xs, norm_mix[0], w_in[:, :inner4].astype(BF16), w_side=w_gate)
    hs = mlstm_core(proj, gcol, a_b_gate[0], a_conv[0], a_norm_h[0], B, T)
    xs = ffn_dense(xs, hs, a_w_out[0].astype(BF16), norm_ffn[0], f_w_gu[0].astype(BF16), f_w_down[0].astype(BF16))

    hd = B_HEADS * dh
    feat_w = LANES - dh
    ncmp = T // CMP_STRIDE

    def slabs(w):
        d_in, cols = w.shape
        return jnp.pad(w.reshape(d_in, cols // dh, dh), ((0, 0), (0, 0), (0, feat_w))).reshape(d_in, -1)

    cmp_cols = 2 * G * dh
    w_kvx = jnp.concatenate([b_w_kv[:, :cmp_cols], slabs(b_w_kv[:, cmp_cols:])], axis=1).astype(BF16)
    kvx = norm_matmul(xs, norm_kv, w_kvx, tn=w_kvx.shape[1] // 2)
    w_q = b_w_q[0]
    w_qg = jnp.pad(w_q[:, hd:], ((0, 0), (0, LANES - 3 * B_HEADS)))
    qp, gl = norm_matmul(xs, norm_mix[1], slabs(w_q[:, :hd] * (dh ** -0.5 * LOG2E)).astype(BF16), w_side=w_qg)

    kvt = kvx[:, :cmp_cols].reshape(B, T, 2 * G, dh).transpose(0, 2, 1, 3)
    pos = b_cmp_pos.transpose(1, 0, 2).reshape(2, 1, CMP_LEN * dh).astype(BF16)
    kvc = nsa_compress(kvt.reshape(B, 2 * G, ncmp, CMP_STRIDE * dh), pos,
                       b_cmp_w1.astype(BF16), b_cmp_w2.astype(BF16))

    def hi_lanes(f):
        return jnp.asarray(np.concatenate([np.zeros((f.shape[0], dh), np.float32), f], axis=1), BF16)

    key_feat = hi_lanes(_pos_features(np.arange(T), feat_w))
    cmp_feat = jnp.asarray(_pos_features(np.arange(ncmp) * CMP_STRIDE + CMP_LEN - 1, feat_w), BF16)
    q_feat = hi_lanes(_slope_features(feat_w)).reshape(G, R, 1, LANES)
    ones_col = np.zeros((1, feat_w), np.float32)
    ones_col[0, 0] = 1.0
    ones_col = hi_lanes(ones_col)
    blk_onehot = (np.arange(T)[:, None] // SEL_BLOCK == np.arange(LANES)[None, :]).astype(np.float32)
    blk_onehot = jnp.asarray(blk_onehot, BF16)
    kc = jnp.concatenate([kvc[:, :G].astype(BF16), jnp.broadcast_to(cmp_feat, (B, G, ncmp, feat_w))], axis=-1)
    vc = kvc[:, G:].astype(BF16)
    glt = gl[:, :3 * B_HEADS].reshape(B, T, G, 3 * R).transpose(0, 2, 1, 3)
    bg = b_b_gate[0].reshape(G, 1, 3 * R)

    nsel = T // SEL_BLOCK
    ci = np.arange(ncmp)[:, None] * CMP_STRIDE
    sj = np.arange(LANES)[None, :] * SEL_BLOCK
    ov = ((ci < sj + SEL_BLOCK) & (ci + CMP_LEN > sj) & (np.arange(LANES)[None, :] < nsel)
          & (np.arange(ncmp)[:, None] < ncmp - 1))
    ov = jnp.asarray(ov.astype(np.float32), BF16)

    oa = nsa_attention(qp.reshape(B, T, B_HEADS * LANES), q_feat, kc, vc, kvx.reshape(B, T, -1),
                       cmp_cols // LANES, key_feat, blk_onehot, ones_col, glt, bg, ov)

    xs, xn, meta, pos, cnt = moe_router(xs, oa.reshape(N, hd), b_w_out[0].astype(BF16), norm_ffn[1], m_router[0])
    ne = N_EXPERTS
    p_rows = 2 * N + ne * MOE_TILE
    i1 = meta[:, META_I1].astype(jnp.int32)
    i2 = meta[:, META_I2].astype(jnp.int32)
    counts = cnt[0, :ne].astype(jnp.int32)
    padded = (counts + MOE_TILE - 1) // MOE_TILE * MOE_TILE
    seg_end = jnp.cumsum(padded)
    seg_start = seg_end - padded
    pos8 = pos[:, :ne].astype(jnp.int32)
    d1 = seg_start[i1] + jnp.take_along_axis(pos8, i1[:, None], axis=1)[:, 0]
    d2 = seg_start[i2] + jnp.take_along_axis(pos8, i2[:, None], axis=1)[:, 0]
    tok = jnp.arange(N, dtype=jnp.int32)
    row_tok = jnp.zeros((p_rows,), jnp.int32).at[jnp.concatenate([d1, d2])].set(jnp.concatenate([tok, tok]))
    tile_start = jnp.arange(p_rows // MOE_TILE, dtype=jnp.int32) * MOE_TILE
    tile_expert = jnp.minimum(jnp.sum(tile_start[:, None] >= seg_end[None, :], axis=1), ne - 1).astype(jnp.int32)
    tile_ok = (tile_start < seg_end[ne - 1]).astype(jnp.int32)

    ys = moe_experts(xn, row_tok, tile_expert, tile_ok, m_w_gu[0].astype(BF16), m_w_down[0].astype(BF16))
    out = moe_combine(xs, meta, ys, d1, d2, norm_final)
    return out.reshape(B, T, D)
```

```python
import functools

import numpy as np
import jax
import jax.numpy as jnp
from jax import lax
from jax.experimental import pallas as pl
from jax.experimental.pallas import tpu as pltpu

F32 = jnp.float32
BF16 = jnp.bfloat16

RMS_EPS = 1e-6
A_HEADS = 4
A_CONV = 4
B_HEADS = 16
B_KV_GROUPS = 4
B_REP = B_HEADS // B_KV_GROUPS
B_HEAD_DIM = 64
CMP_LEN = 32
CMP_STRIDE = 16
SEL_BLOCK = 64
SEL_TOPN = 16
WINDOW = 512
FORCED_SCORE = 1e4
N_EXPERTS = 8

LANES = 128
V7X_VMEM_BYTES = 64 * 1024 * 1024
VMEM_LIMIT = V7X_VMEM_BYTES - 8 * 1024 * 1024
NEG = -1e30


def _cparams(*sem):
    return pltpu.CompilerParams(dimension_semantics=sem, vmem_limit_bytes=VMEM_LIMIT)


def _dot(a, b):
    return jnp.dot(a, b, preferred_element_type=F32)


def _dot_nt(a, b):
    return lax.dot_general(a, b, (((1,), (1,)), ((), ())), preferred_element_type=F32)


def _dot_tn(a, b):
    return lax.dot_general(a, b, (((0,), (0,)), ((), ())), preferred_element_type=F32)


def _split3(a):
    a1 = a.astype(BF16)
    r1 = a - a1.astype(F32)
    a2 = r1.astype(BF16)
    a3 = (r1 - a2.astype(F32)).astype(BF16)
    return a1, a2, a3


def _hi_lo_weight(w):
    w = jnp.pad(w, ((0, 0), (0, LANES - w.shape[1])))
    hi = w.astype(BF16)
    lo = (w - hi.astype(F32)).astype(BF16)
    return jnp.concatenate([hi, lo], axis=1)


def _dot_hi_lo(a, w_hl):
    a1 = a.astype(BF16)
    a2 = (a - a1.astype(F32)).astype(BF16)
    s = _dot(a1, w_hl) + _dot(a2, w_hl)
    return s[:, :LANES] + s[:, LANES:]


def _rms(xf, g):
    return xf * lax.rsqrt(jnp.mean(xf * xf, axis=-1, keepdims=True) + RMS_EPS) * g


def _silu(x):
    return x * jax.nn.sigmoid(x)


def _log_sigmoid(x):
    return jnp.minimum(x, 0.0) - jnp.log(1.0 + jnp.exp(-jnp.abs(x)))


def _norm_matmul_kernel(*refs, side):
    if side:
        x_ref, g_ref, w_ref, ws_ref, o_ref, os_ref, xn_ref = refs
    else:
        x_ref, g_ref, w_ref, o_ref, xn_ref = refs

    @pl.when(pl.program_id(1) == 0)
    def _():
        xn = _rms(x_ref[...], g_ref[...])
        xn_ref[...] = xn.astype(xn_ref.dtype)
        if side:
            os_ref[...] = _dot_hi_lo(xn, ws_ref[...])

    o_ref[...] = _dot(xn_ref[...], w_ref[...]).astype(o_ref.dtype)


def norm_matmul(x, g, w, *, w_side=None, tm=1024, tn=1024):
    n, d = x.shape
    dout = w.shape[1]
    tm = min(tm, n)
    tn = min(tn, dout)
    side = w_side is not None
    in_specs = [pl.BlockSpec((tm, d), lambda i, j: (i, 0)),
                pl.BlockSpec((1, d), lambda i, j: (0, 0)),
                pl.BlockSpec((d, tn), lambda i, j: (0, j))]
    out_specs = [pl.BlockSpec((tm, tn), lambda i, j: (i, j))]
    out_shape = [jax.ShapeDtypeStruct((n, dout), BF16)]
    args = [x, g.reshape(1, d), w]
    if side:
        in_specs.append(pl.BlockSpec((d, 2 * LANES), lambda i, j: (0, 0)))
        out_specs.append(pl.BlockSpec((tm, LANES), lambda i, j: (i, 0)))
        out_shape.append(jax.ShapeDtypeStruct((n, LANES), F32))
        args.append(_hi_lo_weight(w_side))
    out = pl.pallas_call(
        functools.partial(_norm_matmul_kernel, side=side),
        grid=(n // tm, dout // tn),
        in_specs=in_specs,
        out_specs=out_specs,
        out_shape=out_shape,
        scratch_shapes=[pltpu.VMEM((tm, d), BF16)],
        compiler_params=_cparams("parallel", "arbitrary"),
        name="norm_matmul_side" if side else "norm_matmul",
    )(*args)
    return out if side else out[0]


MLSTM_CHUNK = 256


def _mlstm_kernel(qk_ref, v_ref, o_ref, gcol_ref, grow_ref, bcol_ref, brow_ref, convw_ref, gout_ref,
                  out_ref, ct_ref, n_ref, m_ref, prev_ref):
    L = qk_ref.shape[0]
    H = A_HEADS
    inner = v_ref.shape[1]
    dh = inner // H

    @pl.when(pl.program_id(1) == 0)
    def _():
        ct_ref[...] = jnp.zeros_like(ct_ref)
        n_ref[...] = jnp.zeros_like(n_ref)
        m_ref[...] = jnp.zeros_like(m_ref)
        prev_ref[...] = jnp.zeros_like(prev_ref)

    row = lax.broadcasted_iota(jnp.int32, (L, L), 0)
    col = lax.broadcasted_iota(jnp.int32, (L, L), 1)
    causal = col <= row
    tril = jnp.where(causal, 1.0, 0.0).astype(BF16)
    triu = jnp.where(row <= col, 1.0, 0.0).astype(BF16)

    gc = gcol_ref[...] + brow_ref[...]
    gr = grow_ref[...] + bcol_ref[...]
    lfc1, lfc2, lfc3 = _split3(_log_sigmoid(gc))
    lfr1, lfr2, lfr3 = _split3(_log_sigmoid(gr))
    b_c = _dot(tril, lfc1) + _dot(tril, lfc2) + _dot(tril, lfc3)
    b_r = _dot(lfr1, triu) + _dot(lfr2, triu) + _dot(lfr3, triu)

    rowi = lax.broadcasted_iota(jnp.int32, (L, dh), 0)

    def conv(cur, prev, w):
        y = cur * w[A_CONV - 1:A_CONV, :]
        for s in range(1, A_CONV):
            sh = jnp.where(rowi < s, pltpu.roll(prev, s, 0), pltpu.roll(cur, s, 0))
            y = y + sh * w[A_CONV - 1 - s:A_CONV - s, :]
        return y

    for h in range(H):
        hs = slice(h * dh, (h + 1) * dh)
        ks = slice(inner + h * dh, inner + (h + 1) * dh)
        q = conv(qk_ref[:, hs].astype(F32), prev_ref[:, hs].astype(F32), convw_ref[:, hs])
        k = conv(qk_ref[:, ks].astype(F32), prev_ref[:, ks].astype(F32), convw_ref[:, ks]) * (dh ** -0.5)
        v = v_ref[:, hs]
        qb = q.astype(BF16)
        kb = k.astype(BF16)

        li_c = gc[:, h:h + 1]
        bc = b_c[:, H + h:H + h + 1]
        li_r = gr[h:h + 1, :]
        br = b_r[H + h:H + h + 1, :]
        m_prev = m_ref[h:h + 1, 0:1]

        d = jnp.where(causal, bc - br + li_r, -jnp.inf)
        inter = bc + m_prev
        m_t = jnp.maximum(inter, jnp.max(d, axis=-1, keepdims=True))
        w_inter = jnp.exp(inter - m_t)
        s = _dot_nt(qb, kb) * jnp.exp(d - m_t)
        ct = ct_ref[h]
        num = _dot(s.astype(BF16), v) + w_inter * _dot(qb, ct.astype(BF16))
        den = jnp.sum(s, axis=-1, keepdims=True) + w_inter * jnp.sum(q * n_ref[h], axis=-1, keepdims=True)
        hh = num / jnp.maximum(jnp.abs(den), jnp.exp(-m_t))
        hh = hh * lax.rsqrt(jnp.mean(hh * hh, axis=-1, keepdims=True) + RMS_EPS)
        out_ref[:, hs] = (hh * gout_ref[:, hs] * jax.nn.sigmoid(o_ref[:, hs].astype(F32))).astype(out_ref.dtype)

        b_last = bc[L - 1:L, :]
        g = b_last - bc + li_c
        m_new = jnp.maximum(b_last + m_prev, jnp.max(g, axis=0, keepdims=True))
        a_prev = jnp.exp(b_last + m_prev - m_new)
        a_s = jnp.exp(g - m_new)
        ct_ref[h] = a_prev * ct + _dot_tn(kb, (v.astype(F32) * a_s).astype(BF16))
        n_ref[h] = a_prev * n_ref[h] + jnp.sum(k * a_s, axis=0, keepdims=True)
        m_ref[h:h + 1, :] = jnp.broadcast_to(m_new, (1, LANES))

    prev_ref[...] = qk_ref[...]


def mlstm_core(proj, gcol, b_gate, conv_w, g_out, batch, seq):
    n = proj.shape[0]
    inner = proj.shape[1] // 4
    H = A_HEADS
    dh = inner // H
    L = min(MLSTM_CHUNK, seq)
    nc = seq // L
    grow = gcol[:, :2 * H].T
    brow = jnp.pad(b_gate, (0, LANES - 2 * H)).reshape(1, LANES)
    bcol = b_gate.reshape(2 * H, 1)
    return pl.pallas_call(
        _mlstm_kernel,
        grid=(batch, nc),
        in_specs=[pl.BlockSpec((L, 2 * inner), lambda b, c: (b * nc + c, 0)),
                  pl.BlockSpec((L, inner), lambda b, c: (b * nc + c, 2)),
                  pl.BlockSpec((L, inner), lambda b, c: (b * nc + c, 3)),
                  pl.BlockSpec((L, LANES), lambda b, c: (b * nc + c, 0)),
                  pl.BlockSpec((2 * H, L), lambda b, c: (0, b * nc + c)),
                  pl.BlockSpec((2 * H, 1), lambda b, c: (0, 0)),
                  pl.BlockSpec((1, LANES), lambda b, c: (0, 0)),
                  pl.BlockSpec((A_CONV, 2 * inner), lambda b, c: (0, 0)),
                  pl.BlockSpec((1, inner), lambda b, c: (0, 0))],
        out_specs=pl.BlockSpec((L, inner), lambda b, c: (b * nc + c, 0)),
        out_shape=jax.ShapeDtypeStruct((n, inner), BF16),
        scratch_shapes=[pltpu.VMEM((H, dh, dh), F32),
                        pltpu.VMEM((H, 1, dh), F32),
                        pltpu.VMEM((8, LANES), F32),
                        pltpu.VMEM((L, 2 * inner), BF16)],
        compiler_params=_cparams("parallel", "arbitrary"),
        name="mlstm_core",
    )(proj, proj, proj, gcol, grow, bcol, brow, conv_w, g_out.reshape(1, inner))


def _ffn_kernel(x_ref, h_ref, wo_ref, g_ref, wg_ref, wu_ref, wd_ref, o_ref, xn_ref, acc_ref):
    j = pl.program_id(1)

    @pl.when(j == 0)
    def _():
        xf = x_ref[...] + _dot(h_ref[...], wo_ref[...])
        xn_ref[...] = _rms(xf, g_ref[...]).astype(BF16)
        acc_ref[...] = xf

    xn = xn_ref[...]
    a = (_silu(_dot(xn, wg_ref[...])) * _dot(xn, wu_ref[...])).astype(BF16)
    acc_ref[...] += _dot(a, wd_ref[...])

    @pl.when(j == pl.num_programs(1) - 1)
    def _():
        o_ref[...] = acc_ref[...]


def ffn_dense(x, h, w_out, g, w_gu, w_down, *, tm=512, tf=1408):
    n, d = x.shape
    f = w_down.shape[0]
    dh_in = h.shape[1]
    tm = min(tm, n)
    nf = f // tf
    return pl.pallas_call(
        _ffn_kernel,
        grid=(n // tm, nf),
        in_specs=[pl.BlockSpec((tm, d), lambda i, j: (i, 0)),
                  pl.BlockSpec((tm, dh_in), lambda i, j: (i, 0)),
                  pl.BlockSpec((dh_in, d), lambda i, j: (0, 0)),
                  pl.BlockSpec((1, d), lambda i, j: (0, 0)),
                  pl.BlockSpec((d, tf), lambda i, j: (0, j)),
                  pl.BlockSpec((d, tf), lambda i, j: (0, j + nf)),
                  pl.BlockSpec((tf, d), lambda i, j: (j, 0))],
        out_specs=pl.BlockSpec((tm, d), lambda i, j: (i, 0)),
        out_shape=jax.ShapeDtypeStruct((n, d), F32),
        scratch_shapes=[pltpu.VMEM((tm, d), BF16), pltpu.VMEM((tm, d), F32)],
        compiler_params=_cparams("parallel", "arbitrary"),
        name="ffn_dense",
    )(x, h, w_out, g.reshape(1, d), w_gu, w_gu, w_down)


def _compress_kernel(r_ref, pos_ref, w1_ref, w2_ref, o_ref):
    r = r_ref[0, 0]
    w1 = w1_ref[0]
    half = r.shape[1]
    nc = r.shape[0]
    lo = _dot(r, w1[:half])
    hi = _dot(r, w1[half:])
    hid = lo + pltpu.roll(hi, nc - 1, 0) + _dot(pos_ref[0], w1)
    o_ref[0, 0] = _dot(_silu(hid).astype(BF16), w2_ref[0]).astype(o_ref.dtype)


def nsa_compress(r, pos, w1, w2):
    b, c2, nc, half = r.shape
    g = c2 // 2
    hidden = w1.shape[2]
    dh = w2.shape[2]
    return pl.pallas_call(
        _compress_kernel,
        grid=(b, c2),
        in_specs=[pl.BlockSpec((1, 1, nc, half), lambda i, j: (i, j, 0, 0)),
                  pl.BlockSpec((1, 1, 2 * half), lambda i, j: (j // g, 0, 0)),
                  pl.BlockSpec((1, 2 * half, hidden), lambda i, j: (j // g, 0, 0)),
                  pl.BlockSpec((1, hidden, dh), lambda i, j: (j // g, 0, 0))],
        out_specs=pl.BlockSpec((1, 1, nc, dh), lambda i, j: (i, j, 0, 0)),
        out_shape=jax.ShapeDtypeStruct((b, c2, nc, dh), F32),
        compiler_params=_cparams("parallel", "parallel"),
        name="nsa_compress",
    )(r, pos, w1, w2)


NSA_TQ = 256
NSA_TK_SLC = 512
NSA_TK_WIN = 256
NSA_ROWS = 32
SEL_MASK = 2.0 ** 14
LOG2E = float(np.log2(np.e))


def _nsa_kernel(q_ref, qf_ref, kc_ref, vc_ref, ks_ref, vs_ref, kw_ref, vw_ref, kf_ref, oh_ref, one_ref,
                gl_ref, bg_ref, ov_ref,
                out_ref, qx_ref, ksx_ref, vsx_ref, kwx_ref, vwx_ref,
                ms_ref, accs_ref, biass_ref, s0_ref, p0_ref, a0_ref, s1_ref, p1_ref, a1_ref,
                mw_ref, accw_ref, biasw_ref, s2_ref, p2_ref, a2_ref, s3_ref, p3_ref, a3_ref, live_ref,
                *, tq, tks, tkw, n_top):
    R = B_REP
    M = R * tq
    dh = B_HEAD_DIM
    t0 = pl.program_id(2) * tq
    slc_state = (ms_ref, accs_ref, biass_ref, ((s0_ref, p0_ref, a0_ref), (s1_ref, p1_ref, a1_ref)))
    win_state = (mw_ref, accw_ref, biasw_ref, ((s2_ref, p2_ref, a2_ref), (s3_ref, p3_ref, a3_ref)))

    @pl.when(pl.program_id(2) == 0)
    def _():
        kf = kf_ref[...]
        ksx_ref[:, :LANES] = ks_ref[0] + kf
        ksx_ref[:, LANES:] = oh_ref[...]
        kwx_ref[...] = kw_ref[0] + kf
        vsx_ref[...] = vs_ref[0] + one_ref[...]
        vwx_ref[...] = vw_ref[0] + one_ref[...]

    qb = q_ref[0]
    for r in range(R):
        qx_ref[r * tq:(r + 1) * tq, :LANES] = qb[:, r * LANES:(r + 1) * LANES] + qf_ref[0, r]
    q = qx_ref[:, :LANES]

    kc = kc_ref[0, 0]
    ncmp = kc.shape[0]
    s = _dot_nt(q, kc)
    tpos_c = t0 + (lax.broadcasted_iota(jnp.int32, (M, ncmp), 0) & (tq - 1))
    cend = lax.broadcasted_iota(jnp.int32, (M, ncmp), 1) * CMP_STRIDE + (CMP_LEN - 1)
    ok_c = tpos_c >= cend
    s = jnp.where(ok_c, s, NEG)
    p = jnp.where(ok_c, jnp.exp2(s - jnp.max(s, axis=-1, keepdims=True)), 0.0)
    dsum = jnp.sum(p, axis=-1, keepdims=True)
    p = p / jnp.where(dsum > 0, dsum, 1.0)
    o_cmp = _dot(p.astype(BF16), vc_ref[0, 0])

    psum = p[0:tq]
    for r in range(1, R):
        psum = psum + p[r * tq:(r + 1) * tq]
    p1, p2, p3 = _split3(psum)
    ov = ov_ref[...]
    imp = _dot(p1, ov) + _dot(p2, ov) + _dot(p3, ov)
    nsel = ks_ref.shape[1] // SEL_BLOCK
    nselp = max(nsel, 8)
    imp_t = imp.T[:nselp]
    jj = lax.broadcasted_iota(jnp.int32, (nselp, tq), 0)
    tt = t0 + lax.broadcasted_iota(jnp.int32, (nselp, tq), 1)
    cur = tt // SEL_BLOCK
    forced = (jj == 0) | (jj == cur) | (jj == cur - 1)
    iv = jnp.where(forced, FORCED_SCORE, imp_t)
    iv = jnp.where(jj * SEL_BLOCK <= tt, iv, -jnp.inf)
    rank = jnp.zeros((nselp, tq), F32)
    for j2 in range(nsel):
        rv = iv[j2:j2 + 1, :]
        rank = rank + jnp.where(jj > j2, jnp.where(rv >= iv, 1.0, 0.0), jnp.where(rv > iv, 1.0, 0.0))
    unsel = jnp.where(rank < n_top, 0.0, -SEL_MASK)
    if nselp < LANES:
        unsel = jnp.concatenate([unsel, jnp.zeros((LANES - nselp, tq), F32)], axis=0)
    qmask = unsel.T.astype(BF16)
    for r in range(R):
        qx_ref[r * tq:(r + 1) * tq, LANES:] = qmask

    def tile_step(qq, k_ref, v_ref, k0, tk, state, slot, bias):
        m_ref, acc_ref, bias_ref, slots = state
        s_ref, p_ref, a_ref = slots[slot]
        kk = k_ref[pl.ds(k0, tk), :]
        vv = v_ref[pl.ds(k0, tk), :]
        if bias is not None:
            bias_ref[:, :tk] = bias
        s_ref[:, :tk] = _dot_nt(qq, kk)
        for r in range(R):
            for c0 in range(0, tq, NSA_ROWS):
                rows = slice(r * tq + c0, r * tq + c0 + NSA_ROWS)
                sc = s_ref[rows, :tk]
                if bias is not None:
                    sc = sc + bias_ref[c0:c0 + NSA_ROWS, :tk]
                m_old = m_ref[rows, :]
                m_new = jnp.maximum(m_old, jnp.max(sc, axis=-1, keepdims=True))
                a_ref[rows, :] = jnp.exp2(m_old - m_new)
                m_ref[rows, :] = m_new
                p_ref[rows, :tk] = jnp.exp2(sc - jnp.concatenate([m_new] * (tk // LANES), axis=1)).astype(BF16)
        acc_ref[...] = a_ref[...] * acc_ref[...] + _dot(p_ref[:, :tk], vv)

    def reset(state):
        state[0][...] = jnp.full_like(state[0], NEG)
        state[1][...] = jnp.zeros_like(state[1])

    def result(state):
        acc = state[1][...]
        return acc[:, :dh] / acc[:, dh:dh + 1]

    reset(slc_state)
    qx = qx_ref[...]
    n_full = t0 // tks

    blocks_per_tile = tks // SEL_BLOCK
    n_live = jnp.int32(0)
    for j in range(nsel // blocks_per_tile):
        sel_j = rank[j * blocks_per_tile:(j + 1) * blocks_per_tile, :] < n_top
        live = (jnp.max(jnp.where(sel_j, 1.0, 0.0)) > 0.5) & (j < n_full)
        live_ref[n_live] = j
        n_live = n_live + live.astype(jnp.int32)

    def slc_pair(i, carry):
        tile_step(qx, ksx_ref, vsx_ref, pl.multiple_of(live_ref[2 * i] * tks, tks), tks, slc_state, 0, None)
        tile_step(qx, ksx_ref, vsx_ref, pl.multiple_of(live_ref[2 * i + 1] * tks, tks), tks, slc_state, 1, None)
        return carry

    lax.fori_loop(0, n_live // 2, slc_pair, 0)

    @pl.when(n_live % 2 == 1)
    def _():
        tile_step(qx, ksx_ref, vsx_ref, pl.multiple_of(live_ref[n_live - 1] * tks, tks), tks, slc_state, 0, None)

    kd = pl.multiple_of(n_full * tks, tks)
    spos = kd + lax.broadcasted_iota(jnp.int32, (tq, tks), 1)
    tpos = t0 + lax.broadcasted_iota(jnp.int32, (tq, tks), 0)
    tile_step(qx, ksx_ref, vsx_ref, kd, tks, slc_state, 1, jnp.where(spos <= tpos, 0.0, NEG))
    reset(win_state)
    n_win = WINDOW // tkw + max(tq // tkw, 1)
    last = (t0 + tq - 1) // tkw
    for i in range(n_win):
        kw0 = (last - (n_win - 1) + i) * tkw
        spos = kw0 + lax.broadcasted_iota(jnp.int32, (tq, tkw), 1)
        dist = t0 + lax.broadcasted_iota(jnp.int32, (tq, tkw), 0) - spos
        bias = jnp.where((spos >= 0) & (dist >= 0) & (dist < WINDOW), 0.0, NEG)
        tile_step(q, kwx_ref, vwx_ref, pl.multiple_of(jnp.maximum(kw0, 0), tkw), tkw, win_state, i % 2, bias)
    o_slc = result(slc_state)
    o_win = result(win_state)

    gates = jax.nn.sigmoid(gl_ref[0, 0, pl.ds(pl.multiple_of(t0, tq), tq), :] + bg_ref[0])
    outs = []
    for r in range(R):
        rs = slice(r * tq, (r + 1) * tq)
        outs.append(gates[:, 3 * r:3 * r + 1] * o_cmp[rs] + gates[:, 3 * r + 1:3 * r + 2] * o_slc[rs]
                    + gates[:, 3 * r + 2:3 * r + 3] * o_win[rs])
    out_ref[0] = jnp.concatenate(outs, axis=1).astype(out_ref.dtype)


def nsa_attention(qp, q_feat, kc, vc, kvx, kv_off, key_feat, blk_onehot, ones_col, gl, bg, ov):
    b, t, _ = qp.shape
    g, r = q_feat.shape[:2]
    dh = B_HEAD_DIM
    tq = min(NSA_TQ, t)
    tks = min(NSA_TK_SLC, t)
    tkw = min(NSA_TK_WIN, t)
    ncmp = kc.shape[2]
    n_top = min(SEL_TOPN, t // SEL_BLOCK)
    kv_spec = lambda c: pl.BlockSpec((1, t, LANES), lambda i, j, k: (i, 0, kv_off + c * g + j))
    const_spec = lambda a: pl.BlockSpec(a.shape, lambda i, j, k: (0,) * a.ndim)

    def branch_scratch(tk):
        slot = [pltpu.VMEM((r * tq, tk), F32), pltpu.VMEM((r * tq, tk), BF16), pltpu.VMEM((r * tq, LANES), F32)]
        return [pltpu.VMEM((r * tq, LANES), F32), pltpu.VMEM((r * tq, LANES), F32),
                pltpu.VMEM((tq, tk), F32)] + slot + slot

    return pl.pallas_call(
        functools.partial(_nsa_kernel, tq=tq, tks=tks, tkw=tkw, n_top=n_top),
        grid=(b, g, t // tq),
        in_specs=[pl.BlockSpec((1, tq, r * LANES), lambda i, j, k: (i, k, j)),
                  pl.BlockSpec((1, r, 1, LANES), lambda i, j, k: (j, 0, 0, 0)),
                  pl.BlockSpec((1, 1, ncmp, LANES), lambda i, j, k: (i, j, 0, 0)),
                  pl.BlockSpec((1, 1, ncmp, dh), lambda i, j, k: (i, j, 0, 0)),
                  kv_spec(0), kv_spec(1), kv_spec(2), kv_spec(3),
                  const_spec(key_feat), const_spec(blk_onehot), const_spec(ones_col),
                  pl.BlockSpec((1, 1, t, 3 * r), lambda i, j, k: (i, j, 0, 0)),
                  pl.BlockSpec((1, 1, 3 * r), lambda i, j, k: (j, 0, 0)),
                  const_spec(ov)],
        out_specs=pl.BlockSpec((1, tq, r * dh), lambda i, j, k: (i, k, j)),
        out_shape=jax.ShapeDtypeStruct((b, t, g * r * dh), BF16),
        scratch_shapes=[pltpu.VMEM((r * tq, 2 * LANES), BF16),
                        pltpu.VMEM((t, 2 * LANES), BF16),
                        pltpu.VMEM((t, LANES), BF16),
                        pltpu.VMEM((t, LANES), BF16),
                        pltpu.VMEM((t, LANES), BF16),
                        ] + branch_scratch(tks) + branch_scratch(tkw)
                       + [pltpu.SMEM((t // tks + 1,), jnp.int32)],
        compiler_params=_cparams("arbitrary", "arbitrary", "arbitrary"),
        name="nsa_attention",
    )(qp, q_feat, kc, vc, kvx, kvx, kvx, kvx, key_feat, blk_onehot, ones_col, gl, bg, ov)


def _alibi_slopes(n):
    return np.power(2.0, -8.0 * np.arange(1, n + 1) / n).astype(np.float32)


def _np_split3(a):
    a = np.asarray(a, np.float32)
    out = []
    r = a
    for _ in range(3):
        p = r.astype(BF16).astype(np.float32)
        out.append(p)
        r = (r - p).astype(np.float32)
    return out


def _pos_features(pos, width):
    hi = (pos // 64).astype(np.float32)
    lo = (pos % 64).astype(np.float32)
    f = np.zeros((pos.shape[0], width), np.float32)
    f[:, 0:3] = hi[:, None]
    f[:, 3:6] = lo[:, None]
    return f


def _slope_features(width):
    s1, s2, s3 = _np_split3((_alibi_slopes(B_HEADS).astype(np.float64) * LOG2E).astype(np.float32))
    f = np.zeros((B_HEADS, width), np.float32)
    for i, s in enumerate((s1, s2, s3)):
        f[:, i] = 64.0 * s
        f[:, 3 + i] = s
    return f


META_I1, META_I2, META_W1, META_W2 = 0, 1, 2, 3


def _router_kernel(x_ref, h_ref, wo_ref, g_ref, w_ref, xo_ref, xn_ref, meta_ref, pos_ref, cnt_ref, run_ref):
    @pl.when(pl.program_id(0) == 0)
    def _():
        run_ref[...] = jnp.zeros_like(run_ref)

    xf = x_ref[...] + _dot(h_ref[...], wo_ref[...])
    xo_ref[...] = xf
    xn = _rms(xf, g_ref[...])
    xn_ref[...] = xn
    logits = _dot_hi_lo(xn, w_ref[...])
    tm = logits.shape[0]
    lane = lax.broadcasted_iota(jnp.int32, logits.shape, 1)
    logits = jnp.where(lane < N_EXPERTS, logits, -jnp.inf)
    m1 = jnp.max(logits, axis=-1, keepdims=True)
    i1 = jnp.min(jnp.where(logits == m1, lane, LANES), axis=-1, keepdims=True)
    rest = jnp.where(lane == i1, -jnp.inf, logits)
    m2 = jnp.max(rest, axis=-1, keepdims=True)
    i2 = jnp.min(jnp.where(rest == m2, lane, LANES), axis=-1, keepdims=True)
    e2 = jnp.exp(m2 - m1)
    w1 = 1.0 / (1.0 + e2)
    meta = jnp.where(lane == META_I1, i1.astype(F32), 0.0)
    meta = jnp.where(lane == META_I2, i2.astype(F32), meta)
    meta = jnp.where(lane == META_W1, w1, meta)
    meta_ref[...] = jnp.where(lane == META_W2, e2 * w1, meta)

    sel = jnp.where((lane == i1) | (lane == i2), 1.0, 0.0)
    row = lax.broadcasted_iota(jnp.int32, (tm, tm), 0)
    col = lax.broadcasted_iota(jnp.int32, (tm, tm), 1)
    before = jnp.where(col < row, 1.0, 0.0).astype(BF16)
    run = run_ref[...]
    pos_ref[...] = run + _dot(before, sel.astype(BF16))
    run = run + jnp.sum(sel, axis=0, keepdims=True)
    run_ref[...] = run
    cnt_ref[...] = run


def moe_router(x, h, w_out, g, w_router, *, tm=512):
    n, d = x.shape
    dh_in = h.shape[1]
    tm = min(tm, n)
    w = _hi_lo_weight(w_router)
    row_spec = pl.BlockSpec((tm, LANES), lambda i: (i, 0))
    return pl.pallas_call(
        _router_kernel,
        grid=(n // tm,),
        in_specs=[pl.BlockSpec((tm, d), lambda i: (i, 0)),
                  pl.BlockSpec((tm, dh_in), lambda i: (i, 0)),
                  pl.BlockSpec((dh_in, d), lambda i: (0, 0)),
                  pl.BlockSpec((1, d), lambda i: (0, 0)),
                  pl.BlockSpec((d, 2 * LANES), lambda i: (0, 0))],
        out_specs=[pl.BlockSpec((tm, d), lambda i: (i, 0)), pl.BlockSpec((tm, d), lambda i: (i, 0)),
                   row_spec, row_spec,
                   pl.BlockSpec((1, LANES), lambda i: (0, 0))],
        out_shape=[jax.ShapeDtypeStruct((n, d), F32), jax.ShapeDtypeStruct((n, d), F32),
                   jax.ShapeDtypeStruct((n, LANES), F32),
                   jax.ShapeDtypeStruct((n, LANES), F32), jax.ShapeDtypeStruct((1, LANES), F32)],
        scratch_shapes=[pltpu.VMEM((1, LANES), F32)],
        compiler_params=_cparams("arbitrary"),
        name="moe_router",
    )(x, h, w_out, g.reshape(1, d), w)


MOE_TILE = 512


def _moe_kernel(te_ref, ok_ref, idx0_ref, idxn_ref, x_hbm, wg_ref, wu_ref, wd_ref, o_ref, xbuf, xb_ref, sem):
    i = pl.program_id(0)
    j = pl.program_id(1)
    nt = pl.num_programs(0)
    nf = pl.num_programs(1)
    tm = xb_ref.shape[0]
    per_step = tm // nf
    slot = i % 2

    def row_copy(idx_ref, r, s):
        return pltpu.make_async_copy(x_hbm.at[pl.ds(idx_ref[0, 0, r], 1), :], xbuf.at[s, pl.ds(r, 1), :], sem.at[s])

    @pl.when((i == 0) & (j == 0))
    def _():
        def start0(r, c):
            row_copy(idx0_ref, r, 0).start()
            return c
        lax.fori_loop(0, tm, start0, 0)

    @pl.when(j == 0)
    def _():
        for r in range(tm):
            row_copy(idx0_ref, r, slot).wait()
        xb_ref[...] = xbuf[slot].astype(BF16)
        o_ref[...] = jnp.zeros_like(o_ref)

    def prefetch():
        base = j * per_step
        for r in range(per_step):
            row_copy(idxn_ref, base + r, 1 - slot).start()

    def compute():
        xb = xb_ref[...]
        a = (_silu(_dot(xb, wg_ref[0])) * _dot(xb, wu_ref[0])).astype(BF16)
        o_ref[...] += _dot(a, wd_ref[0])

    has_next = i + 1 < nt
    ok = ok_ref[i] > 0

    @pl.when(has_next & ok)
    def _():
        prefetch()
        compute()

    @pl.when(has_next & jnp.logical_not(ok))
    def _():
        prefetch()

    @pl.when(jnp.logical_not(has_next) & ok)
    def _():
        compute()


def moe_experts(x, row_tok, tile_expert, tile_ok, w_gu, w_down, *, tf=1792):
    p = row_tok.shape[0]
    d = x.shape[1]
    ne, f, _ = w_down.shape
    tm = MOE_TILE
    nf = f // tf
    nt = p // tm
    idx = row_tok.reshape(nt, 1, tm)
    grid_spec = pltpu.PrefetchScalarGridSpec(
        num_scalar_prefetch=2,
        grid=(nt, nf),
        in_specs=[pl.BlockSpec((1, 1, tm), lambda i, j, te, ok: (0, 0, 0), memory_space=pltpu.SMEM),
                  pl.BlockSpec((1, 1, tm), lambda i, j, te, ok: (jnp.minimum(i + 1, nt - 1), 0, 0),
                               memory_space=pltpu.SMEM),
                  pl.BlockSpec(memory_space=pl.ANY),
                  pl.BlockSpec((1, d, tf), lambda i, j, te, ok: (te[i], 0, j)),
                  pl.BlockSpec((1, d, tf), lambda i, j, te, ok: (te[i], 0, j + nf)),
                  pl.BlockSpec((1, tf, d), lambda i, j, te, ok: (te[i], j, 0))],
        out_specs=pl.BlockSpec((tm, d), lambda i, j, te, ok: (i, 0)),
        scratch_shapes=[pltpu.VMEM((2, tm, d), F32), pltpu.VMEM((tm, d), BF16),
                        pltpu.SemaphoreType.DMA((2,))])
    return pl.pallas_call(
        _moe_kernel,
        grid_spec=grid_spec,
        out_shape=jax.ShapeDtypeStruct((p, d), F32),
        compiler_params=_cparams("arbitrary", "arbitrary"),
        name="moe_experts",
    )(tile_expert, tile_ok, idx, idx, x, w_gu, w_gu, w_down)


def _combine_kernel(d0_ref, dn_ref, x_ref, meta_ref, y_ref, gf_ref, o_ref, ybuf, sem):
    i = pl.program_id(0)
    nt = pl.num_programs(0)
    rows = o_ref.shape[0]
    slot = i % 2

    def row_copy(d_ref, k, r, s):
        return pltpu.make_async_copy(y_ref.at[pl.ds(d_ref[0, k, r], 1), :], ybuf.at[s, k, pl.ds(r, 1), :], sem.at[s])

    @pl.when(i == 0)
    def _():
        def start0(r, c):
            row_copy(d0_ref, 0, r, 0).start()
            row_copy(d0_ref, 1, r, 0).start()
            return c
        lax.fori_loop(0, rows, start0, 0)

    for r in range(rows):
        row_copy(d0_ref, 0, r, slot).wait()
        row_copy(d0_ref, 1, r, slot).wait()

    @pl.when(i + 1 < nt)
    def _():
        for r in range(rows):
            row_copy(dn_ref, 0, r, 1 - slot).start()
            row_copy(dn_ref, 1, r, 1 - slot).start()

    meta = meta_ref[...]
    w1 = meta[:, META_W1:META_W1 + 1]
    w2 = meta[:, META_W2:META_W2 + 1]
    o_ref[...] = _rms(x_ref[...] + w1 * ybuf[slot, 0] + w2 * ybuf[slot, 1], gf_ref[...])


def moe_combine(x, meta, ys, d1, d2, g_final, *, tc=256):
    n, d = x.shape
    tc = min(tc, n)
    nt = n // tc
    dd = jnp.stack([d1.reshape(nt, tc), d2.reshape(nt, tc)], axis=1)
    return pl.pallas_call(
        _combine_kernel,
        grid=(nt,),
        in_specs=[pl.BlockSpec((1, 2, tc), lambda i: (0, 0, 0), memory_space=pltpu.SMEM),
                  pl.BlockSpec((1, 2, tc), lambda i: (jnp.minimum(i + 1, nt - 1), 0, 0), memory_space=pltpu.SMEM),
                  pl.BlockSpec((tc, d), lambda i: (i, 0)),
                  pl.BlockSpec((tc, LANES), lambda i: (i, 0)),
                  pl.BlockSpec(memory_space=pl.ANY),
                  pl.BlockSpec((1, d), lambda i: (0, 0))],
        out_specs=pl.BlockSpec((tc, d), lambda i: (i, 0)),
        out_shape=jax.ShapeDtypeStruct((n, d), F32),
        scratch_shapes=[pltpu.VMEM((2, 2, tc, d), F32), pltpu.SemaphoreType.DMA((2,))],
        compiler_params=_cparams("arbitrary"),
        name="moe_combine",
    )(dd, dd, x, meta, ys, g_final.reshape(1, d))


def kernel(x, norm_mix, norm_ffn, a_w_in, a_b_gate, a_conv, a_norm_h, a_w_out, norm_kv, b_w_kv,
           b_cmp_pos, b_cmp_w1, b_cmp_w2, b_w_q, b_b_gate, b_w_out, f_w_gu, f_w_down,
           m_router, m_w_gu, m_w_down, norm_final):
    B, T, D = x.shape
    N = B * T
    G, R, dh = B_KV_GROUPS, B_REP, B_HEAD_DIM
    xs = x.reshape(N, D)

    inner4 = a_w_in.shape[2] - 2 * A_HEADS
    w_in = a_w_in[0]
    proj, gcol = norm_matmul(xs, norm_mix[0], w_in[:, :inner4].astype(BF16), w_side=w_in[:, inner4:])
    hs = mlstm_core(proj, gcol, a_b_gate[0], a_conv[0], a_norm_h[0], B, T)
    xs = ffn_dense(xs, hs, a_w_out[0].astype(BF16), norm_ffn[0], f_w_gu[0].astype(BF16), f_w_down[0].astype(BF16))

    hd = B_HEADS * dh
    feat_w = LANES - dh
    ncmp = T // CMP_STRIDE

    def slabs(w):
        d_in, cols = w.shape
        return jnp.pad(w.reshape(d_in, cols // dh, dh), ((0, 0), (0, 0), (0, feat_w))).reshape(d_in, -1)

    cmp_cols = 2 * G * dh
    w_kvx = jnp.concatenate([b_w_kv[:, :cmp_cols], slabs(b_w_kv[:, cmp_cols:])], axis=1).astype(BF16)
    kvx = norm_matmul(xs, norm_kv, w_kvx, tn=w_kvx.shape[1] // 2)
    w_q = b_w_q[0]
    qp, gl = norm_matmul(xs, norm_mix[1], slabs(w_q[:, :hd] * (dh ** -0.5 * LOG2E)).astype(BF16),
                         w_side=w_q[:, hd:])

    kvt = kvx[:, :cmp_cols].reshape(B, T, 2 * G, dh).transpose(0, 2, 1, 3)
    pos = b_cmp_pos.transpose(1, 0, 2).reshape(2, 1, CMP_LEN * dh).astype(BF16)
    kvc = nsa_compress(kvt.reshape(B, 2 * G, ncmp, CMP_STRIDE * dh), pos,
                       b_cmp_w1.astype(BF16), b_cmp_w2.astype(BF16))

    def hi_lanes(f):
        return jnp.asarray(np.concatenate([np.zeros((f.shape[0], dh), np.float32), f], axis=1), BF16)

    key_feat = hi_lanes(_pos_features(np.arange(T), feat_w))
    cmp_feat = jnp.asarray(_pos_features(np.arange(ncmp) * CMP_STRIDE + CMP_LEN - 1, feat_w), BF16)
    q_feat = hi_lanes(_slope_features(feat_w)).reshape(G, R, 1, LANES)
    ones_col = np.zeros((1, feat_w), np.float32)
    ones_col[0, 0] = 1.0
    ones_col = hi_lanes(ones_col)
    blk_onehot = (np.arange(T)[:, None] // SEL_BLOCK == np.arange(LANES)[None, :]).astype(np.float32)
    blk_onehot = jnp.asarray(blk_onehot, BF16)
    kc = jnp.concatenate([kvc[:, :G].astype(BF16), jnp.broadcast_to(cmp_feat, (B, G, ncmp, feat_w))], axis=-1)
    vc = kvc[:, G:].astype(BF16)
    glt = gl[:, :3 * B_HEADS].reshape(B, T, G, 3 * R).transpose(0, 2, 1, 3)
    bg = b_b_gate[0].reshape(G, 1, 3 * R)

    nsel = T // SEL_BLOCK
    ci = np.arange(ncmp)[:, None] * CMP_STRIDE
    sj = np.arange(LANES)[None, :] * SEL_BLOCK
    ov = ((ci < sj + SEL_BLOCK) & (ci + CMP_LEN > sj) & (np.arange(LANES)[None, :] < nsel)
          & (np.arange(ncmp)[:, None] < ncmp - 1))
    ov = jnp.asarray(ov.astype(np.float32), BF16)

    oa = nsa_attention(qp.reshape(B, T, B_HEADS * LANES), q_feat, kc, vc, kvx.reshape(B, T, -1),
                       cmp_cols // LANES, key_feat, blk_onehot, ones_col, glt, bg, ov)

    xs, xn, meta, pos, cnt = moe_router(xs, oa.reshape(N, hd), b_w_out[0].astype(BF16), norm_ffn[1], m_router[0])
    ne = N_EXPERTS
    p_rows = 2 * N + ne * MOE_TILE
    i1 = meta[:, META_I1].astype(jnp.int32)
    i2 = meta[:, META_I2].astype(jnp.int32)
    counts = cnt[0, :ne].astype(jnp.int32)
    padded = (counts + MOE_TILE - 1) // MOE_TILE * MOE_TILE
    seg_end = jnp.cumsum(padded)
    seg_start = seg_end - padded
    pos8 = pos[:, :ne].astype(jnp.int32)
    d1 = seg_start[i1] + jnp.take_along_axis(pos8, i1[:, None], axis=1)[:, 0]
    d2 = seg_start[i2] + jnp.take_along_axis(pos8, i2[:, None], axis=1)[:, 0]
    tok = jnp.arange(N, dtype=jnp.int32)
    row_tok = jnp.zeros((p_rows,), jnp.int32).at[jnp.concatenate([d1, d2])].set(jnp.concatenate([tok, tok]))
    tile_start = jnp.arange(p_rows // MOE_TILE, dtype=jnp.int32) * MOE_TILE
    tile_expert = jnp.minimum(jnp.sum(tile_start[:, None] >= seg_end[None, :], axis=1), ne - 1).astype(jnp.int32)
    tile_ok = (tile_start < seg_end[ne - 1]).astype(jnp.int32)

    ys = moe_experts(xn, row_tok, tile_expert, tile_ok, m_w_gu[0].astype(BF16), m_w_down[0].astype(BF16))
    out = moe_combine(xs, meta, ys, d1, d2, norm_final)
    return out.reshape(B, T, D)
```

```python
import functools

import numpy as np
import jax
import jax.numpy as jnp
from jax import lax
from jax.experimental import pallas as pl
from jax.experimental.pallas import tpu as pltpu

F32 = jnp.float32
BF16 = jnp.bfloat16

RMS_EPS = 1e-6
A_HEADS = 4
A_CONV = 4
B_HEADS = 16
B_KV_GROUPS = 4
B_REP = B_HEADS // B_KV_GROUPS
B_HEAD_DIM = 64
CMP_LEN = 32
CMP_STRIDE = 16
SEL_BLOCK = 64
SEL_TOPN = 16
WINDOW = 512
FORCED_SCORE = 1e4
N_EXPERTS = 8

LANES = 128
V7X_VMEM_BYTES = 64 * 1024 * 1024
VMEM_LIMIT = V7X_VMEM_BYTES - 8 * 1024 * 1024
NEG = -1e30


def _cparams(*sem):
    return pltpu.CompilerParams(dimension_semantics=sem, vmem_limit_bytes=VMEM_LIMIT)


def _dot(a, b):
    return jnp.dot(a, b, preferred_element_type=F32)


def _dot_nt(a, b):
    return lax.dot_general(a, b, (((1,), (1,)), ((), ())), preferred_element_type=F32)


def _dot_tn(a, b):
    return lax.dot_general(a, b, (((0,), (0,)), ((), ())), preferred_element_type=F32)


def _split3(a):
    a1 = a.astype(BF16)
    r1 = a - a1.astype(F32)
    a2 = r1.astype(BF16)
    a3 = (r1 - a2.astype(F32)).astype(BF16)
    return a1, a2, a3


def _hi_lo_weight(w):
    w = jnp.pad(w, ((0, 0), (0, LANES - w.shape[1])))
    hi = w.astype(BF16)
    lo = (w - hi.astype(F32)).astype(BF16)
    return jnp.concatenate([hi, lo], axis=1)


def _dot_hi_lo(a, w_hl):
    a1 = a.astype(BF16)
    a2 = (a - a1.astype(F32)).astype(BF16)
    s = _dot(a1, w_hl) + _dot(a2, w_hl)
    return s[:, :LANES] + s[:, LANES:]


def _rms(xf, g):
    return xf * lax.rsqrt(jnp.mean(xf * xf, axis=-1, keepdims=True) + RMS_EPS) * g


def _silu(x):
    return x * jax.nn.sigmoid(x)


def _log_sigmoid(x):
    return jnp.minimum(x, 0.0) - jnp.log(1.0 + jnp.exp(-jnp.abs(x)))


def _norm_matmul_kernel(*refs, side):
    if side:
        x_ref, g_ref, w_ref, ws_ref, o_ref, os_ref, xn_ref = refs
    else:
        x_ref, g_ref, w_ref, o_ref, xn_ref = refs

    @pl.when(pl.program_id(1) == 0)
    def _():
        xn = _rms(x_ref[...], g_ref[...])
        xn_ref[...] = xn.astype(xn_ref.dtype)
        if side:
            os_ref[...] = _dot_hi_lo(xn, ws_ref[...])

    o_ref[...] = _dot(xn_ref[...], w_ref[...]).astype(o_ref.dtype)


def norm_matmul(x, g, w, *, w_side=None, tm=1024, tn=1024):
    n, d = x.shape
    dout = w.shape[1]
    tm = min(tm, n)
    tn = min(tn, dout)
    side = w_side is not None
    in_specs = [pl.BlockSpec((tm, d), lambda i, j: (i, 0)),
                pl.BlockSpec((1, d), lambda i, j: (0, 0)),
                pl.BlockSpec((d, tn), lambda i, j: (0, j))]
    out_specs = [pl.BlockSpec((tm, tn), lambda i, j: (i, j))]
    out_shape = [jax.ShapeDtypeStruct((n, dout), BF16)]
    args = [x, g.reshape(1, d), w]
    if side:
        in_specs.append(pl.BlockSpec((d, 2 * LANES), lambda i, j: (0, 0)))
        out_specs.append(pl.BlockSpec((tm, LANES), lambda i, j: (i, 0)))
        out_shape.append(jax.ShapeDtypeStruct((n, LANES), F32))
        args.append(_hi_lo_weight(w_side))
    out = pl.pallas_call(
        functools.partial(_norm_matmul_kernel, side=side),
        grid=(n // tm, dout // tn),
        in_specs=in_specs,
        out_specs=out_specs,
        out_shape=out_shape,
        scratch_shapes=[pltpu.VMEM((tm, d), BF16)],
        compiler_params=_cparams("parallel", "arbitrary"),
        name="norm_matmul_side" if side else "norm_matmul",
    )(*args)
    return out if side else out[0]


MLSTM_CHUNK = 256


def _mlstm_kernel(qk_ref, v_ref, o_ref, gcol_ref, grow_ref, bcol_ref, brow_ref, convw_ref, gout_ref,
                  out_ref, ct_ref, n_ref, m_ref, prev_ref):
    L = qk_ref.shape[0]
    H = A_HEADS
    inner = v_ref.shape[1]
    dh = inner // H

    @pl.when(pl.program_id(1) == 0)
    def _():
        ct_ref[...] = jnp.zeros_like(ct_ref)
        n_ref[...] = jnp.zeros_like(n_ref)
        m_ref[...] = jnp.zeros_like(m_ref)
        prev_ref[...] = jnp.zeros_like(prev_ref)

    row = lax.broadcasted_iota(jnp.int32, (L, L), 0)
    col = lax.broadcasted_iota(jnp.int32, (L, L), 1)
    causal = col <= row
    tril = jnp.where(causal, 1.0, 0.0).astype(BF16)
    triu = jnp.where(row <= col, 1.0, 0.0).astype(BF16)

    gc = gcol_ref[...] + brow_ref[...]
    gr = grow_ref[...] + bcol_ref[...]
    lfc1, lfc2, lfc3 = _split3(_log_sigmoid(gc))
    lfr1, lfr2, lfr3 = _split3(_log_sigmoid(gr))
    b_c = _dot(tril, lfc1) + _dot(tril, lfc2) + _dot(tril, lfc3)
    b_r = _dot(lfr1, triu) + _dot(lfr2, triu) + _dot(lfr3, triu)

    rowi = lax.broadcasted_iota(jnp.int32, (L, dh), 0)

    def conv(cur, prev, w):
        y = cur * w[A_CONV - 1:A_CONV, :]
        for s in range(1, A_CONV):
            sh = jnp.where(rowi < s, pltpu.roll(prev, s, 0), pltpu.roll(cur, s, 0))
            y = y + sh * w[A_CONV - 1 - s:A_CONV - s, :]
        return y

    for h in range(H):
        hs = slice(h * dh, (h + 1) * dh)
        ks = slice(inner + h * dh, inner + (h + 1) * dh)
        q = conv(qk_ref[:, hs].astype(F32), prev_ref[:, hs].astype(F32), convw_ref[:, hs])
        k = conv(qk_ref[:, ks].astype(F32), prev_ref[:, ks].astype(F32), convw_ref[:, ks]) * (dh ** -0.5)
        v = v_ref[:, hs]
        qb = q.astype(BF16)
        kb = k.astype(BF16)

        li_c = gc[:, h:h + 1]
        bc = b_c[:, H + h:H + h + 1]
        li_r = gr[h:h + 1, :]
        br = b_r[H + h:H + h + 1, :]
        m_prev = m_ref[h:h + 1, 0:1]

        d = jnp.where(causal, bc - br + li_r, -jnp.inf)
        inter = bc + m_prev
        m_t = jnp.maximum(inter, jnp.max(d, axis=-1, keepdims=True))
        w_inter = jnp.exp(inter - m_t)
        s = _dot_nt(qb, kb) * jnp.exp(d - m_t)
        ct = ct_ref[h]
        num = _dot(s.astype(BF16), v) + w_inter * _dot(qb, ct.astype(BF16))
        den = jnp.sum(s, axis=-1, keepdims=True) + w_inter * jnp.sum(q * n_ref[h], axis=-1, keepdims=True)
        hh = num / jnp.maximum(jnp.abs(den), jnp.exp(-m_t))
        hh = hh * lax.rsqrt(jnp.mean(hh * hh, axis=-1, keepdims=True) + RMS_EPS)
        out_ref[:, hs] = (hh * gout_ref[:, hs] * jax.nn.sigmoid(o_ref[:, hs].astype(F32))).astype(out_ref.dtype)

        b_last = bc[L - 1:L, :]
        g = b_last - bc + li_c
        m_new = jnp.maximum(b_last + m_prev, jnp.max(g, axis=0, keepdims=True))
        a_prev = jnp.exp(b_last + m_prev - m_new)
        a_s = jnp.exp(g - m_new)
        ct_ref[h] = a_prev * ct + _dot_tn(kb, (v.astype(F32) * a_s).astype(BF16))
        n_ref[h] = a_prev * n_ref[h] + jnp.sum(k * a_s, axis=0, keepdims=True)
        m_ref[h:h + 1, :] = jnp.broadcast_to(m_new, (1, LANES))

    prev_ref[...] = qk_ref[...]


def mlstm_core(proj, gcol, b_gate, conv_w, g_out, batch, seq):
    n = proj.shape[0]
    inner = proj.shape[1] // 4
    H = A_HEADS
    dh = inner // H
    L = min(MLSTM_CHUNK, seq)
    nc = seq // L
    grow = gcol[:, :2 * H].T
    brow = jnp.pad(b_gate, (0, LANES - 2 * H)).reshape(1, LANES)
    bcol = b_gate.reshape(2 * H, 1)
    return pl.pallas_call(
        _mlstm_kernel,
        grid=(batch, nc),
        in_specs=[pl.BlockSpec((L, 2 * inner), lambda b, c: (b * nc + c, 0)),
                  pl.BlockSpec((L, inner), lambda b, c: (b * nc + c, 2)),
                  pl.BlockSpec((L, inner), lambda b, c: (b * nc + c, 3)),
                  pl.BlockSpec((L, LANES), lambda b, c: (b * nc + c, 0)),
                  pl.BlockSpec((2 * H, L), lambda b, c: (0, b * nc + c)),
                  pl.BlockSpec((2 * H, 1), lambda b, c: (0, 0)),
                  pl.BlockSpec((1, LANES), lambda b, c: (0, 0)),
                  pl.BlockSpec((A_CONV, 2 * inner), lambda b, c: (0, 0)),
                  pl.BlockSpec((1, inner), lambda b, c: (0, 0))],
        out_specs=pl.BlockSpec((L, inner), lambda b, c: (b * nc + c, 0)),
        out_shape=jax.ShapeDtypeStruct((n, inner), BF16),
        scratch_shapes=[pltpu.VMEM((H, dh, dh), F32),
                        pltpu.VMEM((H, 1, dh), F32),
                        pltpu.VMEM((8, LANES), F32),
                        pltpu.VMEM((L, 2 * inner), BF16)],
        compiler_params=_cparams("parallel", "arbitrary"),
        name="mlstm_core",
    )(proj, proj, proj, gcol, grow, bcol, brow, conv_w, g_out.reshape(1, inner))


def _ffn_kernel(x_ref, h_ref, wo_ref, g_ref, wg_ref, wu_ref, wd_ref, o_ref, xn_ref, acc_ref):
    j = pl.program_id(1)

    @pl.when(j == 0)
    def _():
        xf = x_ref[...] + _dot(h_ref[...], wo_ref[...])
        xn_ref[...] = _rms(xf, g_ref[...]).astype(BF16)
        acc_ref[...] = xf

    xn = xn_ref[...]
    a = (_silu(_dot(xn, wg_ref[...])) * _dot(xn, wu_ref[...])).astype(BF16)
    acc_ref[...] += _dot(a, wd_ref[...])

    @pl.when(j == pl.num_programs(1) - 1)
    def _():
        o_ref[...] = acc_ref[...]


def ffn_dense(x, h, w_out, g, w_gu, w_down, *, tm=512, tf=1408):
    n, d = x.shape
    f = w_down.shape[0]
    dh_in = h.shape[1]
    tm = min(tm, n)
    nf = f // tf
    return pl.pallas_call(
        _ffn_kernel,
        grid=(n // tm, nf),
        in_specs=[pl.BlockSpec((tm, d), lambda i, j: (i, 0)),
                  pl.BlockSpec((tm, dh_in), lambda i, j: (i, 0)),
                  pl.BlockSpec((dh_in, d), lambda i, j: (0, 0)),
                  pl.BlockSpec((1, d), lambda i, j: (0, 0)),
                  pl.BlockSpec((d, tf), lambda i, j: (0, j)),
                  pl.BlockSpec((d, tf), lambda i, j: (0, j + nf)),
                  pl.BlockSpec((tf, d), lambda i, j: (j, 0))],
        out_specs=pl.BlockSpec((tm, d), lambda i, j: (i, 0)),
        out_shape=jax.ShapeDtypeStruct((n, d), F32),
        scratch_shapes=[pltpu.VMEM((tm, d), BF16), pltpu.VMEM((tm, d), F32)],
        compiler_params=_cparams("parallel", "arbitrary"),
        name="ffn_dense",
    )(x, h, w_out, g.reshape(1, d), w_gu, w_gu, w_down)


def _compress_kernel(r_ref, pos_ref, w1_ref, w2_ref, o_ref):
    r = r_ref[0, 0]
    w1 = w1_ref[0]
    half = r.shape[1]
    nc = r.shape[0]
    lo = _dot(r, w1[:half])
    hi = _dot(r, w1[half:])
    hid = lo + pltpu.roll(hi, nc - 1, 0) + _dot(pos_ref[0], w1)
    o_ref[0, 0] = _dot(_silu(hid).astype(BF16), w2_ref[0]).astype(o_ref.dtype)


def nsa_compress(r, pos, w1, w2):
    b, c2, nc, half = r.shape
    g = c2 // 2
    hidden = w1.shape[2]
    dh = w2.shape[2]
    return pl.pallas_call(
        _compress_kernel,
        grid=(b, c2),
        in_specs=[pl.BlockSpec((1, 1, nc, half), lambda i, j: (i, j, 0, 0)),
                  pl.BlockSpec((1, 1, 2 * half), lambda i, j: (j // g, 0, 0)),
                  pl.BlockSpec((1, 2 * half, hidden), lambda i, j: (j // g, 0, 0)),
                  pl.BlockSpec((1, hidden, dh), lambda i, j: (j // g, 0, 0))],
        out_specs=pl.BlockSpec((1, 1, nc, dh), lambda i, j: (i, j, 0, 0)),
        out_shape=jax.ShapeDtypeStruct((b, c2, nc, dh), F32),
        compiler_params=_cparams("parallel", "parallel"),
        name="nsa_compress",
    )(r, pos, w1, w2)


NSA_TQ = 256
NSA_TK_SLC = 512
NSA_TK_WIN = 256
NSA_ROWS = 32
SEL_MASK = 2.0 ** 14
LOG2E = float(np.log2(np.e))


def _nsa_kernel(q_ref, qf_ref, kc_ref, vc_ref, ks_ref, vs_ref, kw_ref, vw_ref, cs_ref, cw_ref, one_ref,
                gl_ref, bg_ref, ov_ref,
                out_ref, qx_ref, ksx_ref, vsx_ref, kwx_ref, vwx_ref,
                ms_ref, accs_ref, biass_ref, s0_ref, p0_ref, a0_ref, s1_ref, p1_ref, a1_ref,
                mw_ref, accw_ref, biasw_ref, s2_ref, p2_ref, a2_ref, s3_ref, p3_ref, a3_ref, live_ref,
                *, tq, tks, tkw, n_top):
    R = B_REP
    M = R * tq
    dh = B_HEAD_DIM
    half = LANES // 2
    t0 = pl.program_id(2) * tq
    slc_state = (ms_ref, accs_ref, biass_ref, ((s0_ref, p0_ref, a0_ref), (s1_ref, p1_ref, a1_ref)))
    win_state = (mw_ref, accw_ref, biasw_ref, ((s2_ref, p2_ref, a2_ref), (s3_ref, p3_ref, a3_ref)))

    @pl.when(pl.program_id(2) == 0)
    def _():
        odd = pl.program_id(1) % 2 == 1
        low_lane = lax.broadcasted_iota(jnp.int32, (ks_ref.shape[1], LANES), 1) < half

        def halves(ref):
            x = ref[0].astype(F32)
            xr = pltpu.roll(x, half, 1)
            return jnp.where(odd, xr, x), jnp.where(odd, x, xr)

        lo, hi = halves(ks_ref)
        ksx_ref[:, :LANES] = jnp.where(low_lane, lo, hi).astype(BF16)
        ksx_ref[:, LANES:] = cs_ref[...]
        lo, hi = halves(kw_ref)
        kwx_ref[:, :LANES] = jnp.where(low_lane, lo, hi).astype(BF16)
        kwx_ref[:, LANES:] = cw_ref[...]
        ones = one_ref[...].astype(F32)
        vsx_ref[...] = jnp.where(low_lane, halves(vs_ref)[0], ones).astype(BF16)
        vwx_ref[...] = jnp.where(low_lane, halves(vw_ref)[0], ones).astype(BF16)

    qb = q_ref[0]
    low_q = lax.broadcasted_iota(jnp.int32, (tq, LANES), 1) < half
    for r in range(R):
        tile = qb[:, (r // 2) * LANES:(r // 2 + 1) * LANES]
        own = (tile, jnp.zeros_like(tile)) if r % 2 == 0 else (jnp.zeros_like(tile), tile)
        qx_ref[r * tq:(r + 1) * tq, :LANES] = jnp.where(low_q, *own)
        qx_ref[r * tq:(r + 1) * tq, LANES:] = jnp.broadcast_to(qf_ref[0, r], (tq, LANES))
    q = qx_ref[...]

    kc = kc_ref[0, 0]
    ncmp = kc.shape[0]
    s = _dot_nt(q, kc)
    tpos_c = t0 + (lax.broadcasted_iota(jnp.int32, (M, ncmp), 0) & (tq - 1))
    cend = lax.broadcasted_iota(jnp.int32, (M, ncmp), 1) * CMP_STRIDE + (CMP_LEN - 1)
    ok_c = tpos_c >= cend
    s = jnp.where(ok_c, s, NEG)
    p = jnp.where(ok_c, jnp.exp2(s - jnp.max(s, axis=-1, keepdims=True)), 0.0)
    dsum = jnp.sum(p, axis=-1, keepdims=True)
    p = p / jnp.where(dsum > 0, dsum, 1.0)
    o_cmp = _dot(p.astype(BF16), vc_ref[0, 0])

    psum = p[0:tq]
    for r in range(1, R):
        psum = psum + p[r * tq:(r + 1) * tq]
    p1, p2, p3 = _split3(psum)
    ov = ov_ref[...]
    imp = _dot(p1, ov) + _dot(p2, ov) + _dot(p3, ov)
    nsel = ks_ref.shape[1] // SEL_BLOCK
    nselp = max(nsel, 8)
    imp_t = imp.T[:nselp]
    jj = lax.broadcasted_iota(jnp.int32, (nselp, tq), 0)
    tt = t0 + lax.broadcasted_iota(jnp.int32, (nselp, tq), 1)
    cur = tt // SEL_BLOCK
    forced = (jj == 0) | (jj == cur) | (jj == cur - 1)
    iv = jnp.where(forced, FORCED_SCORE, imp_t)
    iv = jnp.where(jj * SEL_BLOCK <= tt, iv, -jnp.inf)
    rank = jnp.zeros((nselp, tq), F32)
    for j2 in range(nsel):
        rv = iv[j2:j2 + 1, :]
        rank = rank + jnp.where(jj > j2, jnp.where(rv >= iv, 1.0, 0.0), jnp.where(rv > iv, 1.0, 0.0))
    unsel = jnp.where(rank < n_top, 0.0, -SEL_MASK)
    if nselp < LANES:
        unsel = jnp.concatenate([unsel, jnp.zeros((LANES - nselp, tq), F32)], axis=0)
    qmask = unsel.T.astype(BF16)
    for r in range(R):
        qx_ref[r * tq:(r + 1) * tq, LANES:] = qmask + qf_ref[0, r]

    def tile_step(qq, k_ref, v_ref, k0, tk, state, slot, bias):
        m_ref, acc_ref, bias_ref, slots = state
        s_ref, p_ref, a_ref = slots[slot]
        kk = k_ref[pl.ds(k0, tk), :]
        vv = v_ref[pl.ds(k0, tk), :]
        if bias is not None:
            bias_ref[:, :tk] = bias
        s_ref[:, :tk] = _dot_nt(qq, kk)
        for r in range(R):
            for c0 in range(0, tq, NSA_ROWS):
                rows = slice(r * tq + c0, r * tq + c0 + NSA_ROWS)
                sc = s_ref[rows, :tk]
                if bias is not None:
                    sc = sc + bias_ref[c0:c0 + NSA_ROWS, :tk]
                m_old = m_ref[rows, :]
                m_new = jnp.maximum(m_old, jnp.max(sc, axis=-1, keepdims=True))
                a_ref[rows, :] = jnp.exp2(m_old - m_new)
                m_ref[rows, :] = m_new
                p_ref[rows, :tk] = jnp.exp2(sc - jnp.concatenate([m_new] * (tk // LANES), axis=1)).astype(BF16)
        acc_ref[...] = a_ref[...] * acc_ref[...] + _dot(p_ref[:, :tk], vv)

    def reset(state):
        state[0][...] = jnp.full_like(state[0], NEG)
        state[1][...] = jnp.zeros_like(state[1])

    def result(state):
        acc = state[1][...]
        return acc[:, :dh] / acc[:, dh:dh + 1]

    reset(slc_state)
    qx = qx_ref[...]
    n_full = t0 // tks

    blocks_per_tile = tks // SEL_BLOCK
    n_live = jnp.int32(0)
    for j in range(nsel // blocks_per_tile):
        sel_j = rank[j * blocks_per_tile:(j + 1) * blocks_per_tile, :] < n_top
        live = (jnp.max(jnp.where(sel_j, 1.0, 0.0)) > 0.5) & (j < n_full)
        live_ref[n_live] = j
        n_live = n_live + live.astype(jnp.int32)

    def slc_pair(i, carry):
        tile_step(qx, ksx_ref, vsx_ref, pl.multiple_of(live_ref[2 * i] * tks, tks), tks, slc_state, 0, None)
        tile_step(qx, ksx_ref, vsx_ref, pl.multiple_of(live_ref[2 * i + 1] * tks, tks), tks, slc_state, 1, None)
        return carry

    lax.fori_loop(0, n_live // 2, slc_pair, 0)

    @pl.when(n_live % 2 == 1)
    def _():
        tile_step(qx, ksx_ref, vsx_ref, pl.multiple_of(live_ref[n_live - 1] * tks, tks), tks, slc_state, 0, None)

    kd = pl.multiple_of(n_full * tks, tks)
    spos = kd + lax.broadcasted_iota(jnp.int32, (tq, tks), 1)
    tpos = t0 + lax.broadcasted_iota(jnp.int32, (tq, tks), 0)
    tile_step(qx, ksx_ref, vsx_ref, kd, tks, slc_state, 1, jnp.where(spos <= tpos, 0.0, NEG))
    reset(win_state)
    n_win = WINDOW // tkw + max(tq // tkw, 1)
    last = (t0 + tq - 1) // tkw
    for i in range(n_win):
        kw0 = (last - (n_win - 1) + i) * tkw
        spos = kw0 + lax.broadcasted_iota(jnp.int32, (tq, tkw), 1)
        dist = t0 + lax.broadcasted_iota(jnp.int32, (tq, tkw), 0) - spos
        bias = jnp.where((spos >= 0) & (dist >= 0) & (dist < WINDOW), 0.0, NEG)
        tile_step(q, kwx_ref, vwx_ref, pl.multiple_of(jnp.maximum(kw0, 0), tkw), tkw, win_state, i % 2, bias)
    o_slc = result(slc_state)
    o_win = result(win_state)

    gates = jax.nn.sigmoid(gl_ref[0, 0, pl.ds(pl.multiple_of(t0, tq), tq), :] + bg_ref[0])
    outs = []
    for r in range(R):
        rs = slice(r * tq, (r + 1) * tq)
        outs.append(gates[:, 3 * r:3 * r + 1] * o_cmp[rs] + gates[:, 3 * r + 1:3 * r + 2] * o_slc[rs]
                    + gates[:, 3 * r + 2:3 * r + 3] * o_win[rs])
    out_ref[0] = jnp.concatenate(outs, axis=1).astype(out_ref.dtype)


def nsa_attention(qp, q_feat, kc, vc, kvx, kv_off, slc_const, win_const, ones_col, gl, bg, ov):
    b, t, _ = qp.shape
    g, r = q_feat.shape[:2]
    dh = B_HEAD_DIM
    tq = min(NSA_TQ, t)
    tks = min(NSA_TK_SLC, t)
    tkw = min(NSA_TK_WIN, t)
    ncmp = kc.shape[2]
    n_top = min(SEL_TOPN, t // SEL_BLOCK)
    per_tile = LANES // dh
    assert t // SEL_BLOCK <= LANES // 2 and per_tile == 2 and g % per_tile == 0
    kv_spec = lambda c: pl.BlockSpec((1, t, LANES),
                                     lambda i, j, k: (i, 0, kv_off + c * (g // per_tile) + j // per_tile))
    const_spec = lambda a: pl.BlockSpec(a.shape, lambda i, j, k: (0,) * a.ndim)

    def branch_scratch(tk):
        slot = [pltpu.VMEM((r * tq, tk), F32), pltpu.VMEM((r * tq, tk), BF16), pltpu.VMEM((r * tq, LANES), F32)]
        return [pltpu.VMEM((r * tq, LANES), F32), pltpu.VMEM((r * tq, LANES), F32),
                pltpu.VMEM((tq, tk), F32)] + slot + slot

    return pl.pallas_call(
        functools.partial(_nsa_kernel, tq=tq, tks=tks, tkw=tkw, n_top=n_top),
        grid=(b, g, t // tq),
        in_specs=[pl.BlockSpec((1, tq, r * dh), lambda i, j, k: (i, k, j)),
                  pl.BlockSpec((1, r, 1, LANES), lambda i, j, k: (j, 0, 0, 0)),
                  pl.BlockSpec((1, 1, ncmp, 2 * LANES), lambda i, j, k: (i, j, 0, 0)),
                  pl.BlockSpec((1, 1, ncmp, dh), lambda i, j, k: (i, j, 0, 0)),
                  kv_spec(0), kv_spec(1), kv_spec(2), kv_spec(3),
                  const_spec(slc_const), const_spec(win_const), const_spec(ones_col),
                  pl.BlockSpec((1, 1, t, 3 * r), lambda i, j, k: (i, j, 0, 0)),
                  pl.BlockSpec((1, 1, 3 * r), lambda i, j, k: (j, 0, 0)),
                  const_spec(ov)],
        out_specs=pl.BlockSpec((1, tq, r * dh), lambda i, j, k: (i, k, j)),
        out_shape=jax.ShapeDtypeStruct((b, t, g * r * dh), BF16),
        scratch_shapes=[pltpu.VMEM((r * tq, 2 * LANES), BF16),
                        pltpu.VMEM((t, 2 * LANES), BF16),
                        pltpu.VMEM((t, LANES), BF16),
                        pltpu.VMEM((t, 2 * LANES), BF16),
                        pltpu.VMEM((t, LANES), BF16),
                        ] + branch_scratch(tks) + branch_scratch(tkw)
                       + [pltpu.SMEM((t // tks + 1,), jnp.int32)],
        compiler_params=_cparams("arbitrary", "arbitrary", "arbitrary"),
        name="nsa_attention",
    )(qp, q_feat, kc, vc, kvx, kvx, kvx, kvx, slc_const, win_const, ones_col, gl, bg, ov)


def _alibi_slopes(n):
    return np.power(2.0, -8.0 * np.arange(1, n + 1) / n).astype(np.float32)


def _np_split3(a):
    a = np.asarray(a, np.float32)
    out = []
    r = a
    for _ in range(3):
        p = r.astype(BF16).astype(np.float32)
        out.append(p)
        r = (r - p).astype(np.float32)
    return out


def _pos_features(pos, width):
    hi = (pos // 64).astype(np.float32)
    lo = (pos % 64).astype(np.float32)
    f = np.zeros((pos.shape[0], width), np.float32)
    f[:, 0:3] = hi[:, None]
    f[:, 3:6] = lo[:, None]
    return f


def _slope_features(width):
    s1, s2, s3 = _np_split3((_alibi_slopes(B_HEADS).astype(np.float64) * LOG2E).astype(np.float32))
    f = np.zeros((B_HEADS, width), np.float32)
    for i, s in enumerate((s1, s2, s3)):
        f[:, i] = 64.0 * s
        f[:, 3 + i] = s
    return f


META_I1, META_I2, META_W1, META_W2 = 0, 1, 2, 3


def _router_kernel(x_ref, h_ref, wo_ref, g_ref, w_ref, xo_ref, xn_ref, meta_ref, pos_ref, cnt_ref, run_ref):
    @pl.when(pl.program_id(0) == 0)
    def _():
        run_ref[...] = jnp.zeros_like(run_ref)

    xf = x_ref[...] + _dot(h_ref[...], wo_ref[...])
    xo_ref[...] = xf
    xn = _rms(xf, g_ref[...])
    xn_ref[...] = xn
    logits = _dot_hi_lo(xn, w_ref[...])
    tm = logits.shape[0]
    lane = lax.broadcasted_iota(jnp.int32, logits.shape, 1)
    logits = jnp.where(lane < N_EXPERTS, logits, -jnp.inf)
    m1 = jnp.max(logits, axis=-1, keepdims=True)
    i1 = jnp.min(jnp.where(logits == m1, lane, LANES), axis=-1, keepdims=True)
    rest = jnp.where(lane == i1, -jnp.inf, logits)
    m2 = jnp.max(rest, axis=-1, keepdims=True)
    i2 = jnp.min(jnp.where(rest == m2, lane, LANES), axis=-1, keepdims=True)
    e2 = jnp.exp(m2 - m1)
    w1 = 1.0 / (1.0 + e2)
    meta = jnp.where(lane == META_I1, i1.astype(F32), 0.0)
    meta = jnp.where(lane == META_I2, i2.astype(F32), meta)
    meta = jnp.where(lane == META_W1, w1, meta)
    meta_ref[...] = jnp.where(lane == META_W2, e2 * w1, meta)

    sel = jnp.where((lane == i1) | (lane == i2), 1.0, 0.0)
    row = lax.broadcasted_iota(jnp.int32, (tm, tm), 0)
    col = lax.broadcasted_iota(jnp.int32, (tm, tm), 1)
    before = jnp.where(col < row, 1.0, 0.0).astype(BF16)
    run = run_ref[...]
    pos_ref[...] = run + _dot(before, sel.astype(BF16))
    run = run + jnp.sum(sel, axis=0, keepdims=True)
    run_ref[...] = run
    cnt_ref[...] = run


def moe_router(x, h, w_out, g, w_router, *, tm=512):
    n, d = x.shape
    dh_in = h.shape[1]
    tm = min(tm, n)
    w = _hi_lo_weight(w_router)
    row_spec = pl.BlockSpec((tm, LANES), lambda i: (i, 0))
    return pl.pallas_call(
        _router_kernel,
        grid=(n // tm,),
        in_specs=[pl.BlockSpec((tm, d), lambda i: (i, 0)),
                  pl.BlockSpec((tm, dh_in), lambda i: (i, 0)),
                  pl.BlockSpec((dh_in, d), lambda i: (0, 0)),
                  pl.BlockSpec((1, d), lambda i: (0, 0)),
                  pl.BlockSpec((d, 2 * LANES), lambda i: (0, 0))],
        out_specs=[pl.BlockSpec((tm, d), lambda i: (i, 0)), pl.BlockSpec((tm, d), lambda i: (i, 0)),
                   row_spec, row_spec,
                   pl.BlockSpec((1, LANES), lambda i: (0, 0))],
        out_shape=[jax.ShapeDtypeStruct((n, d), F32), jax.ShapeDtypeStruct((n, d), F32),
                   jax.ShapeDtypeStruct((n, LANES), F32),
                   jax.ShapeDtypeStruct((n, LANES), F32), jax.ShapeDtypeStruct((1, LANES), F32)],
        scratch_shapes=[pltpu.VMEM((1, LANES), F32)],
        compiler_params=_cparams("arbitrary"),
        name="moe_router",
    )(x, h, w_out, g.reshape(1, d), w)


MOE_TILE = 512


def _moe_kernel(te_ref, ok_ref, idx0_ref, idxn_ref, x_hbm, wg_ref, wu_ref, wd_ref, o_ref, xbuf, xb_ref, sem):
    i = pl.program_id(0)
    j = pl.program_id(1)
    nt = pl.num_programs(0)
    nf = pl.num_programs(1)
    tm = xb_ref.shape[0]
    per_step = tm // nf
    slot = i % 2

    def row_copy(idx_ref, r, s):
        return pltpu.make_async_copy(x_hbm.at[pl.ds(idx_ref[0, 0, r], 1), :], xbuf.at[s, pl.ds(r, 1), :], sem.at[s])

    @pl.when((i == 0) & (j == 0))
    def _():
        def start0(r, c):
            row_copy(idx0_ref, r, 0).start()
            return c
        lax.fori_loop(0, tm, start0, 0)

    @pl.when(j == 0)
    def _():
        for r in range(tm):
            row_copy(idx0_ref, r, slot).wait()
        xb_ref[...] = xbuf[slot].astype(BF16)
        o_ref[...] = jnp.zeros_like(o_ref)

    def prefetch():
        base = j * per_step
        for r in range(per_step):
            row_copy(idxn_ref, base + r, 1 - slot).start()

    def compute():
        xb = xb_ref[...]
        a = (_silu(_dot(xb, wg_ref[0])) * _dot(xb, wu_ref[0])).astype(BF16)
        o_ref[...] += _dot(a, wd_ref[0])

    has_next = i + 1 < nt
    ok = ok_ref[i] > 0

    @pl.when(has_next & ok)
    def _():
        prefetch()
        compute()

    @pl.when(has_next & jnp.logical_not(ok))
    def _():
        prefetch()

    @pl.when(jnp.logical_not(has_next) & ok)
    def _():
        compute()


def moe_experts(x, row_tok, tile_expert, tile_ok, w_gu, w_down, *, tf=1792):
    p = row_tok.shape[0]
    d = x.shape[1]
    ne, f, _ = w_down.shape
    tm = MOE_TILE
    nf = f // tf
    nt = p // tm
    idx = row_tok.reshape(nt, 1, tm)
    grid_spec = pltpu.PrefetchScalarGridSpec(
        num_scalar_prefetch=2,
        grid=(nt, nf),
        in_specs=[pl.BlockSpec((1, 1, tm), lambda i, j, te, ok: (0, 0, 0), memory_space=pltpu.SMEM),
                  pl.BlockSpec((1, 1, tm), lambda i, j, te, ok: (jnp.minimum(i + 1, nt - 1), 0, 0),
                               memory_space=pltpu.SMEM),
                  pl.BlockSpec(memory_space=pl.ANY),
                  pl.BlockSpec((1, d, tf), lambda i, j, te, ok: (te[i], 0, j)),
                  pl.BlockSpec((1, d, tf), lambda i, j, te, ok: (te[i], 0, j + nf)),
                  pl.BlockSpec((1, tf, d), lambda i, j, te, ok: (te[i], j, 0))],
        out_specs=pl.BlockSpec((tm, d), lambda i, j, te, ok: (i, 0)),
        scratch_shapes=[pltpu.VMEM((2, tm, d), F32), pltpu.VMEM((tm, d), BF16),
                        pltpu.SemaphoreType.DMA((2,))])
    return pl.pallas_call(
        _moe_kernel,
        grid_spec=grid_spec,
        out_shape=jax.ShapeDtypeStruct((p, d), F32),
        compiler_params=_cparams("arbitrary", "arbitrary"),
        name="moe_experts",
    )(tile_expert, tile_ok, idx, idx, x, w_gu, w_gu, w_down)


def _combine_kernel(d0_ref, dn_ref, x_ref, meta_ref, y_ref, gf_ref, o_ref, ybuf, sem):
    i = pl.program_id(0)
    nt = pl.num_programs(0)
    rows = o_ref.shape[0]
    slot = i % 2

    def row_copy(d_ref, k, r, s):
        return pltpu.make_async_copy(y_ref.at[pl.ds(d_ref[0, k, r], 1), :], ybuf.at[s, k, pl.ds(r, 1), :], sem.at[s])

    @pl.when(i == 0)
    def _():
        def start0(r, c):
            row_copy(d0_ref, 0, r, 0).start()
            row_copy(d0_ref, 1, r, 0).start()
            return c
        lax.fori_loop(0, rows, start0, 0)

    for r in range(rows):
        row_copy(d0_ref, 0, r, slot).wait()
        row_copy(d0_ref, 1, r, slot).wait()

    @pl.when(i + 1 < nt)
    def _():
        for r in range(rows):
            row_copy(dn_ref, 0, r, 1 - slot).start()
            row_copy(dn_ref, 1, r, 1 - slot).start()

    meta = meta_ref[...]
    w1 = meta[:, META_W1:META_W1 + 1]
    w2 = meta[:, META_W2:META_W2 + 1]
    o_ref[...] = _rms(x_ref[...] + w1 * ybuf[slot, 0] + w2 * ybuf[slot, 1], gf_ref[...])


def moe_combine(x, meta, ys, d1, d2, g_final, *, tc=256):
    n, d = x.shape
    tc = min(tc, n)
    nt = n // tc
    dd = jnp.stack([d1.reshape(nt, tc), d2.reshape(nt, tc)], axis=1)
    return pl.pallas_call(
        _combine_kernel,
        grid=(nt,),
        in_specs=[pl.BlockSpec((1, 2, tc), lambda i: (0, 0, 0), memory_space=pltpu.SMEM),
                  pl.BlockSpec((1, 2, tc), lambda i: (jnp.minimum(i + 1, nt - 1), 0, 0), memory_space=pltpu.SMEM),
                  pl.BlockSpec((tc, d), lambda i: (i, 0)),
                  pl.BlockSpec((tc, LANES), lambda i: (i, 0)),
                  pl.BlockSpec(memory_space=pl.ANY),
                  pl.BlockSpec((1, d), lambda i: (0, 0))],
        out_specs=pl.BlockSpec((tc, d), lambda i: (i, 0)),
        out_shape=jax.ShapeDtypeStruct((n, d), F32),
        scratch_shapes=[pltpu.VMEM((2, 2, tc, d), F32), pltpu.SemaphoreType.DMA((2,))],
        compiler_params=_cparams("arbitrary"),
        name="moe_combine",
    )(dd, dd, x, meta, ys, g_final.reshape(1, d))


def kernel(x, norm_mix, norm_ffn, a_w_in, a_b_gate, a_conv, a_norm_h, a_w_out, norm_kv, b_w_kv,
           b_cmp_pos, b_cmp_w1, b_cmp_w2, b_w_q, b_b_gate, b_w_out, f_w_gu, f_w_down,
           m_router, m_w_gu, m_w_down, norm_final):
    B, T, D = x.shape
    N = B * T
    G, R, dh = B_KV_GROUPS, B_REP, B_HEAD_DIM
    xs = x.reshape(N, D)

    inner4 = a_w_in.shape[2] - 2 * A_HEADS
    w_in = a_w_in[0]
    proj, gcol = norm_matmul(xs, norm_mix[0], w_in[:, :inner4].astype(BF16), w_side=w_in[:, inner4:])
    hs = mlstm_core(proj, gcol, a_b_gate[0], a_conv[0], a_norm_h[0], B, T)
    xs = ffn_dense(xs, hs, a_w_out[0].astype(BF16), norm_ffn[0], f_w_gu[0].astype(BF16), f_w_down[0].astype(BF16))

    hd = B_HEADS * dh
    feat_w = LANES - dh
    ncmp = T // CMP_STRIDE

    cmp_cols = 2 * G * dh
    kvx = norm_matmul(xs, norm_kv, b_w_kv.astype(BF16), tn=b_w_kv.shape[1] // 2)
    w_q = b_w_q[0]
    qp, gl = norm_matmul(xs, norm_mix[1], (w_q[:, :hd] * (dh ** -0.5 * LOG2E)).astype(BF16), w_side=w_q[:, hd:])

    kvt = kvx[:, :cmp_cols].reshape(B, T, 2 * G, dh).transpose(0, 2, 1, 3)
    pos = b_cmp_pos.transpose(1, 0, 2).reshape(2, 1, CMP_LEN * dh).astype(BF16)
    kvc = nsa_compress(kvt.reshape(B, 2 * G, ncmp, CMP_STRIDE * dh), pos,
                       b_cmp_w1.astype(BF16), b_cmp_w2.astype(BF16))

    def hi_lanes(f, lo=None):
        lo = np.zeros((f.shape[0], dh), np.float32) if lo is None else lo
        return jnp.asarray(np.concatenate([lo, f], axis=1), BF16)

    key_pos = _pos_features(np.arange(T), feat_w)
    blk_onehot = (np.arange(T)[:, None] // SEL_BLOCK == np.arange(dh)[None, :]).astype(np.float32)
    slc_const = hi_lanes(key_pos, blk_onehot)
    win_const = hi_lanes(key_pos)
    cmp_const = hi_lanes(_pos_features(np.arange(ncmp) * CMP_STRIDE + CMP_LEN - 1, feat_w))
    q_feat = hi_lanes(_slope_features(feat_w)).reshape(G, R, 1, LANES)
    ones_col = np.zeros((1, feat_w), np.float32)
    ones_col[0, 0] = 1.0
    ones_col = hi_lanes(ones_col)
    kcb = kvc[:, :G].astype(BF16)
    kc = jnp.concatenate([kcb, kcb, jnp.broadcast_to(cmp_const, (B, G, ncmp, LANES))], axis=-1)
    vc = kvc[:, G:].astype(BF16)
    glt = gl[:, :3 * B_HEADS].reshape(B, T, G, 3 * R).transpose(0, 2, 1, 3)
    bg = b_b_gate[0].reshape(G, 1, 3 * R)

    nsel = T // SEL_BLOCK
    ci = np.arange(ncmp)[:, None] * CMP_STRIDE
    sj = np.arange(LANES)[None, :] * SEL_BLOCK
    ov = ((ci < sj + SEL_BLOCK) & (ci + CMP_LEN > sj) & (np.arange(LANES)[None, :] < nsel)
          & (np.arange(ncmp)[:, None] < ncmp - 1))
    ov = jnp.asarray(ov.astype(np.float32), BF16)

    oa = nsa_attention(qp.reshape(B, T, hd), q_feat, kc, vc, kvx.reshape(B, T, -1),
                       cmp_cols // LANES, slc_const, win_const, ones_col, glt, bg, ov)

    xs, xn, meta, pos, cnt = moe_router(xs, oa.reshape(N, hd), b_w_out[0].astype(BF16), norm_ffn[1], m_router[0])
    ne = N_EXPERTS
    p_rows = 2 * N + ne * MOE_TILE
    i1 = meta[:, META_I1].astype(jnp.int32)
    i2 = meta[:, META_I2].astype(jnp.int32)
    counts = cnt[0, :ne].astype(jnp.int32)
    padded = (counts + MOE_TILE - 1) // MOE_TILE * MOE_TILE
    seg_end = jnp.cumsum(padded)
    seg_start = seg_end - padded
    pos8 = pos[:, :ne].astype(jnp.int32)
    d1 = seg_start[i1] + jnp.take_along_axis(pos8, i1[:, None], axis=1)[:, 0]
    d2 = seg_start[i2] + jnp.take_along_axis(pos8, i2[:, None], axis=1)[:, 0]
    tok = jnp.arange(N, dtype=jnp.int32)
    row_tok = jnp.zeros((p_rows,), jnp.int32).at[jnp.concatenate([d1, d2])].set(jnp.concatenate([tok, tok]))
    tile_start = jnp.arange(p_rows // MOE_TILE, dtype=jnp.int32) * MOE_TILE
    tile_expert = jnp.minimum(jnp.sum(tile_start[:, None] >= seg_end[None, :], axis=1), ne - 1).astype(jnp.int32)
    tile_ok = (tile_start < seg_end[ne - 1]).astype(jnp.int32)

    ys = moe_experts(xn, row_tok, tile_expert, tile_ok, m_w_gu[0].astype(BF16), m_w_down[0].astype(BF16))
    out = moe_combine(xs, meta, ys, d1, d2, norm_final)
    return out.reshape(B, T, D)
```

```python
import functools

import numpy as np
import jax
import jax.numpy as jnp
from jax import lax
from jax.experimental import pallas as pl
from jax.experimental.pallas import tpu as pltpu

F32 = jnp.float32
BF16 = jnp.bfloat16

RMS_EPS = 1e-6
A_HEADS = 4
A_CONV = 4
B_HEADS = 16
B_KV_GROUPS = 4
B_REP = B_HEADS // B_KV_GROUPS
B_HEAD_DIM = 64
CMP_LEN = 32
CMP_STRIDE = 16
SEL_BLOCK = 64
SEL_TOPN = 16
WINDOW = 512
FORCED_SCORE = 1e4
N_EXPERTS = 8

LANES = 128
V7X_VMEM_BYTES = 64 * 1024 * 1024
VMEM_LIMIT = V7X_VMEM_BYTES - 8 * 1024 * 1024
NEG = -1e30


def _cparams(*sem):
    return pltpu.CompilerParams(dimension_semantics=sem, vmem_limit_bytes=VMEM_LIMIT)


def _dot(a, b):
    return jnp.dot(a, b, preferred_element_type=F32)


def _dot_nt(a, b):
    return lax.dot_general(a, b, (((1,), (1,)), ((), ())), preferred_element_type=F32)


def _dot_tn(a, b):
    return lax.dot_general(a, b, (((0,), (0,)), ((), ())), preferred_element_type=F32)


def _split3(a):
    a1 = a.astype(BF16)
    r1 = a - a1.astype(F32)
    a2 = r1.astype(BF16)
    a3 = (r1 - a2.astype(F32)).astype(BF16)
    return a1, a2, a3


def _hi_lo_weight(w):
    w = jnp.pad(w, ((0, 0), (0, LANES - w.shape[1])))
    hi = w.astype(BF16)
    lo = (w - hi.astype(F32)).astype(BF16)
    return jnp.concatenate([hi, lo], axis=1)


def _dot_hi_lo(a, w_hl):
    a1 = a.astype(BF16)
    a2 = (a - a1.astype(F32)).astype(BF16)
    s = _dot(a1, w_hl) + _dot(a2, w_hl)
    return s[:, :LANES] + s[:, LANES:]


def _rms(xf, g):
    return xf * lax.rsqrt(jnp.mean(xf * xf, axis=-1, keepdims=True) + RMS_EPS) * g


def _silu(x):
    return x * jax.nn.sigmoid(x)


def _log_sigmoid(x):
    return jnp.minimum(x, 0.0) - jnp.log(1.0 + jnp.exp(-jnp.abs(x)))


def _norm_matmul_kernel(*refs, side):
    if side:
        x_ref, g_ref, w_ref, ws_ref, o_ref, os_ref, xn_ref = refs
    else:
        x_ref, g_ref, w_ref, o_ref, xn_ref = refs

    @pl.when(pl.program_id(1) == 0)
    def _():
        xn = _rms(x_ref[...], g_ref[...])
        xn_ref[...] = xn.astype(xn_ref.dtype)
        if side:
            os_ref[...] = _dot_hi_lo(xn, ws_ref[...])

    o_ref[...] = _dot(xn_ref[...], w_ref[...]).astype(o_ref.dtype)


def norm_matmul(x, g, w, *, w_side=None, tm=1024, tn=1024):
    n, d = x.shape
    dout = w.shape[1]
    tm = min(tm, n)
    tn = min(tn, dout)
    side = w_side is not None
    in_specs = [pl.BlockSpec((tm, d), lambda i, j: (i, 0)),
                pl.BlockSpec((1, d), lambda i, j: (0, 0)),
                pl.BlockSpec((d, tn), lambda i, j: (0, j))]
    out_specs = [pl.BlockSpec((tm, tn), lambda i, j: (i, j))]
    out_shape = [jax.ShapeDtypeStruct((n, dout), BF16)]
    args = [x, g.reshape(1, d), w]
    if side:
        in_specs.append(pl.BlockSpec((d, 2 * LANES), lambda i, j: (0, 0)))
        out_specs.append(pl.BlockSpec((tm, LANES), lambda i, j: (i, 0)))
        out_shape.append(jax.ShapeDtypeStruct((n, LANES), F32))
        args.append(_hi_lo_weight(w_side))
    out = pl.pallas_call(
        functools.partial(_norm_matmul_kernel, side=side),
        grid=(n // tm, dout // tn),
        in_specs=in_specs,
        out_specs=out_specs,
        out_shape=out_shape,
        scratch_shapes=[pltpu.VMEM((tm, d), BF16)],
        compiler_params=_cparams("parallel", "arbitrary"),
        name="norm_matmul_side" if side else "norm_matmul",
    )(*args)
    return out if side else out[0]


MLSTM_CHUNK = 256


def _mlstm_kernel(qk_ref, v_ref, o_ref, gcol_ref, grow_ref, bcol_ref, brow_ref, convw_ref, gout_ref,
                  out_ref, ct_ref, n_ref, m_ref, prev_ref):
    L = qk_ref.shape[0]
    H = A_HEADS
    inner = v_ref.shape[1]
    dh = inner // H

    @pl.when(pl.program_id(1) == 0)
    def _():
        ct_ref[...] = jnp.zeros_like(ct_ref)
        n_ref[...] = jnp.zeros_like(n_ref)
        m_ref[...] = jnp.zeros_like(m_ref)
        prev_ref[...] = jnp.zeros_like(prev_ref)

    row = lax.broadcasted_iota(jnp.int32, (L, L), 0)
    col = lax.broadcasted_iota(jnp.int32, (L, L), 1)
    causal = col <= row
    tril = jnp.where(causal, 1.0, 0.0).astype(BF16)
    triu = jnp.where(row <= col, 1.0, 0.0).astype(BF16)

    gc = gcol_ref[...] + brow_ref[...]
    gr = grow_ref[...] + bcol_ref[...]
    lfc1, lfc2, lfc3 = _split3(_log_sigmoid(gc))
    lfr1, lfr2, lfr3 = _split3(_log_sigmoid(gr))
    b_c = _dot(tril, lfc1) + _dot(tril, lfc2) + _dot(tril, lfc3)
    b_r = _dot(lfr1, triu) + _dot(lfr2, triu) + _dot(lfr3, triu)

    rowi = lax.broadcasted_iota(jnp.int32, (L, dh), 0)

    def conv(cur, prev, w):
        y = cur * w[A_CONV - 1:A_CONV, :]
        for s in range(1, A_CONV):
            sh = jnp.where(rowi < s, pltpu.roll(prev, s, 0), pltpu.roll(cur, s, 0))
            y = y + sh * w[A_CONV - 1 - s:A_CONV - s, :]
        return y

    for h in range(H):
        hs = slice(h * dh, (h + 1) * dh)
        ks = slice(inner + h * dh, inner + (h + 1) * dh)
        q = conv(qk_ref[:, hs].astype(F32), prev_ref[:, hs].astype(F32), convw_ref[:, hs])
        k = conv(qk_ref[:, ks].astype(F32), prev_ref[:, ks].astype(F32), convw_ref[:, ks]) * (dh ** -0.5)
        v = v_ref[:, hs]
        qb = q.astype(BF16)
        kb = k.astype(BF16)

        li_c = gc[:, h:h + 1]
        bc = b_c[:, H + h:H + h + 1]
        li_r = gr[h:h + 1, :]
        br = b_r[H + h:H + h + 1, :]
        m_prev = m_ref[h:h + 1, 0:1]

        d = jnp.where(causal, bc - br + li_r, -jnp.inf)
        inter = bc + m_prev
        m_t = jnp.maximum(inter, jnp.max(d, axis=-1, keepdims=True))
        w_inter = jnp.exp(inter - m_t)
        s = _dot_nt(qb, kb) * jnp.exp(d - m_t)
        ct = ct_ref[h]
        num = _dot(s.astype(BF16), v) + w_inter * _dot(qb, ct.astype(BF16))
        den = jnp.sum(s, axis=-1, keepdims=True) + w_inter * jnp.sum(q * n_ref[h], axis=-1, keepdims=True)
        hh = num / jnp.maximum(jnp.abs(den), jnp.exp(-m_t))
        hh = hh * lax.rsqrt(jnp.mean(hh * hh, axis=-1, keepdims=True) + RMS_EPS)
        out_ref[:, hs] = (hh * gout_ref[:, hs] * jax.nn.sigmoid(o_ref[:, hs].astype(F32))).astype(out_ref.dtype)

        b_last = bc[L - 1:L, :]
        g = b_last - bc + li_c
        m_new = jnp.maximum(b_last + m_prev, jnp.max(g, axis=0, keepdims=True))
        a_prev = jnp.exp(b_last + m_prev - m_new)
        a_s = jnp.exp(g - m_new)
        ct_ref[h] = a_prev * ct + _dot_tn(kb, (v.astype(F32) * a_s).astype(BF16))
        n_ref[h] = a_prev * n_ref[h] + jnp.sum(k * a_s, axis=0, keepdims=True)
        m_ref[h:h + 1, :] = jnp.broadcast_to(m_new, (1, LANES))

    prev_ref[...] = qk_ref[...]


def mlstm_core(proj, gcol, b_gate, conv_w, g_out, batch, seq):
    n = proj.shape[0]
    inner = proj.shape[1] // 4
    H = A_HEADS
    dh = inner // H
    L = min(MLSTM_CHUNK, seq)
    nc = seq // L
    grow = gcol[:, :2 * H].T
    brow = jnp.pad(b_gate, (0, LANES - 2 * H)).reshape(1, LANES)
    bcol = b_gate.reshape(2 * H, 1)
    return pl.pallas_call(
        _mlstm_kernel,
        grid=(batch, nc),
        in_specs=[pl.BlockSpec((L, 2 * inner), lambda b, c: (b * nc + c, 0)),
                  pl.BlockSpec((L, inner), lambda b, c: (b * nc + c, 2)),
                  pl.BlockSpec((L, inner), lambda b, c: (b * nc + c, 3)),
                  pl.BlockSpec((L, LANES), lambda b, c: (b * nc + c, 0)),
                  pl.BlockSpec((2 * H, L), lambda b, c: (0, b * nc + c)),
                  pl.BlockSpec((2 * H, 1), lambda b, c: (0, 0)),
                  pl.BlockSpec((1, LANES), lambda b, c: (0, 0)),
                  pl.BlockSpec((A_CONV, 2 * inner), lambda b, c: (0, 0)),
                  pl.BlockSpec((1, inner), lambda b, c: (0, 0))],
        out_specs=pl.BlockSpec((L, inner), lambda b, c: (b * nc + c, 0)),
        out_shape=jax.ShapeDtypeStruct((n, inner), BF16),
        scratch_shapes=[pltpu.VMEM((H, dh, dh), F32),
                        pltpu.VMEM((H, 1, dh), F32),
                        pltpu.VMEM((8, LANES), F32),
                        pltpu.VMEM((L, 2 * inner), BF16)],
        compiler_params=_cparams("parallel", "arbitrary"),
        name="mlstm_core",
    )(proj, proj, proj, gcol, grow, bcol, brow, conv_w, g_out.reshape(1, inner))


def _ffn_kernel(x_ref, h_ref, wo_ref, g_ref, wg_ref, wu_ref, wd_ref, o_ref, xn_ref, acc_ref):
    j = pl.program_id(1)

    @pl.when(j == 0)
    def _():
        xf = x_ref[...] + _dot(h_ref[...], wo_ref[...])
        xn_ref[...] = _rms(xf, g_ref[...]).astype(BF16)
        acc_ref[...] = xf

    xn = xn_ref[...]
    a = (_silu(_dot(xn, wg_ref[...])) * _dot(xn, wu_ref[...])).astype(BF16)
    acc_ref[...] += _dot(a, wd_ref[...])

    @pl.when(j == pl.num_programs(1) - 1)
    def _():
        o_ref[...] = acc_ref[...]


def ffn_dense(x, h, w_out, g, w_gu, w_down, *, tm=512, tf=1408):
    n, d = x.shape
    f = w_down.shape[0]
    dh_in = h.shape[1]
    tm = min(tm, n)
    nf = f // tf
    return pl.pallas_call(
        _ffn_kernel,
        grid=(n // tm, nf),
        in_specs=[pl.BlockSpec((tm, d), lambda i, j: (i, 0)),
                  pl.BlockSpec((tm, dh_in), lambda i, j: (i, 0)),
                  pl.BlockSpec((dh_in, d), lambda i, j: (0, 0)),
                  pl.BlockSpec((1, d), lambda i, j: (0, 0)),
                  pl.BlockSpec((d, tf), lambda i, j: (0, j)),
                  pl.BlockSpec((d, tf), lambda i, j: (0, j + nf)),
                  pl.BlockSpec((tf, d), lambda i, j: (j, 0))],
        out_specs=pl.BlockSpec((tm, d), lambda i, j: (i, 0)),
        out_shape=jax.ShapeDtypeStruct((n, d), F32),
        scratch_shapes=[pltpu.VMEM((tm, d), BF16), pltpu.VMEM((tm, d), F32)],
        compiler_params=_cparams("parallel", "arbitrary"),
        name="ffn_dense",
    )(x, h, w_out, g.reshape(1, d), w_gu, w_gu, w_down)


def _compress_kernel(r_ref, pos_ref, w1_ref, w2_ref, o_ref):
    r = r_ref[0, 0]
    w1 = w1_ref[0]
    half = r.shape[1]
    nc = r.shape[0]
    lo = _dot(r, w1[:half])
    hi = _dot(r, w1[half:])
    hid = lo + pltpu.roll(hi, nc - 1, 0) + _dot(pos_ref[0], w1)
    o_ref[0, 0] = _dot(_silu(hid).astype(BF16), w2_ref[0]).astype(o_ref.dtype)


def nsa_compress(r, pos, w1, w2):
    b, c2, nc, half = r.shape
    g = c2 // 2
    hidden = w1.shape[2]
    dh = w2.shape[2]
    return pl.pallas_call(
        _compress_kernel,
        grid=(b, c2),
        in_specs=[pl.BlockSpec((1, 1, nc, half), lambda i, j: (i, j, 0, 0)),
                  pl.BlockSpec((1, 1, 2 * half), lambda i, j: (j // g, 0, 0)),
                  pl.BlockSpec((1, 2 * half, hidden), lambda i, j: (j // g, 0, 0)),
                  pl.BlockSpec((1, hidden, dh), lambda i, j: (j // g, 0, 0))],
        out_specs=pl.BlockSpec((1, 1, nc, dh), lambda i, j: (i, j, 0, 0)),
        out_shape=jax.ShapeDtypeStruct((b, c2, nc, dh), F32),
        compiler_params=_cparams("parallel", "parallel"),
        name="nsa_compress",
    )(r, pos, w1, w2)


NSA_TQ = 256
NSA_TK_SLC = 512
NSA_TK_WIN = 512
NSA_ROWS = 32
SEL_MASK = 2.0 ** 14
LOG2E = float(np.log2(np.e))


def _nsa_kernel(q_ref, qf_ref, kc_ref, vc_ref, ks_ref, vs_ref, kw_ref, vw_ref, cs_ref, cw_ref, one_ref,
                gl_ref, bg_ref, ov_ref,
                out_ref, qx_ref, ksx_ref, vsx_ref, kwx_ref, vwx_ref,
                ms_ref, accs_ref, biass_ref, s0_ref, p0_ref, a0_ref, s1_ref, p1_ref, a1_ref,
                mw_ref, accw_ref, biasw_ref, s2_ref, p2_ref, a2_ref, s3_ref, p3_ref, a3_ref, live_ref,
                *, tq, tks, tkw, n_top):
    R = B_REP
    M = R * tq
    dh = B_HEAD_DIM
    half = LANES // 2
    t0 = pl.program_id(2) * tq
    slc_state = (ms_ref, accs_ref, biass_ref, ((s0_ref, p0_ref, a0_ref), (s1_ref, p1_ref, a1_ref)))
    win_state = (mw_ref, accw_ref, biasw_ref, ((s2_ref, p2_ref, a2_ref), (s3_ref, p3_ref, a3_ref)))

    @pl.when(pl.program_id(2) == 0)
    def _():
        odd = pl.program_id(1) % 2 == 1
        low_lane = lax.broadcasted_iota(jnp.int32, (ks_ref.shape[1], LANES), 1) < half

        def halves(ref):
            x = ref[0].astype(F32)
            xr = pltpu.roll(x, half, 1)
            return jnp.where(odd, xr, x), jnp.where(odd, x, xr)

        lo, hi = halves(ks_ref)
        ksx_ref[:, :LANES] = jnp.where(low_lane, lo, hi).astype(BF16)
        ksx_ref[:, LANES:] = cs_ref[...]
        lo, hi = halves(kw_ref)
        kwx_ref[:, :LANES] = jnp.where(low_lane, lo, hi).astype(BF16)
        kwx_ref[:, LANES:] = cw_ref[...]
        ones = one_ref[...].astype(F32)
        vsx_ref[...] = jnp.where(low_lane, halves(vs_ref)[0], ones).astype(BF16)
        vwx_ref[...] = jnp.where(low_lane, halves(vw_ref)[0], ones).astype(BF16)

    qb = q_ref[0]
    low_q = lax.broadcasted_iota(jnp.int32, (tq, LANES), 1) < half
    for r in range(R):
        tile = qb[:, (r // 2) * LANES:(r // 2 + 1) * LANES]
        own = (tile, jnp.zeros_like(tile)) if r % 2 == 0 else (jnp.zeros_like(tile), tile)
        qx_ref[r * tq:(r + 1) * tq, :LANES] = jnp.where(low_q, *own)
        qx_ref[r * tq:(r + 1) * tq, LANES:] = jnp.broadcast_to(qf_ref[0, r], (tq, LANES))
    q = qx_ref[...]

    kc = kc_ref[0, 0]
    ncmp = kc.shape[0]
    s = _dot_nt(q, kc)
    tpos_c = t0 + (lax.broadcasted_iota(jnp.int32, (M, ncmp), 0) & (tq - 1))
    cend = lax.broadcasted_iota(jnp.int32, (M, ncmp), 1) * CMP_STRIDE + (CMP_LEN - 1)
    ok_c = tpos_c >= cend
    s = jnp.where(ok_c, s, NEG)
    p = jnp.where(ok_c, jnp.exp2(s - jnp.max(s, axis=-1, keepdims=True)), 0.0)
    dsum = jnp.sum(p, axis=-1, keepdims=True)
    p = p / jnp.where(dsum > 0, dsum, 1.0)
    o_cmp = _dot(p.astype(BF16), vc_ref[0, 0])

    psum = p[0:tq]
    for r in range(1, R):
        psum = psum + p[r * tq:(r + 1) * tq]
    p1, p2, p3 = _split3(psum)
    ov = ov_ref[...]
    imp = _dot(p1, ov) + _dot(p2, ov) + _dot(p3, ov)
    nsel = ks_ref.shape[1] // SEL_BLOCK
    nselp = max(nsel, 8)
    imp_t = imp.T[:nselp]
    jj = lax.broadcasted_iota(jnp.int32, (nselp, tq), 0)
    tt = t0 + lax.broadcasted_iota(jnp.int32, (nselp, tq), 1)
    cur = tt // SEL_BLOCK
    forced = (jj == 0) | (jj == cur) | (jj == cur - 1)
    iv = jnp.where(forced, FORCED_SCORE, imp_t)
    iv = jnp.where(jj * SEL_BLOCK <= tt, iv, -jnp.inf)
    rank = jnp.zeros((nselp, tq), F32)
    for j2 in range(nsel):
        rv = iv[j2:j2 + 1, :]
        rank = rank + jnp.where(jj > j2, jnp.where(rv >= iv, 1.0, 0.0), jnp.where(rv > iv, 1.0, 0.0))
    unsel = jnp.where(rank < n_top, 0.0, -SEL_MASK)
    if nselp < LANES:
        unsel = jnp.concatenate([unsel, jnp.zeros((LANES - nselp, tq), F32)], axis=0)
    qmask = unsel.T.astype(BF16)
    for r in range(R):
        qx_ref[r * tq:(r + 1) * tq, LANES:] = qmask + qf_ref[0, r]

    def tile_step(qq, k_ref, v_ref, k0, tk, state, slot, bias):
        m_ref, acc_ref, bias_ref, slots = state
        s_ref, p_ref, a_ref = slots[slot]
        kk = k_ref[pl.ds(k0, tk), :]
        vv = v_ref[pl.ds(k0, tk), :]
        if bias is not None:
            bias_ref[:, :tk] = bias
        s_ref[:, :tk] = _dot_nt(qq, kk)
        for r in range(R):
            for c0 in range(0, tq, NSA_ROWS):
                rows = slice(r * tq + c0, r * tq + c0 + NSA_ROWS)
                sc = s_ref[rows, :tk]
                if bias is not None:
                    sc = sc + bias_ref[c0:c0 + NSA_ROWS, :tk]
                m_old = m_ref[rows, :]
                m_new = jnp.maximum(m_old, jnp.max(sc, axis=-1, keepdims=True))
                a_ref[rows, :] = jnp.exp2(m_old - m_new)
                m_ref[rows, :] = m_new
                p_ref[rows, :tk] = jnp.exp2(sc - jnp.concatenate([m_new] * (tk // LANES), axis=1)).astype(BF16)
        acc_ref[...] = a_ref[...] * acc_ref[...] + _dot(p_ref[:, :tk], vv)

    def reset(state):
        state[0][...] = jnp.full_like(state[0], NEG)
        state[1][...] = jnp.zeros_like(state[1])

    def result(state):
        acc = state[1][...]
        return acc[:, :dh] / acc[:, dh:dh + 1]

    reset(slc_state)
    qx = qx_ref[...]
    n_full = t0 // tks

    blocks_per_tile = tks // SEL_BLOCK
    n_live = jnp.int32(0)
    for j in range(nsel // blocks_per_tile):
        sel_j = rank[j * blocks_per_tile:(j + 1) * blocks_per_tile, :] < n_top
        live = (jnp.max(jnp.where(sel_j, 1.0, 0.0)) > 0.5) & (j < n_full)
        live_ref[n_live] = j
        n_live = n_live + live.astype(jnp.int32)

    def slc_pair(i, carry):
        tile_step(qx, ksx_ref, vsx_ref, pl.multiple_of(live_ref[2 * i] * tks, tks), tks, slc_state, 0, None)
        tile_step(qx, ksx_ref, vsx_ref, pl.multiple_of(live_ref[2 * i + 1] * tks, tks), tks, slc_state, 1, None)
        return carry

    lax.fori_loop(0, n_live // 2, slc_pair, 0)

    @pl.when(n_live % 2 == 1)
    def _():
        tile_step(qx, ksx_ref, vsx_ref, pl.multiple_of(live_ref[n_live - 1] * tks, tks), tks, slc_state, 0, None)

    kd = pl.multiple_of(n_full * tks, tks)
    spos = kd + lax.broadcasted_iota(jnp.int32, (tq, tks), 1)
    tpos = t0 + lax.broadcasted_iota(jnp.int32, (tq, tks), 0)
    tile_step(qx, ksx_ref, vsx_ref, kd, tks, slc_state, 1, jnp.where(spos <= tpos, 0.0, NEG))
    reset(win_state)
    n_win = WINDOW // tkw + max(tq // tkw, 1)
    last = (t0 + tq - 1) // tkw
    for i in range(n_win):
        kw0 = (last - (n_win - 1) + i) * tkw
        spos = kw0 + lax.broadcasted_iota(jnp.int32, (tq, tkw), 1)
        dist = t0 + lax.broadcasted_iota(jnp.int32, (tq, tkw), 0) - spos
        bias = jnp.where((spos >= 0) & (dist >= 0) & (dist < WINDOW), 0.0, NEG)
        tile_step(q, kwx_ref, vwx_ref, pl.multiple_of(jnp.maximum(kw0, 0), tkw), tkw, win_state, i % 2, bias)
    o_slc = result(slc_state)
    o_win = result(win_state)

    gates = jax.nn.sigmoid(gl_ref[0, 0, pl.ds(pl.multiple_of(t0, tq), tq), :] + bg_ref[0])
    outs = []
    for r in range(R):
        rs = slice(r * tq, (r + 1) * tq)
        outs.append(gates[:, 3 * r:3 * r + 1] * o_cmp[rs] + gates[:, 3 * r + 1:3 * r + 2] * o_slc[rs]
                    + gates[:, 3 * r + 2:3 * r + 3] * o_win[rs])
    out_ref[0] = jnp.concatenate(outs, axis=1).astype(out_ref.dtype)


def nsa_attention(qp, q_feat, kc, vc, kvx, kv_off, slc_const, win_const, ones_col, gl, bg, ov):
    b, t, _ = qp.shape
    g, r = q_feat.shape[:2]
    dh = B_HEAD_DIM
    tq = min(NSA_TQ, t)
    tks = min(NSA_TK_SLC, t)
    tkw = min(NSA_TK_WIN, t)
    ncmp = kc.shape[2]
    n_top = min(SEL_TOPN, t // SEL_BLOCK)
    per_tile = LANES // dh
    assert t // SEL_BLOCK <= LANES // 2 and per_tile == 2 and g % per_tile == 0
    kv_spec = lambda c: pl.BlockSpec((1, t, LANES),
                                     lambda i, j, k: (i, 0, kv_off + c * (g // per_tile) + j // per_tile))
    const_spec = lambda a: pl.BlockSpec(a.shape, lambda i, j, k: (0,) * a.ndim)

    def branch_scratch(tk):
        slot = [pltpu.VMEM((r * tq, tk), F32), pltpu.VMEM((r * tq, tk), BF16), pltpu.VMEM((r * tq, LANES), F32)]
        return [pltpu.VMEM((r * tq, LANES), F32), pltpu.VMEM((r * tq, LANES), F32),
                pltpu.VMEM((tq, tk), F32)] + slot + slot

    return pl.pallas_call(
        functools.partial(_nsa_kernel, tq=tq, tks=tks, tkw=tkw, n_top=n_top),
        grid=(b, g, t // tq),
        in_specs=[pl.BlockSpec((1, tq, r * dh), lambda i, j, k: (i, k, j)),
                  pl.BlockSpec((1, r, 1, LANES), lambda i, j, k: (j, 0, 0, 0)),
                  pl.BlockSpec((1, 1, ncmp, 2 * LANES), lambda i, j, k: (i, j, 0, 0)),
                  pl.BlockSpec((1, 1, ncmp, dh), lambda i, j, k: (i, j, 0, 0)),
                  kv_spec(0), kv_spec(1), kv_spec(2), kv_spec(3),
                  const_spec(slc_const), const_spec(win_const), const_spec(ones_col),
                  pl.BlockSpec((1, 1, t, 3 * r), lambda i, j, k: (i, j, 0, 0)),
                  pl.BlockSpec((1, 1, 3 * r), lambda i, j, k: (j, 0, 0)),
                  const_spec(ov)],
        out_specs=pl.BlockSpec((1, tq, r * dh), lambda i, j, k: (i, k, j)),
        out_shape=jax.ShapeDtypeStruct((b, t, g * r * dh), BF16),
        scratch_shapes=[pltpu.VMEM((r * tq, 2 * LANES), BF16),
                        pltpu.VMEM((t, 2 * LANES), BF16),
                        pltpu.VMEM((t, LANES), BF16),
                        pltpu.VMEM((t, 2 * LANES), BF16),
                        pltpu.VMEM((t, LANES), BF16),
                        ] + branch_scratch(tks) + branch_scratch(tkw)
                       + [pltpu.SMEM((t // tks + 1,), jnp.int32)],
        compiler_params=_cparams("arbitrary", "arbitrary", "arbitrary"),
        name="nsa_attention",
    )(qp, q_feat, kc, vc, kvx, kvx, kvx, kvx, slc_const, win_const, ones_col, gl, bg, ov)


def _alibi_slopes(n):
    return np.power(2.0, -8.0 * np.arange(1, n + 1) / n).astype(np.float32)


def _np_split3(a):
    a = np.asarray(a, np.float32)
    out = []
    r = a
    for _ in range(3):
        p = r.astype(BF16).astype(np.float32)
        out.append(p)
        r = (r - p).astype(np.float32)
    return out


def _pos_features(pos, width):
    hi = (pos // 64).astype(np.float32)
    lo = (pos % 64).astype(np.float32)
    f = np.zeros((pos.shape[0], width), np.float32)
    f[:, 0:3] = hi[:, None]
    f[:, 3:6] = lo[:, None]
    return f


def _slope_features(width):
    s1, s2, s3 = _np_split3((_alibi_slopes(B_HEADS).astype(np.float64) * LOG2E).astype(np.float32))
    f = np.zeros((B_HEADS, width), np.float32)
    for i, s in enumerate((s1, s2, s3)):
        f[:, i] = 64.0 * s
        f[:, 3 + i] = s
    return f


META_I1, META_I2, META_W1, META_W2 = 0, 1, 2, 3


def _router_kernel(x_ref, h_ref, wo_ref, g_ref, w_ref, xo_ref, xn_ref, meta_ref, pos_ref, cnt_ref, run_ref):
    @pl.when(pl.program_id(0) == 0)
    def _():
        run_ref[...] = jnp.zeros_like(run_ref)

    xf = x_ref[...] + _dot(h_ref[...], wo_ref[...])
    xo_ref[...] = xf
    xn = _rms(xf, g_ref[...])
    xn_ref[...] = xn
    logits = _dot_hi_lo(xn, w_ref[...])
    tm = logits.shape[0]
    lane = lax.broadcasted_iota(jnp.int32, logits.shape, 1)
    logits = jnp.where(lane < N_EXPERTS, logits, -jnp.inf)
    m1 = jnp.max(logits, axis=-1, keepdims=True)
    i1 = jnp.min(jnp.where(logits == m1, lane, LANES), axis=-1, keepdims=True)
    rest = jnp.where(lane == i1, -jnp.inf, logits)
    m2 = jnp.max(rest, axis=-1, keepdims=True)
    i2 = jnp.min(jnp.where(rest == m2, lane, LANES), axis=-1, keepdims=True)
    e2 = jnp.exp(m2 - m1)
    w1 = 1.0 / (1.0 + e2)
    meta = jnp.where(lane == META_I1, i1.astype(F32), 0.0)
    meta = jnp.where(lane == META_I2, i2.astype(F32), meta)
    meta = jnp.where(lane == META_W1, w1, meta)
    meta_ref[...] = jnp.where(lane == META_W2, e2 * w1, meta)

    sel = jnp.where((lane == i1) | (lane == i2), 1.0, 0.0)
    row = lax.broadcasted_iota(jnp.int32, (tm, tm), 0)
    col = lax.broadcasted_iota(jnp.int32, (tm, tm), 1)
    before = jnp.where(col < row, 1.0, 0.0).astype(BF16)
    run = run_ref[...]
    pos_ref[...] = run + _dot(before, sel.astype(BF16))
    run = run + jnp.sum(sel, axis=0, keepdims=True)
    run_ref[...] = run
    cnt_ref[...] = run


def moe_router(x, h, w_out, g, w_router, *, tm=512):
    n, d = x.shape
    dh_in = h.shape[1]
    tm = min(tm, n)
    w = _hi_lo_weight(w_router)
    row_spec = pl.BlockSpec((tm, LANES), lambda i: (i, 0))
    return pl.pallas_call(
        _router_kernel,
        grid=(n // tm,),
        in_specs=[pl.BlockSpec((tm, d), lambda i: (i, 0)),
                  pl.BlockSpec((tm, dh_in), lambda i: (i, 0)),
                  pl.BlockSpec((dh_in, d), lambda i: (0, 0)),
                  pl.BlockSpec((1, d), lambda i: (0, 0)),
                  pl.BlockSpec((d, 2 * LANES), lambda i: (0, 0))],
        out_specs=[pl.BlockSpec((tm, d), lambda i: (i, 0)), pl.BlockSpec((tm, d), lambda i: (i, 0)),
                   row_spec, row_spec,
                   pl.BlockSpec((1, LANES), lambda i: (0, 0))],
        out_shape=[jax.ShapeDtypeStruct((n, d), F32), jax.ShapeDtypeStruct((n, d), F32),
                   jax.ShapeDtypeStruct((n, LANES), F32),
                   jax.ShapeDtypeStruct((n, LANES), F32), jax.ShapeDtypeStruct((1, LANES), F32)],
        scratch_shapes=[pltpu.VMEM((1, LANES), F32)],
        compiler_params=_cparams("arbitrary"),
        name="moe_router",
    )(x, h, w_out, g.reshape(1, d), w)


MOE_TILE = 512


def _moe_kernel(te_ref, ok_ref, idx0_ref, idxn_ref, x_hbm, wg_ref, wu_ref, wd_ref, o_ref, xbuf, xb_ref, sem):
    i = pl.program_id(0)
    j = pl.program_id(1)
    nt = pl.num_programs(0)
    nf = pl.num_programs(1)
    tm = xb_ref.shape[0]
    per_step = tm // nf
    slot = i % 2

    def row_copy(idx_ref, r, s):
        return pltpu.make_async_copy(x_hbm.at[pl.ds(idx_ref[0, 0, r], 1), :], xbuf.at[s, pl.ds(r, 1), :], sem.at[s])

    @pl.when((i == 0) & (j == 0))
    def _():
        def start0(r, c):
            row_copy(idx0_ref, r, 0).start()
            return c
        lax.fori_loop(0, tm, start0, 0)

    @pl.when(j == 0)
    def _():
        for r in range(tm):
            row_copy(idx0_ref, r, slot).wait()
        xb_ref[...] = xbuf[slot].astype(BF16)
        o_ref[...] = jnp.zeros_like(o_ref)

    def prefetch():
        base = j * per_step
        for r in range(per_step):
            row_copy(idxn_ref, base + r, 1 - slot).start()

    def compute():
        xb = xb_ref[...]
        a = (_silu(_dot(xb, wg_ref[0])) * _dot(xb, wu_ref[0])).astype(BF16)
        o_ref[...] += _dot(a, wd_ref[0])

    has_next = i + 1 < nt
    ok = ok_ref[i] > 0

    @pl.when(has_next & ok)
    def _():
        prefetch()
        compute()

    @pl.when(has_next & jnp.logical_not(ok))
    def _():
        prefetch()

    @pl.when(jnp.logical_not(has_next) & ok)
    def _():
        compute()


def moe_experts(x, row_tok, tile_expert, tile_ok, w_gu, w_down, *, tf=1792):
    p = row_tok.shape[0]
    d = x.shape[1]
    ne, f, _ = w_down.shape
    tm = MOE_TILE
    nf = f // tf
    nt = p // tm
    idx = row_tok.reshape(nt, 1, tm)
    grid_spec = pltpu.PrefetchScalarGridSpec(
        num_scalar_prefetch=2,
        grid=(nt, nf),
        in_specs=[pl.BlockSpec((1, 1, tm), lambda i, j, te, ok: (0, 0, 0), memory_space=pltpu.SMEM),
                  pl.BlockSpec((1, 1, tm), lambda i, j, te, ok: (jnp.minimum(i + 1, nt - 1), 0, 0),
                               memory_space=pltpu.SMEM),
                  pl.BlockSpec(memory_space=pl.ANY),
                  pl.BlockSpec((1, d, tf), lambda i, j, te, ok: (te[i], 0, j)),
                  pl.BlockSpec((1, d, tf), lambda i, j, te, ok: (te[i], 0, j + nf)),
                  pl.BlockSpec((1, tf, d), lambda i, j, te, ok: (te[i], j, 0))],
        out_specs=pl.BlockSpec((tm, d), lambda i, j, te, ok: (i, 0)),
        scratch_shapes=[pltpu.VMEM((2, tm, d), F32), pltpu.VMEM((tm, d), BF16),
                        pltpu.SemaphoreType.DMA((2,))])
    return pl.pallas_call(
        _moe_kernel,
        grid_spec=grid_spec,
        out_shape=jax.ShapeDtypeStruct((p, d), F32),
        compiler_params=_cparams("arbitrary", "arbitrary"),
        name="moe_experts",
    )(tile_expert, tile_ok, idx, idx, x, w_gu, w_gu, w_down)


def _combine_kernel(d0_ref, dn_ref, x_ref, meta_ref, y_ref, gf_ref, o_ref, ybuf, sem):
    i = pl.program_id(0)
    nt = pl.num_programs(0)
    rows = o_ref.shape[0]
    slot = i % 2

    def row_copy(d_ref, k, r, s):
        return pltpu.make_async_copy(y_ref.at[pl.ds(d_ref[0, k, r], 1), :], ybuf.at[s, k, pl.ds(r, 1), :], sem.at[s])

    @pl.when(i == 0)
    def _():
        def start0(r, c):
            row_copy(d0_ref, 0, r, 0).start()
            row_copy(d0_ref, 1, r, 0).start()
            return c
        lax.fori_loop(0, rows, start0, 0)

    for r in range(rows):
        row_copy(d0_ref, 0, r, slot).wait()
        row_copy(d0_ref, 1, r, slot).wait()

    @pl.when(i + 1 < nt)
    def _():
        for r in range(rows):
            row_copy(dn_ref, 0, r, 1 - slot).start()
            row_copy(dn_ref, 1, r, 1 - slot).start()

    meta = meta_ref[...]
    w1 = meta[:, META_W1:META_W1 + 1]
    w2 = meta[:, META_W2:META_W2 + 1]
    o_ref[...] = _rms(x_ref[...] + w1 * ybuf[slot, 0] + w2 * ybuf[slot, 1], gf_ref[...])


def moe_combine(x, meta, ys, d1, d2, g_final, *, tc=256):
    n, d = x.shape
    tc = min(tc, n)
    nt = n // tc
    dd = jnp.stack([d1.reshape(nt, tc), d2.reshape(nt, tc)], axis=1)
    return pl.pallas_call(
        _combine_kernel,
        grid=(nt,),
        in_specs=[pl.BlockSpec((1, 2, tc), lambda i: (0, 0, 0), memory_space=pltpu.SMEM),
                  pl.BlockSpec((1, 2, tc), lambda i: (jnp.minimum(i + 1, nt - 1), 0, 0), memory_space=pltpu.SMEM),
                  pl.BlockSpec((tc, d), lambda i: (i, 0)),
                  pl.BlockSpec((tc, LANES), lambda i: (i, 0)),
                  pl.BlockSpec(memory_space=pl.ANY),
                  pl.BlockSpec((1, d), lambda i: (0, 0))],
        out_specs=pl.BlockSpec((tc, d), lambda i: (i, 0)),
        out_shape=jax.ShapeDtypeStruct((n, d), F32),
        scratch_shapes=[pltpu.VMEM((2, 2, tc, d), F32), pltpu.SemaphoreType.DMA((2,))],
        compiler_params=_cparams("arbitrary"),
        name="moe_combine",
    )(dd, dd, x, meta, ys, g_final.reshape(1, d))


def kernel(x, norm_mix, norm_ffn, a_w_in, a_b_gate, a_conv, a_norm_h, a_w_out, norm_kv, b_w_kv,
           b_cmp_pos, b_cmp_w1, b_cmp_w2, b_w_q, b_b_gate, b_w_out, f_w_gu, f_w_down,
           m_router, m_w_gu, m_w_down, norm_final):
    B, T, D = x.shape
    N = B * T
    G, R, dh = B_KV_GROUPS, B_REP, B_HEAD_DIM
    xs = x.reshape(N, D)

    inner4 = a_w_in.shape[2] - 2 * A_HEADS
    w_in = a_w_in[0]
    proj, gcol = norm_matmul(xs, norm_mix[0], w_in[:, :inner4].astype(BF16), w_side=w_in[:, inner4:])
    hs = mlstm_core(proj, gcol, a_b_gate[0], a_conv[0], a_norm_h[0], B, T)
    xs = ffn_dense(xs, hs, a_w_out[0].astype(BF16), norm_ffn[0], f_w_gu[0].astype(BF16), f_w_down[0].astype(BF16))

    hd = B_HEADS * dh
    feat_w = LANES - dh
    ncmp = T // CMP_STRIDE

    cmp_cols = 2 * G * dh
    kvx = norm_matmul(xs, norm_kv, b_w_kv.astype(BF16), tn=b_w_kv.shape[1] // 2)
    w_q = b_w_q[0]
    qp, gl = norm_matmul(xs, norm_mix[1], (w_q[:, :hd] * (dh ** -0.5 * LOG2E)).astype(BF16), w_side=w_q[:, hd:])

    kvt = kvx[:, :cmp_cols].reshape(B, T, 2 * G, dh).transpose(0, 2, 1, 3)
    pos = b_cmp_pos.transpose(1, 0, 2).reshape(2, 1, CMP_LEN * dh).astype(BF16)
    kvc = nsa_compress(kvt.reshape(B, 2 * G, ncmp, CMP_STRIDE * dh), pos,
                       b_cmp_w1.astype(BF16), b_cmp_w2.astype(BF16))

    def hi_lanes(f, lo=None):
        lo = np.zeros((f.shape[0], dh), np.float32) if lo is None else lo
        return jnp.asarray(np.concatenate([lo, f], axis=1), BF16)

    key_pos = _pos_features(np.arange(T), feat_w)
    blk_onehot = (np.arange(T)[:, None] // SEL_BLOCK == np.arange(dh)[None, :]).astype(np.float32)
    slc_const = hi_lanes(key_pos, blk_onehot)
    win_const = hi_lanes(key_pos)
    cmp_const = hi_lanes(_pos_features(np.arange(ncmp) * CMP_STRIDE + CMP_LEN - 1, feat_w))
    q_feat = hi_lanes(_slope_features(feat_w)).reshape(G, R, 1, LANES)
    ones_col = np.zeros((1, feat_w), np.float32)
    ones_col[0, 0] = 1.0
    ones_col = hi_lanes(ones_col)
    kcb = kvc[:, :G].astype(BF16)
    kc = jnp.concatenate([kcb, kcb, jnp.broadcast_to(cmp_const, (B, G, ncmp, LANES))], axis=-1)
    vc = kvc[:, G:].astype(BF16)
    glt = gl[:, :3 * B_HEADS].reshape(B, T, G, 3 * R).transpose(0, 2, 1, 3)
    bg = b_b_gate[0].reshape(G, 1, 3 * R)

    nsel = T // SEL_BLOCK
    ci = np.arange(ncmp)[:, None] * CMP_STRIDE
    sj = np.arange(LANES)[None, :] * SEL_BLOCK
    ov = ((ci < sj + SEL_BLOCK) & (ci + CMP_LEN > sj) & (np.arange(LANES)[None, :] < nsel)
          & (np.arange(ncmp)[:, None] < ncmp - 1))
    ov = jnp.asarray(ov.astype(np.float32), BF16)

    oa = nsa_attention(qp.reshape(B, T, hd), q_feat, kc, vc, kvx.reshape(B, T, -1),
                       cmp_cols // LANES, slc_const, win_const, ones_col, glt, bg, ov)

    xs, xn, meta, pos, cnt = moe_router(xs, oa.reshape(N, hd), b_w_out[0].astype(BF16), norm_ffn[1], m_router[0])
    ne = N_EXPERTS
    p_rows = 2 * N + ne * MOE_TILE
    i1 = meta[:, META_I1].astype(jnp.int32)
    i2 = meta[:, META_I2].astype(jnp.int32)
    counts = cnt[0, :ne].astype(jnp.int32)
    padded = (counts + MOE_TILE - 1) // MOE_TILE * MOE_TILE
    seg_end = jnp.cumsum(padded)
    seg_start = seg_end - padded
    pos8 = pos[:, :ne].astype(jnp.int32)
    d1 = seg_start[i1] + jnp.take_along_axis(pos8, i1[:, None], axis=1)[:, 0]
    d2 = seg_start[i2] + jnp.take_along_axis(pos8, i2[:, None], axis=1)[:, 0]
    tok = jnp.arange(N, dtype=jnp.int32)
    row_tok = jnp.zeros((p_rows,), jnp.int32).at[jnp.concatenate([d1, d2])].set(jnp.concatenate([tok, tok]))
    tile_start = jnp.arange(p_rows // MOE_TILE, dtype=jnp.int32) * MOE_TILE
    tile_expert = jnp.minimum(jnp.sum(tile_start[:, None] >= seg_end[None, :], axis=1), ne - 1).astype(jnp.int32)
    tile_ok = (tile_start < seg_end[ne - 1]).astype(jnp.int32)

    ys = moe_experts(xn, row_tok, tile_expert, tile_ok, m_w_gu[0].astype(BF16), m_w_down[0].astype(BF16))
    out = moe_combine(xs, meta, ys, d1, d2, norm_final)
    return out.reshape(B, T, D)
```

```python
import functools

import numpy as np
import jax
import jax.numpy as jnp
from jax import lax
from jax.experimental import pallas as pl
from jax.experimental.pallas import tpu as pltpu

F32 = jnp.float32
BF16 = jnp.bfloat16

RMS_EPS = 1e-6
A_HEADS = 4
A_CONV = 4
B_HEADS = 16
B_KV_GROUPS = 4
B_REP = B_HEADS // B_KV_GROUPS
B_HEAD_DIM = 64
CMP_LEN = 32
CMP_STRIDE = 16
SEL_BLOCK = 64
SEL_TOPN = 16
WINDOW = 512
FORCED_SCORE = 1e4
N_EXPERTS = 8

LANES = 128
V7X_VMEM_BYTES = 64 * 1024 * 1024
VMEM_LIMIT = V7X_VMEM_BYTES - 8 * 1024 * 1024
NEG = -1e30


def _cparams(*sem):
    return pltpu.CompilerParams(dimension_semantics=sem, vmem_limit_bytes=VMEM_LIMIT)


def _dot(a, b):
    return jnp.dot(a, b, preferred_element_type=F32)


def _dot_nt(a, b):
    return lax.dot_general(a, b, (((1,), (1,)), ((), ())), preferred_element_type=F32)


def _dot_tn(a, b):
    return lax.dot_general(a, b, (((0,), (0,)), ((), ())), preferred_element_type=F32)


def _split3(a):
    a1 = a.astype(BF16)
    r1 = a - a1.astype(F32)
    a2 = r1.astype(BF16)
    a3 = (r1 - a2.astype(F32)).astype(BF16)
    return a1, a2, a3


def _hi_lo_weight(w):
    w = jnp.pad(w, ((0, 0), (0, LANES - w.shape[1])))
    hi = w.astype(BF16)
    lo = (w - hi.astype(F32)).astype(BF16)
    return jnp.concatenate([hi, lo], axis=1)


def _dot_hi_lo(a, w_hl):
    a1 = a.astype(BF16)
    a2 = (a - a1.astype(F32)).astype(BF16)
    s = _dot(a1, w_hl) + _dot(a2, w_hl)
    return s[:, :LANES] + s[:, LANES:]


def _rms(xf, g):
    return xf * lax.rsqrt(jnp.mean(xf * xf, axis=-1, keepdims=True) + RMS_EPS) * g


def _silu(x):
    return x * jax.nn.sigmoid(x)


def _log_sigmoid(x):
    return jnp.minimum(x, 0.0) - jnp.log(1.0 + jnp.exp(-jnp.abs(x)))


def _norm_matmul_kernel(*refs, side):
    if side:
        x_ref, g_ref, w_ref, ws_ref, o_ref, os_ref, xn_ref = refs
    else:
        x_ref, g_ref, w_ref, o_ref, xn_ref = refs

    @pl.when(pl.program_id(1) == 0)
    def _():
        xn = _rms(x_ref[...], g_ref[...])
        xn_ref[...] = xn.astype(xn_ref.dtype)
        if side:
            os_ref[...] = _dot_hi_lo(xn, ws_ref[...])

    o_ref[...] = _dot(xn_ref[...], w_ref[...]).astype(o_ref.dtype)


def norm_matmul(x, g, w, *, w_side=None, tm=1024, tn=1024):
    n, d = x.shape
    dout = w.shape[1]
    tm = min(tm, n)
    tn = min(tn, dout)
    side = w_side is not None
    in_specs = [pl.BlockSpec((tm, d), lambda i, j: (i, 0)),
                pl.BlockSpec((1, d), lambda i, j: (0, 0)),
                pl.BlockSpec((d, tn), lambda i, j: (0, j))]
    out_specs = [pl.BlockSpec((tm, tn), lambda i, j: (i, j))]
    out_shape = [jax.ShapeDtypeStruct((n, dout), BF16)]
    args = [x, g.reshape(1, d), w]
    if side:
        in_specs.append(pl.BlockSpec((d, 2 * LANES), lambda i, j: (0, 0)))
        out_specs.append(pl.BlockSpec((tm, LANES), lambda i, j: (i, 0)))
        out_shape.append(jax.ShapeDtypeStruct((n, LANES), F32))
        args.append(_hi_lo_weight(w_side))
    out = pl.pallas_call(
        functools.partial(_norm_matmul_kernel, side=side),
        grid=(n // tm, dout // tn),
        in_specs=in_specs,
        out_specs=out_specs,
        out_shape=out_shape,
        scratch_shapes=[pltpu.VMEM((tm, d), BF16)],
        compiler_params=_cparams("parallel", "arbitrary"),
        name="norm_matmul_side" if side else "norm_matmul",
    )(*args)
    return out if side else out[0]


MLSTM_CHUNK = 256


def _mlstm_kernel(qk_ref, v_ref, o_ref, gcol_ref, grow_ref, bcol_ref, brow_ref, convw_ref, gout_ref,
                  out_ref, ct_ref, n_ref, m_ref, xs_ref):
    L = qk_ref.shape[0]
    H = A_HEADS
    inner = v_ref.shape[1]
    dh = inner // H

    @pl.when(pl.program_id(1) == 0)
    def _():
        ct_ref[...] = jnp.zeros_like(ct_ref)
        n_ref[...] = jnp.zeros_like(n_ref)
        m_ref[...] = jnp.zeros_like(m_ref)
        xs_ref[...] = jnp.zeros_like(xs_ref)

    row = lax.broadcasted_iota(jnp.int32, (L, L), 0)
    col = lax.broadcasted_iota(jnp.int32, (L, L), 1)
    causal = col <= row
    tril = jnp.where(causal, 1.0, 0.0).astype(BF16)
    triu = jnp.where(row <= col, 1.0, 0.0).astype(BF16)

    gc = gcol_ref[...] + brow_ref[...]
    gr = grow_ref[...] + bcol_ref[...]
    lfc1, lfc2, lfc3 = _split3(_log_sigmoid(gc))
    lfr1, lfr2, lfr3 = _split3(_log_sigmoid(gr))
    b_c = _dot(tril, lfc1) + _dot(tril, lfc2) + _dot(tril, lfc3)
    b_r = _dot(lfr1, triu) + _dot(lfr2, triu) + _dot(lfr3, triu)

    halo = xs_ref.shape[0] - L
    xs_ref[halo:, :] = qk_ref[...].astype(F32)

    shift = [jnp.where(row - col == s, 1.0, 0.0).astype(BF16) for s in range(1, A_CONV)]
    rowh = lax.broadcasted_iota(jnp.int32, (halo, dh), 0)

    def conv(cols):
        w = convw_ref[:, cols]
        x = xs_ref[halo:, cols]
        tail = xs_ref[:halo, cols]
        y = x * w[A_CONV - 1:A_CONV, :]
        head = jnp.zeros((halo, dh), F32)
        for s in range(1, A_CONV):
            ws = w[A_CONV - 1 - s:A_CONV - s, :]
            y = y + _dot(shift[s - 1], (x * ws).astype(BF16))
            head = head + jnp.where(rowh < s, pltpu.roll(tail, s, 0), 0.0) * ws
        return jnp.concatenate([y[:halo] + head, y[halo:]], axis=0)

    for h in range(H):
        hs = slice(h * dh, (h + 1) * dh)
        ks = slice(inner + h * dh, inner + (h + 1) * dh)
        q = conv(hs)
        k = conv(ks) * (dh ** -0.5)
        v = v_ref[:, hs]
        qb = q.astype(BF16)
        kb = k.astype(BF16)

        li_c = gc[:, h:h + 1]
        bc = b_c[:, H + h:H + h + 1]
        li_r = gr[h:h + 1, :]
        br = b_r[H + h:H + h + 1, :]
        m_prev = m_ref[h:h + 1, 0:1]

        d = jnp.where(causal, bc - br + li_r, -jnp.inf)
        inter = bc + m_prev
        m_t = jnp.maximum(inter, jnp.max(d, axis=-1, keepdims=True))
        w_inter = jnp.exp(inter - m_t)
        s = _dot_nt(qb, kb) * jnp.exp(d - m_t)
        ct = ct_ref[h]
        num = _dot(s.astype(BF16), v) + w_inter * _dot(qb, ct.astype(BF16))
        den = jnp.sum(s, axis=-1, keepdims=True) + w_inter * jnp.sum(q * n_ref[h], axis=-1, keepdims=True)
        hh = num / jnp.maximum(jnp.abs(den), jnp.exp(-m_t))
        hh = hh * lax.rsqrt(jnp.mean(hh * hh, axis=-1, keepdims=True) + RMS_EPS)
        out_ref[:, hs] = (hh * gout_ref[:, hs] * jax.nn.sigmoid(o_ref[:, hs].astype(F32))).astype(out_ref.dtype)

        b_last = bc[L - 1:L, :]
        g = b_last - bc + li_c
        m_new = jnp.maximum(b_last + m_prev, jnp.max(g, axis=0, keepdims=True))
        a_prev = jnp.exp(b_last + m_prev - m_new)
        a_s = jnp.exp(g - m_new)
        ct_ref[h] = a_prev * ct + _dot_tn(kb, (v.astype(F32) * a_s).astype(BF16))
        n_ref[h] = a_prev * n_ref[h] + jnp.sum(k * a_s, axis=0, keepdims=True)
        m_ref[h:h + 1, :] = jnp.broadcast_to(m_new, (1, LANES))

    xs_ref[:halo, :] = xs_ref[L:, :]


def mlstm_core(proj, gcol, b_gate, conv_w, g_out, batch, seq):
    n = proj.shape[0]
    inner = proj.shape[1] // 4
    H = A_HEADS
    dh = inner // H
    L = min(MLSTM_CHUNK, seq)
    nc = seq // L
    grow = gcol[:, :2 * H].T
    brow = jnp.pad(b_gate, (0, LANES - 2 * H)).reshape(1, LANES)
    bcol = b_gate.reshape(2 * H, 1)
    return pl.pallas_call(
        _mlstm_kernel,
        grid=(batch, nc),
        in_specs=[pl.BlockSpec((L, 2 * inner), lambda b, c: (b * nc + c, 0)),
                  pl.BlockSpec((L, inner), lambda b, c: (b * nc + c, 2)),
                  pl.BlockSpec((L, inner), lambda b, c: (b * nc + c, 3)),
                  pl.BlockSpec((L, LANES), lambda b, c: (b * nc + c, 0)),
                  pl.BlockSpec((2 * H, L), lambda b, c: (0, b * nc + c)),
                  pl.BlockSpec((2 * H, 1), lambda b, c: (0, 0)),
                  pl.BlockSpec((1, LANES), lambda b, c: (0, 0)),
                  pl.BlockSpec((A_CONV, 2 * inner), lambda b, c: (0, 0)),
                  pl.BlockSpec((1, inner), lambda b, c: (0, 0))],
        out_specs=pl.BlockSpec((L, inner), lambda b, c: (b * nc + c, 0)),
        out_shape=jax.ShapeDtypeStruct((n, inner), BF16),
        scratch_shapes=[pltpu.VMEM((H, dh, dh), F32),
                        pltpu.VMEM((H, 1, dh), F32),
                        pltpu.VMEM((8, LANES), F32),
                        pltpu.VMEM((8 + L, 2 * inner), F32)],
        compiler_params=_cparams("parallel", "arbitrary"),
        name="mlstm_core",
    )(proj, proj, proj, gcol, grow, bcol, brow, conv_w, g_out.reshape(1, inner))


def _ffn_kernel(x_ref, h_ref, wo_ref, g_ref, wg_ref, wu_ref, wd_ref, o_ref, xn_ref, acc_ref):
    j = pl.program_id(1)

    @pl.when(j == 0)
    def _():
        xf = x_ref[...] + _dot(h_ref[...], wo_ref[...])
        xn_ref[...] = _rms(xf, g_ref[...]).astype(BF16)
        acc_ref[...] = xf

    xn = xn_ref[...]
    a = (_silu(_dot(xn, wg_ref[...])) * _dot(xn, wu_ref[...])).astype(BF16)
    acc_ref[...] += _dot(a, wd_ref[...])

    @pl.when(j == pl.num_programs(1) - 1)
    def _():
        o_ref[...] = acc_ref[...]


def ffn_dense(x, h, w_out, g, w_gu, w_down, *, tm=512, tf=1408):
    n, d = x.shape
    f = w_down.shape[0]
    dh_in = h.shape[1]
    tm = min(tm, n)
    nf = f // tf
    return pl.pallas_call(
        _ffn_kernel,
        grid=(n // tm, nf),
        in_specs=[pl.BlockSpec((tm, d), lambda i, j: (i, 0)),
                  pl.BlockSpec((tm, dh_in), lambda i, j: (i, 0)),
                  pl.BlockSpec((dh_in, d), lambda i, j: (0, 0)),
                  pl.BlockSpec((1, d), lambda i, j: (0, 0)),
                  pl.BlockSpec((d, tf), lambda i, j: (0, j)),
                  pl.BlockSpec((d, tf), lambda i, j: (0, j + nf)),
                  pl.BlockSpec((tf, d), lambda i, j: (j, 0))],
        out_specs=pl.BlockSpec((tm, d), lambda i, j: (i, 0)),
        out_shape=jax.ShapeDtypeStruct((n, d), F32),
        scratch_shapes=[pltpu.VMEM((tm, d), BF16), pltpu.VMEM((tm, d), F32)],
        compiler_params=_cparams("parallel", "arbitrary"),
        name="ffn_dense",
    )(x, h, w_out, g.reshape(1, d), w_gu, w_gu, w_down)


def _compress_kernel(r_ref, pos_ref, w1_ref, w2_ref, o_ref):
    r = r_ref[0, 0]
    w1 = w1_ref[0]
    half = r.shape[1]
    nc = r.shape[0]
    lo = _dot(r, w1[:half])
    hi = _dot(r, w1[half:])
    hid = lo + pltpu.roll(hi, nc - 1, 0) + _dot(pos_ref[0], w1)
    o_ref[0, 0] = _dot(_silu(hid).astype(BF16), w2_ref[0]).astype(o_ref.dtype)


def nsa_compress(r, pos, w1, w2):
    b, c2, nc, half = r.shape
    g = c2 // 2
    hidden = w1.shape[2]
    dh = w2.shape[2]
    return pl.pallas_call(
        _compress_kernel,
        grid=(b, c2),
        in_specs=[pl.BlockSpec((1, 1, nc, half), lambda i, j: (i, j, 0, 0)),
                  pl.BlockSpec((1, 1, 2 * half), lambda i, j: (j // g, 0, 0)),
                  pl.BlockSpec((1, 2 * half, hidden), lambda i, j: (j // g, 0, 0)),
                  pl.BlockSpec((1, hidden, dh), lambda i, j: (j // g, 0, 0))],
        out_specs=pl.BlockSpec((1, 1, nc, dh), lambda i, j: (i, j, 0, 0)),
        out_shape=jax.ShapeDtypeStruct((b, c2, nc, dh), F32),
        compiler_params=_cparams("parallel", "parallel"),
        name="nsa_compress",
    )(r, pos, w1, w2)


NSA_TQ = 256
NSA_TK_SLC = 512
NSA_TK_WIN = 512
NSA_ROWS = 32
SEL_MASK = 2.0 ** 14
LOG2E = float(np.log2(np.e))


def _nsa_kernel(q_ref, qf_ref, kc_ref, vc_ref, ks_ref, vs_ref, kw_ref, vw_ref, cs_ref, cw_ref, one_ref,
                gl_ref, bg_ref, ov_ref,
                out_ref, qx_ref, ksx_ref, vsx_ref, kwx_ref, vwx_ref,
                ms_ref, accs_ref, biass_ref, s0_ref, p0_ref, a0_ref, s1_ref, p1_ref, a1_ref,
                mw_ref, accw_ref, biasw_ref, s2_ref, p2_ref, a2_ref, s3_ref, p3_ref, a3_ref, live_ref,
                *, tq, tks, tkw, n_top):
    R = B_REP
    M = R * tq
    dh = B_HEAD_DIM
    half = LANES // 2
    t0 = pl.program_id(2) * tq
    slc_state = (ms_ref, accs_ref, biass_ref, ((s0_ref, p0_ref, a0_ref), (s1_ref, p1_ref, a1_ref)))
    win_state = (mw_ref, accw_ref, biasw_ref, ((s2_ref, p2_ref, a2_ref), (s3_ref, p3_ref, a3_ref)))

    @pl.when(pl.program_id(2) == 0)
    def _():
        odd = pl.program_id(1) % 2 == 1
        low_lane = lax.broadcasted_iota(jnp.int32, (ks_ref.shape[1], LANES), 1) < half

        def halves(ref):
            x = ref[0].astype(F32)
            xr = pltpu.roll(x, half, 1)
            return jnp.where(odd, xr, x), jnp.where(odd, x, xr)

        lo, hi = halves(ks_ref)
        ksx_ref[:, :LANES] = jnp.where(low_lane, lo, hi).astype(BF16)
        ksx_ref[:, LANES:] = cs_ref[...]
        lo, hi = halves(kw_ref)
        kwx_ref[:, :LANES] = jnp.where(low_lane, lo, hi).astype(BF16)
        kwx_ref[:, LANES:] = cw_ref[...]
        ones = one_ref[...].astype(F32)
        vsx_ref[...] = jnp.where(low_lane, halves(vs_ref)[0], ones).astype(BF16)
        vwx_ref[...] = jnp.where(low_lane, halves(vw_ref)[0], ones).astype(BF16)

    qb = q_ref[0]
    low_q = lax.broadcasted_iota(jnp.int32, (tq, LANES), 1) < half
    for r in range(R):
        tile = qb[:, (r // 2) * LANES:(r // 2 + 1) * LANES]
        own = (tile, jnp.zeros_like(tile)) if r % 2 == 0 else (jnp.zeros_like(tile), tile)
        qx_ref[r * tq:(r + 1) * tq, :LANES] = jnp.where(low_q, *own)
        qx_ref[r * tq:(r + 1) * tq, LANES:] = jnp.broadcast_to(qf_ref[0, r], (tq, LANES))
    q = qx_ref[...]

    kc = kc_ref[0, 0]
    ncmp = kc.shape[0]
    s = _dot_nt(q, kc)
    tpos_c = t0 + (lax.broadcasted_iota(jnp.int32, (M, ncmp), 0) & (tq - 1))
    cend = lax.broadcasted_iota(jnp.int32, (M, ncmp), 1) * CMP_STRIDE + (CMP_LEN - 1)
    ok_c = tpos_c >= cend
    s = jnp.where(ok_c, s, NEG)
    p = jnp.where(ok_c, jnp.exp2(s - jnp.max(s, axis=-1, keepdims=True)), 0.0)
    dsum = jnp.sum(p, axis=-1, keepdims=True)
    p = p / jnp.where(dsum > 0, dsum, 1.0)
    o_cmp = _dot(p.astype(BF16), vc_ref[0, 0])

    psum = p[0:tq]
    for r in range(1, R):
        psum = psum + p[r * tq:(r + 1) * tq]
    p1, p2, p3 = _split3(psum)
    ov = ov_ref[...]
    imp = _dot(p1, ov) + _dot(p2, ov) + _dot(p3, ov)
    nsel = ks_ref.shape[1] // SEL_BLOCK
    nselp = max(nsel, 8)
    imp_t = imp.T[:nselp]
    jj = lax.broadcasted_iota(jnp.int32, (nselp, tq), 0)
    tt = t0 + lax.broadcasted_iota(jnp.int32, (nselp, tq), 1)
    cur = tt // SEL_BLOCK
    forced = (jj == 0) | (jj == cur) | (jj == cur - 1)
    iv = jnp.where(forced, FORCED_SCORE, imp_t)
    iv = jnp.where(jj * SEL_BLOCK <= tt, iv, -jnp.inf)
    rank = jnp.zeros((nselp, tq), F32)
    for j2 in range(nsel):
        rv = iv[j2:j2 + 1, :]
        rank = rank + jnp.where(jj > j2, jnp.where(rv >= iv, 1.0, 0.0), jnp.where(rv > iv, 1.0, 0.0))
    unsel = jnp.where(rank < n_top, 0.0, -SEL_MASK)
    if nselp < LANES:
        unsel = jnp.concatenate([unsel, jnp.zeros((LANES - nselp, tq), F32)], axis=0)
    qmask = unsel.T.astype(BF16)
    for r in range(R):
        qx_ref[r * tq:(r + 1) * tq, LANES:] = qmask + qf_ref[0, r]

    def tile_step(qq, k_ref, v_ref, k0, tk, state, slot, bias):
        m_ref, acc_ref, bias_ref, slots = state
        s_ref, p_ref, a_ref = slots[slot]
        kk = k_ref[pl.ds(k0, tk), :]
        vv = v_ref[pl.ds(k0, tk), :]
        if bias is not None:
            bias_ref[:, :tk] = bias
        s_ref[:, :tk] = _dot_nt(qq, kk)
        for r in range(R):
            for c0 in range(0, tq, NSA_ROWS):
                rows = slice(r * tq + c0, r * tq + c0 + NSA_ROWS)
                sc = s_ref[rows, :tk]
                if bias is not None:
                    sc = sc + bias_ref[c0:c0 + NSA_ROWS, :tk]
                m_old = m_ref[rows, :]
                m_new = jnp.maximum(m_old, jnp.max(sc, axis=-1, keepdims=True))
                a_ref[rows, :] = jnp.exp2(m_old - m_new)
                m_ref[rows, :] = m_new
                p_ref[rows, :tk] = jnp.exp2(sc - jnp.concatenate([m_new] * (tk // LANES), axis=1)).astype(BF16)
        acc_ref[...] = a_ref[...] * acc_ref[...] + _dot(p_ref[:, :tk], vv)

    def reset(state):
        state[0][...] = jnp.full_like(state[0], NEG)
        state[1][...] = jnp.zeros_like(state[1])

    def result(state):
        acc = state[1][...]
        return acc[:, :dh] / acc[:, dh:dh + 1]

    reset(slc_state)
    qx = qx_ref[...]
    n_full = t0 // tks

    blocks_per_tile = tks // SEL_BLOCK
    n_live = jnp.int32(0)
    for j in range(nsel // blocks_per_tile):
        sel_j = rank[j * blocks_per_tile:(j + 1) * blocks_per_tile, :] < n_top
        live = (jnp.max(jnp.where(sel_j, 1.0, 0.0)) > 0.5) & (j < n_full)
        live_ref[n_live] = j
        n_live = n_live + live.astype(jnp.int32)

    def slc_pair(i, carry):
        tile_step(qx, ksx_ref, vsx_ref, pl.multiple_of(live_ref[2 * i] * tks, tks), tks, slc_state, 0, None)
        tile_step(qx, ksx_ref, vsx_ref, pl.multiple_of(live_ref[2 * i + 1] * tks, tks), tks, slc_state, 1, None)
        return carry

    lax.fori_loop(0, n_live // 2, slc_pair, 0)

    @pl.when(n_live % 2 == 1)
    def _():
        tile_step(qx, ksx_ref, vsx_ref, pl.multiple_of(live_ref[n_live - 1] * tks, tks), tks, slc_state, 0, None)

    kd = pl.multiple_of(n_full * tks, tks)
    spos = kd + lax.broadcasted_iota(jnp.int32, (tq, tks), 1)
    tpos = t0 + lax.broadcasted_iota(jnp.int32, (tq, tks), 0)
    tile_step(qx, ksx_ref, vsx_ref, kd, tks, slc_state, 1, jnp.where(spos <= tpos, 0.0, NEG))
    reset(win_state)
    n_win = WINDOW // tkw + max(tq // tkw, 1)
    last = (t0 + tq - 1) // tkw
    for i in range(n_win):
        kw0 = (last - (n_win - 1) + i) * tkw
        spos = kw0 + lax.broadcasted_iota(jnp.int32, (tq, tkw), 1)
        dist = t0 + lax.broadcasted_iota(jnp.int32, (tq, tkw), 0) - spos
        bias = jnp.where((spos >= 0) & (dist >= 0) & (dist < WINDOW), 0.0, NEG)
        tile_step(q, kwx_ref, vwx_ref, pl.multiple_of(jnp.maximum(kw0, 0), tkw), tkw, win_state, i % 2, bias)
    o_slc = result(slc_state)
    o_win = result(win_state)

    gates = jax.nn.sigmoid(gl_ref[0, 0, pl.ds(pl.multiple_of(t0, tq), tq), :] + bg_ref[0])
    outs = []
    for r in range(R):
        rs = slice(r * tq, (r + 1) * tq)
        outs.append(gates[:, 3 * r:3 * r + 1] * o_cmp[rs] + gates[:, 3 * r + 1:3 * r + 2] * o_slc[rs]
                    + gates[:, 3 * r + 2:3 * r + 3] * o_win[rs])
    out_ref[0] = jnp.concatenate(outs, axis=1).astype(out_ref.dtype)


def nsa_attention(qp, q_feat, kc, vc, kvx, kv_off, slc_const, win_const, ones_col, gl, bg, ov):
    b, t, _ = qp.shape
    g, r = q_feat.shape[:2]
    dh = B_HEAD_DIM
    tq = min(NSA_TQ, t)
    tks = min(NSA_TK_SLC, t)
    tkw = min(NSA_TK_WIN, t)
    ncmp = kc.shape[2]
    n_top = min(SEL_TOPN, t // SEL_BLOCK)
    per_tile = LANES // dh
    assert t // SEL_BLOCK <= LANES // 2 and per_tile == 2 and g % per_tile == 0
    kv_spec = lambda c: pl.BlockSpec((1, t, LANES),
                                     lambda i, j, k: (i, 0, kv_off + c * (g // per_tile) + j // per_tile))
    const_spec = lambda a: pl.BlockSpec(a.shape, lambda i, j, k: (0,) * a.ndim)

    def branch_scratch(tk):
        slot = [pltpu.VMEM((r * tq, tk), F32), pltpu.VMEM((r * tq, tk), BF16), pltpu.VMEM((r * tq, LANES), F32)]
        return [pltpu.VMEM((r * tq, LANES), F32), pltpu.VMEM((r * tq, LANES), F32),
                pltpu.VMEM((tq, tk), F32)] + slot + slot

    return pl.pallas_call(
        functools.partial(_nsa_kernel, tq=tq, tks=tks, tkw=tkw, n_top=n_top),
        grid=(b, g, t // tq),
        in_specs=[pl.BlockSpec((1, tq, r * dh), lambda i, j, k: (i, k, j)),
                  pl.BlockSpec((1, r, 1, LANES), lambda i, j, k: (j, 0, 0, 0)),
                  pl.BlockSpec((1, 1, ncmp, 2 * LANES), lambda i, j, k: (i, j, 0, 0)),
                  pl.BlockSpec((1, 1, ncmp, dh), lambda i, j, k: (i, j, 0, 0)),
                  kv_spec(0), kv_spec(1), kv_spec(2), kv_spec(3),
                  const_spec(slc_const), const_spec(win_const), const_spec(ones_col),
                  pl.BlockSpec((1, 1, t, 3 * r), lambda i, j, k: (i, j, 0, 0)),
                  pl.BlockSpec((1, 1, 3 * r), lambda i, j, k: (j, 0, 0)),
                  const_spec(ov)],
        out_specs=pl.BlockSpec((1, tq, r * dh), lambda i, j, k: (i, k, j)),
        out_shape=jax.ShapeDtypeStruct((b, t, g * r * dh), BF16),
        scratch_shapes=[pltpu.VMEM((r * tq, 2 * LANES), BF16),
                        pltpu.VMEM((t, 2 * LANES), BF16),
                        pltpu.VMEM((t, LANES), BF16),
                        pltpu.VMEM((t, 2 * LANES), BF16),
                        pltpu.VMEM((t, LANES), BF16),
                        ] + branch_scratch(tks) + branch_scratch(tkw)
                       + [pltpu.SMEM((t // tks + 1,), jnp.int32)],
        compiler_params=_cparams("arbitrary", "arbitrary", "arbitrary"),
        name="nsa_attention",
    )(qp, q_feat, kc, vc, kvx, kvx, kvx, kvx, slc_const, win_const, ones_col, gl, bg, ov)


def _alibi_slopes(n):
    return np.power(2.0, -8.0 * np.arange(1, n + 1) / n).astype(np.float32)


def _np_split3(a):
    a = np.asarray(a, np.float32)
    out = []
    r = a
    for _ in range(3):
        p = r.astype(BF16).astype(np.float32)
        out.append(p)
        r = (r - p).astype(np.float32)
    return out


def _pos_features(pos, width):
    hi = (pos // 64).astype(np.float32)
    lo = (pos % 64).astype(np.float32)
    f = np.zeros((pos.shape[0], width), np.float32)
    f[:, 0:3] = hi[:, None]
    f[:, 3:6] = lo[:, None]
    return f


def _slope_features(width):
    s1, s2, s3 = _np_split3((_alibi_slopes(B_HEADS).astype(np.float64) * LOG2E).astype(np.float32))
    f = np.zeros((B_HEADS, width), np.float32)
    for i, s in enumerate((s1, s2, s3)):
        f[:, i] = 64.0 * s
        f[:, 3 + i] = s
    return f


META_I1, META_I2, META_W1, META_W2 = 0, 1, 2, 3


def _router_kernel(x_ref, h_ref, wo_ref, g_ref, w_ref, xo_ref, xn_ref, meta_ref, pos_ref, cnt_ref, run_ref):
    @pl.when(pl.program_id(0) == 0)
    def _():
        run_ref[...] = jnp.zeros_like(run_ref)

    xf = x_ref[...] + _dot(h_ref[...], wo_ref[...])
    xo_ref[...] = xf
    xn = _rms(xf, g_ref[...])
    xn_ref[...] = xn
    logits = _dot_hi_lo(xn, w_ref[...])
    tm = logits.shape[0]
    lane = lax.broadcasted_iota(jnp.int32, logits.shape, 1)
    logits = jnp.where(lane < N_EXPERTS, logits, -jnp.inf)
    m1 = jnp.max(logits, axis=-1, keepdims=True)
    i1 = jnp.min(jnp.where(logits == m1, lane, LANES), axis=-1, keepdims=True)
    rest = jnp.where(lane == i1, -jnp.inf, logits)
    m2 = jnp.max(rest, axis=-1, keepdims=True)
    i2 = jnp.min(jnp.where(rest == m2, lane, LANES), axis=-1, keepdims=True)
    e2 = jnp.exp(m2 - m1)
    w1 = 1.0 / (1.0 + e2)
    meta = jnp.where(lane == META_I1, i1.astype(F32), 0.0)
    meta = jnp.where(lane == META_I2, i2.astype(F32), meta)
    meta = jnp.where(lane == META_W1, w1, meta)
    meta_ref[...] = jnp.where(lane == META_W2, e2 * w1, meta)

    sel = jnp.where((lane == i1) | (lane == i2), 1.0, 0.0)
    row = lax.broadcasted_iota(jnp.int32, (tm, tm), 0)
    col = lax.broadcasted_iota(jnp.int32, (tm, tm), 1)
    before = jnp.where(col < row, 1.0, 0.0).astype(BF16)
    run = run_ref[...]
    pos_ref[...] = run + _dot(before, sel.astype(BF16))
    run = run + jnp.sum(sel, axis=0, keepdims=True)
    run_ref[...] = run
    cnt_ref[...] = run


def moe_router(x, h, w_out, g, w_router, *, tm=512):
    n, d = x.shape
    dh_in = h.shape[1]
    tm = min(tm, n)
    w = _hi_lo_weight(w_router)
    row_spec = pl.BlockSpec((tm, LANES), lambda i: (i, 0))
    return pl.pallas_call(
        _router_kernel,
        grid=(n // tm,),
        in_specs=[pl.BlockSpec((tm, d), lambda i: (i, 0)),
                  pl.BlockSpec((tm, dh_in), lambda i: (i, 0)),
                  pl.BlockSpec((dh_in, d), lambda i: (0, 0)),
                  pl.BlockSpec((1, d), lambda i: (0, 0)),
                  pl.BlockSpec((d, 2 * LANES), lambda i: (0, 0))],
        out_specs=[pl.BlockSpec((tm, d), lambda i: (i, 0)), pl.BlockSpec((tm, d), lambda i: (i, 0)),
                   row_spec, row_spec,
                   pl.BlockSpec((1, LANES), lambda i: (0, 0))],
        out_shape=[jax.ShapeDtypeStruct((n, d), F32), jax.ShapeDtypeStruct((n, d), F32),
                   jax.ShapeDtypeStruct((n, LANES), F32),
                   jax.ShapeDtypeStruct((n, LANES), F32), jax.ShapeDtypeStruct((1, LANES), F32)],
        scratch_shapes=[pltpu.VMEM((1, LANES), F32)],
        compiler_params=_cparams("arbitrary"),
        name="moe_router",
    )(x, h, w_out, g.reshape(1, d), w)


MOE_TILE = 512


def _moe_kernel(te_ref, ok_ref, idx0_ref, idxn_ref, x_hbm, wg_ref, wu_ref, wd_ref, o_ref, xbuf, xb_ref, sem):
    i = pl.program_id(0)
    j = pl.program_id(1)
    nt = pl.num_programs(0)
    nf = pl.num_programs(1)
    tm = xb_ref.shape[0]
    per_step = tm // nf
    slot = i % 2

    def row_copy(idx_ref, r, s):
        return pltpu.make_async_copy(x_hbm.at[pl.ds(idx_ref[0, 0, r], 1), :], xbuf.at[s, pl.ds(r, 1), :], sem.at[s])

    @pl.when((i == 0) & (j == 0))
    def _():
        def start0(r, c):
            row_copy(idx0_ref, r, 0).start()
            return c
        lax.fori_loop(0, tm, start0, 0)

    @pl.when(j == 0)
    def _():
        for r in range(tm):
            row_copy(idx0_ref, r, slot).wait()
        xb_ref[...] = xbuf[slot].astype(BF16)
        o_ref[...] = jnp.zeros_like(o_ref)

    def prefetch():
        base = j * per_step
        for r in range(per_step):
            row_copy(idxn_ref, base + r, 1 - slot).start()

    def compute():
        xb = xb_ref[...]
        a = (_silu(_dot(xb, wg_ref[0])) * _dot(xb, wu_ref[0])).astype(BF16)
        o_ref[...] += _dot(a, wd_ref[0])

    has_next = i + 1 < nt
    ok = ok_ref[i] > 0

    @pl.when(has_next & ok)
    def _():
        prefetch()
        compute()

    @pl.when(has_next & jnp.logical_not(ok))
    def _():
        prefetch()

    @pl.when(jnp.logical_not(has_next) & ok)
    def _():
        compute()


def moe_experts(x, row_tok, tile_expert, tile_ok, w_gu, w_down, *, tf=1792):
    p = row_tok.shape[0]
    d = x.shape[1]
    ne, f, _ = w_down.shape
    tm = MOE_TILE
    nf = f // tf
    nt = p // tm
    idx = row_tok.reshape(nt, 1, tm)
    grid_spec = pltpu.PrefetchScalarGridSpec(
        num_scalar_prefetch=2,
        grid=(nt, nf),
        in_specs=[pl.BlockSpec((1, 1, tm), lambda i, j, te, ok: (0, 0, 0), memory_space=pltpu.SMEM),
                  pl.BlockSpec((1, 1, tm), lambda i, j, te, ok: (jnp.minimum(i + 1, nt - 1), 0, 0),
                               memory_space=pltpu.SMEM),
                  pl.BlockSpec(memory_space=pl.ANY),
                  pl.BlockSpec((1, d, tf), lambda i, j, te, ok: (te[i], 0, j)),
                  pl.BlockSpec((1, d, tf), lambda i, j, te, ok: (te[i], 0, j + nf)),
                  pl.BlockSpec((1, tf, d), lambda i, j, te, ok: (te[i], j, 0))],
        out_specs=pl.BlockSpec((tm, d), lambda i, j, te, ok: (i, 0)),
        scratch_shapes=[pltpu.VMEM((2, tm, d), F32), pltpu.VMEM((tm, d), BF16),
                        pltpu.SemaphoreType.DMA((2,))])
    return pl.pallas_call(
        _moe_kernel,
        grid_spec=grid_spec,
        out_shape=jax.ShapeDtypeStruct((p, d), F32),
        compiler_params=_cparams("arbitrary", "arbitrary"),
        name="moe_experts",
    )(tile_expert, tile_ok, idx, idx, x, w_gu, w_gu, w_down)


def _combine_kernel(d0_ref, dn_ref, x_ref, meta_ref, y_ref, gf_ref, o_ref, ybuf, sem):
    i = pl.program_id(0)
    nt = pl.num_programs(0)
    rows = o_ref.shape[0]
    slot = i % 2

    def row_copy(d_ref, k, r, s):
        return pltpu.make_async_copy(y_ref.at[pl.ds(d_ref[0, k, r], 1), :], ybuf.at[s, k, pl.ds(r, 1), :], sem.at[s])

    @pl.when(i == 0)
    def _():
        def start0(r, c):
            row_copy(d0_ref, 0, r, 0).start()
            row_copy(d0_ref, 1, r, 0).start()
            return c
        lax.fori_loop(0, rows, start0, 0)

    for r in range(rows):
        row_copy(d0_ref, 0, r, slot).wait()
        row_copy(d0_ref, 1, r, slot).wait()

    @pl.when(i + 1 < nt)
    def _():
        for r in range(rows):
            row_copy(dn_ref, 0, r, 1 - slot).start()
            row_copy(dn_ref, 1, r, 1 - slot).start()

    meta = meta_ref[...]
    w1 = meta[:, META_W1:META_W1 + 1]
    w2 = meta[:, META_W2:META_W2 + 1]
    o_ref[...] = _rms(x_ref[...] + w1 * ybuf[slot, 0] + w2 * ybuf[slot, 1], gf_ref[...])


def moe_combine(x, meta, ys, d1, d2, g_final, *, tc=256):
    n, d = x.shape
    tc = min(tc, n)
    nt = n // tc
    dd = jnp.stack([d1.reshape(nt, tc), d2.reshape(nt, tc)], axis=1)
    return pl.pallas_call(
        _combine_kernel,
        grid=(nt,),
        in_specs=[pl.BlockSpec((1, 2, tc), lambda i: (0, 0, 0), memory_space=pltpu.SMEM),
                  pl.BlockSpec((1, 2, tc), lambda i: (jnp.minimum(i + 1, nt - 1), 0, 0), memory_space=pltpu.SMEM),
                  pl.BlockSpec((tc, d), lambda i: (i, 0)),
                  pl.BlockSpec((tc, LANES), lambda i: (i, 0)),
                  pl.BlockSpec(memory_space=pl.ANY),
                  pl.BlockSpec((1, d), lambda i: (0, 0))],
        out_specs=pl.BlockSpec((tc, d), lambda i: (i, 0)),
        out_shape=jax.ShapeDtypeStruct((n, d), F32),
        scratch_shapes=[pltpu.VMEM((2, 2, tc, d), F32), pltpu.SemaphoreType.DMA((2,))],
        compiler_params=_cparams("arbitrary"),
        name="moe_combine",
    )(dd, dd, x, meta, ys, g_final.reshape(1, d))


def kernel(x, norm_mix, norm_ffn, a_w_in, a_b_gate, a_conv, a_norm_h, a_w_out, norm_kv, b_w_kv,
           b_cmp_pos, b_cmp_w1, b_cmp_w2, b_w_q, b_b_gate, b_w_out, f_w_gu, f_w_down,
           m_router, m_w_gu, m_w_down, norm_final):
    B, T, D = x.shape
    N = B * T
    G, R, dh = B_KV_GROUPS, B_REP, B_HEAD_DIM
    xs = x.reshape(N, D)

    inner4 = a_w_in.shape[2] - 2 * A_HEADS
    w_in = a_w_in[0]
    proj, gcol = norm_matmul(xs, norm_mix[0], w_in[:, :inner4].astype(BF16), w_side=w_in[:, inner4:])
    hs = mlstm_core(proj, gcol, a_b_gate[0], a_conv[0], a_norm_h[0], B, T)
    xs = ffn_dense(xs, hs, a_w_out[0].astype(BF16), norm_ffn[0], f_w_gu[0].astype(BF16), f_w_down[0].astype(BF16))

    hd = B_HEADS * dh
    feat_w = LANES - dh
    ncmp = T // CMP_STRIDE

    cmp_cols = 2 * G * dh
    kvx = norm_matmul(xs, norm_kv, b_w_kv.astype(BF16), tn=b_w_kv.shape[1] // 2)
    w_q = b_w_q[0]
    qp, gl = norm_matmul(xs, norm_mix[1], (w_q[:, :hd] * (dh ** -0.5 * LOG2E)).astype(BF16), w_side=w_q[:, hd:])

    kvt = kvx[:, :cmp_cols].reshape(B, T, 2 * G, dh).transpose(0, 2, 1, 3)
    pos = b_cmp_pos.transpose(1, 0, 2).reshape(2, 1, CMP_LEN * dh).astype(BF16)
    kvc = nsa_compress(kvt.reshape(B, 2 * G, ncmp, CMP_STRIDE * dh), pos,
                       b_cmp_w1.astype(BF16), b_cmp_w2.astype(BF16))

    def hi_lanes(f, lo=None):
        lo = np.zeros((f.shape[0], dh), np.float32) if lo is None else lo
        return jnp.asarray(np.concatenate([lo, f], axis=1), BF16)

    key_pos = _pos_features(np.arange(T), feat_w)
    blk_onehot = (np.arange(T)[:, None] // SEL_BLOCK == np.arange(dh)[None, :]).astype(np.float32)
    slc_const = hi_lanes(key_pos, blk_onehot)
    win_const = hi_lanes(key_pos)
    cmp_const = hi_lanes(_pos_features(np.arange(ncmp) * CMP_STRIDE + CMP_LEN - 1, feat_w))
    q_feat = hi_lanes(_slope_features(feat_w)).reshape(G, R, 1, LANES)
    ones_col = np.zeros((1, feat_w), np.float32)
    ones_col[0, 0] = 1.0
    ones_col = hi_lanes(ones_col)
    kcb = kvc[:, :G].astype(BF16)
    kc = jnp.concatenate([kcb, kcb, jnp.broadcast_to(cmp_const, (B, G, ncmp, LANES))], axis=-1)
    vc = kvc[:, G:].astype(BF16)
    glt = gl[:, :3 * B_HEADS].reshape(B, T, G, 3 * R).transpose(0, 2, 1, 3)
    bg = b_b_gate[0].reshape(G, 1, 3 * R)

    nsel = T // SEL_BLOCK
    ci = np.arange(ncmp)[:, None] * CMP_STRIDE
    sj = np.arange(LANES)[None, :] * SEL_BLOCK
    ov = ((ci < sj + SEL_BLOCK) & (ci + CMP_LEN > sj) & (np.arange(LANES)[None, :] < nsel)
          & (np.arange(ncmp)[:, None] < ncmp - 1))
    ov = jnp.asarray(ov.astype(np.float32), BF16)

    oa = nsa_attention(qp.reshape(B, T, hd), q_feat, kc, vc, kvx.reshape(B, T, -1),
                       cmp_cols // LANES, slc_const, win_const, ones_col, glt, bg, ov)

    xs, xn, meta, pos, cnt = moe_router(xs, oa.reshape(N, hd), b_w_out[0].astype(BF16), norm_ffn[1], m_router[0])
    ne = N_EXPERTS
    p_rows = 2 * N + ne * MOE_TILE
    i1 = meta[:, META_I1].astype(jnp.int32)
    i2 = meta[:, META_I2].astype(jnp.int32)
    counts = cnt[0, :ne].astype(jnp.int32)
    padded = (counts + MOE_TILE - 1) // MOE_TILE * MOE_TILE
    seg_end = jnp.cumsum(padded)
    seg_start = seg_end - padded
    pos8 = pos[:, :ne].astype(jnp.int32)
    d1 = seg_start[i1] + jnp.take_along_axis(pos8, i1[:, None], axis=1)[:, 0]
    d2 = seg_start[i2] + jnp.take_along_axis(pos8, i2[:, None], axis=1)[:, 0]
    tok = jnp.arange(N, dtype=jnp.int32)
    row_tok = jnp.zeros((p_rows,), jnp.int32).at[jnp.concatenate([d1, d2])].set(jnp.concatenate([tok, tok]))
    tile_start = jnp.arange(p_rows // MOE_TILE, dtype=jnp.int32) * MOE_TILE
    tile_expert = jnp.minimum(jnp.sum(tile_start[:, None] >= seg_end[None, :], axis=1), ne - 1).astype(jnp.int32)
    tile_ok = (tile_start < seg_end[ne - 1]).astype(jnp.int32)

    ys = moe_experts(xn, row_tok, tile_expert, tile_ok, m_w_gu[0].astype(BF16), m_w_down[0].astype(BF16))
    out = moe_combine(xs, meta, ys, d1, d2, norm_final)
    return out.reshape(B, T, D)
```

```python
import functools

import numpy as np
import jax
import jax.numpy as jnp
from jax import lax
from jax.experimental import pallas as pl
from jax.experimental.pallas import tpu as pltpu

F32 = jnp.float32
BF16 = jnp.bfloat16

RMS_EPS = 1e-6
A_HEADS = 4
A_CONV = 4
B_HEADS = 16
B_KV_GROUPS = 4
B_REP = B_HEADS // B_KV_GROUPS
B_HEAD_DIM = 64
CMP_LEN = 32
CMP_STRIDE = 16
SEL_BLOCK = 64
SEL_TOPN = 16
WINDOW = 512
FORCED_SCORE = 1e4
N_EXPERTS = 8

LANES = 128
V7X_VMEM_BYTES = 64 * 1024 * 1024
VMEM_LIMIT = V7X_VMEM_BYTES - 8 * 1024 * 1024
NEG = -1e30


def _cparams(*sem):
    return pltpu.CompilerParams(dimension_semantics=sem, vmem_limit_bytes=VMEM_LIMIT)


def _dot(a, b):
    return jnp.dot(a, b, preferred_element_type=F32)


def _dot_nt(a, b):
    return lax.dot_general(a, b, (((1,), (1,)), ((), ())), preferred_element_type=F32)


def _dot_tn(a, b):
    return lax.dot_general(a, b, (((0,), (0,)), ((), ())), preferred_element_type=F32)


def _split3(a):
    a1 = a.astype(BF16)
    r1 = a - a1.astype(F32)
    a2 = r1.astype(BF16)
    a3 = (r1 - a2.astype(F32)).astype(BF16)
    return a1, a2, a3


def _hi_lo_weight(w):
    w = jnp.pad(w, ((0, 0), (0, LANES - w.shape[1])))
    hi = w.astype(BF16)
    lo = (w - hi.astype(F32)).astype(BF16)
    return jnp.concatenate([hi, lo], axis=1)


def _dot_hi_lo(a, w_hl):
    a1 = a.astype(BF16)
    a2 = (a - a1.astype(F32)).astype(BF16)
    s = _dot(a1, w_hl) + _dot(a2, w_hl)
    return s[:, :LANES] + s[:, LANES:]


def _rms(xf, g):
    return xf * lax.rsqrt(jnp.mean(xf * xf, axis=-1, keepdims=True) + RMS_EPS) * g


def _silu(x):
    return x * jax.nn.sigmoid(x)


def _log_sigmoid(x):
    return jnp.minimum(x, 0.0) - jnp.log(1.0 + jnp.exp(-jnp.abs(x)))


def _norm_matmul_kernel(*refs, side):
    if side:
        x_ref, g_ref, w_ref, ws_ref, o_ref, os_ref, xn_ref = refs
    else:
        x_ref, g_ref, w_ref, o_ref, xn_ref = refs

    @pl.when(pl.program_id(1) == 0)
    def _():
        xn = _rms(x_ref[...], g_ref[...])
        xn_ref[...] = xn.astype(xn_ref.dtype)
        if side:
            os_ref[...] = _dot_hi_lo(xn, ws_ref[...])

    o_ref[...] = _dot(xn_ref[...], w_ref[...]).astype(o_ref.dtype)


def norm_matmul(x, g, w, *, w_side=None, tm=1024, tn=1024):
    n, d = x.shape
    dout = w.shape[1]
    tm = min(tm, n)
    tn = min(tn, dout)
    side = w_side is not None
    in_specs = [pl.BlockSpec((tm, d), lambda i, j: (i, 0)),
                pl.BlockSpec((1, d), lambda i, j: (0, 0)),
                pl.BlockSpec((d, tn), lambda i, j: (0, j))]
    out_specs = [pl.BlockSpec((tm, tn), lambda i, j: (i, j))]
    out_shape = [jax.ShapeDtypeStruct((n, dout), BF16)]
    args = [x, g.reshape(1, d), w]
    if side:
        in_specs.append(pl.BlockSpec((d, 2 * LANES), lambda i, j: (0, 0)))
        out_specs.append(pl.BlockSpec((tm, LANES), lambda i, j: (i, 0)))
        out_shape.append(jax.ShapeDtypeStruct((n, LANES), F32))
        args.append(_hi_lo_weight(w_side))
    out = pl.pallas_call(
        functools.partial(_norm_matmul_kernel, side=side),
        grid=(n // tm, dout // tn),
        in_specs=in_specs,
        out_specs=out_specs,
        out_shape=out_shape,
        scratch_shapes=[pltpu.VMEM((tm, d), BF16)],
        compiler_params=_cparams("parallel", "arbitrary"),
        name="norm_matmul_side" if side else "norm_matmul",
    )(*args)
    return out if side else out[0]


MLSTM_CHUNK = 256


def _mlstm_kernel(qk_ref, v_ref, o_ref, gcol_ref, grow_ref, bcol_ref, brow_ref, convw_ref, gout_ref,
                  out_ref, ct_ref, n_ref, m_ref, xs_ref):
    L = qk_ref.shape[0]
    H = A_HEADS
    inner = v_ref.shape[1]
    dh = inner // H

    @pl.when(pl.program_id(1) == 0)
    def _():
        ct_ref[...] = jnp.zeros_like(ct_ref)
        n_ref[...] = jnp.zeros_like(n_ref)
        m_ref[...] = jnp.zeros_like(m_ref)
        xs_ref[...] = jnp.zeros_like(xs_ref)

    row = lax.broadcasted_iota(jnp.int32, (L, L), 0)
    col = lax.broadcasted_iota(jnp.int32, (L, L), 1)
    causal = col <= row
    tril = jnp.where(causal, 1.0, 0.0).astype(BF16)
    triu = jnp.where(row <= col, 1.0, 0.0).astype(BF16)

    gc = gcol_ref[...] + brow_ref[...]
    gr = grow_ref[...] + bcol_ref[...]
    lfc1, lfc2, lfc3 = _split3(_log_sigmoid(gc))
    lfr1, lfr2, lfr3 = _split3(_log_sigmoid(gr))
    b_c = _dot(tril, lfc1) + _dot(tril, lfc2) + _dot(tril, lfc3)
    b_r = _dot(lfr1, triu) + _dot(lfr2, triu) + _dot(lfr3, triu)

    halo = xs_ref.shape[0] - L
    xs_ref[halo:, :] = qk_ref[...].astype(F32)

    shift = [jnp.where(row - col == s, 1.0, 0.0).astype(BF16) for s in range(1, A_CONV)]
    rowh = lax.broadcasted_iota(jnp.int32, (halo, dh), 0)

    def conv(cols):
        w = convw_ref[:, cols]
        x = xs_ref[halo:, cols]
        tail = xs_ref[:halo, cols]
        y = x * w[A_CONV - 1:A_CONV, :]
        head = jnp.zeros((halo, dh), F32)
        for s in range(1, A_CONV):
            ws = w[A_CONV - 1 - s:A_CONV - s, :]
            y = y + _dot(shift[s - 1], (x * ws).astype(BF16))
            head = head + jnp.where(rowh < s, pltpu.roll(tail, s, 0), 0.0) * ws
        return jnp.concatenate([y[:halo] + head, y[halo:]], axis=0)

    for h in range(H):
        hs = slice(h * dh, (h + 1) * dh)
        ks = slice(inner + h * dh, inner + (h + 1) * dh)
        q = conv(hs)
        k = conv(ks) * (dh ** -0.5)
        v = v_ref[:, hs]
        qb = q.astype(BF16)
        kb = k.astype(BF16)

        li_c = gc[:, h:h + 1]
        bc = b_c[:, H + h:H + h + 1]
        li_r = gr[h:h + 1, :]
        br = b_r[H + h:H + h + 1, :]
        m_prev = m_ref[h:h + 1, 0:1]

        d = jnp.where(causal, bc - br + li_r, -jnp.inf)
        inter = bc + m_prev
        m_t = jnp.maximum(inter, jnp.max(d, axis=-1, keepdims=True))
        w_inter = jnp.exp(inter - m_t)
        s = _dot_nt(qb, kb) * jnp.exp(d - m_t)
        ct = ct_ref[h]
        num = _dot(s.astype(BF16), v) + w_inter * _dot(qb, ct.astype(BF16))
        den = jnp.sum(s, axis=-1, keepdims=True) + w_inter * jnp.sum(q * n_ref[h], axis=-1, keepdims=True)
        hh = num / jnp.maximum(jnp.abs(den), jnp.exp(-m_t))
        hh = hh * lax.rsqrt(jnp.mean(hh * hh, axis=-1, keepdims=True) + RMS_EPS)
        out_ref[:, hs] = (hh * gout_ref[:, hs] * jax.nn.sigmoid(o_ref[:, hs].astype(F32))).astype(out_ref.dtype)

        b_last = bc[L - 1:L, :]
        g = b_last - bc + li_c
        m_new = jnp.maximum(b_last + m_prev, jnp.max(g, axis=0, keepdims=True))
        a_prev = jnp.exp(b_last + m_prev - m_new)
        a_s = jnp.exp(g - m_new)
        ct_ref[h] = a_prev * ct + _dot_tn(kb, (v.astype(F32) * a_s).astype(BF16))
        n_ref[h] = a_prev * n_ref[h] + jnp.sum(k * a_s, axis=0, keepdims=True)
        m_ref[h:h + 1, :] = jnp.broadcast_to(m_new, (1, LANES))

    xs_ref[:halo, :] = xs_ref[L:, :]


def mlstm_core(proj, gcol, b_gate, conv_w, g_out, batch, seq):
    n = proj.shape[0]
    inner = proj.shape[1] // 4
    H = A_HEADS
    dh = inner // H
    L = min(MLSTM_CHUNK, seq)
    nc = seq // L
    grow = gcol[:, :2 * H].T
    brow = jnp.pad(b_gate, (0, LANES - 2 * H)).reshape(1, LANES)
    bcol = b_gate.reshape(2 * H, 1)
    return pl.pallas_call(
        _mlstm_kernel,
        grid=(batch, nc),
        in_specs=[pl.BlockSpec((L, 2 * inner), lambda b, c: (b * nc + c, 0)),
                  pl.BlockSpec((L, inner), lambda b, c: (b * nc + c, 2)),
                  pl.BlockSpec((L, inner), lambda b, c: (b * nc + c, 3)),
                  pl.BlockSpec((L, LANES), lambda b, c: (b * nc + c, 0)),
                  pl.BlockSpec((2 * H, L), lambda b, c: (0, b * nc + c)),
                  pl.BlockSpec((2 * H, 1), lambda b, c: (0, 0)),
                  pl.BlockSpec((1, LANES), lambda b, c: (0, 0)),
                  pl.BlockSpec((A_CONV, 2 * inner), lambda b, c: (0, 0)),
                  pl.BlockSpec((1, inner), lambda b, c: (0, 0))],
        out_specs=pl.BlockSpec((L, inner), lambda b, c: (b * nc + c, 0)),
        out_shape=jax.ShapeDtypeStruct((n, inner), BF16),
        scratch_shapes=[pltpu.VMEM((H, dh, dh), F32),
                        pltpu.VMEM((H, 1, dh), F32),
                        pltpu.VMEM((8, LANES), F32),
                        pltpu.VMEM((8 + L, 2 * inner), F32)],
        compiler_params=_cparams("parallel", "arbitrary"),
        name="mlstm_core",
    )(proj, proj, proj, gcol, grow, bcol, brow, conv_w, g_out.reshape(1, inner))


def _ffn_kernel(x_ref, h_ref, wo_ref, g_ref, wg_ref, wu_ref, wd_ref, o_ref, xn_ref):
    @pl.when(pl.program_id(1) == 0)
    def _():
        xf = x_ref[...] + _dot(h_ref[...], wo_ref[...])
        xn_ref[...] = _rms(xf, g_ref[...]).astype(BF16)
        o_ref[...] = xf

    xn = xn_ref[...]
    a = (_silu(_dot(xn, wg_ref[...])) * _dot(xn, wu_ref[...])).astype(BF16)
    o_ref[...] += _dot(a, wd_ref[...])


def ffn_dense(x, h, w_out, g, w_gu, w_down, *, tm=512, tf=1408):
    n, d = x.shape
    f = w_down.shape[0]
    dh_in = h.shape[1]
    tm = min(tm, n)
    nf = f // tf
    return pl.pallas_call(
        _ffn_kernel,
        grid=(n // tm, nf),
        in_specs=[pl.BlockSpec((tm, d), lambda i, j: (i, 0)),
                  pl.BlockSpec((tm, dh_in), lambda i, j: (i, 0)),
                  pl.BlockSpec((dh_in, d), lambda i, j: (0, 0)),
                  pl.BlockSpec((1, d), lambda i, j: (0, 0)),
                  pl.BlockSpec((d, tf), lambda i, j: (0, j)),
                  pl.BlockSpec((d, tf), lambda i, j: (0, j + nf)),
                  pl.BlockSpec((tf, d), lambda i, j: (j, 0))],
        out_specs=pl.BlockSpec((tm, d), lambda i, j: (i, 0)),
        out_shape=jax.ShapeDtypeStruct((n, d), F32),
        scratch_shapes=[pltpu.VMEM((tm, d), BF16)],
        compiler_params=_cparams("parallel", "arbitrary"),
        name="ffn_dense",
    )(x, h, w_out, g.reshape(1, d), w_gu, w_gu, w_down)


def _compress_kernel(r_ref, pos_ref, w1_ref, w2_ref, o_ref):
    r = r_ref[0, 0]
    w1 = w1_ref[0]
    half = r.shape[1]
    nc = r.shape[0]
    lo = _dot(r, w1[:half])
    hi = _dot(r, w1[half:])
    hid = lo + pltpu.roll(hi, nc - 1, 0) + _dot(pos_ref[0], w1)
    o_ref[0, 0] = _dot(_silu(hid).astype(BF16), w2_ref[0]).astype(o_ref.dtype)


def nsa_compress(r, pos, w1, w2):
    b, c2, nc, half = r.shape
    g = c2 // 2
    hidden = w1.shape[2]
    dh = w2.shape[2]
    return pl.pallas_call(
        _compress_kernel,
        grid=(b, c2),
        in_specs=[pl.BlockSpec((1, 1, nc, half), lambda i, j: (i, j, 0, 0)),
                  pl.BlockSpec((1, 1, 2 * half), lambda i, j: (j // g, 0, 0)),
                  pl.BlockSpec((1, 2 * half, hidden), lambda i, j: (j // g, 0, 0)),
                  pl.BlockSpec((1, hidden, dh), lambda i, j: (j // g, 0, 0))],
        out_specs=pl.BlockSpec((1, 1, nc, dh), lambda i, j: (i, j, 0, 0)),
        out_shape=jax.ShapeDtypeStruct((b, c2, nc, dh), F32),
        compiler_params=_cparams("parallel", "parallel"),
        name="nsa_compress",
    )(r, pos, w1, w2)


NSA_TQ = 256
NSA_TK_SLC = 512
NSA_TK_WIN = 512
NSA_ROWS = 32
SEL_MASK = 2.0 ** 14
LOG2E = float(np.log2(np.e))


def _nsa_kernel(q_ref, qf_ref, kc_ref, vc_ref, ks_ref, vs_ref, kw_ref, vw_ref, cs_ref, cw_ref, one_ref,
                gl_ref, bg_ref, ov_ref,
                out_ref, qx_ref, ksx_ref, vsx_ref, kwx_ref, vwx_ref,
                ms_ref, accs_ref, biass_ref, s0_ref, p0_ref, a0_ref, s1_ref, p1_ref, a1_ref,
                mw_ref, accw_ref, biasw_ref, s2_ref, p2_ref, a2_ref, s3_ref, p3_ref, a3_ref, live_ref,
                *, tq, tks, tkw, n_top):
    R = B_REP
    M = R * tq
    dh = B_HEAD_DIM
    half = LANES // 2
    t0 = pl.program_id(2) * tq
    slc_state = (ms_ref, accs_ref, biass_ref, ((s0_ref, p0_ref, a0_ref), (s1_ref, p1_ref, a1_ref)))
    win_state = (mw_ref, accw_ref, biasw_ref, ((s2_ref, p2_ref, a2_ref), (s3_ref, p3_ref, a3_ref)))

    @pl.when(pl.program_id(2) == 0)
    def _():
        odd = pl.program_id(1) % 2 == 1
        low_lane = lax.broadcasted_iota(jnp.int32, (ks_ref.shape[1], LANES), 1) < half

        def halves(ref):
            x = ref[0].astype(F32)
            xr = pltpu.roll(x, half, 1)
            return jnp.where(odd, xr, x), jnp.where(odd, x, xr)

        lo, hi = halves(ks_ref)
        ksx_ref[:, :LANES] = jnp.where(low_lane, lo, hi).astype(BF16)
        ksx_ref[:, LANES:] = cs_ref[...]
        lo, hi = halves(kw_ref)
        kwx_ref[:, :LANES] = jnp.where(low_lane, lo, hi).astype(BF16)
        kwx_ref[:, LANES:] = cw_ref[...]
        ones = one_ref[...].astype(F32)
        vsx_ref[...] = jnp.where(low_lane, halves(vs_ref)[0], ones).astype(BF16)
        vwx_ref[...] = jnp.where(low_lane, halves(vw_ref)[0], ones).astype(BF16)

    qb = q_ref[0]
    low_q = lax.broadcasted_iota(jnp.int32, (tq, LANES), 1) < half
    for r in range(R):
        tile = qb[:, (r // 2) * LANES:(r // 2 + 1) * LANES]
        own = (tile, jnp.zeros_like(tile)) if r % 2 == 0 else (jnp.zeros_like(tile), tile)
        qx_ref[r * tq:(r + 1) * tq, :LANES] = jnp.where(low_q, *own)
        qx_ref[r * tq:(r + 1) * tq, LANES:] = jnp.broadcast_to(qf_ref[0, r], (tq, LANES))
    q = qx_ref[...]

    kc = kc_ref[0, 0]
    ncmp = kc.shape[0]
    s = _dot_nt(q, kc)
    tpos_c = t0 + (lax.broadcasted_iota(jnp.int32, (M, ncmp), 0) & (tq - 1))
    cend = lax.broadcasted_iota(jnp.int32, (M, ncmp), 1) * CMP_STRIDE + (CMP_LEN - 1)
    ok_c = tpos_c >= cend
    s = jnp.where(ok_c, s, NEG)
    p = jnp.where(ok_c, jnp.exp2(s - jnp.max(s, axis=-1, keepdims=True)), 0.0)
    dsum = jnp.sum(p, axis=-1, keepdims=True)
    p = p / jnp.where(dsum > 0, dsum, 1.0)
    o_cmp = _dot(p.astype(BF16), vc_ref[0, 0])

    psum = p[0:tq]
    for r in range(1, R):
        psum = psum + p[r * tq:(r + 1) * tq]
    p1, p2, p3 = _split3(psum)
    ov = ov_ref[...]
    imp = _dot(p1, ov) + _dot(p2, ov) + _dot(p3, ov)
    nsel = ks_ref.shape[1] // SEL_BLOCK
    nselp = max(nsel, 8)
    imp_t = imp.T[:nselp]
    jj = lax.broadcasted_iota(jnp.int32, (nselp, tq), 0)
    tt = t0 + lax.broadcasted_iota(jnp.int32, (nselp, tq), 1)
    cur = tt // SEL_BLOCK
    forced = (jj == 0) | (jj == cur) | (jj == cur - 1)
    iv = jnp.where(forced, FORCED_SCORE, imp_t)
    iv = jnp.where(jj * SEL_BLOCK <= tt, iv, -jnp.inf)
    rank = jnp.zeros((nselp, tq), F32)
    for j2 in range(nsel):
        rv = iv[j2:j2 + 1, :]
        rank = rank + jnp.where(jj > j2, jnp.where(rv >= iv, 1.0, 0.0), jnp.where(rv > iv, 1.0, 0.0))
    unsel = jnp.where(rank < n_top, 0.0, -SEL_MASK)
    if nselp < LANES:
        unsel = jnp.concatenate([unsel, jnp.zeros((LANES - nselp, tq), F32)], axis=0)
    qmask = unsel.T.astype(BF16)
    for r in range(R):
        qx_ref[r * tq:(r + 1) * tq, LANES:] = qmask + qf_ref[0, r]

    def tile_step(qq, k_ref, v_ref, k0, tk, state, slot, bias):
        m_ref, acc_ref, bias_ref, slots = state
        s_ref, p_ref, a_ref = slots[slot]
        kk = k_ref[pl.ds(k0, tk), :]
        vv = v_ref[pl.ds(k0, tk), :]
        if bias is not None:
            bias_ref[:, :tk] = bias
        s_ref[:, :tk] = _dot_nt(qq, kk)
        for r in range(R):
            for c0 in range(0, tq, NSA_ROWS):
                rows = slice(r * tq + c0, r * tq + c0 + NSA_ROWS)
                sc = s_ref[rows, :tk]
                if bias is not None:
                    sc = sc + bias_ref[c0:c0 + NSA_ROWS, :tk]
                m_old = m_ref[rows, :]
                m_new = jnp.maximum(m_old, jnp.max(sc, axis=-1, keepdims=True))
                a_ref[rows, :] = jnp.exp2(m_old - m_new)
                m_ref[rows, :] = m_new
                p_ref[rows, :tk] = jnp.exp2(sc - jnp.concatenate([m_new] * (tk // LANES), axis=1)).astype(BF16)
        acc_ref[...] = a_ref[...] * acc_ref[...] + _dot(p_ref[:, :tk], vv)

    def reset(state):
        state[0][...] = jnp.full_like(state[0], NEG)
        state[1][...] = jnp.zeros_like(state[1])

    def result(state):
        acc = state[1][...]
        return acc[:, :dh] / acc[:, dh:dh + 1]

    reset(slc_state)
    qx = qx_ref[...]
    n_full = t0 // tks

    blocks_per_tile = tks // SEL_BLOCK
    n_live = jnp.int32(0)
    for j in range(nsel // blocks_per_tile):
        sel_j = rank[j * blocks_per_tile:(j + 1) * blocks_per_tile, :] < n_top
        live = (jnp.max(jnp.where(sel_j, 1.0, 0.0)) > 0.5) & (j < n_full)
        live_ref[n_live] = j
        n_live = n_live + live.astype(jnp.int32)

    def slc_pair(i, carry):
        tile_step(qx, ksx_ref, vsx_ref, pl.multiple_of(live_ref[2 * i] * tks, tks), tks, slc_state, 0, None)
        tile_step(qx, ksx_ref, vsx_ref, pl.multiple_of(live_ref[2 * i + 1] * tks, tks), tks, slc_state, 1, None)
        return carry

    lax.fori_loop(0, n_live // 2, slc_pair, 0)

    @pl.when(n_live % 2 == 1)
    def _():
        tile_step(qx, ksx_ref, vsx_ref, pl.multiple_of(live_ref[n_live - 1] * tks, tks), tks, slc_state, 0, None)

    kd = pl.multiple_of(n_full * tks, tks)
    spos = kd + lax.broadcasted_iota(jnp.int32, (tq, tks), 1)
    tpos = t0 + lax.broadcasted_iota(jnp.int32, (tq, tks), 0)
    tile_step(qx, ksx_ref, vsx_ref, kd, tks, slc_state, 1, jnp.where(spos <= tpos, 0.0, NEG))
    reset(win_state)
    n_win = WINDOW // tkw + max(tq // tkw, 1)
    last = (t0 + tq - 1) // tkw
    for i in range(n_win):
        kw0 = (last - (n_win - 1) + i) * tkw
        spos = kw0 + lax.broadcasted_iota(jnp.int32, (tq, tkw), 1)
        dist = t0 + lax.broadcasted_iota(jnp.int32, (tq, tkw), 0) - spos
        bias = jnp.where((spos >= 0) & (dist >= 0) & (dist < WINDOW), 0.0, NEG)
        tile_step(q, kwx_ref, vwx_ref, pl.multiple_of(jnp.maximum(kw0, 0), tkw), tkw, win_state, i % 2, bias)
    o_slc = result(slc_state)
    o_win = result(win_state)

    gates = jax.nn.sigmoid(gl_ref[0, 0, pl.ds(pl.multiple_of(t0, tq), tq), :] + bg_ref[0])
    outs = []
    for r in range(R):
        rs = slice(r * tq, (r + 1) * tq)
        outs.append(gates[:, 3 * r:3 * r + 1] * o_cmp[rs] + gates[:, 3 * r + 1:3 * r + 2] * o_slc[rs]
                    + gates[:, 3 * r + 2:3 * r + 3] * o_win[rs])
    out_ref[0] = jnp.concatenate(outs, axis=1).astype(out_ref.dtype)


def nsa_attention(qp, q_feat, kc, vc, kvx, kv_off, slc_const, win_const, ones_col, gl, bg, ov):
    b, t, _ = qp.shape
    g, r = q_feat.shape[:2]
    dh = B_HEAD_DIM
    tq = min(NSA_TQ, t)
    tks = min(NSA_TK_SLC, t)
    tkw = min(NSA_TK_WIN, t)
    ncmp = kc.shape[2]
    n_top = min(SEL_TOPN, t // SEL_BLOCK)
    per_tile = LANES // dh
    assert t // SEL_BLOCK <= LANES // 2 and per_tile == 2 and g % per_tile == 0
    kv_spec = lambda c: pl.BlockSpec((1, t, LANES),
                                     lambda i, j, k: (i, 0, kv_off + c * (g // per_tile) + j // per_tile))
    const_spec = lambda a: pl.BlockSpec(a.shape, lambda i, j, k: (0,) * a.ndim)

    def branch_scratch(tk):
        slot = [pltpu.VMEM((r * tq, tk), F32), pltpu.VMEM((r * tq, tk), BF16), pltpu.VMEM((r * tq, LANES), F32)]
        return [pltpu.VMEM((r * tq, LANES), F32), pltpu.VMEM((r * tq, LANES), F32),
                pltpu.VMEM((tq, tk), F32)] + slot + slot

    return pl.pallas_call(
        functools.partial(_nsa_kernel, tq=tq, tks=tks, tkw=tkw, n_top=n_top),
        grid=(b, g, t // tq),
        in_specs=[pl.BlockSpec((1, tq, r * dh), lambda i, j, k: (i, k, j)),
                  pl.BlockSpec((1, r, 1, LANES), lambda i, j, k: (j, 0, 0, 0)),
                  pl.BlockSpec((1, 1, ncmp, 2 * LANES), lambda i, j, k: (i, j, 0, 0)),
                  pl.BlockSpec((1, 1, ncmp, dh), lambda i, j, k: (i, j, 0, 0)),
                  kv_spec(0), kv_spec(1), kv_spec(2), kv_spec(3),
                  const_spec(slc_const), const_spec(win_const), const_spec(ones_col),
                  pl.BlockSpec((1, 1, t, 3 * r), lambda i, j, k: (i, j, 0, 0)),
                  pl.BlockSpec((1, 1, 3 * r), lambda i, j, k: (j, 0, 0)),
                  const_spec(ov)],
        out_specs=pl.BlockSpec((1, tq, r * dh), lambda i, j, k: (i, k, j)),
        out_shape=jax.ShapeDtypeStruct((b, t, g * r * dh), BF16),
        scratch_shapes=[pltpu.VMEM((r * tq, 2 * LANES), BF16),
                        pltpu.VMEM((t, 2 * LANES), BF16),
                        pltpu.VMEM((t, LANES), BF16),
                        pltpu.VMEM((t, 2 * LANES), BF16),
                        pltpu.VMEM((t, LANES), BF16),
                        ] + branch_scratch(tks) + branch_scratch(tkw)
                       + [pltpu.SMEM((t // tks + 1,), jnp.int32)],
        compiler_params=_cparams("arbitrary", "arbitrary", "arbitrary"),
        name="nsa_attention",
    )(qp, q_feat, kc, vc, kvx, kvx, kvx, kvx, slc_const, win_const, ones_col, gl, bg, ov)


def _alibi_slopes(n):
    return np.power(2.0, -8.0 * np.arange(1, n + 1) / n).astype(np.float32)


def _np_split3(a):
    a = np.asarray(a, np.float32)
    out = []
    r = a
    for _ in range(3):
        p = r.astype(BF16).astype(np.float32)
        out.append(p)
        r = (r - p).astype(np.float32)
    return out


def _pos_features(pos, width):
    hi = (pos // 64).astype(np.float32)
    lo = (pos % 64).astype(np.float32)
    f = np.zeros((pos.shape[0], width), np.float32)
    f[:, 0:3] = hi[:, None]
    f[:, 3:6] = lo[:, None]
    return f


def _slope_features(width):
    s1, s2, s3 = _np_split3((_alibi_slopes(B_HEADS).astype(np.float64) * LOG2E).astype(np.float32))
    f = np.zeros((B_HEADS, width), np.float32)
    for i, s in enumerate((s1, s2, s3)):
        f[:, i] = 64.0 * s
        f[:, 3 + i] = s
    return f


META_I1, META_I2, META_W1, META_W2 = 0, 1, 2, 3


def _router_kernel(x_ref, h_ref, wo_ref, g_ref, w_ref, xo_ref, xn_ref, meta_ref, pos_ref, cnt_ref, run_ref):
    @pl.when(pl.program_id(0) == 0)
    def _():
        run_ref[...] = jnp.zeros_like(run_ref)

    xf = x_ref[...] + _dot(h_ref[...], wo_ref[...])
    xo_ref[...] = xf
    xn = _rms(xf, g_ref[...])
    xn_ref[...] = xn
    logits = _dot_hi_lo(xn, w_ref[...])
    tm = logits.shape[0]
    lane = lax.broadcasted_iota(jnp.int32, logits.shape, 1)
    logits = jnp.where(lane < N_EXPERTS, logits, -jnp.inf)
    m1 = jnp.max(logits, axis=-1, keepdims=True)
    i1 = jnp.min(jnp.where(logits == m1, lane, LANES), axis=-1, keepdims=True)
    rest = jnp.where(lane == i1, -jnp.inf, logits)
    m2 = jnp.max(rest, axis=-1, keepdims=True)
    i2 = jnp.min(jnp.where(rest == m2, lane, LANES), axis=-1, keepdims=True)
    e2 = jnp.exp(m2 - m1)
    w1 = 1.0 / (1.0 + e2)
    meta = jnp.where(lane == META_I1, i1.astype(F32), 0.0)
    meta = jnp.where(lane == META_I2, i2.astype(F32), meta)
    meta = jnp.where(lane == META_W1, w1, meta)
    meta_ref[...] = jnp.where(lane == META_W2, e2 * w1, meta)

    sel = jnp.where((lane == i1) | (lane == i2), 1.0, 0.0)
    row = lax.broadcasted_iota(jnp.int32, (tm, tm), 0)
    col = lax.broadcasted_iota(jnp.int32, (tm, tm), 1)
    before = jnp.where(col < row, 1.0, 0.0).astype(BF16)
    run = run_ref[...]
    pos_ref[...] = run + _dot(before, sel.astype(BF16))
    run = run + jnp.sum(sel, axis=0, keepdims=True)
    run_ref[...] = run
    cnt_ref[...] = run


def moe_router(x, h, w_out, g, w_router, *, tm=512):
    n, d = x.shape
    dh_in = h.shape[1]
    tm = min(tm, n)
    w = _hi_lo_weight(w_router)
    row_spec = pl.BlockSpec((tm, LANES), lambda i: (i, 0))
    return pl.pallas_call(
        _router_kernel,
        grid=(n // tm,),
        in_specs=[pl.BlockSpec((tm, d), lambda i: (i, 0)),
                  pl.BlockSpec((tm, dh_in), lambda i: (i, 0)),
                  pl.BlockSpec((dh_in, d), lambda i: (0, 0)),
                  pl.BlockSpec((1, d), lambda i: (0, 0)),
                  pl.BlockSpec((d, 2 * LANES), lambda i: (0, 0))],
        out_specs=[pl.BlockSpec((tm, d), lambda i: (i, 0)), pl.BlockSpec((tm, d), lambda i: (i, 0)),
                   row_spec, row_spec,
                   pl.BlockSpec((1, LANES), lambda i: (0, 0))],
        out_shape=[jax.ShapeDtypeStruct((n, d), F32), jax.ShapeDtypeStruct((n, d), F32),
                   jax.ShapeDtypeStruct((n, LANES), F32),
                   jax.ShapeDtypeStruct((n, LANES), F32), jax.ShapeDtypeStruct((1, LANES), F32)],
        scratch_shapes=[pltpu.VMEM((1, LANES), F32)],
        compiler_params=_cparams("arbitrary"),
        name="moe_router",
    )(x, h, w_out, g.reshape(1, d), w)


MOE_TILE = 512


def _moe_kernel(te_ref, ok_ref, idx0_ref, idxn_ref, x_hbm, wg_ref, wu_ref, wd_ref, o_ref, xbuf, xb_ref, sem):
    i = pl.program_id(0)
    j = pl.program_id(1)
    nt = pl.num_programs(0)
    nf = pl.num_programs(1)
    tm = xb_ref.shape[0]
    per_step = tm // nf
    slot = i % 2

    def row_copy(idx_ref, r, s):
        return pltpu.make_async_copy(x_hbm.at[pl.ds(idx_ref[0, 0, r], 1), :], xbuf.at[s, pl.ds(r, 1), :], sem.at[s])

    @pl.when((i == 0) & (j == 0))
    def _():
        def start0(r, c):
            row_copy(idx0_ref, r, 0).start()
            return c
        lax.fori_loop(0, tm, start0, 0)

    @pl.when(j == 0)
    def _():
        for r in range(tm):
            row_copy(idx0_ref, r, slot).wait()
        xb_ref[...] = xbuf[slot].astype(BF16)
        o_ref[...] = jnp.zeros_like(o_ref)

    def prefetch():
        base = j * per_step
        for r in range(per_step):
            row_copy(idxn_ref, base + r, 1 - slot).start()

    def compute():
        xb = xb_ref[...]
        a = (_silu(_dot(xb, wg_ref[0])) * _dot(xb, wu_ref[0])).astype(BF16)
        o_ref[...] += _dot(a, wd_ref[0])

    has_next = i + 1 < nt
    ok = ok_ref[i] > 0

    @pl.when(has_next & ok)
    def _():
        prefetch()
        compute()

    @pl.when(has_next & jnp.logical_not(ok))
    def _():
        prefetch()

    @pl.when(jnp.logical_not(has_next) & ok)
    def _():
        compute()


def moe_experts(x, row_tok, tile_expert, tile_ok, w_gu, w_down, *, tf=1792):
    p = row_tok.shape[0]
    d = x.shape[1]
    ne, f, _ = w_down.shape
    tm = MOE_TILE
    nf = f // tf
    nt = p // tm
    idx = row_tok.reshape(nt, 1, tm)
    grid_spec = pltpu.PrefetchScalarGridSpec(
        num_scalar_prefetch=2,
        grid=(nt, nf),
        in_specs=[pl.BlockSpec((1, 1, tm), lambda i, j, te, ok: (0, 0, 0), memory_space=pltpu.SMEM),
                  pl.BlockSpec((1, 1, tm), lambda i, j, te, ok: (jnp.minimum(i + 1, nt - 1), 0, 0),
                               memory_space=pltpu.SMEM),
                  pl.BlockSpec(memory_space=pl.ANY),
                  pl.BlockSpec((1, d, tf), lambda i, j, te, ok: (te[i], 0, j)),
                  pl.BlockSpec((1, d, tf), lambda i, j, te, ok: (te[i], 0, j + nf)),
                  pl.BlockSpec((1, tf, d), lambda i, j, te, ok: (te[i], j, 0))],
        out_specs=pl.BlockSpec((tm, d), lambda i, j, te, ok: (i, 0)),
        scratch_shapes=[pltpu.VMEM((2, tm, d), F32), pltpu.VMEM((tm, d), BF16),
                        pltpu.SemaphoreType.DMA((2,))])
    return pl.pallas_call(
        _moe_kernel,
        grid_spec=grid_spec,
        out_shape=jax.ShapeDtypeStruct((p, d), F32),
        compiler_params=_cparams("arbitrary", "arbitrary"),
        name="moe_experts",
    )(tile_expert, tile_ok, idx, idx, x, w_gu, w_gu, w_down)


def _combine_kernel(d0_ref, dn_ref, x_ref, meta_ref, y_ref, gf_ref, o_ref, ybuf, sem):
    i = pl.program_id(0)
    nt = pl.num_programs(0)
    rows = o_ref.shape[0]
    slot = i % 2

    def row_copy(d_ref, k, r, s):
        return pltpu.make_async_copy(y_ref.at[pl.ds(d_ref[0, k, r], 1), :], ybuf.at[s, k, pl.ds(r, 1), :], sem.at[s])

    @pl.when(i == 0)
    def _():
        def start0(r, c):
            row_copy(d0_ref, 0, r, 0).start()
            row_copy(d0_ref, 1, r, 0).start()
            return c
        lax.fori_loop(0, rows, start0, 0)

    for r in range(rows):
        row_copy(d0_ref, 0, r, slot).wait()
        row_copy(d0_ref, 1, r, slot).wait()

    @pl.when(i + 1 < nt)
    def _():
        for r in range(rows):
            row_copy(dn_ref, 0, r, 1 - slot).start()
            row_copy(dn_ref, 1, r, 1 - slot).start()

    meta = meta_ref[...]
    w1 = meta[:, META_W1:META_W1 + 1]
    w2 = meta[:, META_W2:META_W2 + 1]
    o_ref[...] = _rms(x_ref[...] + w1 * ybuf[slot, 0] + w2 * ybuf[slot, 1], gf_ref[...])


def moe_combine(x, meta, ys, d1, d2, g_final, *, tc=256):
    n, d = x.shape
    tc = min(tc, n)
    nt = n // tc
    dd = jnp.stack([d1.reshape(nt, tc), d2.reshape(nt, tc)], axis=1)
    return pl.pallas_call(
        _combine_kernel,
        grid=(nt,),
        in_specs=[pl.BlockSpec((1, 2, tc), lambda i: (0, 0, 0), memory_space=pltpu.SMEM),
                  pl.BlockSpec((1, 2, tc), lambda i: (jnp.minimum(i + 1, nt - 1), 0, 0), memory_space=pltpu.SMEM),
                  pl.BlockSpec((tc, d), lambda i: (i, 0)),
                  pl.BlockSpec((tc, LANES), lambda i: (i, 0)),
                  pl.BlockSpec(memory_space=pl.ANY),
                  pl.BlockSpec((1, d), lambda i: (0, 0))],
        out_specs=pl.BlockSpec((tc, d), lambda i: (i, 0)),
        out_shape=jax.ShapeDtypeStruct((n, d), F32),
        scratch_shapes=[pltpu.VMEM((2, 2, tc, d), F32), pltpu.SemaphoreType.DMA((2,))],
        compiler_params=_cparams("arbitrary"),
        name="moe_combine",
    )(dd, dd, x, meta, ys, g_final.reshape(1, d))


def kernel(x, norm_mix, norm_ffn, a_w_in, a_b_gate, a_conv, a_norm_h, a_w_out, norm_kv, b_w_kv,
           b_cmp_pos, b_cmp_w1, b_cmp_w2, b_w_q, b_b_gate, b_w_out, f_w_gu, f_w_down,
           m_router, m_w_gu, m_w_down, norm_final):
    B, T, D = x.shape
    N = B * T
    G, R, dh = B_KV_GROUPS, B_REP, B_HEAD_DIM
    xs = x.reshape(N, D)

    inner4 = a_w_in.shape[2] - 2 * A_HEADS
    w_in = a_w_in[0]
    proj, gcol = norm_matmul(xs, norm_mix[0], w_in[:, :inner4].astype(BF16), w_side=w_in[:, inner4:])
    hs = mlstm_core(proj, gcol, a_b_gate[0], a_conv[0], a_norm_h[0], B, T)
    xs = ffn_dense(xs, hs, a_w_out[0].astype(BF16), norm_ffn[0], f_w_gu[0].astype(BF16), f_w_down[0].astype(BF16))

    hd = B_HEADS * dh
    feat_w = LANES - dh
    ncmp = T // CMP_STRIDE

    cmp_cols = 2 * G * dh
    kvx = norm_matmul(xs, norm_kv, b_w_kv.astype(BF16), tn=b_w_kv.shape[1] // 2)
    w_q = b_w_q[0]
    qp, gl = norm_matmul(xs, norm_mix[1], (w_q[:, :hd] * (dh ** -0.5 * LOG2E)).astype(BF16), w_side=w_q[:, hd:])

    kvt = kvx[:, :cmp_cols].reshape(B, T, 2 * G, dh).transpose(0, 2, 1, 3)
    pos = b_cmp_pos.transpose(1, 0, 2).reshape(2, 1, CMP_LEN * dh).astype(BF16)
    kvc = nsa_compress(kvt.reshape(B, 2 * G, ncmp, CMP_STRIDE * dh), pos,
                       b_cmp_w1.astype(BF16), b_cmp_w2.astype(BF16))

    def hi_lanes(f, lo=None):
        lo = np.zeros((f.shape[0], dh), np.float32) if lo is None else lo
        return jnp.asarray(np.concatenate([lo, f], axis=1), BF16)

    key_pos = _pos_features(np.arange(T), feat_w)
    blk_onehot = (np.arange(T)[:, None] // SEL_BLOCK == np.arange(dh)[None, :]).astype(np.float32)
    slc_const = hi_lanes(key_pos, blk_onehot)
    win_const = hi_lanes(key_pos)
    cmp_const = hi_lanes(_pos_features(np.arange(ncmp) * CMP_STRIDE + CMP_LEN - 1, feat_w))
    q_feat = hi_lanes(_slope_features(feat_w)).reshape(G, R, 1, LANES)
    ones_col = np.zeros((1, feat_w), np.float32)
    ones_col[0, 0] = 1.0
    ones_col = hi_lanes(ones_col)
    kcb = kvc[:, :G].astype(BF16)
    kc = jnp.concatenate([kcb, kcb, jnp.broadcast_to(cmp_const, (B, G, ncmp, LANES))], axis=-1)
    vc = kvc[:, G:].astype(BF16)
    glt = gl[:, :3 * B_HEADS].reshape(B, T, G, 3 * R).transpose(0, 2, 1, 3)
    bg = b_b_gate[0].reshape(G, 1, 3 * R)

    nsel = T // SEL_BLOCK
    ci = np.arange(ncmp)[:, None] * CMP_STRIDE
    sj = np.arange(LANES)[None, :] * SEL_BLOCK
    ov = ((ci < sj + SEL_BLOCK) & (ci + CMP_LEN > sj) & (np.arange(LANES)[None, :] < nsel)
          & (np.arange(ncmp)[:, None] < ncmp - 1))
    ov = jnp.asarray(ov.astype(np.float32), BF16)

    oa = nsa_attention(qp.reshape(B, T, hd), q_feat, kc, vc, kvx.reshape(B, T, -1),
                       cmp_cols // LANES, slc_const, win_const, ones_col, glt, bg, ov)

    xs, xn, meta, pos, cnt = moe_router(xs, oa.reshape(N, hd), b_w_out[0].astype(BF16), norm_ffn[1], m_router[0])
    ne = N_EXPERTS
    p_rows = 2 * N + ne * MOE_TILE
    i1 = meta[:, META_I1].astype(jnp.int32)
    i2 = meta[:, META_I2].astype(jnp.int32)
    counts = cnt[0, :ne].astype(jnp.int32)
    padded = (counts + MOE_TILE - 1) // MOE_TILE * MOE_TILE
    seg_end = jnp.cumsum(padded)
    seg_start = seg_end - padded
    pos8 = pos[:, :ne].astype(jnp.int32)
    d1 = seg_start[i1] + jnp.take_along_axis(pos8, i1[:, None], axis=1)[:, 0]
    d2 = seg_start[i2] + jnp.take_along_axis(pos8, i2[:, None], axis=1)[:, 0]
    tok = jnp.arange(N, dtype=jnp.int32)
    row_tok = jnp.zeros((p_rows,), jnp.int32).at[jnp.concatenate([d1, d2])].set(jnp.concatenate([tok, tok]))
    tile_start = jnp.arange(p_rows // MOE_TILE, dtype=jnp.int32) * MOE_TILE
    tile_expert = jnp.minimum(jnp.sum(tile_start[:, None] >= seg_end[None, :], axis=1), ne - 1).astype(jnp.int32)
    tile_ok = (tile_start < seg_end[ne - 1]).astype(jnp.int32)

    ys = moe_experts(xn, row_tok, tile_expert, tile_ok, m_w_gu[0].astype(BF16), m_w_down[0].astype(BF16))
    out = moe_combine(xs, meta, ys, d1, d2, norm_final)
    return out.reshape(B, T, D)
```

```python
import functools

import numpy as np
import jax
import jax.numpy as jnp
from jax import lax
from jax.experimental import pallas as pl
from jax.experimental.pallas import tpu as pltpu

F32 = jnp.float32
BF16 = jnp.bfloat16

RMS_EPS = 1e-6
A_HEADS = 4
A_CONV = 4
B_HEADS = 16
B_KV_GROUPS = 4
B_REP = B_HEADS // B_KV_GROUPS
B_HEAD_DIM = 64
CMP_LEN = 32
CMP_STRIDE = 16
SEL_BLOCK = 64
SEL_TOPN = 16
WINDOW = 512
FORCED_SCORE = 1e4
N_EXPERTS = 8

LANES = 128
V7X_VMEM_BYTES = 64 * 1024 * 1024
VMEM_LIMIT = V7X_VMEM_BYTES - 8 * 1024 * 1024
NEG = -1e30


def _cparams(*sem):
    return pltpu.CompilerParams(dimension_semantics=sem, vmem_limit_bytes=VMEM_LIMIT)


def _dot(a, b):
    return jnp.dot(a, b, preferred_element_type=F32)


def _dot_nt(a, b):
    return lax.dot_general(a, b, (((1,), (1,)), ((), ())), preferred_element_type=F32)


def _dot_tn(a, b):
    return lax.dot_general(a, b, (((0,), (0,)), ((), ())), preferred_element_type=F32)


def _split3(a):
    a1 = a.astype(BF16)
    r1 = a - a1.astype(F32)
    a2 = r1.astype(BF16)
    a3 = (r1 - a2.astype(F32)).astype(BF16)
    return a1, a2, a3


def _hi_lo_weight(w):
    w = jnp.pad(w, ((0, 0), (0, LANES - w.shape[1])))
    hi = w.astype(BF16)
    lo = (w - hi.astype(F32)).astype(BF16)
    return jnp.concatenate([hi, lo], axis=1)


def _dot_hi_lo(a, w_hl):
    a1 = a.astype(BF16)
    a2 = (a - a1.astype(F32)).astype(BF16)
    s = _dot(a1, w_hl) + _dot(a2, w_hl)
    return s[:, :LANES] + s[:, LANES:]


def _rms(xf, g):
    return xf * lax.rsqrt(jnp.mean(xf * xf, axis=-1, keepdims=True) + RMS_EPS) * g


def _silu(x):
    return x * jax.nn.sigmoid(x)


def _log_sigmoid(x):
    return jnp.minimum(x, 0.0) - jnp.log(1.0 + jnp.exp(-jnp.abs(x)))


def _norm_matmul_kernel(*refs, side):
    if side:
        x_ref, g_ref, w_ref, ws_ref, o_ref, os_ref, xn_ref = refs
    else:
        x_ref, g_ref, w_ref, o_ref, xn_ref = refs

    @pl.when(pl.program_id(1) == 0)
    def _():
        xn = _rms(x_ref[...], g_ref[...])
        xn_ref[...] = xn.astype(xn_ref.dtype)
        if side:
            os_ref[...] = _dot_hi_lo(xn, ws_ref[...])

    o_ref[...] = _dot(xn_ref[...], w_ref[...]).astype(o_ref.dtype)


def norm_matmul(x, g, w, *, w_side=None, tm=1024, tn=1024):
    n, d = x.shape
    dout = w.shape[1]
    tm = min(tm, n)
    tn = min(tn, dout)
    side = w_side is not None
    in_specs = [pl.BlockSpec((tm, d), lambda i, j: (i, 0)),
                pl.BlockSpec((1, d), lambda i, j: (0, 0)),
                pl.BlockSpec((d, tn), lambda i, j: (0, j))]
    out_specs = [pl.BlockSpec((tm, tn), lambda i, j: (i, j))]
    out_shape = [jax.ShapeDtypeStruct((n, dout), BF16)]
    args = [x, g.reshape(1, d), w]
    if side:
        in_specs.append(pl.BlockSpec((d, 2 * LANES), lambda i, j: (0, 0)))
        out_specs.append(pl.BlockSpec((tm, LANES), lambda i, j: (i, 0)))
        out_shape.append(jax.ShapeDtypeStruct((n, LANES), F32))
        args.append(_hi_lo_weight(w_side))
    out = pl.pallas_call(
        functools.partial(_norm_matmul_kernel, side=side),
        grid=(n // tm, dout // tn),
        in_specs=in_specs,
        out_specs=out_specs,
        out_shape=out_shape,
        scratch_shapes=[pltpu.VMEM((tm, d), BF16)],
        compiler_params=_cparams("parallel", "arbitrary"),
        name="norm_matmul_side" if side else "norm_matmul",
    )(*args)
    return out if side else out[0]


MLSTM_CHUNK = 256


def _mlstm_kernel(qk_ref, v_ref, o_ref, gcol_ref, grow_ref, bcol_ref, brow_ref, convw_ref, gout_ref,
                  out_ref, ct_ref, n_ref, m_ref, xs_ref):
    L = qk_ref.shape[0]
    H = A_HEADS
    inner = v_ref.shape[1]
    dh = inner // H

    @pl.when(pl.program_id(1) == 0)
    def _():
        ct_ref[...] = jnp.zeros_like(ct_ref)
        n_ref[...] = jnp.zeros_like(n_ref)
        m_ref[...] = jnp.zeros_like(m_ref)
        xs_ref[...] = jnp.zeros_like(xs_ref)

    row = lax.broadcasted_iota(jnp.int32, (L, L), 0)
    col = lax.broadcasted_iota(jnp.int32, (L, L), 1)
    causal = col <= row
    tril = jnp.where(causal, 1.0, 0.0).astype(BF16)
    triu = jnp.where(row <= col, 1.0, 0.0).astype(BF16)

    gc = gcol_ref[...] + brow_ref[...]
    gr = grow_ref[...] + bcol_ref[...]
    lfc1, lfc2, lfc3 = _split3(_log_sigmoid(gc))
    lfr1, lfr2, lfr3 = _split3(_log_sigmoid(gr))
    b_c = _dot(tril, lfc1) + _dot(tril, lfc2) + _dot(tril, lfc3)
    b_r = _dot(lfr1, triu) + _dot(lfr2, triu) + _dot(lfr3, triu)

    halo = xs_ref.shape[0] - L
    xs_ref[halo:, :] = qk_ref[...].astype(F32)

    shift = [jnp.where(row - col == s, 1.0, 0.0).astype(BF16) for s in range(1, A_CONV)]
    rowh = lax.broadcasted_iota(jnp.int32, (halo, dh), 0)

    def conv(cols):
        w = convw_ref[:, cols]
        x = xs_ref[halo:, cols]
        tail = xs_ref[:halo, cols]
        y = x * w[A_CONV - 1:A_CONV, :]
        head = jnp.zeros((halo, dh), F32)
        for s in range(1, A_CONV):
            ws = w[A_CONV - 1 - s:A_CONV - s, :]
            y = y + _dot(shift[s - 1], (x * ws).astype(BF16))
            head = head + jnp.where(rowh < s, pltpu.roll(tail, s, 0), 0.0) * ws
        return jnp.concatenate([y[:halo] + head, y[halo:]], axis=0)

    for h in range(H):
        hs = slice(h * dh, (h + 1) * dh)
        ks = slice(inner + h * dh, inner + (h + 1) * dh)
        q = conv(hs)
        k = conv(ks) * (dh ** -0.5)
        v = v_ref[:, hs]
        qb = q.astype(BF16)
        kb = k.astype(BF16)

        li_c = gc[:, h:h + 1]
        bc = b_c[:, H + h:H + h + 1]
        li_r = gr[h:h + 1, :]
        br = b_r[H + h:H + h + 1, :]
        m_prev = m_ref[h:h + 1, 0:1]

        d = jnp.where(causal, bc - br + li_r, -jnp.inf)
        inter = bc + m_prev
        m_t = jnp.maximum(inter, jnp.max(d, axis=-1, keepdims=True))
        w_inter = jnp.exp(inter - m_t)
        s = _dot_nt(qb, kb) * jnp.exp(d - m_t)
        ct = ct_ref[h]
        num = _dot(s.astype(BF16), v) + w_inter * _dot(qb, ct.astype(BF16))
        den = jnp.sum(s, axis=-1, keepdims=True) + w_inter * jnp.sum(q * n_ref[h], axis=-1, keepdims=True)
        hh = num / jnp.maximum(jnp.abs(den), jnp.exp(-m_t))
        hh = hh * lax.rsqrt(jnp.mean(hh * hh, axis=-1, keepdims=True) + RMS_EPS)
        out_ref[:, hs] = (hh * gout_ref[:, hs] * jax.nn.sigmoid(o_ref[:, hs].astype(F32))).astype(out_ref.dtype)

        b_last = bc[L - 1:L, :]
        g = b_last - bc + li_c
        m_new = jnp.maximum(b_last + m_prev, jnp.max(g, axis=0, keepdims=True))
        a_prev = jnp.exp(b_last + m_prev - m_new)
        a_s = jnp.exp(g - m_new)
        ct_ref[h] = a_prev * ct + _dot_tn(kb, (v.astype(F32) * a_s).astype(BF16))
        n_ref[h] = a_prev * n_ref[h] + jnp.sum(k * a_s, axis=0, keepdims=True)
        m_ref[h:h + 1, :] = jnp.broadcast_to(m_new, (1, LANES))

    xs_ref[:halo, :] = xs_ref[L:, :]


def mlstm_core(proj, gcol, b_gate, conv_w, g_out, batch, seq):
    n = proj.shape[0]
    inner = proj.shape[1] // 4
    H = A_HEADS
    dh = inner // H
    L = min(MLSTM_CHUNK, seq)
    nc = seq // L
    grow = gcol[:, :2 * H].T
    brow = jnp.pad(b_gate, (0, LANES - 2 * H)).reshape(1, LANES)
    bcol = b_gate.reshape(2 * H, 1)
    return pl.pallas_call(
        _mlstm_kernel,
        grid=(batch, nc),
        in_specs=[pl.BlockSpec((L, 2 * inner), lambda b, c: (b * nc + c, 0)),
                  pl.BlockSpec((L, inner), lambda b, c: (b * nc + c, 2)),
                  pl.BlockSpec((L, inner), lambda b, c: (b * nc + c, 3)),
                  pl.BlockSpec((L, LANES), lambda b, c: (b * nc + c, 0)),
                  pl.BlockSpec((2 * H, L), lambda b, c: (0, b * nc + c)),
                  pl.BlockSpec((2 * H, 1), lambda b, c: (0, 0)),
                  pl.BlockSpec((1, LANES), lambda b, c: (0, 0)),
                  pl.BlockSpec((A_CONV, 2 * inner), lambda b, c: (0, 0)),
                  pl.BlockSpec((1, inner), lambda b, c: (0, 0))],
        out_specs=pl.BlockSpec((L, inner), lambda b, c: (b * nc + c, 0)),
        out_shape=jax.ShapeDtypeStruct((n, inner), BF16),
        scratch_shapes=[pltpu.VMEM((H, dh, dh), F32),
                        pltpu.VMEM((H, 1, dh), F32),
                        pltpu.VMEM((8, LANES), F32),
                        pltpu.VMEM((8 + L, 2 * inner), F32)],
        compiler_params=_cparams("parallel", "arbitrary"),
        name="mlstm_core",
    )(proj, proj, proj, gcol, grow, bcol, brow, conv_w, g_out.reshape(1, inner))


def _ffn_kernel(x_ref, h_ref, wo_ref, g_ref, wg_ref, wu_ref, wd_ref, o_ref, xn_ref):
    @pl.when(pl.program_id(1) == 0)
    def _():
        xf = x_ref[...] + _dot(h_ref[...], wo_ref[...])
        xn_ref[...] = _rms(xf, g_ref[...]).astype(BF16)
        o_ref[...] = xf

    xn = xn_ref[...]
    a = (_silu(_dot(xn, wg_ref[...])) * _dot(xn, wu_ref[...])).astype(BF16)
    o_ref[...] += _dot(a, wd_ref[...])


def ffn_dense(x, h, w_out, g, w_gu, w_down, *, tm=512):
    n, d = x.shape
    f = w_down.shape[0]
    dh_in = h.shape[1]
    tm = min(tm, n)
    tf, nf = f, 1
    once = pl.Buffered(1)
    return pl.pallas_call(
        _ffn_kernel,
        grid=(n // tm, nf),
        in_specs=[pl.BlockSpec((tm, d), lambda i, j: (i, 0)),
                  pl.BlockSpec((tm, dh_in), lambda i, j: (i, 0)),
                  pl.BlockSpec((dh_in, d), lambda i, j: (0, 0), pipeline_mode=once),
                  pl.BlockSpec((1, d), lambda i, j: (0, 0)),
                  pl.BlockSpec((d, tf), lambda i, j: (0, j), pipeline_mode=once),
                  pl.BlockSpec((d, tf), lambda i, j: (0, j + nf), pipeline_mode=once),
                  pl.BlockSpec((tf, d), lambda i, j: (j, 0), pipeline_mode=once)],
        out_specs=pl.BlockSpec((tm, d), lambda i, j: (i, 0)),
        out_shape=jax.ShapeDtypeStruct((n, d), F32),
        scratch_shapes=[pltpu.VMEM((tm, d), BF16)],
        compiler_params=_cparams("parallel", "arbitrary"),
        name="ffn_dense",
    )(x, h, w_out, g.reshape(1, d), w_gu, w_gu, w_down)


def _compress_kernel(r_ref, pos_ref, w1_ref, w2_ref, o_ref):
    r = r_ref[0, 0]
    w1 = w1_ref[0]
    half = r.shape[1]
    nc = r.shape[0]
    lo = _dot(r, w1[:half])
    hi = _dot(r, w1[half:])
    hid = lo + pltpu.roll(hi, nc - 1, 0) + _dot(pos_ref[0], w1)
    o_ref[0, 0] = _dot(_silu(hid).astype(BF16), w2_ref[0]).astype(o_ref.dtype)


def nsa_compress(r, pos, w1, w2):
    b, c2, nc, half = r.shape
    g = c2 // 2
    hidden = w1.shape[2]
    dh = w2.shape[2]
    return pl.pallas_call(
        _compress_kernel,
        grid=(b, c2),
        in_specs=[pl.BlockSpec((1, 1, nc, half), lambda i, j: (i, j, 0, 0)),
                  pl.BlockSpec((1, 1, 2 * half), lambda i, j: (j // g, 0, 0)),
                  pl.BlockSpec((1, 2 * half, hidden), lambda i, j: (j // g, 0, 0)),
                  pl.BlockSpec((1, hidden, dh), lambda i, j: (j // g, 0, 0))],
        out_specs=pl.BlockSpec((1, 1, nc, dh), lambda i, j: (i, j, 0, 0)),
        out_shape=jax.ShapeDtypeStruct((b, c2, nc, dh), F32),
        compiler_params=_cparams("parallel", "parallel"),
        name="nsa_compress",
    )(r, pos, w1, w2)


NSA_TQ = 256
NSA_TK_SLC = 512
NSA_TK_WIN = 512
NSA_ROWS = 32
SEL_MASK = 2.0 ** 14
LOG2E = float(np.log2(np.e))


def _nsa_kernel(q_ref, qf_ref, kc_ref, vc_ref, ks_ref, vs_ref, kw_ref, vw_ref, cs_ref, cw_ref, one_ref,
                gl_ref, bg_ref, ov_ref,
                out_ref, qx_ref, ksx_ref, vsx_ref, kwx_ref, vwx_ref,
                ms_ref, accs_ref, biass_ref, s0_ref, p0_ref, a0_ref, s1_ref, p1_ref, a1_ref,
                mw_ref, accw_ref, biasw_ref, s2_ref, p2_ref, a2_ref, s3_ref, p3_ref, a3_ref, live_ref,
                *, tq, tks, tkw, n_top):
    R = B_REP
    M = R * tq
    dh = B_HEAD_DIM
    half = LANES // 2
    t0 = pl.program_id(2) * tq
    slc_state = (ms_ref, accs_ref, biass_ref, ((s0_ref, p0_ref, a0_ref), (s1_ref, p1_ref, a1_ref)))
    win_state = (mw_ref, accw_ref, biasw_ref, ((s2_ref, p2_ref, a2_ref), (s3_ref, p3_ref, a3_ref)))

    @pl.when(pl.program_id(2) == 0)
    def _():
        odd = pl.program_id(1) % 2 == 1
        low_lane = lax.broadcasted_iota(jnp.int32, (ks_ref.shape[1], LANES), 1) < half

        def halves(ref):
            x = ref[0].astype(F32)
            xr = pltpu.roll(x, half, 1)
            return jnp.where(odd, xr, x), jnp.where(odd, x, xr)

        lo, hi = halves(ks_ref)
        ksx_ref[:, :LANES] = jnp.where(low_lane, lo, hi).astype(BF16)
        ksx_ref[:, LANES:] = cs_ref[...]
        lo, hi = halves(kw_ref)
        kwx_ref[:, :LANES] = jnp.where(low_lane, lo, hi).astype(BF16)
        kwx_ref[:, LANES:] = cw_ref[...]
        ones = one_ref[...].astype(F32)
        vsx_ref[...] = jnp.where(low_lane, halves(vs_ref)[0], ones).astype(BF16)
        vwx_ref[...] = jnp.where(low_lane, halves(vw_ref)[0], ones).astype(BF16)

    qb = q_ref[0]
    low_q = lax.broadcasted_iota(jnp.int32, (tq, LANES), 1) < half
    for r in range(R):
        tile = qb[:, (r // 2) * LANES:(r // 2 + 1) * LANES]
        own = (tile, jnp.zeros_like(tile)) if r % 2 == 0 else (jnp.zeros_like(tile), tile)
        qx_ref[r * tq:(r + 1) * tq, :LANES] = jnp.where(low_q, *own)
        qx_ref[r * tq:(r + 1) * tq, LANES:] = jnp.broadcast_to(qf_ref[0, r], (tq, LANES))
    q = qx_ref[...]

    kc = kc_ref[0, 0]
    ncmp = kc.shape[0]
    s = _dot_nt(q, kc)
    tpos_c = t0 + (lax.broadcasted_iota(jnp.int32, (M, ncmp), 0) & (tq - 1))
    cend = lax.broadcasted_iota(jnp.int32, (M, ncmp), 1) * CMP_STRIDE + (CMP_LEN - 1)
    ok_c = tpos_c >= cend
    s = jnp.where(ok_c, s, NEG)
    p = jnp.where(ok_c, jnp.exp2(s - jnp.max(s, axis=-1, keepdims=True)), 0.0)
    dsum = jnp.sum(p, axis=-1, keepdims=True)
    p = p / jnp.where(dsum > 0, dsum, 1.0)
    o_cmp = _dot(p.astype(BF16), vc_ref[0, 0])

    psum = p[0:tq]
    for r in range(1, R):
        psum = psum + p[r * tq:(r + 1) * tq]
    p1, p2, p3 = _split3(psum)
    ov = ov_ref[...]
    imp = _dot(p1, ov) + _dot(p2, ov) + _dot(p3, ov)
    nsel = ks_ref.shape[1] // SEL_BLOCK
    nselp = max(nsel, 8)
    imp_t = imp.T[:nselp]
    jj = lax.broadcasted_iota(jnp.int32, (nselp, tq), 0)
    tt = t0 + lax.broadcasted_iota(jnp.int32, (nselp, tq), 1)
    cur = tt // SEL_BLOCK
    forced = (jj == 0) | (jj == cur) | (jj == cur - 1)
    iv = jnp.where(forced, FORCED_SCORE, imp_t)
    iv = jnp.where(jj * SEL_BLOCK <= tt, iv, -jnp.inf)
    rank = jnp.zeros((nselp, tq), F32)
    for j2 in range(nsel):
        rv = iv[j2:j2 + 1, :]
        rank = rank + jnp.where(jj > j2, jnp.where(rv >= iv, 1.0, 0.0), jnp.where(rv > iv, 1.0, 0.0))
    unsel = jnp.where(rank < n_top, 0.0, -SEL_MASK)
    if nselp < LANES:
        unsel = jnp.concatenate([unsel, jnp.zeros((LANES - nselp, tq), F32)], axis=0)
    qmask = unsel.T.astype(BF16)
    for r in range(R):
        qx_ref[r * tq:(r + 1) * tq, LANES:] = qmask + qf_ref[0, r]

    def tile_step(qq, k_ref, v_ref, k0, tk, state, slot, bias):
        m_ref, acc_ref, bias_ref, slots = state
        s_ref, p_ref, a_ref = slots[slot]
        kk = k_ref[pl.ds(k0, tk), :]
        vv = v_ref[pl.ds(k0, tk), :]
        if bias is not None:
            bias_ref[:, :tk] = bias
        s_ref[:, :tk] = _dot_nt(qq, kk)
        for r in range(R):
            for c0 in range(0, tq, NSA_ROWS):
                rows = slice(r * tq + c0, r * tq + c0 + NSA_ROWS)
                sc = s_ref[rows, :tk]
                if bias is not None:
                    sc = sc + bias_ref[c0:c0 + NSA_ROWS, :tk]
                m_old = m_ref[rows, :]
                m_new = jnp.maximum(m_old, jnp.max(sc, axis=-1, keepdims=True))
                a_ref[rows, :] = jnp.exp2(m_old - m_new)
                m_ref[rows, :] = m_new
                p_ref[rows, :tk] = jnp.exp2(sc - jnp.concatenate([m_new] * (tk // LANES), axis=1)).astype(BF16)
        acc_ref[...] = a_ref[...] * acc_ref[...] + _dot(p_ref[:, :tk], vv)

    def reset(state):
        state[0][...] = jnp.full_like(state[0], NEG)
        state[1][...] = jnp.zeros_like(state[1])

    def result(state):
        acc = state[1][...]
        return acc[:, :dh] / acc[:, dh:dh + 1]

    reset(slc_state)
    qx = qx_ref[...]
    n_full = t0 // tks

    blocks_per_tile = tks // SEL_BLOCK
    n_live = jnp.int32(0)
    for j in range(nsel // blocks_per_tile):
        sel_j = rank[j * blocks_per_tile:(j + 1) * blocks_per_tile, :] < n_top
        live = (jnp.max(jnp.where(sel_j, 1.0, 0.0)) > 0.5) & (j < n_full)
        live_ref[n_live] = j
        n_live = n_live + live.astype(jnp.int32)

    def slc_pair(i, carry):
        tile_step(qx, ksx_ref, vsx_ref, pl.multiple_of(live_ref[2 * i] * tks, tks), tks, slc_state, 0, None)
        tile_step(qx, ksx_ref, vsx_ref, pl.multiple_of(live_ref[2 * i + 1] * tks, tks), tks, slc_state, 1, None)
        return carry

    lax.fori_loop(0, n_live // 2, slc_pair, 0)

    @pl.when(n_live % 2 == 1)
    def _():
        tile_step(qx, ksx_ref, vsx_ref, pl.multiple_of(live_ref[n_live - 1] * tks, tks), tks, slc_state, 0, None)

    kd = pl.multiple_of(n_full * tks, tks)
    spos = kd + lax.broadcasted_iota(jnp.int32, (tq, tks), 1)
    tpos = t0 + lax.broadcasted_iota(jnp.int32, (tq, tks), 0)
    tile_step(qx, ksx_ref, vsx_ref, kd, tks, slc_state, 1, jnp.where(spos <= tpos, 0.0, NEG))
    reset(win_state)
    n_win = WINDOW // tkw + max(tq // tkw, 1)
    last = (t0 + tq - 1) // tkw
    for i in range(n_win):
        kw0 = (last - (n_win - 1) + i) * tkw
        spos = kw0 + lax.broadcasted_iota(jnp.int32, (tq, tkw), 1)
        dist = t0 + lax.broadcasted_iota(jnp.int32, (tq, tkw), 0) - spos
        bias = jnp.where((spos >= 0) & (dist >= 0) & (dist < WINDOW), 0.0, NEG)
        tile_step(q, kwx_ref, vwx_ref, pl.multiple_of(jnp.maximum(kw0, 0), tkw), tkw, win_state, i % 2, bias)
    o_slc = result(slc_state)
    o_win = result(win_state)

    gates = jax.nn.sigmoid(gl_ref[0, 0, pl.ds(pl.multiple_of(t0, tq), tq), :] + bg_ref[0])
    outs = []
    for r in range(R):
        rs = slice(r * tq, (r + 1) * tq)
        outs.append(gates[:, 3 * r:3 * r + 1] * o_cmp[rs] + gates[:, 3 * r + 1:3 * r + 2] * o_slc[rs]
                    + gates[:, 3 * r + 2:3 * r + 3] * o_win[rs])
    out_ref[0] = jnp.concatenate(outs, axis=1).astype(out_ref.dtype)


def nsa_attention(qp, q_feat, kc, vc, kvx, kv_off, slc_const, win_const, ones_col, gl, bg, ov):
    b, t, _ = qp.shape
    g, r = q_feat.shape[:2]
    dh = B_HEAD_DIM
    tq = min(NSA_TQ, t)
    tks = min(NSA_TK_SLC, t)
    tkw = min(NSA_TK_WIN, t)
    ncmp = kc.shape[2]
    n_top = min(SEL_TOPN, t // SEL_BLOCK)
    per_tile = LANES // dh
    assert t // SEL_BLOCK <= LANES // 2 and per_tile == 2 and g % per_tile == 0
    kv_spec = lambda c: pl.BlockSpec((1, t, LANES),
                                     lambda i, j, k: (i, 0, kv_off + c * (g // per_tile) + j // per_tile))
    const_spec = lambda a: pl.BlockSpec(a.shape, lambda i, j, k: (0,) * a.ndim)

    def branch_scratch(tk):
        slot = [pltpu.VMEM((r * tq, tk), F32), pltpu.VMEM((r * tq, tk), BF16), pltpu.VMEM((r * tq, LANES), F32)]
        return [pltpu.VMEM((r * tq, LANES), F32), pltpu.VMEM((r * tq, LANES), F32),
                pltpu.VMEM((tq, tk), F32)] + slot + slot

    return pl.pallas_call(
        functools.partial(_nsa_kernel, tq=tq, tks=tks, tkw=tkw, n_top=n_top),
        grid=(b, g, t // tq),
        in_specs=[pl.BlockSpec((1, tq, r * dh), lambda i, j, k: (i, k, j)),
                  pl.BlockSpec((1, r, 1, LANES), lambda i, j, k: (j, 0, 0, 0)),
                  pl.BlockSpec((1, 1, ncmp, 2 * LANES), lambda i, j, k: (i, j, 0, 0)),
                  pl.BlockSpec((1, 1, ncmp, dh), lambda i, j, k: (i, j, 0, 0)),
                  kv_spec(0), kv_spec(1), kv_spec(2), kv_spec(3),
                  const_spec(slc_const), const_spec(win_const), const_spec(ones_col),
                  pl.BlockSpec((1, 1, t, 3 * r), lambda i, j, k: (i, j, 0, 0)),
                  pl.BlockSpec((1, 1, 3 * r), lambda i, j, k: (j, 0, 0)),
                  const_spec(ov)],
        out_specs=pl.BlockSpec((1, tq, r * dh), lambda i, j, k: (i, k, j)),
        out_shape=jax.ShapeDtypeStruct((b, t, g * r * dh), BF16),
        scratch_shapes=[pltpu.VMEM((r * tq, 2 * LANES), BF16),
                        pltpu.VMEM((t, 2 * LANES), BF16),
                        pltpu.VMEM((t, LANES), BF16),
                        pltpu.VMEM((t, 2 * LANES), BF16),
                        pltpu.VMEM((t, LANES), BF16),
                        ] + branch_scratch(tks) + branch_scratch(tkw)
                       + [pltpu.SMEM((t // tks + 1,), jnp.int32)],
        compiler_params=_cparams("arbitrary", "arbitrary", "arbitrary"),
        name="nsa_attention",
    )(qp, q_feat, kc, vc, kvx, kvx, kvx, kvx, slc_const, win_const, ones_col, gl, bg, ov)


def _alibi_slopes(n):
    return np.power(2.0, -8.0 * np.arange(1, n + 1) / n).astype(np.float32)


def _np_split3(a):
    a = np.asarray(a, np.float32)
    out = []
    r = a
    for _ in range(3):
        p = r.astype(BF16).astype(np.float32)
        out.append(p)
        r = (r - p).astype(np.float32)
    return out


def _pos_features(pos, width):
    hi = (pos // 64).astype(np.float32)
    lo = (pos % 64).astype(np.float32)
    f = np.zeros((pos.shape[0], width), np.float32)
    f[:, 0:3] = hi[:, None]
    f[:, 3:6] = lo[:, None]
    return f


def _slope_features(width):
    s1, s2, s3 = _np_split3((_alibi_slopes(B_HEADS).astype(np.float64) * LOG2E).astype(np.float32))
    f = np.zeros((B_HEADS, width), np.float32)
    for i, s in enumerate((s1, s2, s3)):
        f[:, i] = 64.0 * s
        f[:, 3 + i] = s
    return f


META_I1, META_I2, META_W1, META_W2 = 0, 1, 2, 3


def _router_kernel(x_ref, h_ref, wo_ref, g_ref, w_ref, xo_ref, xn_ref, meta_ref, pos_ref, cnt_ref, run_ref):
    @pl.when(pl.program_id(0) == 0)
    def _():
        run_ref[...] = jnp.zeros_like(run_ref)

    xf = x_ref[...] + _dot(h_ref[...], wo_ref[...])
    xo_ref[...] = xf
    xn = _rms(xf, g_ref[...])
    xn_ref[...] = xn
    logits = _dot_hi_lo(xn, w_ref[...])
    tm = logits.shape[0]
    lane = lax.broadcasted_iota(jnp.int32, logits.shape, 1)
    logits = jnp.where(lane < N_EXPERTS, logits, -jnp.inf)
    m1 = jnp.max(logits, axis=-1, keepdims=True)
    i1 = jnp.min(jnp.where(logits == m1, lane, LANES), axis=-1, keepdims=True)
    rest = jnp.where(lane == i1, -jnp.inf, logits)
    m2 = jnp.max(rest, axis=-1, keepdims=True)
    i2 = jnp.min(jnp.where(rest == m2, lane, LANES), axis=-1, keepdims=True)
    e2 = jnp.exp(m2 - m1)
    w1 = 1.0 / (1.0 + e2)
    meta = jnp.where(lane == META_I1, i1.astype(F32), 0.0)
    meta = jnp.where(lane == META_I2, i2.astype(F32), meta)
    meta = jnp.where(lane == META_W1, w1, meta)
    meta_ref[...] = jnp.where(lane == META_W2, e2 * w1, meta)

    sel = jnp.where((lane == i1) | (lane == i2), 1.0, 0.0)
    row = lax.broadcasted_iota(jnp.int32, (tm, tm), 0)
    col = lax.broadcasted_iota(jnp.int32, (tm, tm), 1)
    before = jnp.where(col < row, 1.0, 0.0).astype(BF16)
    run = run_ref[...]
    pos_ref[...] = run + _dot(before, sel.astype(BF16))
    run = run + jnp.sum(sel, axis=0, keepdims=True)
    run_ref[...] = run
    cnt_ref[...] = run


def moe_router(x, h, w_out, g, w_router, *, tm=512):
    n, d = x.shape
    dh_in = h.shape[1]
    tm = min(tm, n)
    w = _hi_lo_weight(w_router)
    row_spec = pl.BlockSpec((tm, LANES), lambda i: (i, 0))
    return pl.pallas_call(
        _router_kernel,
        grid=(n // tm,),
        in_specs=[pl.BlockSpec((tm, d), lambda i: (i, 0)),
                  pl.BlockSpec((tm, dh_in), lambda i: (i, 0)),
                  pl.BlockSpec((dh_in, d), lambda i: (0, 0)),
                  pl.BlockSpec((1, d), lambda i: (0, 0)),
                  pl.BlockSpec((d, 2 * LANES), lambda i: (0, 0))],
        out_specs=[pl.BlockSpec((tm, d), lambda i: (i, 0)), pl.BlockSpec((tm, d), lambda i: (i, 0)),
                   row_spec, row_spec,
                   pl.BlockSpec((1, LANES), lambda i: (0, 0))],
        out_shape=[jax.ShapeDtypeStruct((n, d), F32), jax.ShapeDtypeStruct((n, d), F32),
                   jax.ShapeDtypeStruct((n, LANES), F32),
                   jax.ShapeDtypeStruct((n, LANES), F32), jax.ShapeDtypeStruct((1, LANES), F32)],
        scratch_shapes=[pltpu.VMEM((1, LANES), F32)],
        compiler_params=_cparams("arbitrary"),
        name="moe_router",
    )(x, h, w_out, g.reshape(1, d), w)


MOE_TILE = 512


def _moe_kernel(te_ref, ok_ref, idx0_ref, idxn_ref, x_hbm, wg_ref, wu_ref, wd_ref, o_ref, xbuf, xb_ref, sem):
    i = pl.program_id(0)
    j = pl.program_id(1)
    nt = pl.num_programs(0)
    nf = pl.num_programs(1)
    tm = xb_ref.shape[0]
    per_step = tm // nf
    slot = i % 2

    def row_copy(idx_ref, r, s):
        return pltpu.make_async_copy(x_hbm.at[pl.ds(idx_ref[0, 0, r], 1), :], xbuf.at[s, pl.ds(r, 1), :], sem.at[s])

    @pl.when((i == 0) & (j == 0))
    def _():
        def start0(r, c):
            row_copy(idx0_ref, r, 0).start()
            return c
        lax.fori_loop(0, tm, start0, 0)

    @pl.when(j == 0)
    def _():
        for r in range(tm):
            row_copy(idx0_ref, r, slot).wait()
        xb_ref[...] = xbuf[slot].astype(BF16)
        o_ref[...] = jnp.zeros_like(o_ref)

    def prefetch():
        base = j * per_step
        for r in range(per_step):
            row_copy(idxn_ref, base + r, 1 - slot).start()

    def compute():
        xb = xb_ref[...]
        a = (_silu(_dot(xb, wg_ref[0])) * _dot(xb, wu_ref[0])).astype(BF16)
        o_ref[...] += _dot(a, wd_ref[0])

    has_next = i + 1 < nt
    ok = ok_ref[i] > 0

    @pl.when(has_next & ok)
    def _():
        prefetch()
        compute()

    @pl.when(has_next & jnp.logical_not(ok))
    def _():
        prefetch()

    @pl.when(jnp.logical_not(has_next) & ok)
    def _():
        compute()


def moe_experts(x, row_tok, tile_expert, tile_ok, w_gu, w_down, *, tf=1792):
    p = row_tok.shape[0]
    d = x.shape[1]
    ne, f, _ = w_down.shape
    tm = MOE_TILE
    nf = f // tf
    nt = p // tm
    idx = row_tok.reshape(nt, 1, tm)
    grid_spec = pltpu.PrefetchScalarGridSpec(
        num_scalar_prefetch=2,
        grid=(nt, nf),
        in_specs=[pl.BlockSpec((1, 1, tm), lambda i, j, te, ok: (0, 0, 0), memory_space=pltpu.SMEM),
                  pl.BlockSpec((1, 1, tm), lambda i, j, te, ok: (jnp.minimum(i + 1, nt - 1), 0, 0),
                               memory_space=pltpu.SMEM),
                  pl.BlockSpec(memory_space=pl.ANY),
                  pl.BlockSpec((1, d, tf), lambda i, j, te, ok: (te[i], 0, j)),
                  pl.BlockSpec((1, d, tf), lambda i, j, te, ok: (te[i], 0, j + nf)),
                  pl.BlockSpec((1, tf, d), lambda i, j, te, ok: (te[i], j, 0))],
        out_specs=pl.BlockSpec((tm, d), lambda i, j, te, ok: (i, 0)),
        scratch_shapes=[pltpu.VMEM((2, tm, d), F32), pltpu.VMEM((tm, d), BF16),
                        pltpu.SemaphoreType.DMA((2,))])
    return pl.pallas_call(
        _moe_kernel,
        grid_spec=grid_spec,
        out_shape=jax.ShapeDtypeStruct((p, d), F32),
        compiler_params=_cparams("arbitrary", "arbitrary"),
        name="moe_experts",
    )(tile_expert, tile_ok, idx, idx, x, w_gu, w_gu, w_down)


def _combine_kernel(d0_ref, dn_ref, x_ref, meta_ref, y_ref, gf_ref, o_ref, ybuf, sem):
    i = pl.program_id(0)
    nt = pl.num_programs(0)
    rows = o_ref.shape[0]
    slot = i % 2

    def row_copy(d_ref, k, r, s):
        return pltpu.make_async_copy(y_ref.at[pl.ds(d_ref[0, k, r], 1), :], ybuf.at[s, k, pl.ds(r, 1), :], sem.at[s])

    @pl.when(i == 0)
    def _():
        def start0(r, c):
            row_copy(d0_ref, 0, r, 0).start()
            row_copy(d0_ref, 1, r, 0).start()
            return c
        lax.fori_loop(0, rows, start0, 0)

    for r in range(rows):
        row_copy(d0_ref, 0, r, slot).wait()
        row_copy(d0_ref, 1, r, slot).wait()

    @pl.when(i + 1 < nt)
    def _():
        for r in range(rows):
            row_copy(dn_ref, 0, r, 1 - slot).start()
            row_copy(dn_ref, 1, r, 1 - slot).start()

    meta = meta_ref[...]
    w1 = meta[:, META_W1:META_W1 + 1]
    w2 = meta[:, META_W2:META_W2 + 1]
    o_ref[...] = _rms(x_ref[...] + w1 * ybuf[slot, 0] + w2 * ybuf[slot, 1], gf_ref[...])


def moe_combine(x, meta, ys, d1, d2, g_final, *, tc=256):
    n, d = x.shape
    tc = min(tc, n)
    nt = n // tc
    dd = jnp.stack([d1.reshape(nt, tc), d2.reshape(nt, tc)], axis=1)
    return pl.pallas_call(
        _combine_kernel,
        grid=(nt,),
        in_specs=[pl.BlockSpec((1, 2, tc), lambda i: (0, 0, 0), memory_space=pltpu.SMEM),
                  pl.BlockSpec((1, 2, tc), lambda i: (jnp.minimum(i + 1, nt - 1), 0, 0), memory_space=pltpu.SMEM),
                  pl.BlockSpec((tc, d), lambda i: (i, 0)),
                  pl.BlockSpec((tc, LANES), lambda i: (i, 0)),
                  pl.BlockSpec(memory_space=pl.ANY),
                  pl.BlockSpec((1, d), lambda i: (0, 0))],
        out_specs=pl.BlockSpec((tc, d), lambda i: (i, 0)),
        out_shape=jax.ShapeDtypeStruct((n, d), F32),
        scratch_shapes=[pltpu.VMEM((2, 2, tc, d), F32), pltpu.SemaphoreType.DMA((2,))],
        compiler_params=_cparams("arbitrary"),
        name="moe_combine",
    )(dd, dd, x, meta, ys, g_final.reshape(1, d))


def kernel(x, norm_mix, norm_ffn, a_w_in, a_b_gate, a_conv, a_norm_h, a_w_out, norm_kv, b_w_kv,
           b_cmp_pos, b_cmp_w1, b_cmp_w2, b_w_q, b_b_gate, b_w_out, f_w_gu, f_w_down,
           m_router, m_w_gu, m_w_down, norm_final):
    B, T, D = x.shape
    N = B * T
    G, R, dh = B_KV_GROUPS, B_REP, B_HEAD_DIM
    xs = x.reshape(N, D)

    inner4 = a_w_in.shape[2] - 2 * A_HEADS
    w_in = a_w_in[0]
    proj, gcol = norm_matmul(xs, norm_mix[0], w_in[:, :inner4].astype(BF16), w_side=w_in[:, inner4:])
    hs = mlstm_core(proj, gcol, a_b_gate[0], a_conv[0], a_norm_h[0], B, T)
    xs = ffn_dense(xs, hs, a_w_out[0].astype(BF16), norm_ffn[0], f_w_gu[0].astype(BF16), f_w_down[0].astype(BF16))

    hd = B_HEADS * dh
    feat_w = LANES - dh
    ncmp = T // CMP_STRIDE

    cmp_cols = 2 * G * dh
    kvx = norm_matmul(xs, norm_kv, b_w_kv.astype(BF16), tn=b_w_kv.shape[1] // 2)
    w_q = b_w_q[0]
    qp, gl = norm_matmul(xs, norm_mix[1], (w_q[:, :hd] * (dh ** -0.5 * LOG2E)).astype(BF16), w_side=w_q[:, hd:])

    kvt = kvx[:, :cmp_cols].reshape(B, T, 2 * G, dh).transpose(0, 2, 1, 3)
    pos = b_cmp_pos.transpose(1, 0, 2).reshape(2, 1, CMP_LEN * dh).astype(BF16)
    kvc = nsa_compress(kvt.reshape(B, 2 * G, ncmp, CMP_STRIDE * dh), pos,
                       b_cmp_w1.astype(BF16), b_cmp_w2.astype(BF16))

    def hi_lanes(f, lo=None):
        lo = np.zeros((f.shape[0], dh), np.float32) if lo is None else lo
        return jnp.asarray(np.concatenate([lo, f], axis=1), BF16)

    key_pos = _pos_features(np.arange(T), feat_w)
    blk_onehot = (np.arange(T)[:, None] // SEL_BLOCK == np.arange(dh)[None, :]).astype(np.float32)
    slc_const = hi_lanes(key_pos, blk_onehot)
    win_const = hi_lanes(key_pos)
    cmp_const = hi_lanes(_pos_features(np.arange(ncmp) * CMP_STRIDE + CMP_LEN - 1, feat_w))
    q_feat = hi_lanes(_slope_features(feat_w)).reshape(G, R, 1, LANES)
    ones_col = np.zeros((1, feat_w), np.float32)
    ones_col[0, 0] = 1.0
    ones_col = hi_lanes(ones_col)
    kcb = kvc[:, :G].astype(BF16)
    kc = jnp.concatenate([kcb, kcb, jnp.broadcast_to(cmp_const, (B, G, ncmp, LANES))], axis=-1)
    vc = kvc[:, G:].astype(BF16)
    glt = gl[:, :3 * B_HEADS].reshape(B, T, G, 3 * R).transpose(0, 2, 1, 3)
    bg = b_b_gate[0].reshape(G, 1, 3 * R)

    nsel = T // SEL_BLOCK
    ci = np.arange(ncmp)[:, None] * CMP_STRIDE
    sj = np.arange(LANES)[None, :] * SEL_BLOCK
    ov = ((ci < sj + SEL_BLOCK) & (ci + CMP_LEN > sj) & (np.arange(LANES)[None, :] < nsel)
          & (np.arange(ncmp)[:, None] < ncmp - 1))
    ov = jnp.asarray(ov.astype(np.float32), BF16)

    oa = nsa_attention(qp.reshape(B, T, hd), q_feat, kc, vc, kvx.reshape(B, T, -1),
                       cmp_cols // LANES, slc_const, win_const, ones_col, glt, bg, ov)

    xs, xn, meta, pos, cnt = moe_router(xs, oa.reshape(N, hd), b_w_out[0].astype(BF16), norm_ffn[1], m_router[0])
    ne = N_EXPERTS
    p_rows = 2 * N + ne * MOE_TILE
    i1 = meta[:, META_I1].astype(jnp.int32)
    i2 = meta[:, META_I2].astype(jnp.int32)
    counts = cnt[0, :ne].astype(jnp.int32)
    padded = (counts + MOE_TILE - 1) // MOE_TILE * MOE_TILE
    seg_end = jnp.cumsum(padded)
    seg_start = seg_end - padded
    pos8 = pos[:, :ne].astype(jnp.int32)
    d1 = seg_start[i1] + jnp.take_along_axis(pos8, i1[:, None], axis=1)[:, 0]
    d2 = seg_start[i2] + jnp.take_along_axis(pos8, i2[:, None], axis=1)[:, 0]
    tok = jnp.arange(N, dtype=jnp.int32)
    row_tok = jnp.zeros((p_rows,), jnp.int32).at[jnp.concatenate([d1, d2])].set(jnp.concatenate([tok, tok]))
    tile_start = jnp.arange(p_rows // MOE_TILE, dtype=jnp.int32) * MOE_TILE
    tile_expert = jnp.minimum(jnp.sum(tile_start[:, None] >= seg_end[None, :], axis=1), ne - 1).astype(jnp.int32)
    tile_ok = (tile_start < seg_end[ne - 1]).astype(jnp.int32)

    ys = moe_experts(xn, row_tok, tile_expert, tile_ok, m_w_gu[0].astype(BF16), m_w_down[0].astype(BF16))
    out = moe_combine(xs, meta, ys, d1, d2, norm_final)
    return out.reshape(B, T, D)
```

```python
import functools

import numpy as np
import jax
import jax.numpy as jnp
from jax import lax
from jax.experimental import pallas as pl
from jax.experimental.pallas import tpu as pltpu

F32 = jnp.float32
BF16 = jnp.bfloat16

RMS_EPS = 1e-6
A_HEADS = 4
A_CONV = 4
B_HEADS = 16
B_KV_GROUPS = 4
B_REP = B_HEADS // B_KV_GROUPS
B_HEAD_DIM = 64
CMP_LEN = 32
CMP_STRIDE = 16
SEL_BLOCK = 64
SEL_TOPN = 16
WINDOW = 512
FORCED_SCORE = 1e4
N_EXPERTS = 8

LANES = 128
V7X_VMEM_BYTES = 64 * 1024 * 1024
VMEM_LIMIT = V7X_VMEM_BYTES - 8 * 1024 * 1024
NEG = -1e30


def _cparams(*sem):
    return pltpu.CompilerParams(dimension_semantics=sem, vmem_limit_bytes=VMEM_LIMIT)


def _dot(a, b):
    return jnp.dot(a, b, preferred_element_type=F32)


def _dot_nt(a, b):
    return lax.dot_general(a, b, (((1,), (1,)), ((), ())), preferred_element_type=F32)


def _dot_tn(a, b):
    return lax.dot_general(a, b, (((0,), (0,)), ((), ())), preferred_element_type=F32)


def _split3(a):
    a1 = a.astype(BF16)
    r1 = a - a1.astype(F32)
    a2 = r1.astype(BF16)
    a3 = (r1 - a2.astype(F32)).astype(BF16)
    return a1, a2, a3


def _hi_lo_weight(w):
    w = jnp.pad(w, ((0, 0), (0, LANES - w.shape[1])))
    hi = w.astype(BF16)
    lo = (w - hi.astype(F32)).astype(BF16)
    return jnp.concatenate([hi, lo], axis=1)


def _dot_hi_lo(a, w_hl):
    a1 = a.astype(BF16)
    a2 = (a - a1.astype(F32)).astype(BF16)
    s = _dot(a1, w_hl) + _dot(a2, w_hl)
    return s[:, :LANES] + s[:, LANES:]


def _rms(xf, g):
    return xf * lax.rsqrt(jnp.mean(xf * xf, axis=-1, keepdims=True) + RMS_EPS) * g


def _silu(x):
    return x * jax.nn.sigmoid(x)


def _log_sigmoid(x):
    return jnp.minimum(x, 0.0) - jnp.log(1.0 + jnp.exp(-jnp.abs(x)))


def _norm_matmul_kernel(*refs, side):
    if side:
        x_ref, g_ref, w_ref, ws_ref, o_ref, os_ref, xn_ref = refs
    else:
        x_ref, g_ref, w_ref, o_ref, xn_ref = refs

    @pl.when(pl.program_id(1) == 0)
    def _():
        xn = _rms(x_ref[...], g_ref[...])
        xn_ref[...] = xn.astype(xn_ref.dtype)
        if side:
            os_ref[...] = _dot_hi_lo(xn, ws_ref[...])

    o_ref[...] = _dot(xn_ref[...], w_ref[...]).astype(o_ref.dtype)


def norm_matmul(x, g, w, *, w_side=None, tm=1024, tn=1024):
    n, d = x.shape
    dout = w.shape[1]
    tm = min(tm, n)
    tn = min(tn, dout)
    side = w_side is not None
    in_specs = [pl.BlockSpec((tm, d), lambda i, j: (i, 0)),
                pl.BlockSpec((1, d), lambda i, j: (0, 0)),
                pl.BlockSpec((d, tn), lambda i, j: (0, j))]
    out_specs = [pl.BlockSpec((tm, tn), lambda i, j: (i, j))]
    out_shape = [jax.ShapeDtypeStruct((n, dout), BF16)]
    args = [x, g.reshape(1, d), w]
    if side:
        in_specs.append(pl.BlockSpec((d, 2 * LANES), lambda i, j: (0, 0)))
        out_specs.append(pl.BlockSpec((tm, LANES), lambda i, j: (i, 0)))
        out_shape.append(jax.ShapeDtypeStruct((n, LANES), F32))
        args.append(_hi_lo_weight(w_side))
    out = pl.pallas_call(
        functools.partial(_norm_matmul_kernel, side=side),
        grid=(n // tm, dout // tn),
        in_specs=in_specs,
        out_specs=out_specs,
        out_shape=out_shape,
        scratch_shapes=[pltpu.VMEM((tm, d), BF16)],
        compiler_params=_cparams("parallel", "arbitrary"),
        name="norm_matmul_side" if side else "norm_matmul",
    )(*args)
    return out if side else out[0]


MLSTM_CHUNK = 256


def _mlstm_kernel(qk_ref, v_ref, o_ref, gcol_ref, grow_ref, bcol_ref, brow_ref, convw_ref, gout_ref,
                  out_ref, ct_ref, n_ref, m_ref, xs_ref):
    L = qk_ref.shape[0]
    H = A_HEADS
    inner = v_ref.shape[1]
    dh = inner // H

    @pl.when(pl.program_id(1) == 0)
    def _():
        ct_ref[...] = jnp.zeros_like(ct_ref)
        n_ref[...] = jnp.zeros_like(n_ref)
        m_ref[...] = jnp.zeros_like(m_ref)
        xs_ref[...] = jnp.zeros_like(xs_ref)

    row = lax.broadcasted_iota(jnp.int32, (L, L), 0)
    col = lax.broadcasted_iota(jnp.int32, (L, L), 1)
    causal = col <= row
    tril = jnp.where(causal, 1.0, 0.0).astype(BF16)
    triu = jnp.where(row <= col, 1.0, 0.0).astype(BF16)

    gc = gcol_ref[...] + brow_ref[...]
    gr = grow_ref[...] + bcol_ref[...]
    lfc1, lfc2, lfc3 = _split3(_log_sigmoid(gc))
    lfr1, lfr2, lfr3 = _split3(_log_sigmoid(gr))
    b_c = _dot(tril, lfc1) + _dot(tril, lfc2) + _dot(tril, lfc3)
    b_r = _dot(lfr1, triu) + _dot(lfr2, triu) + _dot(lfr3, triu)

    halo = xs_ref.shape[0] - L
    xs_ref[halo:, :] = qk_ref[...].astype(F32)

    shift = [jnp.where(row - col == s, 1.0, 0.0).astype(BF16) for s in range(1, A_CONV)]
    rowh = lax.broadcasted_iota(jnp.int32, (halo, dh), 0)

    def conv(cols):
        w = convw_ref[:, cols]
        x = xs_ref[halo:, cols]
        tail = xs_ref[:halo, cols]
        y = x * w[A_CONV - 1:A_CONV, :]
        head = jnp.zeros((halo, dh), F32)
        for s in range(1, A_CONV):
            ws = w[A_CONV - 1 - s:A_CONV - s, :]
            y = y + _dot(shift[s - 1], (x * ws).astype(BF16))
            head = head + jnp.where(rowh < s, pltpu.roll(tail, s, 0), 0.0) * ws
        return jnp.concatenate([y[:halo] + head, y[halo:]], axis=0)

    for h in range(H):
        hs = slice(h * dh, (h + 1) * dh)
        ks = slice(inner + h * dh, inner + (h + 1) * dh)
        q = conv(hs)
        k = conv(ks) * (dh ** -0.5)
        v = v_ref[:, hs]
        qb = q.astype(BF16)
        kb = k.astype(BF16)

        li_c = gc[:, h:h + 1]
        bc = b_c[:, H + h:H + h + 1]
        li_r = gr[h:h + 1, :]
        br = b_r[H + h:H + h + 1, :]
        m_prev = m_ref[h:h + 1, 0:1]

        d = jnp.where(causal, bc - br + li_r, -jnp.inf)
        inter = bc + m_prev
        m_t = jnp.maximum(inter, jnp.max(d, axis=-1, keepdims=True))
        w_inter = jnp.exp(inter - m_t)
        s = _dot_nt(qb, kb) * jnp.exp(d - m_t)
        ct = ct_ref[h]
        num = _dot(s.astype(BF16), v) + w_inter * _dot(qb, ct.astype(BF16))
        den = jnp.sum(s, axis=-1, keepdims=True) + w_inter * jnp.sum(q * n_ref[h], axis=-1, keepdims=True)
        hh = num / jnp.maximum(jnp.abs(den), jnp.exp(-m_t))
        hh = hh * lax.rsqrt(jnp.mean(hh * hh, axis=-1, keepdims=True) + RMS_EPS)
        out_ref[:, hs] = (hh * gout_ref[:, hs] * jax.nn.sigmoid(o_ref[:, hs].astype(F32))).astype(out_ref.dtype)

        b_last = bc[L - 1:L, :]
        g = b_last - bc + li_c
        m_new = jnp.maximum(b_last + m_prev, jnp.max(g, axis=0, keepdims=True))
        a_prev = jnp.exp(b_last + m_prev - m_new)
        a_s = jnp.exp(g - m_new)
        ct_ref[h] = a_prev * ct + _dot_tn(kb, (v.astype(F32) * a_s).astype(BF16))
        n_ref[h] = a_prev * n_ref[h] + jnp.sum(k * a_s, axis=0, keepdims=True)
        m_ref[h:h + 1, :] = jnp.broadcast_to(m_new, (1, LANES))

    xs_ref[:halo, :] = xs_ref[L:, :]


def mlstm_core(proj, gcol, b_gate, conv_w, g_out, batch, seq):
    n = proj.shape[0]
    inner = proj.shape[1] // 4
    H = A_HEADS
    dh = inner // H
    L = min(MLSTM_CHUNK, seq)
    nc = seq // L
    grow = gcol[:, :2 * H].T
    brow = jnp.pad(b_gate, (0, LANES - 2 * H)).reshape(1, LANES)
    bcol = b_gate.reshape(2 * H, 1)
    return pl.pallas_call(
        _mlstm_kernel,
        grid=(batch, nc),
        in_specs=[pl.BlockSpec((L, 2 * inner), lambda b, c: (b * nc + c, 0)),
                  pl.BlockSpec((L, inner), lambda b, c: (b * nc + c, 2)),
                  pl.BlockSpec((L, inner), lambda b, c: (b * nc + c, 3)),
                  pl.BlockSpec((L, LANES), lambda b, c: (b * nc + c, 0)),
                  pl.BlockSpec((2 * H, L), lambda b, c: (0, b * nc + c)),
                  pl.BlockSpec((2 * H, 1), lambda b, c: (0, 0)),
                  pl.BlockSpec((1, LANES), lambda b, c: (0, 0)),
                  pl.BlockSpec((A_CONV, 2 * inner), lambda b, c: (0, 0)),
                  pl.BlockSpec((1, inner), lambda b, c: (0, 0))],
        out_specs=pl.BlockSpec((L, inner), lambda b, c: (b * nc + c, 0)),
        out_shape=jax.ShapeDtypeStruct((n, inner), BF16),
        scratch_shapes=[pltpu.VMEM((H, dh, dh), F32),
                        pltpu.VMEM((H, 1, dh), F32),
                        pltpu.VMEM((8, LANES), F32),
                        pltpu.VMEM((8 + L, 2 * inner), F32)],
        compiler_params=_cparams("parallel", "arbitrary"),
        name="mlstm_core",
    )(proj, proj, proj, gcol, grow, bcol, brow, conv_w, g_out.reshape(1, inner))


def _ffn_kernel(x_ref, h_ref, wo_ref, g_ref, wg_ref, wu_ref, wd_ref, o_ref, xn_ref):
    @pl.when(pl.program_id(1) == 0)
    def _():
        xf = x_ref[...] + _dot(h_ref[...], wo_ref[...])
        xn_ref[...] = _rms(xf, g_ref[...]).astype(BF16)
        o_ref[...] = xf

    xn = xn_ref[...]
    a = (_silu(_dot(xn, wg_ref[...])) * _dot(xn, wu_ref[...])).astype(BF16)
    o_ref[...] += _dot(a, wd_ref[...])


def ffn_dense(x, h, w_out, g, w_gu, w_down, *, tm=512):
    n, d = x.shape
    f = w_down.shape[0]
    dh_in = h.shape[1]
    tm = min(tm, n)
    tf, nf = f, 1
    once = pl.Buffered(1)
    return pl.pallas_call(
        _ffn_kernel,
        grid=(n // tm, nf),
        in_specs=[pl.BlockSpec((tm, d), lambda i, j: (i, 0)),
                  pl.BlockSpec((tm, dh_in), lambda i, j: (i, 0)),
                  pl.BlockSpec((dh_in, d), lambda i, j: (0, 0), pipeline_mode=once),
                  pl.BlockSpec((1, d), lambda i, j: (0, 0)),
                  pl.BlockSpec((d, tf), lambda i, j: (0, j), pipeline_mode=once),
                  pl.BlockSpec((d, tf), lambda i, j: (0, j + nf), pipeline_mode=once),
                  pl.BlockSpec((tf, d), lambda i, j: (j, 0), pipeline_mode=once)],
        out_specs=pl.BlockSpec((tm, d), lambda i, j: (i, 0)),
        out_shape=jax.ShapeDtypeStruct((n, d), F32),
        scratch_shapes=[pltpu.VMEM((tm, d), BF16)],
        compiler_params=_cparams("parallel", "arbitrary"),
        name="ffn_dense",
    )(x, h, w_out, g.reshape(1, d), w_gu, w_gu, w_down)


def _compress_kernel(r_ref, pos_ref, w1_ref, w2_ref, o_ref):
    r = r_ref[0, 0]
    w1 = w1_ref[0]
    half = r.shape[1]
    nc = r.shape[0]
    lo = _dot(r, w1[:half])
    hi = _dot(r, w1[half:])
    hid = lo + pltpu.roll(hi, nc - 1, 0) + _dot(pos_ref[0], w1)
    o_ref[0, 0] = _dot(_silu(hid).astype(BF16), w2_ref[0]).astype(o_ref.dtype)


def nsa_compress(r, pos, w1, w2):
    b, c2, nc, half = r.shape
    g = c2 // 2
    hidden = w1.shape[2]
    dh = w2.shape[2]
    return pl.pallas_call(
        _compress_kernel,
        grid=(b, c2),
        in_specs=[pl.BlockSpec((1, 1, nc, half), lambda i, j: (i, j, 0, 0)),
                  pl.BlockSpec((1, 1, 2 * half), lambda i, j: (j // g, 0, 0)),
                  pl.BlockSpec((1, 2 * half, hidden), lambda i, j: (j // g, 0, 0)),
                  pl.BlockSpec((1, hidden, dh), lambda i, j: (j // g, 0, 0))],
        out_specs=pl.BlockSpec((1, 1, nc, dh), lambda i, j: (i, j, 0, 0)),
        out_shape=jax.ShapeDtypeStruct((b, c2, nc, dh), F32),
        compiler_params=_cparams("parallel", "parallel"),
        name="nsa_compress",
    )(r, pos, w1, w2)


NSA_TQ = 256
NSA_TK_SLC = 512
NSA_TK_WIN = 512
NSA_ROWS = 32
SEL_MASK = 2.0 ** 14
LOG2E = float(np.log2(np.e))


def _nsa_kernel(q_ref, qf_ref, kc_ref, vc_ref, ks_ref, vs_ref, kw_ref, vw_ref, cs_ref, cw_ref, one_ref,
                gl_ref, bg_ref, ov_ref,
                out_ref, qx_ref, ksx_ref, vsx_ref, kwx_ref, vwx_ref,
                ms_ref, accs_ref, biass_ref, s0_ref, p0_ref, a0_ref, s1_ref, p1_ref, a1_ref,
                mw_ref, accw_ref, biasw_ref, s2_ref, p2_ref, a2_ref, s3_ref, p3_ref, a3_ref, live_ref,
                *, tq, tks, tkw, n_top):
    R = B_REP
    M = R * tq
    dh = B_HEAD_DIM
    half = LANES // 2
    t0 = pl.program_id(2) * tq
    slc_state = (ms_ref, accs_ref, biass_ref, ((s0_ref, p0_ref, a0_ref), (s1_ref, p1_ref, a1_ref)))
    win_state = (mw_ref, accw_ref, biasw_ref, ((s2_ref, p2_ref, a2_ref), (s3_ref, p3_ref, a3_ref)))

    @pl.when(pl.program_id(2) == 0)
    def _():
        odd = pl.program_id(1) % 2 == 1
        low_lane = lax.broadcasted_iota(jnp.int32, (ks_ref.shape[1], LANES), 1) < half

        def halves(ref):
            x = ref[0].astype(F32)
            xr = pltpu.roll(x, half, 1)
            return jnp.where(odd, xr, x), jnp.where(odd, x, xr)

        lo, hi = halves(ks_ref)
        ksx_ref[:, :LANES] = jnp.where(low_lane, lo, hi).astype(BF16)
        ksx_ref[:, LANES:] = cs_ref[...]
        lo, hi = halves(kw_ref)
        kwx_ref[:, :LANES] = jnp.where(low_lane, lo, hi).astype(BF16)
        kwx_ref[:, LANES:] = cw_ref[...]
        ones = one_ref[...].astype(F32)
        vsx_ref[...] = jnp.where(low_lane, halves(vs_ref)[0], ones).astype(BF16)
        vwx_ref[...] = jnp.where(low_lane, halves(vw_ref)[0], ones).astype(BF16)

    qb = q_ref[0]
    low_q = lax.broadcasted_iota(jnp.int32, (tq, LANES), 1) < half
    for r in range(R):
        tile = qb[:, (r // 2) * LANES:(r // 2 + 1) * LANES]
        own = (tile, jnp.zeros_like(tile)) if r % 2 == 0 else (jnp.zeros_like(tile), tile)
        qx_ref[r * tq:(r + 1) * tq, :LANES] = jnp.where(low_q, *own)
        qx_ref[r * tq:(r + 1) * tq, LANES:] = jnp.broadcast_to(qf_ref[0, r], (tq, LANES))
    q = qx_ref[...]

    kc = kc_ref[0, 0]
    ncmp = kc.shape[0]
    s = _dot_nt(q, kc)
    tpos_c = t0 + (lax.broadcasted_iota(jnp.int32, (M, ncmp), 0) & (tq - 1))
    cend = lax.broadcasted_iota(jnp.int32, (M, ncmp), 1) * CMP_STRIDE + (CMP_LEN - 1)
    ok_c = tpos_c >= cend
    s = jnp.where(ok_c, s, NEG)
    p = jnp.where(ok_c, jnp.exp2(s - jnp.max(s, axis=-1, keepdims=True)), 0.0)
    dsum = jnp.sum(p, axis=-1, keepdims=True)
    p = p / jnp.where(dsum > 0, dsum, 1.0)
    o_cmp = _dot(p.astype(BF16), vc_ref[0, 0])

    psum = p[0:tq]
    for r in range(1, R):
        psum = psum + p[r * tq:(r + 1) * tq]
    p1, p2, p3 = _split3(psum)
    ov = ov_ref[...]
    imp = _dot(p1, ov) + _dot(p2, ov) + _dot(p3, ov)
    nsel = ks_ref.shape[1] // SEL_BLOCK
    nselp = max(nsel, 8)
    imp_t = imp.T[:nselp]
    jj = lax.broadcasted_iota(jnp.int32, (nselp, tq), 0)
    tt = t0 + lax.broadcasted_iota(jnp.int32, (nselp, tq), 1)
    cur = tt // SEL_BLOCK
    forced = (jj == 0) | (jj == cur) | (jj == cur - 1)
    iv = jnp.where(forced, FORCED_SCORE, imp_t)
    iv = jnp.where(jj * SEL_BLOCK <= tt, iv, -jnp.inf)
    rank = jnp.zeros((nselp, tq), F32)
    for j2 in range(nsel):
        rv = iv[j2:j2 + 1, :]
        rank = rank + jnp.where(jj > j2, jnp.where(rv >= iv, 1.0, 0.0), jnp.where(rv > iv, 1.0, 0.0))
    unsel = jnp.where(rank < n_top, 0.0, -SEL_MASK)
    if nselp < LANES:
        unsel = jnp.concatenate([unsel, jnp.zeros((LANES - nselp, tq), F32)], axis=0)
    qmask = unsel.T.astype(BF16)
    for r in range(R):
        qx_ref[r * tq:(r + 1) * tq, LANES:] = qmask + qf_ref[0, r]

    def tile_step(qq, k_ref, v_ref, k0, tk, state, slot, bias):
        m_ref, acc_ref, bias_ref, slots = state
        s_ref, p_ref, a_ref = slots[slot]
        kk = k_ref[pl.ds(k0, tk), :]
        vv = v_ref[pl.ds(k0, tk), :]
        if bias is not None:
            bias_ref[:, :tk] = bias
        s_ref[:, :tk] = _dot_nt(qq, kk)
        for r in range(R):
            for c0 in range(0, tq, NSA_ROWS):
                rows = slice(r * tq + c0, r * tq + c0 + NSA_ROWS)
                sc = s_ref[rows, :tk]
                if bias is not None:
                    sc = sc + bias_ref[c0:c0 + NSA_ROWS, :tk]
                m_old = m_ref[rows, :]
                m_new = jnp.maximum(m_old, jnp.max(sc, axis=-1, keepdims=True))
                a_ref[rows, :] = jnp.exp2(m_old - m_new)
                m_ref[rows, :] = m_new
                p_ref[rows, :tk] = jnp.exp2(sc - jnp.concatenate([m_new] * (tk // LANES), axis=1)).astype(BF16)
        acc_ref[...] = a_ref[...] * acc_ref[...] + _dot(p_ref[:, :tk], vv)

    def reset(state):
        state[0][...] = jnp.full_like(state[0], NEG)
        state[1][...] = jnp.zeros_like(state[1])

    def result(state):
        acc = state[1][...]
        return acc[:, :dh] / acc[:, dh:dh + 1]

    reset(slc_state)
    qx = qx_ref[...]
    n_full = t0 // tks

    blocks_per_tile = tks // SEL_BLOCK
    n_live = jnp.int32(0)
    for j in range(nsel // blocks_per_tile):
        sel_j = rank[j * blocks_per_tile:(j + 1) * blocks_per_tile, :] < n_top
        live = (jnp.max(jnp.where(sel_j, 1.0, 0.0)) > 0.5) & (j < n_full)
        live_ref[n_live] = j
        n_live = n_live + live.astype(jnp.int32)

    def slc_pair(i, carry):
        tile_step(qx, ksx_ref, vsx_ref, pl.multiple_of(live_ref[2 * i] * tks, tks), tks, slc_state, 0, None)
        tile_step(qx, ksx_ref, vsx_ref, pl.multiple_of(live_ref[2 * i + 1] * tks, tks), tks, slc_state, 1, None)
        return carry

    lax.fori_loop(0, n_live // 2, slc_pair, 0)

    @pl.when(n_live % 2 == 1)
    def _():
        tile_step(qx, ksx_ref, vsx_ref, pl.multiple_of(live_ref[n_live - 1] * tks, tks), tks, slc_state, 0, None)

    kd = pl.multiple_of(n_full * tks, tks)
    spos = kd + lax.broadcasted_iota(jnp.int32, (tq, tks), 1)
    tpos = t0 + lax.broadcasted_iota(jnp.int32, (tq, tks), 0)
    tile_step(qx, ksx_ref, vsx_ref, kd, tks, slc_state, 1, jnp.where(spos <= tpos, 0.0, NEG))
    reset(win_state)
    n_win = WINDOW // tkw + max(tq // tkw, 1)
    last = (t0 + tq - 1) // tkw
    for i in range(n_win):
        kw0 = (last - (n_win - 1) + i) * tkw
        spos = kw0 + lax.broadcasted_iota(jnp.int32, (tq, tkw), 1)
        dist = t0 + lax.broadcasted_iota(jnp.int32, (tq, tkw), 0) - spos
        bias = jnp.where((spos >= 0) & (dist >= 0) & (dist < WINDOW), 0.0, NEG)
        tile_step(q, kwx_ref, vwx_ref, pl.multiple_of(jnp.maximum(kw0, 0), tkw), tkw, win_state, i % 2, bias)
    o_slc = result(slc_state)
    o_win = result(win_state)

    gates = jax.nn.sigmoid(gl_ref[0, 0, pl.ds(pl.multiple_of(t0, tq), tq), :] + bg_ref[0])
    outs = []
    for r in range(R):
        rs = slice(r * tq, (r + 1) * tq)
        outs.append(gates[:, 3 * r:3 * r + 1] * o_cmp[rs] + gates[:, 3 * r + 1:3 * r + 2] * o_slc[rs]
                    + gates[:, 3 * r + 2:3 * r + 3] * o_win[rs])
    out_ref[0] = jnp.concatenate(outs, axis=1).astype(out_ref.dtype)


def nsa_attention(qp, q_feat, kc, vc, kvx, kv_off, slc_const, win_const, ones_col, gl, bg, ov):
    b, t, _ = qp.shape
    g, r = q_feat.shape[:2]
    dh = B_HEAD_DIM
    tq = min(NSA_TQ, t)
    tks = min(NSA_TK_SLC, t)
    tkw = min(NSA_TK_WIN, t)
    ncmp = kc.shape[2]
    n_top = min(SEL_TOPN, t // SEL_BLOCK)
    per_tile = LANES // dh
    assert t // SEL_BLOCK <= LANES // 2 and per_tile == 2 and g % per_tile == 0
    kv_spec = lambda c: pl.BlockSpec((1, t, LANES),
                                     lambda i, j, k: (i, 0, kv_off + c * (g // per_tile) + j // per_tile))
    const_spec = lambda a: pl.BlockSpec(a.shape, lambda i, j, k: (0,) * a.ndim)

    def branch_scratch(tk):
        slot = [pltpu.VMEM((r * tq, tk), F32), pltpu.VMEM((r * tq, tk), BF16), pltpu.VMEM((r * tq, LANES), F32)]
        return [pltpu.VMEM((r * tq, LANES), F32), pltpu.VMEM((r * tq, LANES), F32),
                pltpu.VMEM((tq, tk), F32)] + slot + slot

    return pl.pallas_call(
        functools.partial(_nsa_kernel, tq=tq, tks=tks, tkw=tkw, n_top=n_top),
        grid=(b, g, t // tq),
        in_specs=[pl.BlockSpec((1, tq, r * dh), lambda i, j, k: (i, k, j)),
                  pl.BlockSpec((1, r, 1, LANES), lambda i, j, k: (j, 0, 0, 0)),
                  pl.BlockSpec((1, 1, ncmp, 2 * LANES), lambda i, j, k: (i, j, 0, 0)),
                  pl.BlockSpec((1, 1, ncmp, dh), lambda i, j, k: (i, j, 0, 0)),
                  kv_spec(0), kv_spec(1), kv_spec(2), kv_spec(3),
                  const_spec(slc_const), const_spec(win_const), const_spec(ones_col),
                  pl.BlockSpec((1, 1, t, 3 * r), lambda i, j, k: (i, j, 0, 0)),
                  pl.BlockSpec((1, 1, 3 * r), lambda i, j, k: (j, 0, 0)),
                  const_spec(ov)],
        out_specs=pl.BlockSpec((1, tq, r * dh), lambda i, j, k: (i, k, j)),
        out_shape=jax.ShapeDtypeStruct((b, t, g * r * dh), BF16),
        scratch_shapes=[pltpu.VMEM((r * tq, 2 * LANES), BF16),
                        pltpu.VMEM((t, 2 * LANES), BF16),
                        pltpu.VMEM((t, LANES), BF16),
                        pltpu.VMEM((t, 2 * LANES), BF16),
                        pltpu.VMEM((t, LANES), BF16),
                        ] + branch_scratch(tks) + branch_scratch(tkw)
                       + [pltpu.SMEM((t // tks + 1,), jnp.int32)],
        compiler_params=_cparams("arbitrary", "arbitrary", "arbitrary"),
        name="nsa_attention",
    )(qp, q_feat, kc, vc, kvx, kvx, kvx, kvx, slc_const, win_const, ones_col, gl, bg, ov)


def _alibi_slopes(n):
    return np.power(2.0, -8.0 * np.arange(1, n + 1) / n).astype(np.float32)


def _np_split3(a):
    a = np.asarray(a, np.float32)
    out = []
    r = a
    for _ in range(3):
        p = r.astype(BF16).astype(np.float32)
        out.append(p)
        r = (r - p).astype(np.float32)
    return out


def _pos_features(pos, width):
    hi = (pos // 64).astype(np.float32)
    lo = (pos % 64).astype(np.float32)
    f = np.zeros((pos.shape[0], width), np.float32)
    f[:, 0:3] = hi[:, None]
    f[:, 3:6] = lo[:, None]
    return f


def _slope_features(width):
    s1, s2, s3 = _np_split3((_alibi_slopes(B_HEADS).astype(np.float64) * LOG2E).astype(np.float32))
    f = np.zeros((B_HEADS, width), np.float32)
    for i, s in enumerate((s1, s2, s3)):
        f[:, i] = 64.0 * s
        f[:, 3 + i] = s
    return f


META_I1, META_I2, META_W1, META_W2 = 0, 1, 2, 3


def _router_kernel(x_ref, h_ref, wo_ref, g_ref, w_ref, xo_ref, xn_ref, meta_ref, pos_ref, cnt_ref, run_ref):
    @pl.when(pl.program_id(0) == 0)
    def _():
        run_ref[...] = jnp.zeros_like(run_ref)

    xf = x_ref[...] + _dot(h_ref[...], wo_ref[...])
    xo_ref[...] = xf
    xn = _rms(xf, g_ref[...])
    xn_ref[...] = xn
    logits = _dot_hi_lo(xn, w_ref[...])
    tm = logits.shape[0]
    lane = lax.broadcasted_iota(jnp.int32, logits.shape, 1)
    logits = jnp.where(lane < N_EXPERTS, logits, -jnp.inf)
    m1 = jnp.max(logits, axis=-1, keepdims=True)
    i1 = jnp.min(jnp.where(logits == m1, lane, LANES), axis=-1, keepdims=True)
    rest = jnp.where(lane == i1, -jnp.inf, logits)
    m2 = jnp.max(rest, axis=-1, keepdims=True)
    i2 = jnp.min(jnp.where(rest == m2, lane, LANES), axis=-1, keepdims=True)
    e2 = jnp.exp(m2 - m1)
    w1 = 1.0 / (1.0 + e2)
    meta = jnp.where(lane == META_I1, i1.astype(F32), 0.0)
    meta = jnp.where(lane == META_I2, i2.astype(F32), meta)
    meta = jnp.where(lane == META_W1, w1, meta)
    meta_ref[...] = jnp.where(lane == META_W2, e2 * w1, meta)

    sel = jnp.where((lane == i1) | (lane == i2), 1.0, 0.0)
    row = lax.broadcasted_iota(jnp.int32, (tm, tm), 0)
    col = lax.broadcasted_iota(jnp.int32, (tm, tm), 1)
    before = jnp.where(col < row, 1.0, 0.0).astype(BF16)
    run = run_ref[...]
    pos_ref[...] = run + _dot(before, sel.astype(BF16))
    run = run + jnp.sum(sel, axis=0, keepdims=True)
    run_ref[...] = run
    cnt_ref[...] = run


def moe_router(x, h, w_out, g, w_router, *, tm=512):
    n, d = x.shape
    dh_in = h.shape[1]
    tm = min(tm, n)
    w = _hi_lo_weight(w_router)
    row_spec = pl.BlockSpec((tm, LANES), lambda i: (i, 0))
    return pl.pallas_call(
        _router_kernel,
        grid=(n // tm,),
        in_specs=[pl.BlockSpec((tm, d), lambda i: (i, 0)),
                  pl.BlockSpec((tm, dh_in), lambda i: (i, 0)),
                  pl.BlockSpec((dh_in, d), lambda i: (0, 0)),
                  pl.BlockSpec((1, d), lambda i: (0, 0)),
                  pl.BlockSpec((d, 2 * LANES), lambda i: (0, 0))],
        out_specs=[pl.BlockSpec((tm, d), lambda i: (i, 0)), pl.BlockSpec((tm, d), lambda i: (i, 0)),
                   row_spec, row_spec,
                   pl.BlockSpec((1, LANES), lambda i: (0, 0))],
        out_shape=[jax.ShapeDtypeStruct((n, d), F32), jax.ShapeDtypeStruct((n, d), F32),
                   jax.ShapeDtypeStruct((n, LANES), F32),
                   jax.ShapeDtypeStruct((n, LANES), F32), jax.ShapeDtypeStruct((1, LANES), F32)],
        scratch_shapes=[pltpu.VMEM((1, LANES), F32)],
        compiler_params=_cparams("arbitrary"),
        name="moe_router",
    )(x, h, w_out, g.reshape(1, d), w)


MOE_TILE = 512


def _moe_kernel(te_ref, ok_ref, idx0_ref, idxn_ref, x_hbm, wg_ref, wu_ref, wd_ref, o_ref, xbuf, xb_ref, sem):
    i = pl.program_id(0)
    j = pl.program_id(1)
    nt = pl.num_programs(0)
    nf = pl.num_programs(1)
    tm = xb_ref.shape[0]
    per_step = tm // nf
    slot = i % 2

    def row_copy(idx_ref, r, s):
        return pltpu.make_async_copy(x_hbm.at[pl.ds(idx_ref[0, 0, r], 1), :], xbuf.at[s, pl.ds(r, 1), :], sem.at[s])

    @pl.when((i == 0) & (j == 0))
    def _():
        def start0(r, c):
            row_copy(idx0_ref, r, 0).start()
            return c
        lax.fori_loop(0, tm, start0, 0)

    @pl.when(j == 0)
    def _():
        for r in range(tm):
            row_copy(idx0_ref, r, slot).wait()
        xb_ref[...] = xbuf[slot].astype(BF16)
        o_ref[...] = jnp.zeros_like(o_ref)

    def prefetch():
        base = j * per_step
        for r in range(per_step):
            row_copy(idxn_ref, base + r, 1 - slot).start()

    def compute():
        xb = xb_ref[...]
        a = (_silu(_dot(xb, wg_ref[0])) * _dot(xb, wu_ref[0])).astype(BF16)
        o_ref[...] += _dot(a, wd_ref[0])

    has_next = i + 1 < nt
    ok = ok_ref[i] > 0

    @pl.when(has_next & ok)
    def _():
        prefetch()
        compute()

    @pl.when(has_next & jnp.logical_not(ok))
    def _():
        prefetch()

    @pl.when(jnp.logical_not(has_next) & ok)
    def _():
        compute()


def moe_experts(x, row_tok, tile_expert, tile_ok, w_gu, w_down, *, tf=1792):
    p = row_tok.shape[0]
    d = x.shape[1]
    ne, f, _ = w_down.shape
    tm = MOE_TILE
    nf = f // tf
    nt = p // tm
    idx = row_tok.reshape(nt, 1, tm)
    grid_spec = pltpu.PrefetchScalarGridSpec(
        num_scalar_prefetch=2,
        grid=(nt, nf),
        in_specs=[pl.BlockSpec((1, 1, tm), lambda i, j, te, ok: (0, 0, 0), memory_space=pltpu.SMEM),
                  pl.BlockSpec((1, 1, tm), lambda i, j, te, ok: (jnp.minimum(i + 1, nt - 1), 0, 0),
                               memory_space=pltpu.SMEM),
                  pl.BlockSpec(memory_space=pl.ANY),
                  pl.BlockSpec((1, d, tf), lambda i, j, te, ok: (te[i], 0, j)),
                  pl.BlockSpec((1, d, tf), lambda i, j, te, ok: (te[i], 0, j + nf)),
                  pl.BlockSpec((1, tf, d), lambda i, j, te, ok: (te[i], j, 0))],
        out_specs=pl.BlockSpec((tm, d), lambda i, j, te, ok: (i, 0)),
        scratch_shapes=[pltpu.VMEM((2, tm, d), F32), pltpu.VMEM((tm, d), BF16),
                        pltpu.SemaphoreType.DMA((2,))])
    return pl.pallas_call(
        _moe_kernel,
        grid_spec=grid_spec,
        out_shape=jax.ShapeDtypeStruct((p, d), F32),
        compiler_params=_cparams("arbitrary", "arbitrary"),
        name="moe_experts",
    )(tile_expert, tile_ok, idx, idx, x, w_gu, w_gu, w_down)


def _combine_kernel(d0_ref, dn_ref, x_ref, meta_ref, y_ref, gf_ref, o_ref, ybuf, sem):
    i = pl.program_id(0)
    nt = pl.num_programs(0)
    rows = o_ref.shape[0]
    slot = i % 2

    def row_copy(d_ref, k, r, s):
        return pltpu.make_async_copy(y_ref.at[pl.ds(d_ref[0, k, r], 1), :], ybuf.at[s, k, pl.ds(r, 1), :], sem.at[s])

    @pl.when(i == 0)
    def _():
        def start0(r, c):
            row_copy(d0_ref, 0, r, 0).start()
            row_copy(d0_ref, 1, r, 0).start()
            return c
        lax.fori_loop(0, rows, start0, 0)

    for r in range(rows):
        row_copy(d0_ref, 0, r, slot).wait()
        row_copy(d0_ref, 1, r, slot).wait()

    @pl.when(i + 1 < nt)
    def _():
        for r in range(rows):
            row_copy(dn_ref, 0, r, 1 - slot).start(priority=0)
            row_copy(dn_ref, 1, r, 1 - slot).start(priority=1)

    meta = meta_ref[...]
    w1 = meta[:, META_W1:META_W1 + 1]
    w2 = meta[:, META_W2:META_W2 + 1]
    o_ref[...] = _rms(x_ref[...] + w1 * ybuf[slot, 0] + w2 * ybuf[slot, 1], gf_ref[...])


def moe_combine(x, meta, ys, d1, d2, g_final, *, tc=256):
    n, d = x.shape
    tc = min(tc, n)
    nt = n // tc
    dd = jnp.stack([d1.reshape(nt, tc), d2.reshape(nt, tc)], axis=1)
    return pl.pallas_call(
        _combine_kernel,
        grid=(nt,),
        in_specs=[pl.BlockSpec((1, 2, tc), lambda i: (0, 0, 0), memory_space=pltpu.SMEM),
                  pl.BlockSpec((1, 2, tc), lambda i: (jnp.minimum(i + 1, nt - 1), 0, 0), memory_space=pltpu.SMEM),
                  pl.BlockSpec((tc, d), lambda i: (i, 0)),
                  pl.BlockSpec((tc, LANES), lambda i: (i, 0)),
                  pl.BlockSpec(memory_space=pl.ANY),
                  pl.BlockSpec((1, d), lambda i: (0, 0))],
        out_specs=pl.BlockSpec((tc, d), lambda i: (i, 0)),
        out_shape=jax.ShapeDtypeStruct((n, d), F32),
        scratch_shapes=[pltpu.VMEM((2, 2, tc, d), F32), pltpu.SemaphoreType.DMA((2,))],
        compiler_params=_cparams("arbitrary"),
        name="moe_combine",
    )(dd, dd, x, meta, ys, g_final.reshape(1, d))


def kernel(x, norm_mix, norm_ffn, a_w_in, a_b_gate, a_conv, a_norm_h, a_w_out, norm_kv, b_w_kv,
           b_cmp_pos, b_cmp_w1, b_cmp_w2, b_w_q, b_b_gate, b_w_out, f_w_gu, f_w_down,
           m_router, m_w_gu, m_w_down, norm_final):
    B, T, D = x.shape
    N = B * T
    G, R, dh = B_KV_GROUPS, B_REP, B_HEAD_DIM
    xs = x.reshape(N, D)

    inner4 = a_w_in.shape[2] - 2 * A_HEADS
    w_in = a_w_in[0]
    proj, gcol = norm_matmul(xs, norm_mix[0], w_in[:, :inner4].astype(BF16), w_side=w_in[:, inner4:])
    hs = mlstm_core(proj, gcol, a_b_gate[0], a_conv[0], a_norm_h[0], B, T)
    xs = ffn_dense(xs, hs, a_w_out[0].astype(BF16), norm_ffn[0], f_w_gu[0].astype(BF16), f_w_down[0].astype(BF16))

    hd = B_HEADS * dh
    feat_w = LANES - dh
    ncmp = T // CMP_STRIDE

    cmp_cols = 2 * G * dh
    kvx = norm_matmul(xs, norm_kv, b_w_kv.astype(BF16), tn=b_w_kv.shape[1] // 2)
    w_q = b_w_q[0]
    qp, gl = norm_matmul(xs, norm_mix[1], (w_q[:, :hd] * (dh ** -0.5 * LOG2E)).astype(BF16), w_side=w_q[:, hd:])

    kvt = kvx[:, :cmp_cols].reshape(B, T, 2 * G, dh).transpose(0, 2, 1, 3)
    pos = b_cmp_pos.transpose(1, 0, 2).reshape(2, 1, CMP_LEN * dh).astype(BF16)
    kvc = nsa_compress(kvt.reshape(B, 2 * G, ncmp, CMP_STRIDE * dh), pos,
                       b_cmp_w1.astype(BF16), b_cmp_w2.astype(BF16))

    def hi_lanes(f, lo=None):
        lo = np.zeros((f.shape[0], dh), np.float32) if lo is None else lo
        return jnp.asarray(np.concatenate([lo, f], axis=1), BF16)

    key_pos = _pos_features(np.arange(T), feat_w)
    blk_onehot = (np.arange(T)[:, None] // SEL_BLOCK == np.arange(dh)[None, :]).astype(np.float32)
    slc_const = hi_lanes(key_pos, blk_onehot)
    win_const = hi_lanes(key_pos)
    cmp_const = hi_lanes(_pos_features(np.arange(ncmp) * CMP_STRIDE + CMP_LEN - 1, feat_w))
    q_feat = hi_lanes(_slope_features(feat_w)).reshape(G, R, 1, LANES)
    ones_col = np.zeros((1, feat_w), np.float32)
    ones_col[0, 0] = 1.0
    ones_col = hi_lanes(ones_col)
    kcb = kvc[:, :G].astype(BF16)
    kc = jnp.concatenate([kcb, kcb, jnp.broadcast_to(cmp_const, (B, G, ncmp, LANES))], axis=-1)
    vc = kvc[:, G:].astype(BF16)
    glt = gl[:, :3 * B_HEADS].reshape(B, T, G, 3 * R).transpose(0, 2, 1, 3)
    bg = b_b_gate[0].reshape(G, 1, 3 * R)

    nsel = T // SEL_BLOCK
    ci = np.arange(ncmp)[:, None] * CMP_STRIDE
    sj = np.arange(LANES)[None, :] * SEL_BLOCK
    ov = ((ci < sj + SEL_BLOCK) & (ci + CMP_LEN > sj) & (np.arange(LANES)[None, :] < nsel)
          & (np.arange(ncmp)[:, None] < ncmp - 1))
    ov = jnp.asarray(ov.astype(np.float32), BF16)

    oa = nsa_attention(qp.reshape(B, T, hd), q_feat, kc, vc, kvx.reshape(B, T, -1),
                       cmp_cols // LANES, slc_const, win_const, ones_col, glt, bg, ov)

    xs, xn, meta, pos, cnt = moe_router(xs, oa.reshape(N, hd), b_w_out[0].astype(BF16), norm_ffn[1], m_router[0])
    ne = N_EXPERTS
    p_rows = 2 * N + ne * MOE_TILE
    i1 = meta[:, META_I1].astype(jnp.int32)
    i2 = meta[:, META_I2].astype(jnp.int32)
    counts = cnt[0, :ne].astype(jnp.int32)
    padded = (counts + MOE_TILE - 1) // MOE_TILE * MOE_TILE
    seg_end = jnp.cumsum(padded)
    seg_start = seg_end - padded
    pos8 = pos[:, :ne].astype(jnp.int32)
    d1 = seg_start[i1] + jnp.take_along_axis(pos8, i1[:, None], axis=1)[:, 0]
    d2 = seg_start[i2] + jnp.take_along_axis(pos8, i2[:, None], axis=1)[:, 0]
    tok = jnp.arange(N, dtype=jnp.int32)
    row_tok = jnp.zeros((p_rows,), jnp.int32).at[jnp.concatenate([d1, d2])].set(jnp.concatenate([tok, tok]))
    tile_start = jnp.arange(p_rows // MOE_TILE, dtype=jnp.int32) * MOE_TILE
    tile_expert = jnp.minimum(jnp.sum(tile_start[:, None] >= seg_end[None, :], axis=1), ne - 1).astype(jnp.int32)
    tile_ok = (tile_start < seg_end[ne - 1]).astype(jnp.int32)

    ys = moe_experts(xn, row_tok, tile_expert, tile_ok, m_w_gu[0].astype(BF16), m_w_down[0].astype(BF16))
    out = moe_combine(xs, meta, ys, d1, d2, norm_final)
    return out.reshape(B, T, D)
```

```python
import functools

import numpy as np
import jax
import jax.numpy as jnp
from jax import lax
from jax.experimental import pallas as pl
from jax.experimental.pallas import tpu as pltpu

F32 = jnp.float32
BF16 = jnp.bfloat16

RMS_EPS = 1e-6
A_HEADS = 4
A_CONV = 4
B_HEADS = 16
B_KV_GROUPS = 4
B_REP = B_HEADS // B_KV_GROUPS
B_HEAD_DIM = 64
CMP_LEN = 32
CMP_STRIDE = 16
SEL_BLOCK = 64
SEL_TOPN = 16
WINDOW = 512
FORCED_SCORE = 1e4
N_EXPERTS = 8

LANES = 128
V7X_VMEM_BYTES = 64 * 1024 * 1024
VMEM_LIMIT = V7X_VMEM_BYTES - 8 * 1024 * 1024
NEG = -1e30


def _cparams(*sem):
    return pltpu.CompilerParams(dimension_semantics=sem, vmem_limit_bytes=VMEM_LIMIT)


def _dot(a, b):
    return jnp.dot(a, b, preferred_element_type=F32)


def _dot_nt(a, b):
    return lax.dot_general(a, b, (((1,), (1,)), ((), ())), preferred_element_type=F32)


def _dot_tn(a, b):
    return lax.dot_general(a, b, (((0,), (0,)), ((), ())), preferred_element_type=F32)


def _split3(a):
    a1 = a.astype(BF16)
    r1 = a - a1.astype(F32)
    a2 = r1.astype(BF16)
    a3 = (r1 - a2.astype(F32)).astype(BF16)
    return a1, a2, a3


def _hi_lo_weight(w):
    w = jnp.pad(w, ((0, 0), (0, LANES - w.shape[1])))
    hi = w.astype(BF16)
    lo = (w - hi.astype(F32)).astype(BF16)
    return jnp.concatenate([hi, lo], axis=1)


def _dot_hi_lo(a, w_hl):
    a1 = a.astype(BF16)
    a2 = (a - a1.astype(F32)).astype(BF16)
    s = _dot(a1, w_hl) + _dot(a2, w_hl)
    return s[:, :LANES] + s[:, LANES:]


def _rms(xf, g):
    return xf * lax.rsqrt(jnp.mean(xf * xf, axis=-1, keepdims=True) + RMS_EPS) * g


def _silu(x):
    return x * jax.nn.sigmoid(x)


def _log_sigmoid(x):
    return jnp.minimum(x, 0.0) - jnp.log(1.0 + jnp.exp(-jnp.abs(x)))


def _norm_matmul_kernel(*refs, side):
    if side:
        x_ref, g_ref, w_ref, ws_ref, o_ref, os_ref, xn_ref = refs
    else:
        x_ref, g_ref, w_ref, o_ref, xn_ref = refs

    @pl.when(pl.program_id(1) == 0)
    def _():
        xn = _rms(x_ref[...], g_ref[...])
        xn_ref[...] = xn.astype(xn_ref.dtype)
        if side:
            os_ref[...] = _dot_hi_lo(xn, ws_ref[...])

    o_ref[...] = _dot(xn_ref[...], w_ref[...]).astype(o_ref.dtype)


def norm_matmul(x, g, w, *, w_side=None, tm=1024, tn=1024):
    n, d = x.shape
    dout = w.shape[1]
    tm = min(tm, n)
    tn = min(tn, dout)
    side = w_side is not None
    in_specs = [pl.BlockSpec((tm, d), lambda i, j: (i, 0)),
                pl.BlockSpec((1, d), lambda i, j: (0, 0)),
                pl.BlockSpec((d, tn), lambda i, j: (0, j))]
    out_specs = [pl.BlockSpec((tm, tn), lambda i, j: (i, j))]
    out_shape = [jax.ShapeDtypeStruct((n, dout), BF16)]
    args = [x, g.reshape(1, d), w]
    if side:
        in_specs.append(pl.BlockSpec((d, 2 * LANES), lambda i, j: (0, 0)))
        out_specs.append(pl.BlockSpec((tm, LANES), lambda i, j: (i, 0)))
        out_shape.append(jax.ShapeDtypeStruct((n, LANES), F32))
        args.append(_hi_lo_weight(w_side))
    out = pl.pallas_call(
        functools.partial(_norm_matmul_kernel, side=side),
        grid=(n // tm, dout // tn),
        in_specs=in_specs,
        out_specs=out_specs,
        out_shape=out_shape,
        scratch_shapes=[pltpu.VMEM((tm, d), BF16)],
        compiler_params=_cparams("parallel", "arbitrary"),
        name="norm_matmul_side" if side else "norm_matmul",
    )(*args)
    return out if side else out[0]


MLSTM_CHUNK = 256


def _mlstm_kernel(qk_ref, v_ref, o_ref, gcol_ref, grow_ref, bcol_ref, brow_ref, convw_ref, gout_ref,
                  out_ref, ct_ref, n_ref, m_ref, xs_ref):
    L = qk_ref.shape[0]
    H = A_HEADS
    inner = v_ref.shape[1]
    dh = inner // H

    @pl.when(pl.program_id(1) == 0)
    def _():
        ct_ref[...] = jnp.zeros_like(ct_ref)
        n_ref[...] = jnp.zeros_like(n_ref)
        m_ref[...] = jnp.zeros_like(m_ref)
        xs_ref[...] = jnp.zeros_like(xs_ref)

    row = lax.broadcasted_iota(jnp.int32, (L, L), 0)
    col = lax.broadcasted_iota(jnp.int32, (L, L), 1)
    causal = col <= row
    tril = jnp.where(causal, 1.0, 0.0).astype(BF16)
    triu = jnp.where(row <= col, 1.0, 0.0).astype(BF16)

    gc = gcol_ref[...] + brow_ref[...]
    gr = grow_ref[...] + bcol_ref[...]
    lfc1, lfc2, lfc3 = _split3(_log_sigmoid(gc))
    lfr1, lfr2, lfr3 = _split3(_log_sigmoid(gr))
    b_c = _dot(tril, lfc1) + _dot(tril, lfc2) + _dot(tril, lfc3)
    b_r = _dot(lfr1, triu) + _dot(lfr2, triu) + _dot(lfr3, triu)

    halo = xs_ref.shape[0] - L
    xs_ref[halo:, :] = qk_ref[...].astype(F32)

    shift = [jnp.where(row - col == s, 1.0, 0.0).astype(BF16) for s in range(1, A_CONV)]
    rowh = lax.broadcasted_iota(jnp.int32, (halo, dh), 0)

    def conv(cols):
        w = convw_ref[:, cols]
        x = xs_ref[halo:, cols]
        tail = xs_ref[:halo, cols]
        y = x * w[A_CONV - 1:A_CONV, :]
        head = jnp.zeros((halo, dh), F32)
        for s in range(1, A_CONV):
            ws = w[A_CONV - 1 - s:A_CONV - s, :]
            y = y + _dot(shift[s - 1], (x * ws).astype(BF16))
            head = head + jnp.where(rowh < s, pltpu.roll(tail, s, 0), 0.0) * ws
        return jnp.concatenate([y[:halo] + head, y[halo:]], axis=0)

    for h in range(H):
        hs = slice(h * dh, (h + 1) * dh)
        ks = slice(inner + h * dh, inner + (h + 1) * dh)
        q = conv(hs)
        k = conv(ks) * (dh ** -0.5)
        v = v_ref[:, hs]
        qb = q.astype(BF16)
        kb = k.astype(BF16)

        li_c = gc[:, h:h + 1]
        bc = b_c[:, H + h:H + h + 1]
        li_r = gr[h:h + 1, :]
        br = b_r[H + h:H + h + 1, :]
        m_prev = m_ref[h:h + 1, 0:1]

        d = jnp.where(causal, bc - br + li_r, -jnp.inf)
        inter = bc + m_prev
        m_t = jnp.maximum(inter, jnp.max(d, axis=-1, keepdims=True))
        w_inter = jnp.exp(inter - m_t)
        s = _dot_nt(qb, kb) * jnp.exp(d - m_t)
        ct = ct_ref[h]
        num = _dot(s.astype(BF16), v) + w_inter * _dot(qb, ct.astype(BF16))
        den = jnp.sum(s, axis=-1, keepdims=True) + w_inter * jnp.sum(q * n_ref[h], axis=-1, keepdims=True)
        hh = num / jnp.maximum(jnp.abs(den), jnp.exp(-m_t))
        hh = hh * lax.rsqrt(jnp.mean(hh * hh, axis=-1, keepdims=True) + RMS_EPS)
        out_ref[:, hs] = (hh * gout_ref[:, hs] * jax.nn.sigmoid(o_ref[:, hs].astype(F32))).astype(out_ref.dtype)

        b_last = bc[L - 1:L, :]
        g = b_last - bc + li_c
        m_new = jnp.maximum(b_last + m_prev, jnp.max(g, axis=0, keepdims=True))
        a_prev = jnp.exp(b_last + m_prev - m_new)
        a_s = jnp.exp(g - m_new)
        ct_ref[h] = a_prev * ct + _dot_tn(kb, (v.astype(F32) * a_s).astype(BF16))
        n_ref[h] = a_prev * n_ref[h] + jnp.sum(k * a_s, axis=0, keepdims=True)
        m_ref[h:h + 1, :] = jnp.broadcast_to(m_new, (1, LANES))

    xs_ref[:halo, :] = xs_ref[L:, :]


def mlstm_core(proj, gcol, b_gate, conv_w, g_out, batch, seq):
    n = proj.shape[0]
    inner = proj.shape[1] // 4
    H = A_HEADS
    dh = inner // H
    L = min(MLSTM_CHUNK, seq)
    nc = seq // L
    grow = gcol[:, :2 * H].T
    brow = jnp.pad(b_gate, (0, LANES - 2 * H)).reshape(1, LANES)
    bcol = b_gate.reshape(2 * H, 1)
    return pl.pallas_call(
        _mlstm_kernel,
        grid=(batch, nc),
        in_specs=[pl.BlockSpec((L, 2 * inner), lambda b, c: (b * nc + c, 0)),
                  pl.BlockSpec((L, inner), lambda b, c: (b * nc + c, 2)),
                  pl.BlockSpec((L, inner), lambda b, c: (b * nc + c, 3)),
                  pl.BlockSpec((L, LANES), lambda b, c: (b * nc + c, 0)),
                  pl.BlockSpec((2 * H, L), lambda b, c: (0, b * nc + c)),
                  pl.BlockSpec((2 * H, 1), lambda b, c: (0, 0)),
                  pl.BlockSpec((1, LANES), lambda b, c: (0, 0)),
                  pl.BlockSpec((A_CONV, 2 * inner), lambda b, c: (0, 0)),
                  pl.BlockSpec((1, inner), lambda b, c: (0, 0))],
        out_specs=pl.BlockSpec((L, inner), lambda b, c: (b * nc + c, 0)),
        out_shape=jax.ShapeDtypeStruct((n, inner), BF16),
        scratch_shapes=[pltpu.VMEM((H, dh, dh), F32),
                        pltpu.VMEM((H, 1, dh), F32),
                        pltpu.VMEM((8, LANES), F32),
                        pltpu.VMEM((8 + L, 2 * inner), F32)],
        compiler_params=_cparams("parallel", "arbitrary"),
        name="mlstm_core",
    )(proj, proj, proj, gcol, grow, bcol, brow, conv_w, g_out.reshape(1, inner))


def _ffn_kernel(x_ref, h_ref, wo_ref, g_ref, wg_ref, wu_ref, wd_ref, o_ref, xn_ref):
    @pl.when(pl.program_id(1) == 0)
    def _():
        xf = x_ref[...] + _dot(h_ref[...], wo_ref[...])
        xn_ref[...] = _rms(xf, g_ref[...]).astype(BF16)
        o_ref[...] = xf

    xn = xn_ref[...]
    a = (_silu(_dot(xn, wg_ref[...])) * _dot(xn, wu_ref[...])).astype(BF16)
    o_ref[...] += _dot(a, wd_ref[...])


def ffn_dense(x, h, w_out, g, w_gu, w_down, *, tm=512):
    n, d = x.shape
    f = w_down.shape[0]
    dh_in = h.shape[1]
    tm = min(tm, n)
    tf, nf = f, 1
    once = pl.Buffered(1)
    return pl.pallas_call(
        _ffn_kernel,
        grid=(n // tm, nf),
        in_specs=[pl.BlockSpec((tm, d), lambda i, j: (i, 0)),
                  pl.BlockSpec((tm, dh_in), lambda i, j: (i, 0)),
                  pl.BlockSpec((dh_in, d), lambda i, j: (0, 0), pipeline_mode=once),
                  pl.BlockSpec((1, d), lambda i, j: (0, 0)),
                  pl.BlockSpec((d, tf), lambda i, j: (0, j), pipeline_mode=once),
                  pl.BlockSpec((d, tf), lambda i, j: (0, j + nf), pipeline_mode=once),
                  pl.BlockSpec((tf, d), lambda i, j: (j, 0), pipeline_mode=once)],
        out_specs=pl.BlockSpec((tm, d), lambda i, j: (i, 0)),
        out_shape=jax.ShapeDtypeStruct((n, d), F32),
        scratch_shapes=[pltpu.VMEM((tm, d), BF16)],
        compiler_params=_cparams("parallel", "arbitrary"),
        name="ffn_dense",
    )(x, h, w_out, g.reshape(1, d), w_gu, w_gu, w_down)


def _compress_kernel(r_ref, pos_ref, w1_ref, w2_ref, o_ref):
    r = r_ref[0, 0]
    w1 = w1_ref[0]
    half = r.shape[1]
    nc = r.shape[0]
    lo = _dot(r, w1[:half])
    hi = _dot(r, w1[half:])
    hid = lo + pltpu.roll(hi, nc - 1, 0) + _dot(pos_ref[0], w1)
    o_ref[0, 0] = _dot(_silu(hid).astype(BF16), w2_ref[0]).astype(o_ref.dtype)


def nsa_compress(r, pos, w1, w2):
    b, c2, nc, half = r.shape
    g = c2 // 2
    hidden = w1.shape[2]
    dh = w2.shape[2]
    return pl.pallas_call(
        _compress_kernel,
        grid=(b, c2),
        in_specs=[pl.BlockSpec((1, 1, nc, half), lambda i, j: (i, j, 0, 0)),
                  pl.BlockSpec((1, 1, 2 * half), lambda i, j: (j // g, 0, 0)),
                  pl.BlockSpec((1, 2 * half, hidden), lambda i, j: (j // g, 0, 0)),
                  pl.BlockSpec((1, hidden, dh), lambda i, j: (j // g, 0, 0))],
        out_specs=pl.BlockSpec((1, 1, nc, dh), lambda i, j: (i, j, 0, 0)),
        out_shape=jax.ShapeDtypeStruct((b, c2, nc, dh), F32),
        compiler_params=_cparams("parallel", "parallel"),
        name="nsa_compress",
    )(r, pos, w1, w2)


NSA_TQ = 256
NSA_TK_SLC = 512
NSA_TK_WIN = 512
NSA_ROWS = 32
SEL_MASK = 2.0 ** 14
LOG2E = float(np.log2(np.e))


def _nsa_kernel(q_ref, qf_ref, kc_ref, vc_ref, ks_ref, vs_ref, kw_ref, vw_ref, cs_ref, cw_ref, one_ref,
                gl_ref, bg_ref, ov_ref,
                out_ref, qx_ref, ksx_ref, vsx_ref, kwx_ref, vwx_ref,
                ms_ref, accs_ref, biass_ref, s0_ref, p0_ref, a0_ref, s1_ref, p1_ref, a1_ref,
                mw_ref, accw_ref, biasw_ref, s2_ref, p2_ref, a2_ref, s3_ref, p3_ref, a3_ref, live_ref,
                *, tq, tks, tkw, n_top):
    R = B_REP
    M = R * tq
    dh = B_HEAD_DIM
    half = LANES // 2
    t0 = pl.program_id(2) * tq
    slc_state = (ms_ref, accs_ref, biass_ref, ((s0_ref, p0_ref, a0_ref), (s1_ref, p1_ref, a1_ref)))
    win_state = (mw_ref, accw_ref, biasw_ref, ((s2_ref, p2_ref, a2_ref), (s3_ref, p3_ref, a3_ref)))

    @pl.when(pl.program_id(2) == 0)
    def _():
        odd = pl.program_id(1) % 2 == 1
        low_lane = lax.broadcasted_iota(jnp.int32, (ks_ref.shape[1], LANES), 1) < half

        def halves(ref):
            x = ref[0].astype(F32)
            xr = pltpu.roll(x, half, 1)
            return jnp.where(odd, xr, x), jnp.where(odd, x, xr)

        lo, hi = halves(ks_ref)
        ksx_ref[:, :LANES] = jnp.where(low_lane, lo, hi).astype(BF16)
        ksx_ref[:, LANES:] = cs_ref[...]
        lo, hi = halves(kw_ref)
        kwx_ref[:, :LANES] = jnp.where(low_lane, lo, hi).astype(BF16)
        kwx_ref[:, LANES:] = cw_ref[...]
        ones = one_ref[...].astype(F32)
        vsx_ref[...] = jnp.where(low_lane, halves(vs_ref)[0], ones).astype(BF16)
        vwx_ref[...] = jnp.where(low_lane, halves(vw_ref)[0], ones).astype(BF16)

    qb = q_ref[0]
    low_q = lax.broadcasted_iota(jnp.int32, (tq, LANES), 1) < half
    for r in range(R):
        tile = qb[:, (r // 2) * LANES:(r // 2 + 1) * LANES]
        own = (tile, jnp.zeros_like(tile)) if r % 2 == 0 else (jnp.zeros_like(tile), tile)
        qx_ref[r * tq:(r + 1) * tq, :LANES] = jnp.where(low_q, *own)
        qx_ref[r * tq:(r + 1) * tq, LANES:] = jnp.broadcast_to(qf_ref[0, r], (tq, LANES))
    q = qx_ref[...]

    kc = kc_ref[0, 0]
    ncmp = kc.shape[0]
    s = _dot_nt(q, kc)
    tpos_c = t0 + (lax.broadcasted_iota(jnp.int32, (M, ncmp), 0) & (tq - 1))
    cend = lax.broadcasted_iota(jnp.int32, (M, ncmp), 1) * CMP_STRIDE + (CMP_LEN - 1)
    ok_c = tpos_c >= cend
    s = jnp.where(ok_c, s, NEG)
    p = jnp.where(ok_c, jnp.exp2(s - jnp.max(s, axis=-1, keepdims=True)), 0.0)
    dsum = jnp.sum(p, axis=-1, keepdims=True)
    p = p / jnp.where(dsum > 0, dsum, 1.0)
    o_cmp = _dot(p.astype(BF16), vc_ref[0, 0])

    psum = p[0:tq]
    for r in range(1, R):
        psum = psum + p[r * tq:(r + 1) * tq]
    p1, p2, p3 = _split3(psum)
    ov = ov_ref[...]
    imp = _dot(p1, ov) + _dot(p2, ov) + _dot(p3, ov)
    nsel = ks_ref.shape[1] // SEL_BLOCK
    nselp = max(nsel, 8)
    imp_t = imp.T[:nselp]
    jj = lax.broadcasted_iota(jnp.int32, (nselp, tq), 0)
    tt = t0 + lax.broadcasted_iota(jnp.int32, (nselp, tq), 1)
    cur = tt // SEL_BLOCK
    forced = (jj == 0) | (jj == cur) | (jj == cur - 1)
    iv = jnp.where(forced, FORCED_SCORE, imp_t)
    iv = jnp.where(jj * SEL_BLOCK <= tt, iv, -jnp.inf)
    rank = jnp.zeros((nselp, tq), F32)
    for j2 in range(nsel):
        rv = iv[j2:j2 + 1, :]
        rank = rank + jnp.where(jj > j2, jnp.where(rv >= iv, 1.0, 0.0), jnp.where(rv > iv, 1.0, 0.0))
    unsel = jnp.where(rank < n_top, 0.0, -SEL_MASK)
    if nselp < LANES:
        unsel = jnp.concatenate([unsel, jnp.zeros((LANES - nselp, tq), F32)], axis=0)
    qmask = unsel.T.astype(BF16)
    for r in range(R):
        qx_ref[r * tq:(r + 1) * tq, LANES:] = qmask + qf_ref[0, r]

    def tile_step(qq, k_ref, v_ref, k0, tk, state, slot, bias):
        m_ref, acc_ref, bias_ref, slots = state
        s_ref, p_ref, a_ref = slots[slot]
        kk = k_ref[pl.ds(k0, tk), :]
        vv = v_ref[pl.ds(k0, tk), :]
        if bias is not None:
            bias_ref[:, :tk] = bias
        s_ref[:, :tk] = _dot_nt(qq, kk)
        for r in range(R):
            for c0 in range(0, tq, NSA_ROWS):
                rows = slice(r * tq + c0, r * tq + c0 + NSA_ROWS)
                sc = s_ref[rows, :tk]
                if bias is not None:
                    sc = sc + bias_ref[c0:c0 + NSA_ROWS, :tk]
                m_old = m_ref[rows, :]
                m_new = jnp.maximum(m_old, jnp.max(sc, axis=-1, keepdims=True))
                a_ref[rows, :] = jnp.exp2(m_old - m_new)
                m_ref[rows, :] = m_new
                p_ref[rows, :tk] = jnp.exp2(sc - jnp.concatenate([m_new] * (tk // LANES), axis=1)).astype(BF16)
        acc_ref[...] = a_ref[...] * acc_ref[...] + _dot(p_ref[:, :tk], vv)

    def reset(state):
        state[0][...] = jnp.full_like(state[0], NEG)
        state[1][...] = jnp.zeros_like(state[1])

    def result(state):
        acc = state[1][...]
        return acc[:, :dh] / acc[:, dh:dh + 1]

    reset(slc_state)
    qx = qx_ref[...]
    n_full = t0 // tks

    blocks_per_tile = tks // SEL_BLOCK
    n_live = jnp.int32(0)
    for j in range(nsel // blocks_per_tile):
        sel_j = rank[j * blocks_per_tile:(j + 1) * blocks_per_tile, :] < n_top
        live = (jnp.max(jnp.where(sel_j, 1.0, 0.0)) > 0.5) & (j < n_full)
        live_ref[n_live] = j
        n_live = n_live + live.astype(jnp.int32)

    def slc_pair(i, carry):
        tile_step(qx, ksx_ref, vsx_ref, pl.multiple_of(live_ref[2 * i] * tks, tks), tks, slc_state, 0, None)
        tile_step(qx, ksx_ref, vsx_ref, pl.multiple_of(live_ref[2 * i + 1] * tks, tks), tks, slc_state, 1, None)
        return carry

    lax.fori_loop(0, n_live // 2, slc_pair, 0)

    @pl.when(n_live % 2 == 1)
    def _():
        tile_step(qx, ksx_ref, vsx_ref, pl.multiple_of(live_ref[n_live - 1] * tks, tks), tks, slc_state, 0, None)

    kd = pl.multiple_of(n_full * tks, tks)
    spos = kd + lax.broadcasted_iota(jnp.int32, (tq, tks), 1)
    tpos = t0 + lax.broadcasted_iota(jnp.int32, (tq, tks), 0)
    tile_step(qx, ksx_ref, vsx_ref, kd, tks, slc_state, 1, jnp.where(spos <= tpos, 0.0, NEG))
    reset(win_state)
    n_win = WINDOW // tkw + max(tq // tkw, 1)
    last = (t0 + tq - 1) // tkw
    for i in range(n_win):
        kw0 = (last - (n_win - 1) + i) * tkw
        spos = kw0 + lax.broadcasted_iota(jnp.int32, (tq, tkw), 1)
        dist = t0 + lax.broadcasted_iota(jnp.int32, (tq, tkw), 0) - spos
        bias = jnp.where((spos >= 0) & (dist >= 0) & (dist < WINDOW), 0.0, NEG)
        tile_step(q, kwx_ref, vwx_ref, pl.multiple_of(jnp.maximum(kw0, 0), tkw), tkw, win_state, i % 2, bias)
    o_slc = result(slc_state)
    o_win = result(win_state)

    gates = jax.nn.sigmoid(gl_ref[0, 0, pl.ds(pl.multiple_of(t0, tq), tq), :] + bg_ref[0])
    outs = []
    for r in range(R):
        rs = slice(r * tq, (r + 1) * tq)
        outs.append(gates[:, 3 * r:3 * r + 1] * o_cmp[rs] + gates[:, 3 * r + 1:3 * r + 2] * o_slc[rs]
                    + gates[:, 3 * r + 2:3 * r + 3] * o_win[rs])
    out_ref[0] = jnp.concatenate(outs, axis=1).astype(out_ref.dtype)


def nsa_attention(qp, q_feat, kc, vc, kvx, kv_off, slc_const, win_const, ones_col, gl, bg, ov):
    b, t, _ = qp.shape
    g, r = q_feat.shape[:2]
    dh = B_HEAD_DIM
    tq = min(NSA_TQ, t)
    tks = min(NSA_TK_SLC, t)
    tkw = min(NSA_TK_WIN, t)
    ncmp = kc.shape[2]
    n_top = min(SEL_TOPN, t // SEL_BLOCK)
    per_tile = LANES // dh
    assert t // SEL_BLOCK <= LANES // 2 and per_tile == 2 and g % per_tile == 0
    kv_spec = lambda c: pl.BlockSpec((1, t, LANES),
                                     lambda i, j, k: (i, 0, kv_off + c * (g // per_tile) + j // per_tile))
    const_spec = lambda a: pl.BlockSpec(a.shape, lambda i, j, k: (0,) * a.ndim)

    def branch_scratch(tk):
        slot = [pltpu.VMEM((r * tq, tk), F32), pltpu.VMEM((r * tq, tk), BF16), pltpu.VMEM((r * tq, LANES), F32)]
        return [pltpu.VMEM((r * tq, LANES), F32), pltpu.VMEM((r * tq, LANES), F32),
                pltpu.VMEM((tq, tk), F32)] + slot + slot

    return pl.pallas_call(
        functools.partial(_nsa_kernel, tq=tq, tks=tks, tkw=tkw, n_top=n_top),
        grid=(b, g, t // tq),
        in_specs=[pl.BlockSpec((1, tq, r * dh), lambda i, j, k: (i, k, j)),
                  pl.BlockSpec((1, r, 1, LANES), lambda i, j, k: (j, 0, 0, 0)),
                  pl.BlockSpec((1, 1, ncmp, 2 * LANES), lambda i, j, k: (i, j, 0, 0)),
                  pl.BlockSpec((1, 1, ncmp, dh), lambda i, j, k: (i, j, 0, 0)),
                  kv_spec(0), kv_spec(1), kv_spec(2), kv_spec(3),
                  const_spec(slc_const), const_spec(win_const), const_spec(ones_col),
                  pl.BlockSpec((1, 1, t, 3 * r), lambda i, j, k: (i, j, 0, 0)),
                  pl.BlockSpec((1, 1, 3 * r), lambda i, j, k: (j, 0, 0)),
                  const_spec(ov)],
        out_specs=pl.BlockSpec((1, tq, r * dh), lambda i, j, k: (i, k, j)),
        out_shape=jax.ShapeDtypeStruct((b, t, g * r * dh), BF16),
        scratch_shapes=[pltpu.VMEM((r * tq, 2 * LANES), BF16),
                        pltpu.VMEM((t, 2 * LANES), BF16),
                        pltpu.VMEM((t, LANES), BF16),
                        pltpu.VMEM((t, 2 * LANES), BF16),
                        pltpu.VMEM((t, LANES), BF16),
                        ] + branch_scratch(tks) + branch_scratch(tkw)
                       + [pltpu.SMEM((t // tks + 1,), jnp.int32)],
        compiler_params=_cparams("arbitrary", "arbitrary", "arbitrary"),
        name="nsa_attention",
    )(qp, q_feat, kc, vc, kvx, kvx, kvx, kvx, slc_const, win_const, ones_col, gl, bg, ov)


def _alibi_slopes(n):
    return np.power(2.0, -8.0 * np.arange(1, n + 1) / n).astype(np.float32)


def _np_split3(a):
    a = np.asarray(a, np.float32)
    out = []
    r = a
    for _ in range(3):
        p = r.astype(BF16).astype(np.float32)
        out.append(p)
        r = (r - p).astype(np.float32)
    return out


def _pos_features(pos, width):
    hi = (pos // 64).astype(np.float32)
    lo = (pos % 64).astype(np.float32)
    f = np.zeros((pos.shape[0], width), np.float32)
    f[:, 0:3] = hi[:, None]
    f[:, 3:6] = lo[:, None]
    return f


def _slope_features(width):
    s1, s2, s3 = _np_split3((_alibi_slopes(B_HEADS).astype(np.float64) * LOG2E).astype(np.float32))
    f = np.zeros((B_HEADS, width), np.float32)
    for i, s in enumerate((s1, s2, s3)):
        f[:, i] = 64.0 * s
        f[:, 3 + i] = s
    return f


META_I1, META_I2, META_W1, META_W2 = 0, 1, 2, 3


def _router_kernel(x_ref, h_ref, wo_ref, g_ref, w_ref, xo_ref, xn_ref, meta_ref, pos_ref, cnt_ref, run_ref):
    @pl.when(pl.program_id(0) == 0)
    def _():
        run_ref[...] = jnp.zeros_like(run_ref)

    xf = x_ref[...] + _dot(h_ref[...], wo_ref[...])
    xo_ref[...] = xf
    xn = _rms(xf, g_ref[...])
    xn_ref[...] = xn
    logits = _dot_hi_lo(xn, w_ref[...])
    tm = logits.shape[0]
    lane = lax.broadcasted_iota(jnp.int32, logits.shape, 1)
    logits = jnp.where(lane < N_EXPERTS, logits, -jnp.inf)
    m1 = jnp.max(logits, axis=-1, keepdims=True)
    i1 = jnp.min(jnp.where(logits == m1, lane, LANES), axis=-1, keepdims=True)
    rest = jnp.where(lane == i1, -jnp.inf, logits)
    m2 = jnp.max(rest, axis=-1, keepdims=True)
    i2 = jnp.min(jnp.where(rest == m2, lane, LANES), axis=-1, keepdims=True)
    e2 = jnp.exp(m2 - m1)
    w1 = 1.0 / (1.0 + e2)
    meta = jnp.where(lane == META_I1, i1.astype(F32), 0.0)
    meta = jnp.where(lane == META_I2, i2.astype(F32), meta)
    meta = jnp.where(lane == META_W1, w1, meta)
    meta_ref[...] = jnp.where(lane == META_W2, e2 * w1, meta)

    sel = jnp.where((lane == i1) | (lane == i2), 1.0, 0.0)
    row = lax.broadcasted_iota(jnp.int32, (tm, tm), 0)
    col = lax.broadcasted_iota(jnp.int32, (tm, tm), 1)
    before = jnp.where(col < row, 1.0, 0.0).astype(BF16)
    run = run_ref[...]
    pos_ref[...] = run + _dot(before, sel.astype(BF16))
    run = run + jnp.sum(sel, axis=0, keepdims=True)
    run_ref[...] = run
    cnt_ref[...] = run


def moe_router(x, h, w_out, g, w_router, *, tm=512):
    n, d = x.shape
    dh_in = h.shape[1]
    tm = min(tm, n)
    w = _hi_lo_weight(w_router)
    row_spec = pl.BlockSpec((tm, LANES), lambda i: (i, 0))
    return pl.pallas_call(
        _router_kernel,
        grid=(n // tm,),
        in_specs=[pl.BlockSpec((tm, d), lambda i: (i, 0)),
                  pl.BlockSpec((tm, dh_in), lambda i: (i, 0)),
                  pl.BlockSpec((dh_in, d), lambda i: (0, 0)),
                  pl.BlockSpec((1, d), lambda i: (0, 0)),
                  pl.BlockSpec((d, 2 * LANES), lambda i: (0, 0))],
        out_specs=[pl.BlockSpec((tm, d), lambda i: (i, 0)), pl.BlockSpec((tm, d), lambda i: (i, 0)),
                   row_spec, row_spec,
                   pl.BlockSpec((1, LANES), lambda i: (0, 0))],
        out_shape=[jax.ShapeDtypeStruct((n, d), F32), jax.ShapeDtypeStruct((n, d), F32),
                   jax.ShapeDtypeStruct((n, LANES), F32),
                   jax.ShapeDtypeStruct((n, LANES), F32), jax.ShapeDtypeStruct((1, LANES), F32)],
        scratch_shapes=[pltpu.VMEM((1, LANES), F32)],
        compiler_params=_cparams("arbitrary"),
        name="moe_router",
    )(x, h, w_out, g.reshape(1, d), w)


MOE_TILE = 512
MOE_CHUNK = 256


def _moe_kernel(te_ref, ok_ref, idx0_ref, idxn_ref, x_hbm, wg_ref, wu_ref, wd_ref, o_ref, xbuf, xb_ref, sem):
    i = pl.program_id(0)
    j = pl.program_id(1)
    nt = pl.num_programs(0)
    nf = pl.num_programs(1)
    tm = xb_ref.shape[0]
    per_step = tm // nf
    slot = i % 2

    def row_copy(idx_ref, r, s):
        return pltpu.make_async_copy(x_hbm.at[pl.ds(idx_ref[0, 0, r], 1), :], xbuf.at[s, pl.ds(r, 1), :], sem.at[s])

    @pl.when((i == 0) & (j == 0))
    def _():
        def start0(r, c):
            row_copy(idx0_ref, r, 0).start()
            return c
        lax.fori_loop(0, tm, start0, 0)

    @pl.when(j == 0)
    def _():
        for r in range(tm):
            row_copy(idx0_ref, r, slot).wait()
        xb_ref[...] = xbuf[slot].astype(BF16)
        o_ref[...] = jnp.zeros_like(o_ref)

    def prefetch():
        base = j * per_step
        for r in range(per_step):
            row_copy(idxn_ref, base + r, 1 - slot).start()

    def compute():
        xb = xb_ref[...]
        a = (_silu(_dot(xb, wg_ref[0])) * _dot(xb, wu_ref[0])).astype(BF16)
        o_ref[...] += _dot(a, wd_ref[0])

    def compute_and_prefetch():
        xb = xb_ref[...]
        base = j * per_step
        n_chunk = wd_ref.shape[1] // MOE_CHUNK
        per_chunk = -(-per_step // n_chunk)
        parts = []
        zero = jnp.int32(0)
        for c in range(n_chunk):
            for r in range(min(c * per_chunk, per_step), min((c + 1) * per_chunk, per_step)):
                pltpu.make_async_copy(x_hbm.at[pl.ds(idxn_ref[0, 0, base + r] + zero, 1), :],
                                      xbuf.at[1 - slot, pl.ds(base + r, 1), :], sem.at[1 - slot]).start()
            cs = slice(c * MOE_CHUNK, (c + 1) * MOE_CHUNK)
            ac = _silu(_dot(xb, wg_ref[0, :, cs])) * _dot(xb, wu_ref[0, :, cs])
            shifted_out = lax.shift_right_logical(pltpu.bitcast(ac[:8, :LANES], jnp.uint32), jnp.uint32(32))
            zero = jnp.max(shifted_out.astype(F32)).astype(jnp.int32)
            parts.append(ac.astype(BF16))
        o_ref[...] += _dot(jnp.concatenate(parts, axis=1), wd_ref[0])

    has_next = i + 1 < nt
    ok = ok_ref[i] > 0

    @pl.when(has_next & ok)
    def _():
        compute_and_prefetch()

    @pl.when(has_next & jnp.logical_not(ok))
    def _():
        prefetch()

    @pl.when(jnp.logical_not(has_next) & ok)
    def _():
        compute()


def moe_experts(x, row_tok, tile_expert, tile_ok, w_gu, w_down, *, tf=1792):
    p = row_tok.shape[0]
    d = x.shape[1]
    ne, f, _ = w_down.shape
    tm = MOE_TILE
    nf = f // tf
    nt = p // tm
    idx = row_tok.reshape(nt, 1, tm)
    grid_spec = pltpu.PrefetchScalarGridSpec(
        num_scalar_prefetch=2,
        grid=(nt, nf),
        in_specs=[pl.BlockSpec((1, 1, tm), lambda i, j, te, ok: (0, 0, 0), memory_space=pltpu.SMEM),
                  pl.BlockSpec((1, 1, tm), lambda i, j, te, ok: (jnp.minimum(i + 1, nt - 1), 0, 0),
                               memory_space=pltpu.SMEM),
                  pl.BlockSpec(memory_space=pl.ANY),
                  pl.BlockSpec((1, d, tf), lambda i, j, te, ok: (te[i], 0, j)),
                  pl.BlockSpec((1, d, tf), lambda i, j, te, ok: (te[i], 0, j + nf)),
                  pl.BlockSpec((1, tf, d), lambda i, j, te, ok: (te[i], j, 0))],
        out_specs=pl.BlockSpec((tm, d), lambda i, j, te, ok: (i, 0)),
        scratch_shapes=[pltpu.VMEM((2, tm, d), F32), pltpu.VMEM((tm, d), BF16),
                        pltpu.SemaphoreType.DMA((2,))])
    return pl.pallas_call(
        _moe_kernel,
        grid_spec=grid_spec,
        out_shape=jax.ShapeDtypeStruct((p, d), F32),
        compiler_params=_cparams("arbitrary", "arbitrary"),
        name="moe_experts",
    )(tile_expert, tile_ok, idx, idx, x, w_gu, w_gu, w_down)


def _combine_kernel(d0_ref, dn_ref, x_ref, meta_ref, y_ref, gf_ref, o_ref, ybuf, sem):
    i = pl.program_id(0)
    nt = pl.num_programs(0)
    rows = o_ref.shape[0]
    slot = i % 2

    def row_copy(d_ref, k, r, s):
        return pltpu.make_async_copy(y_ref.at[pl.ds(d_ref[0, k, r], 1), :], ybuf.at[s, k, pl.ds(r, 1), :], sem.at[s])

    @pl.when(i == 0)
    def _():
        def start0(r, c):
            row_copy(d0_ref, 0, r, 0).start()
            row_copy(d0_ref, 1, r, 0).start()
            return c
        lax.fori_loop(0, rows, start0, 0)

    for r in range(rows):
        row_copy(d0_ref, 0, r, slot).wait()
        row_copy(d0_ref, 1, r, slot).wait()

    @pl.when(i + 1 < nt)
    def _():
        for r in range(rows):
            row_copy(dn_ref, 0, r, 1 - slot).start(priority=0)
            row_copy(dn_ref, 1, r, 1 - slot).start(priority=1)

    meta = meta_ref[...]
    w1 = meta[:, META_W1:META_W1 + 1]
    w2 = meta[:, META_W2:META_W2 + 1]
    o_ref[...] = _rms(x_ref[...] + w1 * ybuf[slot, 0] + w2 * ybuf[slot, 1], gf_ref[...])


def moe_combine(x, meta, ys, d1, d2, g_final, *, tc=256):
    n, d = x.shape
    tc = min(tc, n)
    nt = n // tc
    dd = jnp.stack([d1.reshape(nt, tc), d2.reshape(nt, tc)], axis=1)
    return pl.pallas_call(
        _combine_kernel,
        grid=(nt,),
        in_specs=[pl.BlockSpec((1, 2, tc), lambda i: (0, 0, 0), memory_space=pltpu.SMEM),
                  pl.BlockSpec((1, 2, tc), lambda i: (jnp.minimum(i + 1, nt - 1), 0, 0), memory_space=pltpu.SMEM),
                  pl.BlockSpec((tc, d), lambda i: (i, 0)),
                  pl.BlockSpec((tc, LANES), lambda i: (i, 0)),
                  pl.BlockSpec(memory_space=pl.ANY),
                  pl.BlockSpec((1, d), lambda i: (0, 0))],
        out_specs=pl.BlockSpec((tc, d), lambda i: (i, 0)),
        out_shape=jax.ShapeDtypeStruct((n, d), F32),
        scratch_shapes=[pltpu.VMEM((2, 2, tc, d), F32), pltpu.SemaphoreType.DMA((2,))],
        compiler_params=_cparams("arbitrary"),
        name="moe_combine",
    )(dd, dd, x, meta, ys, g_final.reshape(1, d))


def kernel(x, norm_mix, norm_ffn, a_w_in, a_b_gate, a_conv, a_norm_h, a_w_out, norm_kv, b_w_kv,
           b_cmp_pos, b_cmp_w1, b_cmp_w2, b_w_q, b_b_gate, b_w_out, f_w_gu, f_w_down,
           m_router, m_w_gu, m_w_down, norm_final):
    B, T, D = x.shape
    N = B * T
    G, R, dh = B_KV_GROUPS, B_REP, B_HEAD_DIM
    xs = x.reshape(N, D)

    inner4 = a_w_in.shape[2] - 2 * A_HEADS
    w_in = a_w_in[0]
    proj, gcol = norm_matmul(xs, norm_mix[0], w_in[:, :inner4].astype(BF16), w_side=w_in[:, inner4:])
    hs = mlstm_core(proj, gcol, a_b_gate[0], a_conv[0], a_norm_h[0], B, T)
    xs = ffn_dense(xs, hs, a_w_out[0].astype(BF16), norm_ffn[0], f_w_gu[0].astype(BF16), f_w_down[0].astype(BF16))

    hd = B_HEADS * dh
    feat_w = LANES - dh
    ncmp = T // CMP_STRIDE

    cmp_cols = 2 * G * dh
    kvx = norm_matmul(xs, norm_kv, b_w_kv.astype(BF16), tn=b_w_kv.shape[1] // 2)
    w_q = b_w_q[0]
    qp, gl = norm_matmul(xs, norm_mix[1], (w_q[:, :hd] * (dh ** -0.5 * LOG2E)).astype(BF16), w_side=w_q[:, hd:])

    kvt = kvx[:, :cmp_cols].reshape(B, T, 2 * G, dh).transpose(0, 2, 1, 3)
    pos = b_cmp_pos.transpose(1, 0, 2).reshape(2, 1, CMP_LEN * dh).astype(BF16)
    kvc = nsa_compress(kvt.reshape(B, 2 * G, ncmp, CMP_STRIDE * dh), pos,
                       b_cmp_w1.astype(BF16), b_cmp_w2.astype(BF16))

    def hi_lanes(f, lo=None):
        lo = np.zeros((f.shape[0], dh), np.float32) if lo is None else lo
        return jnp.asarray(np.concatenate([lo, f], axis=1), BF16)

    key_pos = _pos_features(np.arange(T), feat_w)
    blk_onehot = (np.arange(T)[:, None] // SEL_BLOCK == np.arange(dh)[None, :]).astype(np.float32)
    slc_const = hi_lanes(key_pos, blk_onehot)
    win_const = hi_lanes(key_pos)
    cmp_const = hi_lanes(_pos_features(np.arange(ncmp) * CMP_STRIDE + CMP_LEN - 1, feat_w))
    q_feat = hi_lanes(_slope_features(feat_w)).reshape(G, R, 1, LANES)
    ones_col = np.zeros((1, feat_w), np.float32)
    ones_col[0, 0] = 1.0
    ones_col = hi_lanes(ones_col)
    kcb = kvc[:, :G].astype(BF16)
    kc = jnp.concatenate([kcb, kcb, jnp.broadcast_to(cmp_const, (B, G, ncmp, LANES))], axis=-1)
    vc = kvc[:, G:].astype(BF16)
    glt = gl[:, :3 * B_HEADS].reshape(B, T, G, 3 * R).transpose(0, 2, 1, 3)
    bg = b_b_gate[0].reshape(G, 1, 3 * R)

    nsel = T // SEL_BLOCK
    ci = np.arange(ncmp)[:, None] * CMP_STRIDE
    sj = np.arange(LANES)[None, :] * SEL_BLOCK
    ov = ((ci < sj + SEL_BLOCK) & (ci + CMP_LEN > sj) & (np.arange(LANES)[None, :] < nsel)
          & (np.arange(ncmp)[:, None] < ncmp - 1))
    ov = jnp.asarray(ov.astype(np.float32), BF16)

    oa = nsa_attention(qp.reshape(B, T, hd), q_feat, kc, vc, kvx.reshape(B, T, -1),
                       cmp_cols // LANES, slc_const, win_const, ones_col, glt, bg, ov)

    xs, xn, meta, pos, cnt = moe_router(xs, oa.reshape(N, hd), b_w_out[0].astype(BF16), norm_ffn[1], m_router[0])
    ne = N_EXPERTS
    p_rows = 2 * N + ne * MOE_TILE
    i1 = meta[:, META_I1].astype(jnp.int32)
    i2 = meta[:, META_I2].astype(jnp.int32)
    counts = cnt[0, :ne].astype(jnp.int32)
    padded = (counts + MOE_TILE - 1) // MOE_TILE * MOE_TILE
    seg_end = jnp.cumsum(padded)
    seg_start = seg_end - padded
    pos8 = pos[:, :ne].astype(jnp.int32)
    d1 = seg_start[i1] + jnp.take_along_axis(pos8, i1[:, None], axis=1)[:, 0]
    d2 = seg_start[i2] + jnp.take_along_axis(pos8, i2[:, None], axis=1)[:, 0]
    tok = jnp.arange(N, dtype=jnp.int32)
    row_tok = jnp.zeros((p_rows,), jnp.int32).at[jnp.concatenate([d1, d2])].set(jnp.concatenate([tok, tok]))
    tile_start = jnp.arange(p_rows // MOE_TILE, dtype=jnp.int32) * MOE_TILE
    tile_expert = jnp.minimum(jnp.sum(tile_start[:, None] >= seg_end[None, :], axis=1), ne - 1).astype(jnp.int32)
    tile_ok = (tile_start < seg_end[ne - 1]).astype(jnp.int32)

    ys = moe_experts(xn, row_tok, tile_expert, tile_ok, m_w_gu[0].astype(BF16), m_w_down[0].astype(BF16))
    out = moe_combine(xs, meta, ys, d1, d2, norm_final)
    return out.reshape(B, T, D)
```
